```python
import math
import jax
import jax.numpy as jnp
from jax import lax
import numpy as np

D_MODEL = 1024
BATCH = 16
SEQ = 256
DEPTH = 4
DEC_BATCH = 2
DEC_SEQ = 2048
PAST_LEN = 512

GRID_W = 64

N_MIXERS = 3
N_HYENA = (DEPTH + 2) // 3
N_MLA = (DEPTH + 1) // 3
N_RWKV = DEPTH // 3

DEEPNORM_ALPHA = (2.0 * DEPTH) ** 0.25
DEEPNORM_BETA = (8.0 * DEPTH) ** -0.25
LN_EPS = 1e-5
RMS_EPS = 1e-6

HY_WIDTH = D_MODEL
HY_BANDS = 16
HY_EMB = 1 + 2 * HY_BANDS
HY_FFN = 64
HY_DECAY_MIN = -math.log(1e-2) / 1.5
HY_DECAY_MAX = -math.log(1e-2) / 0.3

MLA_HEADS = 16
MLA_Q_RANK = 256
MLA_KV_RANK = 128
MLA_NOPE_DIM = 64
MLA_ROPE_DIM = 32
MLA_V_DIM = 64
ROPE_BASE = 10000.0
Q_BLOCK = 128

RW_HEAD_DIM = 64
RW_HEADS = D_MODEL // RW_HEAD_DIM
RW_DECAY_LORA = 64
RW_A_LORA = 64
RW_GN_EPS = 64e-5

kernel_name = 'hybrid_hyena_mla_rwkv7_diffusion_step'


def _layer_norm(x, g, b):
    xf = x.astype(jnp.float32)
    mu = jnp.mean(xf, -1, keepdims=True)
    var = jnp.mean(jnp.square(xf - mu), -1, keepdims=True)
    return ((xf - mu) * lax.rsqrt(var + LN_EPS) * g + b).astype(x.dtype)


def _rms_norm(x, g):
    xf = x.astype(jnp.float32)
    return (xf * lax.rsqrt(jnp.mean(xf * xf, -1, keepdims=True) + RMS_EPS) * g).astype(x.dtype)


def _neighbours(u):
    up = jnp.pad(u, ((0, 0), (1, 1), (0, 0)))
    return up[:, :-2], up[:, 2:]


def _modulation(cond, w, b):
    m = jax.nn.silu(cond) @ w + b
    return jnp.split(m, 3, axis=-1)


def _hyena_filter(L, w1, b1, w2, b2, w3, b3, freq, decay):
    pos = jnp.arange(L, dtype=jnp.float32)
    t = jnp.linspace(0.0, 1.0, L, dtype=jnp.float32)
    bands = jnp.linspace(1e-4, HY_BANDS - 1, HY_BANDS, dtype=jnp.float32)
    ang = (2.0 * math.pi / L) * pos[:, None] * bands[None, :]
    feat = jnp.concatenate([t[:, None], jnp.cos(ang), -jnp.sin(ang)], -1)
    hdn = jnp.sin(freq[0] * (feat @ w1 + b1))
    hdn = jnp.sin(freq[1] * (hdn @ w2 + b2))
    hf = (hdn @ w3 + b3).astype(jnp.float32).reshape(L, 2, HY_WIDTH)
    window = jnp.exp(-t[:, None, None] * jnp.abs(decay.astype(jnp.float32))[None])
    h = hf * window
    zero = jnp.zeros((1, HY_WIDTH), jnp.float32)
    return jnp.concatenate([h[:, 0], zero, h[:0:-1, 1]], 0)


def _hyena_mixer(h, w_in, conv_w, conv_b, w1, b1, w2, b2, w3, b3, freq, decay, skip, w_out):
    B, L, _ = h.shape
    u = h @ w_in
    u3, z = u[..., :3 * HY_WIDTH], u[..., 3 * HY_WIDTH:]
    prev, nxt = _neighbours(u3)
    u3 = prev * conv_w[0] + u3 * conv_w[1] + nxt * conv_w[2] + conv_b
    x0, x1, v = jnp.split(u3, 3, axis=-1)
    filt = _hyena_filter(L, w1, b1, w2, b2, w3, b3, freq, decay)
    v = v * x1
    vf = jnp.fft.rfft(v.astype(jnp.float32), n=2 * L, axis=1)
    kf = jnp.fft.rfft(filt, axis=0)
    y = jnp.fft.irfft(vf * kf[None], n=2 * L, axis=1)[:, :L].astype(h.dtype) + v * skip
    y = y * x0
    return (y * jax.nn.silu(z)) @ w_out


def _axial_rope_tables(rows):
    half = MLA_ROPE_DIM // 2
    inv = ROPE_BASE ** (-jnp.arange(0, half, 2, dtype=jnp.float32) / half)
    r = jnp.repeat(jnp.arange(rows, dtype=jnp.float32), GRID_W)
    col = jnp.tile(jnp.arange(GRID_W, dtype=jnp.float32), rows)
    ar, ac = r[:, None] * inv, col[:, None] * inv
    ang = jnp.concatenate([ar, ar, ac, ac], -1)
    return jnp.cos(ang), jnp.sin(ang)


def _rotate_half(x):
    a, b = jnp.split(x, 2, axis=-1)
    return jnp.concatenate([-b, a], -1)


def _apply_axial_rope(x, cos, sin):
    half = MLA_ROPE_DIM // 2
    rot = jnp.concatenate([_rotate_half(x[..., :half]), _rotate_half(x[..., half:])], -1)
    return (x.astype(jnp.float32) * cos + rot.astype(jnp.float32) * sin).astype(x.dtype)


def _mla_project(h, w_in, q_norm, kv_norm, w_q_up):
    B, L, _ = h.shape
    u = h @ w_in
    q_c, kv_c, k_pe, z = jnp.split(
        u, [MLA_Q_RANK, MLA_Q_RANK + MLA_KV_RANK, MLA_Q_RANK + MLA_KV_RANK + MLA_ROPE_DIM], axis=-1)
    q = (_rms_norm(q_c, q_norm) @ w_q_up).reshape(B, L, MLA_HEADS, MLA_NOPE_DIM + MLA_ROPE_DIM)
    ckv = _rms_norm(kv_c, kv_norm)
    return q[..., :MLA_NOPE_DIM], q[..., MLA_NOPE_DIM:], ckv, k_pe, z


def _mla_expand(ckv, w_kv_up):
    B, L, _ = ckv.shape
    kv = (ckv @ w_kv_up).reshape(B, L, MLA_HEADS, MLA_NOPE_DIM + MLA_V_DIM)
    return kv[..., :MLA_NOPE_DIM], kv[..., MLA_NOPE_DIM:]


def _mla_attend(q_nope, q_pe, k_nope, k_pe, v):
    B, Lq, H, _ = q_nope.shape
    nb = Lq // Q_BLOCK
    scale = (MLA_NOPE_DIM + MLA_ROPE_DIM) ** -0.5

    def blocks(a):
        return a.reshape(B, nb, Q_BLOCK, *a.shape[2:]).swapaxes(0, 1)

    def one(qb):
        qn, qp = qb
        s = jnp.einsum('bqhd,bkhd->bhqk', qn, k_nope) + jnp.einsum('bqhr,bkr->bhqk', qp, k_pe)
        p = jax.nn.softmax(s.astype(jnp.float32) * scale, axis=-1).astype(v.dtype)
        return jnp.einsum('bhqk,bkhd->bqhd', p, v)

    o = lax.map(one, (blocks(q_nope), blocks(q_pe)))
    return o.swapaxes(0, 1).reshape(B, Lq, H * MLA_V_DIM)


def _mla_context(h, w_in, q_norm, kv_norm, w_q_up, w_kv_up, w_out):
    q_nope, q_pe, ckv, k_pe, z = _mla_project(h, w_in, q_norm, kv_norm, w_q_up)
    k_nope, v = _mla_expand(ckv, w_kv_up)
    o = _mla_attend(q_nope, q_pe, k_nope, k_pe, v)
    return (o * jax.nn.silu(z)) @ w_out, ckv, k_pe


def _mla_latent(h, ctx_ckv, ctx_kpe, cos, sin, w_in, q_norm, kv_norm, w_q_up, w_kv_up, w_out):
    q_nope, q_pe, ckv, k_pe, z = _mla_project(h, w_in, q_norm, kv_norm, w_q_up)
    q_pe = _apply_axial_rope(q_pe, cos[:, None, :], sin[:, None, :])
    k_pe = _apply_axial_rope(k_pe, cos, sin)
    ckv_all = jnp.concatenate([ctx_ckv.astype(ckv.dtype), ckv], 1)
    kpe_all = jnp.concatenate([ctx_kpe.astype(k_pe.dtype), k_pe], 1)
    k_nope, v = _mla_expand(ckv_all, w_kv_up)
    o = _mla_attend(q_nope, q_pe, k_nope, kpe_all, v)
    return (o * jax.nn.silu(z)) @ w_out


def _rwkv_mixer(h, s0, mu, w_in, w0, w1, w2, a0, a1, a2, k_k, k_a, r_k, gn_g, gn_b, w_out):
    B, L, D = h.shape
    H, N = RW_HEADS, RW_HEAD_DIM
    f32 = jnp.float32
    prev, nxt = _neighbours(h)
    d = 0.5 * (prev + nxt) - h
    xs = h[None] + d[None] * mu[:, None, None, :]
    xr, xw, xk, xv, xa, xg = xs[0], xs[1], xs[2], xs[3], xs[4], xs[5]
    r, k, v, z = jnp.einsum('pbld,pde->pble', jnp.stack([xr, xk, xv, xg]), w_in)
    wl = w0[:, None, None, :] + jnp.einsum(
        'nblr,nrd->nbld', jnp.tanh(jnp.einsum('bld,ndr->nblr', xw, w1)), w2)
    decay = jnp.exp(-jnp.exp(-jax.nn.softplus(-wl.astype(f32)) - 0.5))
    a = jax.nn.sigmoid((a0[:, None, None, :] + jnp.einsum(
        'nblr,nrd->nbld', jnp.einsum('bld,ndr->nblr', xa, a1), a2)).astype(f32))

    def heads(t):
        return t.reshape(*t.shape[:-1], H, N)

    kf = k.astype(f32)
    kk = heads(kf * k_k)
    kk = kk * lax.rsqrt(jnp.sum(kk * kk, -1, keepdims=True) + 1e-12)
    kd = heads(kf[None] * (1.0 + (a - 1.0) * k_a))
    bd = kk[None] * heads(a)
    rh, vh = heads(r.astype(f32)), heads(v.astype(f32))

    def orient(t):
        return jnp.stack([t[0], t[1][:, ::-1]], 2).transpose(1, 0, 2, 3, 4)

    def both(t):
        return orient(jnp.broadcast_to(t, (2,) + t.shape))

    seq = (both(rh), orient(heads(decay)), orient(kd), both(vh), both(-kk), orient(bd))

    def step(S, inp):
        r_t, w_t, k_t, v_t, a_t, b_t = inp
        sa = jnp.einsum('bdhij,bdhj->bdhi', S, a_t)
        S = S * w_t[..., None, :] + sa[..., None] * b_t[..., None, :] + v_t[..., None] * k_t[..., None, :]
        return S, jnp.einsum('bdhij,bdhj->bdhi', S, r_t)

    s_fin, y = lax.scan(step, s0.astype(f32), seq)
    y = y.transpose(1, 0, 2, 3, 4)
    y = y[:, :, 0] + y[:, ::-1, 1]
    m = jnp.mean(y, -1, keepdims=True)
    var = jnp.mean(jnp.square(y - m), -1, keepdims=True)
    yn = (y - m) * lax.rsqrt(var + RW_GN_EPS) * gn_g.reshape(H, N) + gn_b.reshape(H, N)
    bonus = jnp.sum(jnp.sum(rh[None] * kd * r_k, -1, keepdims=True), 0) * vh
    o = (yn + bonus).reshape(B, L, D).astype(h.dtype)
    return (o * jax.nn.silu(z)) @ w_out, s_fin


def setup_inputs(seed: int = 0) -> dict:
    key = jax.random.key(seed)
    ks = iter(jax.random.split(key, 64))

    def nrm(shape, scale=1.0):
        return jax.random.normal(next(ks), shape, jnp.float32) * scale

    D = D_MODEL
    hqk = MLA_NOPE_DIM + MLA_ROPE_DIM
    mla_in = MLA_Q_RANK + MLA_KV_RANK + MLA_ROPE_DIM + MLA_HEADS * MLA_V_DIM
    hy_dec = jnp.linspace(HY_DECAY_MIN, HY_DECAY_MAX, HY_WIDTH, dtype=jnp.float32)
    return {
        'x_prompt': nrm((BATCH, SEQ, D)),
        'x_sample': nrm((DEC_BATCH, DEC_SEQ, D)),
        'cache_mla_ckv': nrm((DEC_BATCH, N_MLA, PAST_LEN, MLA_KV_RANK)),
        'cache_mla_kpe': nrm((DEC_BATCH, N_MLA, PAST_LEN, MLA_ROPE_DIM)),
        'state_rwkv': nrm((DEC_BATCH, N_RWKV, 2, RW_HEADS, RW_HEAD_DIM, RW_HEAD_DIM), 0.5),
        'c': nrm((DEC_BATCH, D)),
        'c_ctx': nrm((D,)),
        'mod_w': nrm((DEPTH, D, 3 * D), 0.5 * D ** -0.5),
        'mod_b': nrm((DEPTH, 3 * D), 0.02),
        'ln_g': 1.0 + nrm((DEPTH, D), 0.02),
        'ln_b': nrm((DEPTH, D), 0.02),
        'hy_w_in': nrm((N_HYENA, D, 4 * HY_WIDTH), D ** -0.5),
        'hy_conv_w': nrm((N_HYENA, 3, 3 * HY_WIDTH), 3 ** -0.5),
        'hy_conv_b': nrm((N_HYENA, 3 * HY_WIDTH), 0.02),
        'hy_ffn_w1': nrm((N_HYENA, HY_EMB, HY_FFN), HY_EMB ** -0.5),
        'hy_ffn_b1': nrm((N_HYENA, HY_FFN), 0.02),
        'hy_ffn_w2': nrm((N_HYENA, HY_FFN, HY_FFN), HY_FFN ** -0.5),
        'hy_ffn_b2': nrm((N_HYENA, HY_FFN), 0.02),
        'hy_ffn_w3': nrm((N_HYENA, HY_FFN, 2 * HY_WIDTH), HY_FFN ** -0.5),
        'hy_ffn_b3': nrm((N_HYENA, 2 * HY_WIDTH), 0.02),
        'hy_freq': 1.0 + nrm((N_HYENA, 2, HY_FFN), 0.1),
        'hy_decay': hy_dec[None, None, :] * (1.0 + nrm((N_HYENA, 2, HY_WIDTH), 0.05)),
        'hy_skip': nrm((N_HYENA, HY_WIDTH)),
        'hy_w_out': nrm((N_HYENA, HY_WIDTH, D), HY_WIDTH ** -0.5 * DEEPNORM_BETA),
        'mla_w_in': nrm((N_MLA, D, mla_in), D ** -0.5),
        'mla_q_norm': 1.0 + nrm((N_MLA, MLA_Q_RANK), 0.02),
        'mla_kv_norm': 1.0 + nrm((N_MLA, MLA_KV_RANK), 0.02),
        'mla_w_q_up': nrm((N_MLA, MLA_Q_RANK, MLA_HEADS * hqk), MLA_Q_RANK ** -0.5),
        'mla_w_kv_up': nrm((N_MLA, MLA_KV_RANK, MLA_HEADS * (MLA_NOPE_DIM + MLA_V_DIM)), MLA_KV_RANK ** -0.5),
        'mla_w_out': nrm((N_MLA, MLA_HEADS * MLA_V_DIM, D), (MLA_HEADS * MLA_V_DIM) ** -0.5 * DEEPNORM_BETA),
        'rw_mu': jax.random.uniform(next(ks), (N_RWKV, 6, D), jnp.float32),
        'rw_w_in': nrm((N_RWKV, 4, D, D), D ** -0.5),
        'rw_w0': jnp.linspace(-6.0, 1.0, D, dtype=jnp.float32)[None, None, :] + nrm((N_RWKV, 2, D), 0.1),
        'rw_w1': nrm((N_RWKV, 2, D, RW_DECAY_LORA), D ** -0.5),
        'rw_w2': nrm((N_RWKV, 2, RW_DECAY_LORA, D), 0.1 * RW_DECAY_LORA ** -0.5),
        'rw_a0': nrm((N_RWKV, 2, D), 0.1),
        'rw_a1': nrm((N_RWKV, 2, D, RW_A_LORA), D ** -0.5),
        'rw_a2': nrm((N_RWKV, 2, RW_A_LORA, D), 0.1 * RW_A_LORA ** -0.5),
        'rw_k_k': 0.85 + nrm((N_RWKV, D), 0.05),
        'rw_k_a': 1.0 + nrm((N_RWKV, D), 0.05),
        'rw_r_k': nrm((N_RWKV, RW_HEADS, RW_HEAD_DIM), 0.1),
        'rw_gn_g': 1.0 + nrm((N_RWKV, D), 0.02),
        'rw_gn_b': nrm((N_RWKV, D), 0.02),
        'rw_w_out': nrm((N_RWKV, D, D), D ** -0.5 * DEEPNORM_BETA),
    }


def reference(x_prompt, x_sample, cache_mla_ckv, cache_mla_kpe, state_rwkv, c, c_ctx,
              mod_w, mod_b, ln_g, ln_b,
              hy_w_in, hy_conv_w, hy_conv_b, hy_ffn_w1, hy_ffn_b1, hy_ffn_w2, hy_ffn_b2,
              hy_ffn_w3, hy_ffn_b3, hy_freq, hy_decay, hy_skip, hy_w_out,
              mla_w_in, mla_q_norm, mla_kv_norm, mla_w_q_up, mla_w_kv_up, mla_w_out,
              rw_mu, rw_w_in, rw_w0, rw_w1, rw_w2, rw_a0, rw_a1, rw_a2, rw_k_k, rw_k_a,
              rw_r_k, rw_gn_g, rw_gn_b, rw_w_out):
    rows = x_sample.shape[1] // GRID_W
    cos, sin = _axial_rope_tables(rows)
    xp, xs = x_prompt, x_sample
    new_ckv, new_kpe, new_rw = [], [], []
    for i in range(DEPTH):
        kind, j = i % N_MIXERS, i // N_MIXERS
        sh_p, sc_p, g_p = _modulation(c_ctx[None, None, :], mod_w[i], mod_b[i])
        sh_s, sc_s, g_s = _modulation(c[:, None, :], mod_w[i], mod_b[i])
        hp = xp * (1.0 + sc_p) + sh_p
        hs = xs * (1.0 + sc_s) + sh_s
        if kind == 0:
            hyp = (hy_w_in[j], hy_conv_w[j], hy_conv_b[j], hy_ffn_w1[j], hy_ffn_b1[j],
                   hy_ffn_w2[j], hy_ffn_b2[j], hy_ffn_w3[j], hy_ffn_b3[j], hy_freq[j],
                   hy_decay[j], hy_skip[j], hy_w_out[j])
            op = _hyena_mixer(hp, *hyp)
            os_ = _hyena_mixer(hs, *hyp)
        elif kind == 1:
            mp = (mla_w_in[j], mla_q_norm[j], mla_kv_norm[j], mla_w_q_up[j], mla_w_kv_up[j], mla_w_out[j])
            op, ckv, kpe = _mla_context(hp, *mp)
            os_ = _mla_latent(hs, cache_mla_ckv[:, j], cache_mla_kpe[:, j], cos, sin, *mp)
            new_ckv.append(ckv)
            new_kpe.append(kpe)
        else:
            rp = (rw_mu[j], rw_w_in[j], rw_w0[j], rw_w1[j], rw_w2[j], rw_a0[j], rw_a1[j], rw_a2[j],
                  rw_k_k[j], rw_k_a[j], rw_r_k[j], rw_gn_g[j], rw_gn_b[j], rw_w_out[j])
            s_zero = jnp.zeros((xp.shape[0], 2, RW_HEADS, RW_HEAD_DIM, RW_HEAD_DIM), jnp.float32)
            op, st = _rwkv_mixer(hp, s_zero, *rp)
            os_, _ = _rwkv_mixer(hs, state_rwkv[:, j], *rp)
            new_rw.append(st)
        xp = _layer_norm(DEEPNORM_ALPHA * xp + g_p * op, ln_g[i], ln_b[i])
        xs = _layer_norm(DEEPNORM_ALPHA * xs + g_s * os_, ln_g[i], ln_b[i])
    return (xp, xs, jnp.stack(new_ckv, 1), jnp.stack(new_kpe, 1), jnp.stack(new_rw, 1))
```

```python
import functools
import math

import numpy as np
import jax
import jax.numpy as jnp
from jax import lax
from jax.experimental import pallas as pl
from jax.experimental.pallas import tpu as pltpu

F32 = jnp.float32
BF16 = jnp.bfloat16
HIGHEST = lax.Precision.HIGHEST

D = 1024
B_P, L_P = 16, 256
B_S, L_S = 2, 2048
T_P = B_P * L_P
T_S = B_S * L_S
T = T_P + T_S
PAST = 512
DEPTH = 4
DEEPNORM_ALPHA = (2.0 * DEPTH) ** 0.25
LN_EPS = 1e-5
RMS_EPS = 1e-6
HY_BANDS = 16
HY_FFN = 64
MLA_HEADS = 16
MLA_Q_RANK = 256
MLA_KV_RANK = 128
MLA_NOPE = 64
MLA_ROPE = 32
MLA_V = 64
ROPE_BASE = 10000.0
GRID_W = 64
RW_N = 64
RW_H = D // RW_N
RW_LORA = 64
RW_GN_EPS = 64e-5

TM = 256
NT_P = T_P // TM
NT_S_SEQ = L_S // TM
NT = T // TM
HALO = 8
CHUNK = 64
VMEM_LIMIT = 52 * 1024 * 1024


def _cparams(sem):
    return pltpu.CompilerParams(dimension_semantics=sem, vmem_limit_bytes=VMEM_LIMIT)


def _group(i):
    return jnp.where(i < NT_P, 0, 1 + (i - NT_P) // NT_S_SEQ)


def _sigmoid(x):
    return 1.0 / (1.0 + jnp.exp(-x))


def _silu(x):
    return x * _sigmoid(x)


def _dot(a, b):
    return jnp.dot(a, b, preferred_element_type=F32)


def _dot_nt(a, b):
    return lax.dot_general(a, b, (((1,), (1,)), ((), ())), preferred_element_type=F32)


def _dot_hi(a, b):
    return jnp.dot(a, b, preferred_element_type=F32, precision=HIGHEST)


def _split(x):
    hi = x.astype(BF16)
    lo = (x - hi.astype(F32)).astype(BF16)
    return hi, lo


def _dot3(ahi, alo, x):
    xhi, xlo = _split(x)
    return _dot(ahi, xhi) + _dot(ahi, xlo) + _dot(alo, xhi)


def _head_sum(x, ones_bd):
    hi, lo = _split(x)
    return _dot(hi, ones_bd) + _dot(lo, ones_bd)


def _modulate(x, m):
    return x * (1.0 + m[:, D:2 * D]) + m[:, :D]


def _layer_norm_rows(y, g, b):
    mu = jnp.mean(y, axis=-1, keepdims=True)
    yc = y - mu
    var = jnp.mean(yc * yc, axis=-1, keepdims=True)
    return yc * lax.rsqrt(var + LN_EPS) * g + b


def _neighbour_rows(cur, prev_halo, next_halo, has_prev, has_next):
    rows = cur.shape[0]
    ridx = lax.broadcasted_iota(jnp.int32, cur.shape, 0)
    pr = jnp.where(has_prev, prev_halo[HALO - 1:HALO, :], 0.0)
    nx = jnp.where(has_next, next_halo[0:1, :], 0.0)
    prev = jnp.where(ridx == 0, pr, pltpu.roll(cur, 1, axis=0))
    nxt = jnp.where(ridx == rows - 1, nx, pltpu.roll(cur, rows - 1, axis=0))
    return prev, nxt


def _tile_has_neighbours(i):
    k = (i - NT_P) % NT_S_SEQ
    is_s = i >= NT_P
    return jnp.logical_and(is_s, k != 0), jnp.logical_and(is_s, k != NT_S_SEQ - 1)


def _halo_specs(width, col_of):
    r = TM // HALO
    prev = pl.BlockSpec((HALO, width), lambda i, *a: (jnp.maximum(i * r - 1, 0), col_of(i, *a)))
    nxt = pl.BlockSpec((HALO, width), lambda i, *a: (jnp.minimum((i + 1) * r, T // HALO - 1), col_of(i, *a)))
    return prev, nxt


def _mod_kernel(c_ref, w_ref, b_ref, o_ref):
    o_ref[...] = _dot_hi(_silu(c_ref[...]), w_ref[...]) + b_ref[...]


def _modulation_table(cond8, mod_w, mod_b):
    tn = 1024
    return pl.pallas_call(
        _mod_kernel,
        grid=(DEPTH, 3 * D // tn),
        in_specs=[pl.BlockSpec((8, D), lambda l, j: (0, 0)),
                  pl.BlockSpec((None, D, tn), lambda l, j: (l, 0, j)),
                  pl.BlockSpec((None, 1, tn), lambda l, j: (l, 0, j))],
        out_specs=pl.BlockSpec((None, 8, tn), lambda l, j: (l, 0, j)),
        out_shape=jax.ShapeDtypeStruct((DEPTH, 8, 3 * D), F32),
        compiler_params=_cparams(("arbitrary", "arbitrary")),
        name="modulation",
    )(cond8, mod_w, mod_b.reshape(DEPTH, 1, 3 * D))


def _mod_spec(layer):
    return pl.BlockSpec((None, 1, 3 * D), lambda i, *a: (layer * 8 + _group(i), 0, 0))


def _out_ln_tail(g, w_ref, x_ref, m_ref, lng_ref, lnb_ref, o_ref):
    mix = _dot(g.astype(BF16), w_ref[...])
    gate = m_ref[...][:, 2 * D:]
    y = DEEPNORM_ALPHA * x_ref[...] + gate * mix
    o_ref[...] = _layer_norm_rows(y, lng_ref[...], lnb_ref[...])


def _out_ln2_kernel(gp_ref, gs_ref, w_ref, x_ref, m_ref, lng_ref, lnb_ref, o_ref):
    g = jnp.where(pl.program_id(0) < NT_P, gp_ref[...], gs_ref[...])
    _out_ln_tail(g, w_ref, x_ref, m_ref, lng_ref, lnb_ref, o_ref)


def _out_proj_ln2(g_p, g_s, w_out, x, mod, layer, ln_g, ln_b):
    row = pl.BlockSpec((TM, D), lambda i: (i, 0))
    vec = pl.BlockSpec((1, D), lambda i: (0, 0))
    return pl.pallas_call(
        _out_ln2_kernel,
        grid=(NT,),
        in_specs=[pl.BlockSpec((TM, D), lambda i: (jnp.minimum(i, NT_P - 1), 0)),
                  pl.BlockSpec((TM, D), lambda i: (jnp.maximum(i - NT_P, 0), 0)),
                  pl.BlockSpec((D, D), lambda i: (0, 0)),
                  row, _mod_spec(layer), vec, vec],
        out_specs=row,
        out_shape=jax.ShapeDtypeStruct((T, D), F32),
        compiler_params=_cparams(("arbitrary",)),
        name="out_proj_ln",
    )(g_p, g_s, w_out.astype(BF16), x, mod, ln_g.reshape(1, D), ln_b.reshape(1, D))


def _hy_in_kernel(x_ref, m_ref, w_ref, o_ref):
    h = _modulate(x_ref[...], m_ref[...])
    o_ref[...] = _dot(h.astype(BF16), w_ref[...])


def _hy_in_proj(x, mod, layer, w_in):
    n = w_in.shape[1]
    return pl.pallas_call(
        _hy_in_kernel,
        grid=(NT,),
        in_specs=[pl.BlockSpec((TM, D), lambda i: (i, 0)), _mod_spec(layer),
                  pl.BlockSpec((D, n), lambda i: (0, 0))],
        out_specs=pl.BlockSpec((TM, n), lambda i: (i, 0)),
        out_shape=jax.ShapeDtypeStruct((T, n), F32),
        compiler_params=_cparams(("arbitrary",)),
        name="hyena_in_proj",
    )(x, mod, w_in.astype(BF16))


def _hy_conv3_kernel(x0_ref, x0p_ref, x0n_ref, x1_ref, x1p_ref, x1n_ref, v_ref, vp_ref, vn_ref, z_ref,
                     w0_ref, w1_ref, w2_ref, b0_ref, b1_ref, b2_ref, vv_ref, gate_ref):
    has_prev, has_next = _tile_has_neighbours(pl.program_id(0))

    def conv(c_ref, p_ref, n_ref, w_ref, b_ref):
        cur = c_ref[...]
        prev, nxt = _neighbour_rows(cur, p_ref[...], n_ref[...], has_prev, has_next)
        w = w_ref[...]
        return prev * w[0:1] + cur * w[1:2] + nxt * w[2:3] + b_ref[...]

    x0 = conv(x0_ref, x0p_ref, x0n_ref, w0_ref, b0_ref)
    x1 = conv(x1_ref, x1p_ref, x1n_ref, w1_ref, b1_ref)
    v = conv(v_ref, vp_ref, vn_ref, w2_ref, b2_ref)
    vv_ref[...] = v * x1
    gate_ref[...] = x0 * _silu(z_ref[...])


def _hy_conv3(u, conv_w, conv_b):
    ct = 512
    nct = D // ct
    in_specs = []
    for grp in range(3):
        in_specs.append(pl.BlockSpec((TM, ct), lambda i, j, grp=grp: (i, grp * nct + j)))
        in_specs.extend(_halo_specs(ct, lambda i, j, grp=grp: grp * nct + j))
    in_specs.append(pl.BlockSpec((TM, ct), lambda i, j: (i, 3 * nct + j)))
    for grp in range(3):
        in_specs.append(pl.BlockSpec((3, ct), lambda i, j, grp=grp: (0, grp * nct + j)))
    for grp in range(3):
        in_specs.append(pl.BlockSpec((1, ct), lambda i, j, grp=grp: (0, grp * nct + j)))
    out = pl.BlockSpec((TM, ct), lambda i, j: (i, j))
    args = [u] * 10 + [conv_w] * 3 + [conv_b.reshape(1, 3 * D)] * 3
    return pl.pallas_call(
        _hy_conv3_kernel,
        grid=(NT, nct),
        in_specs=in_specs,
        out_specs=[out, out],
        out_shape=[jax.ShapeDtypeStruct((T, D), F32)] * 2,
        compiler_params=_cparams(("arbitrary", "arbitrary")),
        name="hyena_conv3_gate",
    )(*args)


def _hy_filter_kernel(t_ref, bands_ref, wt_ref, wc_ref, ws_ref, b1_ref, w2_ref, b2_ref, w3_ref, b3_ref,
                      f0_ref, f1_ref, dec_ref, hs_ref, hd_ref, nyq_ref, *, L, tr):
    i = pl.program_id(0)
    ridx = lax.broadcasted_iota(jnp.int32, (tr, 1), 0) + i * tr
    pos = ridx.astype(F32)
    t = t_ref[...]
    ang = ((2.0 * math.pi / L) * pos) * bands_ref[...]
    pre = t * wt_ref[...] + _dot_hi(jnp.cos(ang), wc_ref[...]) + _dot_hi(jnp.sin(ang), ws_ref[...])
    hdn = jnp.sin(f0_ref[...] * (pre + b1_ref[...]))
    hdn = jnp.sin(f1_ref[...] * (_dot_hi(hdn, w2_ref[...]) + b2_ref[...]))
    hf = _dot_hi(hdn, w3_ref[...]) + b3_ref[...]
    h = hf * jnp.exp(-t * jnp.abs(dec_ref[...]))
    h0 = h[:, :D]
    h1 = jnp.where(ridx == 0, 0.0, h[:, D:])
    hsum = h0 + h1
    hs_ref[...] = hsum
    hd_ref[...] = h1 - h0
    alt = jnp.where((ridx & 1) == 0, 1.0, -1.0)
    part =jnp.broadcast_to(jnp.sum(alt * hsum, axis=0, keepdims=True), (8, D))

    @pl.when(i == 0)
    def _():
        nyq_ref[...] = part

    @pl.when(i > 0)
    def _():
        nyq_ref[...] += part


def _hy_filter(L, w1, b1, w2, b2, w3, b3, freq, decay):
    tr = 256
    t = jnp.linspace(0.0, 1.0, L, dtype=F32).reshape(L, 1)
    bands = jnp.linspace(1e-4, HY_BANDS - 1, HY_BANDS, dtype=F32)
    bands = jnp.pad(bands, (0, 128 - HY_BANDS)).reshape(1, 128)
    wt = w1[0:1]
    wc = jnp.pad(w1[1:1 + HY_BANDS], ((0, 128 - HY_BANDS), (0, 0)))
    ws = jnp.pad(-w1[1 + HY_BANDS:], ((0, 128 - HY_BANDS), (0, 0)))
    full = lambda shape: pl.BlockSpec(shape, lambda i: (0, 0))
    rows = pl.BlockSpec((tr, D), lambda i: (i, 0))
    return pl.pallas_call(
        functools.partial(_hy_filter_kernel, L=L, tr=tr),
        grid=(L // tr,),
        in_specs=[pl.BlockSpec((tr, 1), lambda i: (i, 0)), full((1, 128)), full((1, HY_FFN)),
                  full((128, HY_FFN)), full((128, HY_FFN)), full((1, HY_FFN)),
                  full((HY_FFN, HY_FFN)), full((1, HY_FFN)), full((HY_FFN, 2 * D)), full((1, 2 * D)),
                  full((1, HY_FFN)), full((1, HY_FFN)), full((1, 2 * D))],
        out_specs=[rows, rows, pl.BlockSpec((8, D), lambda i: (0, 0))],
        out_shape=[jax.ShapeDtypeStruct((L, D), F32), jax.ShapeDtypeStruct((L, D), F32),
                   jax.ShapeDtypeStruct((8, D), F32)],
        compiler_params=_cparams(("arbitrary",)),
        name="hyena_filter",
    )(t, bands, wt, wc, ws, b1.reshape(1, -1), w2, b2.reshape(1, -1), w3, b3.reshape(1, -1),
      freq[0:1], freq[1:2], decay.reshape(1, 2 * D))


def _dft_tables(L):
    n = 2 * L
    k = jnp.arange(L, dtype=jnp.int32)
    ang = ((k[:, None] * k[None, :]) % n).astype(F32) * (2.0 * math.pi / n)
    c = jnp.cos(ang)
    s = jnp.sin(ang)
    alt = jnp.where(k % 2 == 0, 1.0, -1.0).astype(F32)
    s = jnp.where(k[:, None] == 0, alt[None, :], s)
    chi, clo = _split(c)
    shi, slo = _split(s)
    return chi, clo, shi, slo, shi.T, slo.T


def _dft_fwd_kernel(chi_ref, clo_ref, shi_ref, slo_ref, *refs):
    x1_ref, x2_ref = refs[0], refs[-3]
    oc_ref, os_ref = refs[-2:]
    oc_ref[...] = _dot3(chi_ref[...], clo_ref[...], x1_ref[...])
    os_ref[...] = _dot3(shi_ref[...], slo_ref[...], x2_ref[...])


def _dft_col_tile(L):
    return 512 if L <= 512 else 256


def _dft_fwd(tabs, xs, L, nb, row_blk0):
    chi, clo, shi, slo = tabs[:4]
    tk = min(L, 512)
    tn = _dft_col_tile(L)
    nk = L // tk
    a_spec = pl.BlockSpec((tk, L), lambda b, j, k: (k, 0))
    x_spec = pl.BlockSpec((L, tn), lambda b, j, k: (row_blk0 + b, j))
    o_spec = pl.BlockSpec((tk, tn), lambda b, j, k: (b * nk + k, j))
    return pl.pallas_call(
        _dft_fwd_kernel,
        grid=(nb, D // tn, nk),
        in_specs=[a_spec] * 4 + [x_spec] * len(xs),
        out_specs=[o_spec, o_spec],
        out_shape=[jax.ShapeDtypeStruct((nb * L, D), F32)] * 2,
        compiler_params=_cparams(("arbitrary", "arbitrary", "arbitrary")),
        name="hyena_dft_fwd",
    )(chi, clo, shi, slo, *xs)


def _dft_inv_kernel(chi_ref, clo_ref, sthi_ref, stlo_ref, vc_ref, vs_ref, kre_ref, kim_ref, nyq_ref,
                    vv_ref, skip_ref, gate_ref, o_ref, *, L, tk):
    k = pl.program_id(2)
    nk = pl.num_programs(2)
    vc, vs, kre = vc_ref[...], vs_ref[...], kre_ref[...]
    bin0 = jnp.logical_and(lax.broadcasted_iota(jnp.int32, vc.shape, 0) == 0, k == 0)
    kim = jnp.where(bin0, nyq_ref[0:1, :], kim_ref[...])
    inv_n = 1.0 / (2 * L)
    yre = jnp.where(bin0, vc * kre * inv_n, (vc * kre + vs * kim) * (2.0 * inv_n))
    yim = jnp.where(bin0, vs * kim * inv_n, (vs * kre - vc * kim) * (2.0 * inv_n))
    contrib = _dot3(chi_ref[...], clo_ref[...], yre) + _dot3(sthi_ref[...], stlo_ref[...], yim)

    @pl.when(k == 0)
    def _():
        o_ref[...] = contrib

    @pl.when(k > 0)
    def _():
        o_ref[...] += contrib

    @pl.when(k == nk - 1)
    def _():
        o_ref[...] = (o_ref[...] + vv_ref[...] * skip_ref[...]) * gate_ref[...]


def _dft_inv(tabs, vc, vs, kre, kim, nyq, vv, skip, gate, L, nb, row_blk0):
    chi, clo, _, _, sthi, stlo = tabs
    tk = min(L, 512)
    tn = _dft_col_tile(L)
    nk = L // tk
    a_spec = pl.BlockSpec((L, tk), lambda b, j, k: (0, k))
    v_spec = pl.BlockSpec((tk, tn), lambda b, j, k: (b * nk + k, j))
    k_spec = pl.BlockSpec((tk, tn), lambda b, j, k: (k, j))
    row_spec = pl.BlockSpec((L, tn), lambda b, j, k: (row_blk0 + b, j))
    return pl.pallas_call(
        functools.partial(_dft_inv_kernel, L=L, tk=tk),
        grid=(nb, D // tn, nk),
        in_specs=[a_spec] * 4 + [v_spec, v_spec, k_spec, k_spec,
                                 pl.BlockSpec((8, tn), lambda b, j, k: (0, j)),
                                 row_spec, pl.BlockSpec((1, tn), lambda b, j, k: (0, j)), row_spec],
        out_specs=pl.BlockSpec((L, tn), lambda b, j, k: (b, j)),
        out_shape=jax.ShapeDtypeStruct((nb * L, D), F32),
        compiler_params=_cparams(("arbitrary", "arbitrary", "arbitrary")),
        name="hyena_dft_inv_gate",
    )(chi, clo, sthi, stlo, vc, vs, kre, kim, nyq, vv, skip.reshape(1, D), gate)


def _hyena_layer(x, mod, layer, tabs_p, tabs_s, ln_g, ln_b, w_in, conv_w, conv_b, w1, b1, w2, b2, w3, b3,
                 freq, decay, skip, w_out):
    u = _hy_in_proj(x, mod, layer, w_in)
    vv, gate = _hy_conv3(u, conv_w, conv_b)
    gs = []
    for L, nb, blk0, tabs in ((L_P, B_P, 0, tabs_p), (L_S, B_S, T_P // L_S, tabs_s)):
        hsum, hdiff, nyq = _hy_filter(L, w1, b1, w2, b2, w3, b3, freq, decay)
        kre, kim = _dft_fwd(tabs, (hsum, hdiff), L, 1, 0)
        vc, vs = _dft_fwd(tabs, (vv,), L, nb, blk0)
        gs.append(_dft_inv(tabs, vc, vs, kre, kim, nyq, vv, skip, gate, L, nb, blk0))
    return _out_proj_ln2(gs[0], gs[1], w_out, x, mod, layer, ln_g, ln_b)


def _rope_tables():
    rows = L_S // GRID_W
    half = MLA_ROPE // 2
    inv = ROPE_BASE ** (-jnp.arange(0, half, 2, dtype=F32) / half)
    r = jnp.repeat(jnp.arange(rows, dtype=F32), GRID_W)
    col = jnp.tile(jnp.arange(GRID_W, dtype=F32), rows)
    ar, ac = r[:, None] * inv, col[:, None] * inv
    ang = jnp.concatenate([ar, ar, ac, ac], -1)
    cos, sin = jnp.cos(ang), jnp.sin(ang)
    cos = jnp.concatenate([jnp.ones((TM, MLA_ROPE), F32), cos], 0)
    sin = jnp.concatenate([jnp.zeros((TM, MLA_ROPE), F32), sin], 0)
    return cos, sin, jnp.tile(cos, (1, MLA_HEADS)), jnp.tile(sin, (1, MLA_HEADS))


def _rope_rot_cols(w):
    idx = np.concatenate([np.arange(8, 16), np.arange(0, 8), np.arange(24, 32), np.arange(16, 24)])
    sign = np.concatenate([-np.ones(8), np.ones(8), -np.ones(8), np.ones(8)]).astype(np.float32)
    return w[..., idx] * sign


def _mla_in_kernel(x_ref, m_ref, wq_ref, wkv_ref, wkp_ref, wz_ref, qn_ref, kvn_ref, wqn_ref, wqp_ref, wqr_ref,
                   c32_ref, s32_ref, c512_ref, s512_ref,
                   qno_ref, qpe_ref, ckv_ref, kpe_ref, kpr_ref, sz_ref):
    h = _modulate(x_ref[...], m_ref[...]).astype(BF16)
    q_c = _dot(h, wq_ref[...])
    kv_c = _dot(h, wkv_ref[...])
    kp2 = _dot(h, wkp_ref[...])
    z = _dot(h, wz_ref[...])

    def rms(v, g):
        return v * lax.rsqrt(jnp.mean(v * v, axis=-1, keepdims=True) + RMS_EPS) * g

    qn = rms(q_c, qn_ref[...]).astype(BF16)
    scale = (MLA_NOPE + MLA_ROPE) ** -0.5
    qno_ref[...] = (_dot(qn, wqn_ref[...]) * scale).astype(BF16)
    q_pe = _dot(qn, wqp_ref[...]) * c512_ref[...] + _dot(qn, wqr_ref[...]) * s512_ref[...]
    qpe_ref[...] = (q_pe * scale).astype(BF16)
    ckv_ref[...] = rms(kv_c, kvn_ref[...])
    kpe = kp2[:, :MLA_ROPE]
    kpe_ref[...] = kpe
    kpr_ref[...] = kpe * c32_ref[...] + kp2[:, MLA_ROPE:] * s32_ref[...]
    sz_ref[...] = _silu(z)


def _mla_in_proj(x, mod, layer, w_in, q_norm, kv_norm, w_q_up, rope):
    c32, s32, c512, s512 = rope
    o1, o2, o3 = MLA_Q_RANK, MLA_Q_RANK + MLA_KV_RANK, MLA_Q_RANK + MLA_KV_RANK + MLA_ROPE
    wq, wkv, wkp, wz = w_in[:, :o1], w_in[:, o1:o2], w_in[:, o2:o3], w_in[:, o3:]
    wkp2 = jnp.concatenate([wkp, _rope_rot_cols(wkp)], -1)
    wqu = w_q_up.reshape(MLA_Q_RANK, MLA_HEADS, MLA_NOPE + MLA_ROPE)
    wqn = wqu[:, :, :MLA_NOPE].reshape(MLA_Q_RANK, MLA_HEADS * MLA_NOPE)
    wqp = wqu[:, :, MLA_NOPE:]
    wqr = _rope_rot_cols(wqp).reshape(MLA_Q_RANK, MLA_HEADS * MLA_ROPE)
    wqp = wqp.reshape(MLA_Q_RANK, MLA_HEADS * MLA_ROPE)
    full = lambda a: pl.BlockSpec(a.shape, lambda i: (0,) * a.ndim)
    rope_idx = lambda i: jnp.where(i < NT_P, 0, 1 + (i - NT_P) % NT_S_SEQ)
    rows = lambda n: pl.BlockSpec((TM, n), lambda i: (i, 0))
    tab = lambda n: pl.BlockSpec((TM, n), lambda i: (rope_idx(i), 0))
    weights = [wq.astype(BF16), wkv.astype(BF16), wkp2.astype(BF16), wz.astype(BF16),
               q_norm.reshape(1, -1), kv_norm.reshape(1, -1),
               wqn.astype(BF16), wqp.astype(BF16), wqr.astype(BF16)]
    npe = MLA_HEADS * MLA_ROPE
    return pl.pallas_call(
        _mla_in_kernel,
        grid=(NT,),
        in_specs=[rows(D), _mod_spec(layer)] + [full(a) for a in weights]
                 + [tab(MLA_ROPE), tab(MLA_ROPE), tab(npe), tab(npe)],
        out_specs=[rows(D), rows(npe), rows(MLA_KV_RANK), rows(MLA_ROPE), rows(MLA_ROPE), rows(D)],
        out_shape=[jax.ShapeDtypeStruct((T, D), BF16), jax.ShapeDtypeStruct((T, npe), BF16),
                   jax.ShapeDtypeStruct((T, MLA_KV_RANK), F32), jax.ShapeDtypeStruct((T, MLA_ROPE), F32),
                   jax.ShapeDtypeStruct((T, MLA_ROPE), F32), jax.ShapeDtypeStruct((T, D), F32)],
        compiler_params=_cparams(("arbitrary",)),
        name="mla_in_proj",
    )(x, mod, *weights, c32, s32, c512, s512)


ATT_HG = 4


def _mla_attn_kernel(*refs, n_cache, n_new):
    if n_cache:
        (qn_ref, qp_ref, ckv_ref, kpr_ref, cckv_ref, ckpe_ref, wk_ref, wv_ref, sz_ref,
         o_ref, k_s, v_s, kp_s) = refs
    else:
        qn_ref, qp_ref, ckv_ref, kpr_ref, wk_ref, wv_ref, sz_ref, o_ref, k_s, v_s, kp_s = refs

    @pl.when(pl.program_id(2) == 0)
    def _():
        wk, wv = wk_ref[...], wv_ref[...]
        if n_cache:
            cc = cckv_ref[...].astype(BF16)
            k_s[0:n_cache, :] = _dot(cc, wk).astype(BF16)
            v_s[0:n_cache, :] = _dot(cc, wv).astype(BF16)
            kp_s[0:n_cache, :] = ckpe_ref[...].astype(BF16)
        cn = ckv_ref[...].astype(BF16)
        k_s[n_cache:n_cache + n_new, :] = _dot(cn, wk).astype(BF16)
        v_s[n_cache:n_cache + n_new, :] = _dot(cn, wv).astype(BF16)
        kp_s[n_cache:n_cache + n_new, :] = kpr_ref[...].astype(BF16)

    kp = kp_s[...]
    k_all, v_all = k_s[...], v_s[...]
    qn_all, qp_all = qn_ref[...], qp_ref[...]
    outs = []
    for hh in range(ATT_HG):
        qn = qn_all[:, hh * MLA_NOPE:(hh + 1) * MLA_NOPE]
        qp = qp_all[:, hh * MLA_ROPE:(hh + 1) * MLA_ROPE]
        s = _dot_nt(qn, k_all[:, hh * MLA_NOPE:(hh + 1) * MLA_NOPE]) + _dot_nt(qp, kp)
        p = jnp.exp(s - jnp.max(s, axis=-1, keepdims=True))
        l = jnp.sum(p, axis=-1, keepdims=True)
        o = _dot(p.astype(BF16), v_all[:, hh * MLA_V:(hh + 1) * MLA_V])
        outs.append(o / l)
    o_ref[...] = jnp.concatenate(outs, axis=-1) * sz_ref[...]


def _mla_attention(qno, qpe, ckv, kpr, sz, w_kv_up, cache_ckv, cache_kpe, *, nb, L, row_blk0):
    wkv = w_kv_up.reshape(MLA_KV_RANK, MLA_HEADS, MLA_NOPE + MLA_V)
    wk = wkv[:, :, :MLA_NOPE].reshape(MLA_KV_RANK, D).astype(BF16)
    wv = wkv[:, :, MLA_NOPE:].reshape(MLA_KV_RANK, D).astype(BF16)
    n_cache = 0 if cache_ckv is None else cache_ckv.shape[1]
    lk = n_cache + L
    nq = L // TM
    wq = ATT_HG * MLA_NOPE
    wp = ATT_HG * MLA_ROPE
    qrow = lambda w: pl.BlockSpec((TM, w), lambda b, g, q: (row_blk0 + b * nq + q, g))
    seq = lambda w: pl.BlockSpec((L, w), lambda b, g, q: (row_blk0 * TM // L + b, 0))
    wspec = pl.BlockSpec((MLA_KV_RANK, wq), lambda b, g, q: (0, g))
    in_specs = [qrow(wq), qrow(wp), seq(MLA_KV_RANK), seq(MLA_ROPE)]
    args = [qno, qpe, ckv, kpr]
    if n_cache:
        in_specs += [pl.BlockSpec((None, n_cache, MLA_KV_RANK), lambda b, g, q: (b, 0, 0)),
                     pl.BlockSpec((None, n_cache, MLA_ROPE), lambda b, g, q: (b, 0, 0))]
        args += [cache_ckv, cache_kpe]
    in_specs += [wspec, wspec, qrow(wq)]
    args += [wk, wv, sz]
    return pl.pallas_call(
        functools.partial(_mla_attn_kernel, n_cache=n_cache, n_new=L),
        grid=(nb, MLA_HEADS // ATT_HG, nq),
        in_specs=in_specs,
        out_specs=pl.BlockSpec((TM, wq), lambda b, g, q: (b * nq + q, g)),
        out_shape=jax.ShapeDtypeStruct((nb * L, D), F32),
        scratch_shapes=[pltpu.VMEM((lk, wq), BF16), pltpu.VMEM((lk, wq), BF16), pltpu.VMEM((lk, MLA_ROPE), BF16)],
        compiler_params=_cparams(("arbitrary", "arbitrary", "arbitrary")),
        name="mla_attention",
    )(*args)


def _mla_layer(x, mod, layer, rope, ln_g, ln_b, cache_ckv, cache_kpe, w_in, q_norm, kv_norm, w_q_up, w_kv_up, w_out):
    qno, qpe, ckv, kpe, kpr, sz = _mla_in_proj(x, mod, layer, w_in, q_norm, kv_norm, w_q_up, rope)
    g_p = _mla_attention(qno, qpe, ckv, kpr, sz, w_kv_up, None, None, nb=B_P, L=L_P, row_blk0=0)
    g_s = _mla_attention(qno, qpe, ckv, kpr, sz, w_kv_up, cache_ckv, cache_kpe, nb=B_S, L=L_S, row_blk0=NT_P)
    x_new = _out_proj_ln2(g_p, g_s, w_out, x, mod, layer, ln_g, ln_b)
    new_ckv = ckv[:T_P].reshape(B_P, 1, L_P, MLA_KV_RANK)
    new_kpe = kpe[:T_P].reshape(B_P, 1, L_P, MLA_ROPE)
    return x_new, new_ckv, new_kpe


def _rw_in_kernel(x_ref, xp_ref, xn_ref, m_ref, mu_ref, wr_ref, wk_ref, wv_ref, wg_ref, w1_ref, a1_ref,
                  w2_ref, a2_ref, w0_ref, a0_ref, kk_ref, ka_ref, rk_ref, ones_ref,
                  r_ref, v_ref, sz_ref, nkk_ref, bonus_ref, lw_ref, kd_ref, bd_ref):
    i = pl.program_id(0)
    has_prev, has_next = _tile_has_neighbours(i)
    m = m_ref[...]
    h = _modulate(x_ref[...], m)
    prev, nxt = _neighbour_rows(h, _modulate(xp_ref[...], m), _modulate(xn_ref[...], m), has_prev, has_next)
    d = 0.5 * (prev + nxt) - h
    mu = mu_ref[...]

    def mix(p):
        return (h + d * mu[p:p + 1]).astype(BF16)

    r = _dot(mix(0), wr_ref[...])
    tw = jnp.tanh(_dot(mix(1), w1_ref[...])).astype(BF16)
    k = _dot(mix(2), wk_ref[...])
    v = _dot(mix(3), wv_ref[...])
    ta = _dot(mix(4), a1_ref[...]).astype(BF16)
    z = _dot(mix(5), wg_ref[...])
    r_ref[...] = r
    v_ref[...] = v
    sz_ref[...] = _silu(z)
    ones_bd = ones_ref[...]
    kk = k * kk_ref[...]
    kk = kk * lax.rsqrt(_head_sum(kk * kk, ones_bd) + 1e-12)
    nkk_ref[...] = -kk
    coef = jnp.zeros_like(r)
    for n in range(2):
        wl = w0_ref[n:n + 1, :] + _dot(tw, w2_ref[n])
        lw_ref[n] = -math.exp(-0.5) * _sigmoid(wl)
        a = _sigmoid(a0_ref[n:n + 1, :] + _dot(ta, a2_ref[n]))
        kd = k * (1.0 + (a - 1.0) * ka_ref[...])
        kd_ref[n] = kd
        bd_ref[n] = kk * a
        coef = coef + r * kd * rk_ref[...]
    bonus_ref[...] = _head_sum(coef, ones_bd) * v


def _pad_lora_up(w):
    z = jnp.zeros_like(w[0])
    return jnp.stack([jnp.concatenate([w[0], z], 0), jnp.concatenate([z, w[1]], 0)])


def _head_ones():
    h = np.arange(D) // RW_N
    return jnp.asarray(h[:, None] == h[None, :], dtype=BF16)


def _rw_in_proj(x, mod, layer, ones_bd, mu, w_in, w0, w1, w2, a0, a1, a2, k_k, k_a, r_k):
    mu8 = jnp.pad(mu, ((0, 2), (0, 0)))
    w1c = jnp.concatenate([w1[0], w1[1]], -1).astype(BF16)
    a1c = jnp.concatenate([a1[0], a1[1]], -1).astype(BF16)
    w2p = _pad_lora_up(w2).astype(BF16)
    a2p = _pad_lora_up(a2).astype(BF16)
    row = pl.BlockSpec((TM, D), lambda i: (i, 0))
    row2 = pl.BlockSpec((2, TM, D), lambda i: (0, i, 0))
    prev, nxt = _halo_specs(D, lambda i: 0)
    full = lambda a: pl.BlockSpec(a.shape, lambda i: (0,) * a.ndim)
    consts = [mu8, w_in[0].astype(BF16), w_in[1].astype(BF16), w_in[2].astype(BF16), w_in[3].astype(BF16),
              w1c, a1c, w2p, a2p, w0, a0, k_k.reshape(1, D), k_a.reshape(1, D), r_k.reshape(1, D), ones_bd]
    return pl.pallas_call(
        _rw_in_kernel,
        grid=(NT,),
        in_specs=[row, prev, nxt, _mod_spec(layer)] + [full(a) for a in consts],
        out_specs=[row] * 5 + [row2] * 3,
        out_shape=[jax.ShapeDtypeStruct((T, D), F32)] * 5 + [jax.ShapeDtypeStruct((2, T, D), F32)] * 3,
        compiler_params=_cparams(("arbitrary",)),
        name="rwkv_in_proj",
    )(x, x, x, mod, *consts)


def _rw_scan_kernel(rowblk_ref, dirn_ref, first_ref, unit_ref, r_ref, v_ref, nkk_ref, lw_ref, kd_ref, bd_ref,
                    s0_ref, y_ref, s_ref):
    step = pl.program_id(0)
    bwd = dirn_ref[step] == 1
    c = CHUNK

    @pl.when(first_ref[step] == 1)
    def _():
        s_ref[...] = s0_ref[...]

    sgn = jnp.where(bwd, -1, 1)
    ri = lax.broadcasted_iota(jnp.int32, (c, c), 0)
    ci = lax.broadcasted_iota(jnp.int32, (c, c), 1)
    incl = (ri - ci) * sgn >= 0
    strict = (ri - ci) * sgn > 0
    ri2 = lax.broadcasted_iota(jnp.int32, (c, 2 * c), 0)
    ci2 = lax.broadcasted_iota(jnp.int32, (c, 2 * c), 1)
    cm2 = jnp.where(ci2 >= c, ci2 - c, ci2)
    mask_2 = (ri2 - cm2) * sgn >= 0
    mask_k = jnp.logical_and((ri2 - cm2) * sgn > 0, ci2 >= c)
    eye_f = (ri == ci).astype(F32)
    eye = eye_f.astype(BF16)
    off_masks = []
    for lg in range(6):
        same_2m = (ri >> (lg + 1)) == (ci >> (lg + 1))
        diff_m = (ri >> lg) != (ci >> lg)
        off_masks.append(jnp.logical_and(strict, jnp.logical_and(same_2m, diff_m)))

    lw = lw_ref[...]
    g = _dot_hi(incl.astype(F32), lw)
    e_in = jnp.exp(g)
    e_ex = jnp.exp(g - lw)
    e_inv = jnp.exp(-g)
    gtot = jnp.where(bwd, g[0:1, :], g[c - 1:c, :])
    e_rem = jnp.exp(gtot - g)
    e_tot = jnp.exp(gtot)
    a_t = nkk_ref[...] * e_ex
    r_t = r_ref[...] * e_in
    kd, bd = kd_ref[...], bd_ref[...]
    b_t = bd * e_inv
    k_t = kd * e_inv
    b_e = bd * e_rem
    k_e = kd * e_rem
    v = v_ref[...]

    for h in range(RW_H):
        sl = slice(h * RW_N, (h + 1) * RW_N)
        s_old = s_ref[h]
        ar = jnp.concatenate([a_t[:, sl], r_t[:, sl]], axis=0).astype(BF16)
        bk = jnp.concatenate([b_t[:, sl], k_t[:, sl]], axis=0).astype(BF16)
        bke = jnp.concatenate([b_e[:, sl], k_e[:, sl]], axis=0).astype(BF16)
        vh = v[:, sl]
        gm = _dot_nt(ar, bk)
        lab = gm[:c, :c]
        tinv = eye_f
        for off in off_masks:
            tb = tinv.astype(BF16)
            lt = _dot(jnp.where(off, lab, 0.0).astype(BF16), tb)
            tinv = tinv + _dot(tb, lt.astype(BF16))
        a_s = _dot_nt(ar, s_old.astype(BF16))
        vv2 = jnp.concatenate([vh, vh], axis=0).astype(BF16)
        w1 = a_s[:c] + _dot(jnp.where(mask_k, gm[:c], 0.0).astype(BF16), vv2)
        u = _dot(tinv.astype(BF16), w1.astype(BF16))
        uv = jnp.concatenate([u, vh], axis=0).astype(BF16)
        y = a_s[c:] + _dot(jnp.where(mask_2, gm[c:], 0.0).astype(BF16), uv)
        y_ref[:, sl] = y
        uv_t = _dot_nt(eye, uv[:c]), _dot_nt(eye, uv[c:])
        uvt = jnp.concatenate(uv_t, axis=1).astype(BF16)
        s_ref[h] = s_old * e_tot[:, sl] + _dot(uvt, bke)


def _rw_scan_tables():
    rowblk, dirn, first, unit = [], [], [], []
    u = 0
    for nb, L, blk0 in ((B_P, L_P, 0), (B_S, L_S, T_P // CHUNK)):
        nc = L // CHUNK
        for b in range(nb):
            for d in range(2):
                for j in range(nc):
                    cn = j if d == 0 else nc - 1 - j
                    rowblk.append(blk0 + b * nc + cn)
                    dirn.append(d)
                    first.append(1 if j == 0 else 0)
                    unit.append(u)
                u += 1
    as_i32 = lambda a: jnp.asarray(np.asarray(a, np.int32))
    return as_i32(rowblk), as_i32(dirn), as_i32(first), as_i32(unit), u


def _rw_scan(r, v, nkk, lw, kd, bd, s0_all):
    rowblk, dirn, first, unit, n_units = _rw_scan_tables()
    n_steps = rowblk.shape[0]
    tok = pl.BlockSpec((CHUNK, D), lambda s, rb, dr, fs, un: (rb[s], 0))
    tok2 = pl.BlockSpec((None, CHUNK, D), lambda s, rb, dr, fs, un: (dr[s], rb[s], 0))
    st = pl.BlockSpec((None, RW_H, RW_N, RW_N), lambda s, rb, dr, fs, un: (un[s], 0, 0, 0))
    return pl.pallas_call(
        _rw_scan_kernel,
        grid_spec=pltpu.PrefetchScalarGridSpec(
            num_scalar_prefetch=4,
            grid=(n_steps,),
            in_specs=[tok, tok, tok, tok2, tok2, tok2, st],
            out_specs=[tok2, st],
        ),
        out_shape=[jax.ShapeDtypeStruct((2, T, D), F32),
                   jax.ShapeDtypeStruct((n_units, RW_H, RW_N, RW_N), F32)],
        compiler_params=_cparams(("arbitrary",)),
        name="rwkv_scan",
    )(rowblk, dirn, first, unit, r, v, nkk, lw, kd, bd, s0_all)


def _rw_out_kernel(y_ref, bonus_ref, sz_ref, gg_ref, gb_ref, ones_ref, w_ref, x_ref, m_ref, lng_ref, lnb_ref, o_ref):
    ones_bd = ones_ref[...]
    y = y_ref[0] + y_ref[1]
    mean = _head_sum(y, ones_bd) * (1.0 / RW_N)
    yc = y - mean
    var = _head_sum(yc * yc, ones_bd) * (1.0 / RW_N)
    yn = yc * lax.rsqrt(var + RW_GN_EPS) * gg_ref[...] + gb_ref[...]
    g = (yn + bonus_ref[...]) * sz_ref[...]
    _out_ln_tail(g, w_ref, x_ref, m_ref, lng_ref, lnb_ref, o_ref)


def _rw_out_proj_ln(y, bonus, sz, gn_g, gn_b, ones_bd, w_out, x, mod, layer, ln_g, ln_b):
    row = pl.BlockSpec((TM, D), lambda i: (i, 0))
    vec = pl.BlockSpec((1, D), lambda i: (0, 0))
    mat = pl.BlockSpec((D, D), lambda i: (0, 0))
    return pl.pallas_call(
        _rw_out_kernel,
        grid=(NT,),
        in_specs=[pl.BlockSpec((2, TM, D), lambda i: (0, i, 0)), row, row, vec, vec, mat, mat, row,
                  _mod_spec(layer), vec, vec],
        out_specs=row,
        out_shape=jax.ShapeDtypeStruct((T, D), F32),
        compiler_params=_cparams(("arbitrary",)),
        name="rwkv_out_proj_ln",
    )(y, bonus, sz, gn_g.reshape(1, D), gn_b.reshape(1, D), ones_bd, w_out.astype(BF16), x, mod,
      ln_g.reshape(1, D), ln_b.reshape(1, D))


def _rwkv_layer(x, mod, layer, ln_g, ln_b, state, mu, w_in, w0, w1, w2, a0, a1, a2, k_k, k_a, r_k, gn_g, gn_b, w_out):
    ones_bd = _head_ones()
    r, v, sz, nkk, bonus, lw, kd, bd = _rw_in_proj(x, mod, layer, ones_bd, mu, w_in, w0, w1, w2, a0, a1, a2,
                                                  k_k, k_a, r_k)
    n_p = B_P * 2
    s0_all = jnp.concatenate([jnp.zeros((n_p, RW_H, RW_N, RW_N), F32),
                              state.astype(F32).reshape(B_S * 2, RW_H, RW_N, RW_N)], 0)
    y, s_fin = _rw_scan(r, v, nkk, lw, kd, bd, s0_all)
    x_new = _rw_out_proj_ln(y, bonus, sz, gn_g, gn_b, ones_bd, w_out, x, mod, layer, ln_g, ln_b)
    new_state = s_fin[:n_p].reshape(B_P, 1, 2, RW_H, RW_N, RW_N)
    return x_new, new_state


def kernel(x_prompt, x_sample, cache_mla_ckv, cache_mla_kpe, state_rwkv, c, c_ctx, mod_w, mod_b, ln_g, ln_b, hy_w_in, hy_conv_w, hy_conv_b, hy_ffn_w1, hy_ffn_b1, hy_ffn_w2, hy_ffn_b2, hy_ffn_w3, hy_ffn_b3, hy_freq, hy_decay, hy_skip, hy_w_out, mla_w_in, mla_q_norm, mla_kv_norm, mla_w_q_up, mla_w_kv_up, mla_w_out, rw_mu, rw_w_in, rw_w0, rw_w1, rw_w2, rw_a0, rw_a1, rw_a2, rw_k_k, rw_k_a, rw_r_k, rw_gn_g, rw_gn_b, rw_w_out):
    x = jnp.concatenate([x_prompt.reshape(T_P, D), x_sample.reshape(T_S, D)], 0)
    cond8 = jnp.concatenate([c_ctx[None, :], c, jnp.zeros((8 - 1 - B_S, D), F32)], 0)
    mod = _modulation_table(cond8, mod_w, mod_b).reshape(DEPTH * 8, 1, 3 * D)
    tabs_p = _dft_tables(L_P)
    tabs_s = _dft_tables(L_S)
    rope = _rope_tables()
    new_ckv = new_kpe = new_state = None
    for i in range(DEPTH):
        kind, j = i % 3, i // 3
        if kind == 0:
            x = _hyena_layer(x, mod, i, tabs_p, tabs_s, ln_g[i], ln_b[i], hy_w_in[j], hy_conv_w[j], hy_conv_b[j],
                             hy_ffn_w1[j], hy_ffn_b1[j], hy_ffn_w2[j], hy_ffn_b2[j], hy_ffn_w3[j], hy_ffn_b3[j],
                             hy_freq[j], hy_decay[j], hy_skip[j], hy_w_out[j])
        elif kind == 1:
            x, new_ckv, new_kpe = _mla_layer(x, mod, i, rope, ln_g[i], ln_b[i], cache_mla_ckv[:, j],
                                             cache_mla_kpe[:, j], mla_w_in[j], mla_q_norm[j], mla_kv_norm[j],
                                             mla_w_q_up[j], mla_w_kv_up[j], mla_w_out[j])
        else:
            x, new_state = _rwkv_layer(x, mod, i, ln_g[i], ln_b[i], state_rwkv[:, j], rw_mu[j], rw_w_in[j],
                                       rw_w0[j], rw_w1[j], rw_w2[j], rw_a0[j], rw_a1[j], rw_a2[j], rw_k_k[j],
                                       rw_k_a[j], rw_r_k[j], rw_gn_g[j], rw_gn_b[j], rw_w_out[j])
    return (x[:T_P].reshape(B_P, L_P, D), x[T_P:].reshape(B_S, L_S, D), new_ckv, new_kpe, new_state)
```

```python
import functools
import math

import numpy as np
import jax
import jax.numpy as jnp
from jax import lax
from jax.experimental import pallas as pl
from jax.experimental.pallas import tpu as pltpu

F32 = jnp.float32
BF16 = jnp.bfloat16
HIGHEST = lax.Precision.HIGHEST

D = 1024
B_P, L_P = 16, 256
B_S, L_S = 2, 2048
T_P = B_P * L_P
T_S = B_S * L_S
T = T_P + T_S
PAST = 512
DEPTH = 4
DEEPNORM_ALPHA = (2.0 * DEPTH) ** 0.25
LN_EPS = 1e-5
RMS_EPS = 1e-6
HY_BANDS = 16
HY_FFN = 64
MLA_HEADS = 16
MLA_Q_RANK = 256
MLA_KV_RANK = 128
MLA_NOPE = 64
MLA_ROPE = 32
MLA_V = 64
ROPE_BASE = 10000.0
GRID_W = 64
RW_N = 64
RW_H = D // RW_N
RW_LORA = 64
RW_GN_EPS = 64e-5

TM = 256
NT_P = T_P // TM
NT_S_SEQ = L_S // TM
NT = T // TM
HALO = 8
CHUNK = 64
VMEM_LIMIT = 52 * 1024 * 1024


def _cparams(sem):
    return pltpu.CompilerParams(dimension_semantics=sem, vmem_limit_bytes=VMEM_LIMIT)


def _group(i):
    return jnp.where(i < NT_P, 0, 1 + (i - NT_P) // NT_S_SEQ)


def _sigmoid(x):
    return 1.0 / (1.0 + jnp.exp(-x))


def _silu(x):
    return x * _sigmoid(x)


def _dot(a, b):
    return jnp.dot(a, b, preferred_element_type=F32)


def _dot_nt(a, b):
    return lax.dot_general(a, b, (((1,), (1,)), ((), ())), preferred_element_type=F32)


def _dot_hi(a, b):
    return jnp.dot(a, b, preferred_element_type=F32, precision=HIGHEST)


def _split(x):
    hi = x.astype(BF16)
    lo = (x - hi.astype(F32)).astype(BF16)
    return hi, lo


def _dot3(ahi, alo, x):
    xhi, xlo = _split(x)
    return _dot(ahi, xhi) + _dot(ahi, xlo) + _dot(alo, xhi)


def _head_sum(x, ones_bd):
    hi, lo = _split(x)
    return _dot(hi, ones_bd) + _dot(lo, ones_bd)


def _modulate(x, m):
    return x * (1.0 + m[:, D:2 * D]) + m[:, :D]


def _layer_norm_rows(y, g, b):
    mu = jnp.mean(y, axis=-1, keepdims=True)
    yc = y - mu
    var = jnp.mean(yc * yc, axis=-1, keepdims=True)
    return yc * lax.rsqrt(var + LN_EPS) * g + b


def _neighbour_rows(cur, prev_halo, next_halo, has_prev, has_next):
    rows = cur.shape[0]
    ridx = lax.broadcasted_iota(jnp.int32, cur.shape, 0)
    pr = jnp.where(has_prev, prev_halo[HALO - 1:HALO, :], 0.0)
    nx = jnp.where(has_next, next_halo[0:1, :], 0.0)
    prev = jnp.where(ridx == 0, pr, pltpu.roll(cur, 1, axis=0))
    nxt = jnp.where(ridx == rows - 1, nx, pltpu.roll(cur, rows - 1, axis=0))
    return prev, nxt


def _tile_has_neighbours(i):
    k = (i - NT_P) % NT_S_SEQ
    is_s = i >= NT_P
    return jnp.logical_and(is_s, k != 0), jnp.logical_and(is_s, k != NT_S_SEQ - 1)


def _halo_specs(width, col_of):
    r = TM // HALO
    prev = pl.BlockSpec((HALO, width), lambda i, *a: (jnp.maximum(i * r - 1, 0), col_of(i, *a)))
    nxt = pl.BlockSpec((HALO, width), lambda i, *a: (jnp.minimum((i + 1) * r, T // HALO - 1), col_of(i, *a)))
    return prev, nxt


def _mod_kernel(c_ref, w_ref, b_ref, o_ref):
    o_ref[...] = _dot_hi(_silu(c_ref[...]), w_ref[...]) + b_ref[...]


def _modulation_table(cond8, mod_w, mod_b):
    tn = 1024
    return pl.pallas_call(
        _mod_kernel,
        grid=(DEPTH, 3 * D // tn),
        in_specs=[pl.BlockSpec((8, D), lambda l, j: (0, 0)),
                  pl.BlockSpec((None, D, tn), lambda l, j: (l, 0, j)),
                  pl.BlockSpec((None, 1, tn), lambda l, j: (l, 0, j))],
        out_specs=pl.BlockSpec((None, 8, tn), lambda l, j: (l, 0, j)),
        out_shape=jax.ShapeDtypeStruct((DEPTH, 8, 3 * D), F32),
        compiler_params=_cparams(("arbitrary", "arbitrary")),
        name="modulation",
    )(cond8, mod_w, mod_b.reshape(DEPTH, 1, 3 * D))


def _mod_spec(layer):
    return pl.BlockSpec((None, 1, 3 * D), lambda i, *a: (layer * 8 + _group(i), 0, 0))


def _out_ln_tail(g, w_ref, x_ref, m_ref, lng_ref, lnb_ref, o_ref):
    mix = _dot(g.astype(BF16), w_ref[...])
    gate = m_ref[...][:, 2 * D:]
    y = DEEPNORM_ALPHA * x_ref[...] + gate * mix
    o_ref[...] = _layer_norm_rows(y, lng_ref[...], lnb_ref[...])


def _out_ln2_kernel(gp_ref, gs_ref, w_ref, x_ref, m_ref, lng_ref, lnb_ref, o_ref):
    g = jnp.where(pl.program_id(0) < NT_P, gp_ref[...], gs_ref[...])
    _out_ln_tail(g, w_ref, x_ref, m_ref, lng_ref, lnb_ref, o_ref)


def _out_proj_ln2(g_p, g_s, w_out, x, mod, layer, ln_g, ln_b):
    row = pl.BlockSpec((TM, D), lambda i: (i, 0))
    vec = pl.BlockSpec((1, D), lambda i: (0, 0))
    return pl.pallas_call(
        _out_ln2_kernel,
        grid=(NT,),
        in_specs=[pl.BlockSpec((TM, D), lambda i: (jnp.minimum(i, NT_P - 1), 0)),
                  pl.BlockSpec((TM, D), lambda i: (jnp.maximum(i - NT_P, 0), 0)),
                  pl.BlockSpec((D, D), lambda i: (0, 0)),
                  row, _mod_spec(layer), vec, vec],
        out_specs=row,
        out_shape=jax.ShapeDtypeStruct((T, D), F32),
        compiler_params=_cparams(("arbitrary",)),
        name="out_proj_ln",
    )(g_p, g_s, w_out.astype(BF16), x, mod, ln_g.reshape(1, D), ln_b.reshape(1, D))


def _hy_in_kernel(x_ref, m_ref, w_ref, o_ref):
    h = _modulate(x_ref[...], m_ref[...])
    o_ref[...] = _dot(h.astype(BF16), w_ref[...])


def _hy_in_proj(x, mod, layer, w_in):
    n = w_in.shape[1]
    return pl.pallas_call(
        _hy_in_kernel,
        grid=(NT,),
        in_specs=[pl.BlockSpec((TM, D), lambda i: (i, 0)), _mod_spec(layer),
                  pl.BlockSpec((D, n), lambda i: (0, 0))],
        out_specs=pl.BlockSpec((TM, n), lambda i: (i, 0)),
        out_shape=jax.ShapeDtypeStruct((T, n), F32),
        compiler_params=_cparams(("arbitrary",)),
        name="hyena_in_proj",
    )(x, mod, w_in.astype(BF16))


def _hy_conv3_kernel(x0_ref, x0p_ref, x0n_ref, x1_ref, x1p_ref, x1n_ref, v_ref, vp_ref, vn_ref, z_ref,
                     w0_ref, w1_ref, w2_ref, b0_ref, b1_ref, b2_ref, vv_ref, gate_ref):
    has_prev, has_next = _tile_has_neighbours(pl.program_id(0))

    def conv(c_ref, p_ref, n_ref, w_ref, b_ref):
        cur = c_ref[...]
        prev, nxt = _neighbour_rows(cur, p_ref[...], n_ref[...], has_prev, has_next)
        w = w_ref[...]
        return prev * w[0:1] + cur * w[1:2] + nxt * w[2:3] + b_ref[...]

    x0 = conv(x0_ref, x0p_ref, x0n_ref, w0_ref, b0_ref)
    x1 = conv(x1_ref, x1p_ref, x1n_ref, w1_ref, b1_ref)
    v = conv(v_ref, vp_ref, vn_ref, w2_ref, b2_ref)
    vv_ref[...] = v * x1
    gate_ref[...] = x0 * _silu(z_ref[...])


def _hy_conv3(u, conv_w, conv_b):
    ct = 512
    nct = D // ct
    in_specs = []
    for grp in range(3):
        in_specs.append(pl.BlockSpec((TM, ct), lambda i, j, grp=grp: (i, grp * nct + j)))
        in_specs.extend(_halo_specs(ct, lambda i, j, grp=grp: grp * nct + j))
    in_specs.append(pl.BlockSpec((TM, ct), lambda i, j: (i, 3 * nct + j)))
    for grp in range(3):
        in_specs.append(pl.BlockSpec((3, ct), lambda i, j, grp=grp: (0, grp * nct + j)))
    for grp in range(3):
        in_specs.append(pl.BlockSpec((1, ct), lambda i, j, grp=grp: (0, grp * nct + j)))
    out = pl.BlockSpec((TM, ct), lambda i, j: (i, j))
    args = [u] * 10 + [conv_w] * 3 + [conv_b.reshape(1, 3 * D)] * 3
    return pl.pallas_call(
        _hy_conv3_kernel,
        grid=(NT, nct),
        in_specs=in_specs,
        out_specs=[out, out],
        out_shape=[jax.ShapeDtypeStruct((T, D), F32)] * 2,
        compiler_params=_cparams(("arbitrary", "arbitrary")),
        name="hyena_conv3_gate",
    )(*args)


def _hy_filter_kernel(t_ref, bands_ref, wt_ref, wc_ref, ws_ref, b1_ref, w2_ref, b2_ref, w3_ref, b3_ref,
                      f0_ref, f1_ref, dec_ref, hs_ref, hd_ref, nyq_ref, *, L, tr):
    i = pl.program_id(0)
    ridx = lax.broadcasted_iota(jnp.int32, (tr, 1), 0) + i * tr
    pos = ridx.astype(F32)
    t = t_ref[...]
    ang = ((2.0 * math.pi / L) * pos) * bands_ref[...]
    pre = t * wt_ref[...] + _dot_hi(jnp.cos(ang), wc_ref[...]) + _dot_hi(jnp.sin(ang), ws_ref[...])
    hdn = jnp.sin(f0_ref[...] * (pre + b1_ref[...]))
    hdn = jnp.sin(f1_ref[...] * (_dot_hi(hdn, w2_ref[...]) + b2_ref[...]))
    hf = _dot_hi(hdn, w3_ref[...]) + b3_ref[...]
    h = hf * jnp.exp(-t * jnp.abs(dec_ref[...]))
    h0 = h[:, :D]
    h1 = jnp.where(ridx == 0, 0.0, h[:, D:])
    hsum = h0 + h1
    hs_ref[...] = hsum
    hd_ref[...] = h1 - h0
    alt = jnp.where((ridx & 1) == 0, 1.0, -1.0)
    part =jnp.broadcast_to(jnp.sum(alt * hsum, axis=0, keepdims=True), (8, D))

    @pl.when(i == 0)
    def _():
        nyq_ref[...] = part

    @pl.when(i > 0)
    def _():
        nyq_ref[...] += part


def _hy_filter(L, w1, b1, w2, b2, w3, b3, freq, decay):
    tr = 256
    t = jnp.linspace(0.0, 1.0, L, dtype=F32).reshape(L, 1)
    bands = jnp.linspace(1e-4, HY_BANDS - 1, HY_BANDS, dtype=F32)
    bands = jnp.pad(bands, (0, 128 - HY_BANDS)).reshape(1, 128)
    wt = w1[0:1]
    wc = jnp.pad(w1[1:1 + HY_BANDS], ((0, 128 - HY_BANDS), (0, 0)))
    ws = jnp.pad(-w1[1 + HY_BANDS:], ((0, 128 - HY_BANDS), (0, 0)))
    full = lambda shape: pl.BlockSpec(shape, lambda i: (0, 0))
    rows = pl.BlockSpec((tr, D), lambda i: (i, 0))
    return pl.pallas_call(
        functools.partial(_hy_filter_kernel, L=L, tr=tr),
        grid=(L // tr,),
        in_specs=[pl.BlockSpec((tr, 1), lambda i: (i, 0)), full((1, 128)), full((1, HY_FFN)),
                  full((128, HY_FFN)), full((128, HY_FFN)), full((1, HY_FFN)),
                  full((HY_FFN, HY_FFN)), full((1, HY_FFN)), full((HY_FFN, 2 * D)), full((1, 2 * D)),
                  full((1, HY_FFN)), full((1, HY_FFN)), full((1, 2 * D))],
        out_specs=[rows, rows, pl.BlockSpec((8, D), lambda i: (0, 0))],
        out_shape=[jax.ShapeDtypeStruct((L, D), F32), jax.ShapeDtypeStruct((L, D), F32),
                   jax.ShapeDtypeStruct((8, D), F32)],
        compiler_params=_cparams(("arbitrary",)),
        name="hyena_filter",
    )(t, bands, wt, wc, ws, b1.reshape(1, -1), w2, b2.reshape(1, -1), w3, b3.reshape(1, -1),
      freq[0:1], freq[1:2], decay.reshape(1, 2 * D))


def _dft_tables(L):
    n = 2 * L
    k = jnp.arange(L, dtype=jnp.int32)
    ang = ((k[:, None] * k[None, :]) % n).astype(F32) * (2.0 * math.pi / n)
    c = jnp.cos(ang)
    s = jnp.sin(ang)
    alt = jnp.where(k % 2 == 0, 1.0, -1.0).astype(F32)
    s = jnp.where(k[:, None] == 0, alt[None, :], s)
    chi, clo = _split(c)
    shi, slo = _split(s)
    return chi, clo, shi, slo, shi.T, slo.T


def _dft_fwd_kernel(chi_ref, clo_ref, shi_ref, slo_ref, *refs):
    x1_ref, x2_ref = refs[0], refs[-3]
    oc_ref, os_ref = refs[-2:]
    oc_ref[...] = _dot3(chi_ref[...], clo_ref[...], x1_ref[...])
    os_ref[...] = _dot3(shi_ref[...], slo_ref[...], x2_ref[...])


def _dft_col_tile(L):
    return 512 if L <= 512 else 256


def _dft_fwd(tabs, xs, L, nb, row_blk0):
    chi, clo, shi, slo = tabs[:4]
    tk = min(L, 512)
    tn = _dft_col_tile(L)
    nk = L // tk
    a_spec = pl.BlockSpec((tk, L), lambda b, j, k: (k, 0))
    x_spec = pl.BlockSpec((L, tn), lambda b, j, k: (row_blk0 + b, j))
    o_spec = pl.BlockSpec((tk, tn), lambda b, j, k: (b * nk + k, j))
    return pl.pallas_call(
        _dft_fwd_kernel,
        grid=(nb, D // tn, nk),
        in_specs=[a_spec] * 4 + [x_spec] * len(xs),
        out_specs=[o_spec, o_spec],
        out_shape=[jax.ShapeDtypeStruct((nb * L, D), F32)] * 2,
        compiler_params=_cparams(("arbitrary", "arbitrary", "arbitrary")),
        name="hyena_dft_fwd",
    )(chi, clo, shi, slo, *xs)


def _dft_inv_kernel(chi_ref, clo_ref, sthi_ref, stlo_ref, vc_ref, vs_ref, kre_ref, kim_ref, nyq_ref,
                    vv_ref, skip_ref, gate_ref, o_ref, *, L, tk):
    k = pl.program_id(2)
    nk = pl.num_programs(2)
    vc, vs, kre = vc_ref[...], vs_ref[...], kre_ref[...]
    bin0 = jnp.logical_and(lax.broadcasted_iota(jnp.int32, vc.shape, 0) == 0, k == 0)
    kim = jnp.where(bin0, nyq_ref[0:1, :], kim_ref[...])
    inv_n = 1.0 / (2 * L)
    yre = jnp.where(bin0, vc * kre * inv_n, (vc * kre + vs * kim) * (2.0 * inv_n))
    yim = jnp.where(bin0, vs * kim * inv_n, (vs * kre - vc * kim) * (2.0 * inv_n))
    contrib = _dot3(chi_ref[...], clo_ref[...], yre) + _dot3(sthi_ref[...], stlo_ref[...], yim)

    @pl.when(k == 0)
    def _():
        o_ref[...] = contrib

    @pl.when(k > 0)
    def _():
        o_ref[...] += contrib

    @pl.when(k == nk - 1)
    def _():
        o_ref[...] = (o_ref[...] + vv_ref[...] * skip_ref[...]) * gate_ref[...]


def _dft_inv(tabs, vc, vs, kre, kim, nyq, vv, skip, gate, L, nb, row_blk0):
    chi, clo, _, _, sthi, stlo = tabs
    tk = min(L, 512)
    tn = _dft_col_tile(L)
    nk = L // tk
    a_spec = pl.BlockSpec((L, tk), lambda b, j, k: (0, k))
    v_spec = pl.BlockSpec((tk, tn), lambda b, j, k: (b * nk + k, j))
    k_spec = pl.BlockSpec((tk, tn), lambda b, j, k: (k, j))
    row_spec = pl.BlockSpec((L, tn), lambda b, j, k: (row_blk0 + b, j))
    return pl.pallas_call(
        functools.partial(_dft_inv_kernel, L=L, tk=tk),
        grid=(nb, D // tn, nk),
        in_specs=[a_spec] * 4 + [v_spec, v_spec, k_spec, k_spec,
                                 pl.BlockSpec((8, tn), lambda b, j, k: (0, j)),
                                 row_spec, pl.BlockSpec((1, tn), lambda b, j, k: (0, j)), row_spec],
        out_specs=pl.BlockSpec((L, tn), lambda b, j, k: (b, j)),
        out_shape=jax.ShapeDtypeStruct((nb * L, D), F32),
        compiler_params=_cparams(("arbitrary", "arbitrary", "arbitrary")),
        name="hyena_dft_inv_gate",
    )(chi, clo, sthi, stlo, vc, vs, kre, kim, nyq, vv, skip.reshape(1, D), gate)


def _hyena_layer(x, mod, layer, tabs_p, tabs_s, ln_g, ln_b, w_in, conv_w, conv_b, w1, b1, w2, b2, w3, b3,
                 freq, decay, skip, w_out):
    u = _hy_in_proj(x, mod, layer, w_in)
    vv, gate = _hy_conv3(u, conv_w, conv_b)
    gs = []
    for L, nb, blk0, tabs in ((L_P, B_P, 0, tabs_p), (L_S, B_S, T_P // L_S, tabs_s)):
        hsum, hdiff, nyq = _hy_filter(L, w1, b1, w2, b2, w3, b3, freq, decay)
        kre, kim = _dft_fwd(tabs, (hsum, hdiff), L, 1, 0)
        vc, vs = _dft_fwd(tabs, (vv,), L, nb, blk0)
        gs.append(_dft_inv(tabs, vc, vs, kre, kim, nyq, vv, skip, gate, L, nb, blk0))
    return _out_proj_ln2(gs[0], gs[1], w_out, x, mod, layer, ln_g, ln_b)


def _rope_tables():
    rows = L_S // GRID_W
    half = MLA_ROPE // 2
    inv = ROPE_BASE ** (-jnp.arange(0, half, 2, dtype=F32) / half)
    r = jnp.repeat(jnp.arange(rows, dtype=F32), GRID_W)
    col = jnp.tile(jnp.arange(GRID_W, dtype=F32), rows)
    ar, ac = r[:, None] * inv, col[:, None] * inv
    ang = jnp.concatenate([ar, ar, ac, ac], -1)
    cos, sin = jnp.cos(ang), jnp.sin(ang)
    cos = jnp.concatenate([jnp.ones((TM, MLA_ROPE), F32), cos], 0)
    sin = jnp.concatenate([jnp.zeros((TM, MLA_ROPE), F32), sin], 0)
    return cos, sin, jnp.tile(cos, (1, MLA_HEADS)), jnp.tile(sin, (1, MLA_HEADS))


def _rope_rot_cols(w):
    idx = np.concatenate([np.arange(8, 16), np.arange(0, 8), np.arange(24, 32), np.arange(16, 24)])
    sign = np.concatenate([-np.ones(8), np.ones(8), -np.ones(8), np.ones(8)]).astype(np.float32)
    return w[..., idx] * sign


def _mla_in_kernel(x_ref, m_ref, wq_ref, wkv_ref, wkp_ref, wz_ref, qn_ref, kvn_ref, wqn_ref, wqp_ref, wqr_ref,
                   c32_ref, s32_ref, c512_ref, s512_ref,
                   qno_ref, qpe_ref, ckv_ref, kpe_ref, kpr_ref, sz_ref):
    h = _modulate(x_ref[...], m_ref[...]).astype(BF16)
    q_c = _dot(h, wq_ref[...])
    kv_c = _dot(h, wkv_ref[...])
    kp2 = _dot(h, wkp_ref[...])
    z = _dot(h, wz_ref[...])

    def rms(v, g):
        return v * lax.rsqrt(jnp.mean(v * v, axis=-1, keepdims=True) + RMS_EPS) * g

    qn = rms(q_c, qn_ref[...]).astype(BF16)
    scale = (MLA_NOPE + MLA_ROPE) ** -0.5
    qno_ref[...] = (_dot(qn, wqn_ref[...]) * scale).astype(BF16)
    q_pe = _dot(qn, wqp_ref[...]) * c512_ref[...] + _dot(qn, wqr_ref[...]) * s512_ref[...]
    qpe_ref[...] = (q_pe * scale).astype(BF16)
    ckv_ref[...] = rms(kv_c, kvn_ref[...])
    kpe = kp2[:, :MLA_ROPE]
    kpe_ref[...] = kpe
    kpr_ref[...] = kpe * c32_ref[...] + kp2[:, MLA_ROPE:] * s32_ref[...]
    sz_ref[...] = _silu(z)


def _mla_in_proj(x, mod, layer, w_in, q_norm, kv_norm, w_q_up, rope):
    c32, s32, c512, s512 = rope
    o1, o2, o3 = MLA_Q_RANK, MLA_Q_RANK + MLA_KV_RANK, MLA_Q_RANK + MLA_KV_RANK + MLA_ROPE
    wq, wkv, wkp, wz = w_in[:, :o1], w_in[:, o1:o2], w_in[:, o2:o3], w_in[:, o3:]
    wkp2 = jnp.concatenate([wkp, _rope_rot_cols(wkp)], -1)
    wqu = w_q_up.reshape(MLA_Q_RANK, MLA_HEADS, MLA_NOPE + MLA_ROPE)
    wqn = wqu[:, :, :MLA_NOPE].reshape(MLA_Q_RANK, MLA_HEADS * MLA_NOPE)
    wqp = wqu[:, :, MLA_NOPE:]
    wqr = _rope_rot_cols(wqp).reshape(MLA_Q_RANK, MLA_HEADS * MLA_ROPE)
    wqp = wqp.reshape(MLA_Q_RANK, MLA_HEADS * MLA_ROPE)
    full = lambda a: pl.BlockSpec(a.shape, lambda i: (0,) * a.ndim)
    rope_idx = lambda i: jnp.where(i < NT_P, 0, 1 + (i - NT_P) % NT_S_SEQ)
    rows = lambda n: pl.BlockSpec((TM, n), lambda i: (i, 0))
    tab = lambda n: pl.BlockSpec((TM, n), lambda i: (rope_idx(i), 0))
    weights = [wq.astype(BF16), wkv.astype(BF16), wkp2.astype(BF16), wz.astype(BF16),
               q_norm.reshape(1, -1), kv_norm.reshape(1, -1),
               wqn.astype(BF16), wqp.astype(BF16), wqr.astype(BF16)]
    npe = MLA_HEADS * MLA_ROPE
    return pl.pallas_call(
        _mla_in_kernel,
        grid=(NT,),
        in_specs=[rows(D), _mod_spec(layer)] + [full(a) for a in weights]
                 + [tab(MLA_ROPE), tab(MLA_ROPE), tab(npe), tab(npe)],
        out_specs=[rows(D), rows(npe), rows(MLA_KV_RANK), rows(MLA_ROPE), rows(MLA_ROPE), rows(D)],
        out_shape=[jax.ShapeDtypeStruct((T, D), BF16), jax.ShapeDtypeStruct((T, npe), BF16),
                   jax.ShapeDtypeStruct((T, MLA_KV_RANK), F32), jax.ShapeDtypeStruct((T, MLA_ROPE), F32),
                   jax.ShapeDtypeStruct((T, MLA_ROPE), F32), jax.ShapeDtypeStruct((T, D), F32)],
        compiler_params=_cparams(("arbitrary",)),
        name="mla_in_proj",
    )(x, mod, *weights, c32, s32, c512, s512)


ATT_HG = 4


def _mla_attn_kernel(*refs, n_cache, n_new):
    if n_cache:
        (qn_ref, qp_ref, ckv_ref, kpr_ref, cckv_ref, ckpe_ref, wk_ref, wv_ref, sz_ref,
         o_ref, k_s, v_s, kp_s) = refs
    else:
        qn_ref, qp_ref, ckv_ref, kpr_ref, wk_ref, wv_ref, sz_ref, o_ref, k_s, v_s, kp_s = refs

    @pl.when(pl.program_id(2) == 0)
    def _():
        wk, wv = wk_ref[...], wv_ref[...]
        if n_cache:
            cc = cckv_ref[...].astype(BF16)
            k_s[0:n_cache, :] = _dot(cc, wk).astype(BF16)
            v_s[0:n_cache, :] = _dot(cc, wv).astype(BF16)
            kp_s[0:n_cache, :] = ckpe_ref[...].astype(BF16)
        cn = ckv_ref[...].astype(BF16)
        k_s[n_cache:n_cache + n_new, :] = _dot(cn, wk).astype(BF16)
        v_s[n_cache:n_cache + n_new, :] = _dot(cn, wv).astype(BF16)
        kp_s[n_cache:n_cache + n_new, :] = kpr_ref[...].astype(BF16)

    kp = kp_s[...]
    k_all, v_all = k_s[...], v_s[...]
    qn_all, qp_all = qn_ref[...], qp_ref[...]
    outs = []
    for hh in range(ATT_HG):
        qn = qn_all[:, hh * MLA_NOPE:(hh + 1) * MLA_NOPE]
        qp = qp_all[:, hh * MLA_ROPE:(hh + 1) * MLA_ROPE]
        s = _dot_nt(qn, k_all[:, hh * MLA_NOPE:(hh + 1) * MLA_NOPE]) + _dot_nt(qp, kp)
        p = jnp.exp(s - jnp.max(s, axis=-1, keepdims=True))
        l = jnp.sum(p, axis=-1, keepdims=True)
        o = _dot(p.astype(BF16), v_all[:, hh * MLA_V:(hh + 1) * MLA_V])
        outs.append(o / l)
    o_ref[...] = jnp.concatenate(outs, axis=-1) * sz_ref[...]


def _mla_attention(qno, qpe, ckv, kpr, sz, w_kv_up, cache_ckv, cache_kpe, *, nb, L, row_blk0):
    wkv = w_kv_up.reshape(MLA_KV_RANK, MLA_HEADS, MLA_NOPE + MLA_V)
    wk = wkv[:, :, :MLA_NOPE].reshape(MLA_KV_RANK, D).astype(BF16)
    wv = wkv[:, :, MLA_NOPE:].reshape(MLA_KV_RANK, D).astype(BF16)
    n_cache = 0 if cache_ckv is None else cache_ckv.shape[1]
    lk = n_cache + L
    nq = L // TM
    wq = ATT_HG * MLA_NOPE
    wp = ATT_HG * MLA_ROPE
    qrow = lambda w: pl.BlockSpec((TM, w), lambda b, g, q: (row_blk0 + b * nq + q, g))
    seq = lambda w: pl.BlockSpec((L, w), lambda b, g, q: (row_blk0 * TM // L + b, 0))
    wspec = pl.BlockSpec((MLA_KV_RANK, wq), lambda b, g, q: (0, g))
    in_specs = [qrow(wq), qrow(wp), seq(MLA_KV_RANK), seq(MLA_ROPE)]
    args = [qno, qpe, ckv, kpr]
    if n_cache:
        in_specs += [pl.BlockSpec((None, n_cache, MLA_KV_RANK), lambda b, g, q: (b, 0, 0)),
                     pl.BlockSpec((None, n_cache, MLA_ROPE), lambda b, g, q: (b, 0, 0))]
        args += [cache_ckv, cache_kpe]
    in_specs += [wspec, wspec, qrow(wq)]
    args += [wk, wv, sz]
    return pl.pallas_call(
        functools.partial(_mla_attn_kernel, n_cache=n_cache, n_new=L),
        grid=(nb, MLA_HEADS // ATT_HG, nq),
        in_specs=in_specs,
        out_specs=pl.BlockSpec((TM, wq), lambda b, g, q: (b * nq + q, g)),
        out_shape=jax.ShapeDtypeStruct((nb * L, D), F32),
        scratch_shapes=[pltpu.VMEM((lk, wq), BF16), pltpu.VMEM((lk, wq), BF16), pltpu.VMEM((lk, MLA_ROPE), BF16)],
        compiler_params=_cparams(("arbitrary", "arbitrary", "arbitrary")),
        name="mla_attention",
    )(*args)


def _mla_layer(x, mod, layer, rope, ln_g, ln_b, cache_ckv, cache_kpe, w_in, q_norm, kv_norm, w_q_up, w_kv_up, w_out):
    qno, qpe, ckv, kpe, kpr, sz = _mla_in_proj(x, mod, layer, w_in, q_norm, kv_norm, w_q_up, rope)
    g_p = _mla_attention(qno, qpe, ckv, kpr, sz, w_kv_up, None, None, nb=B_P, L=L_P, row_blk0=0)
    g_s = _mla_attention(qno, qpe, ckv, kpr, sz, w_kv_up, cache_ckv, cache_kpe, nb=B_S, L=L_S, row_blk0=NT_P)
    x_new = _out_proj_ln2(g_p, g_s, w_out, x, mod, layer, ln_g, ln_b)
    new_ckv = ckv[:T_P].reshape(B_P, 1, L_P, MLA_KV_RANK)
    new_kpe = kpe[:T_P].reshape(B_P, 1, L_P, MLA_ROPE)
    return x_new, new_ckv, new_kpe


def _rw_in_kernel(x_ref, xp_ref, xn_ref, m_ref, mu_ref, wr_ref, wk_ref, wv_ref, wg_ref, w1_ref, a1_ref,
                  w2_ref, a2_ref, w0_ref, a0_ref, kk_ref, ka_ref, rk_ref, ones_ref,
                  r_ref, v_ref, sz_ref, nkk_ref, bonus_ref, lw_ref, kd_ref, bd_ref):
    i = pl.program_id(0)
    has_prev, has_next = _tile_has_neighbours(i)
    m = m_ref[...]
    h = _modulate(x_ref[...], m)
    prev, nxt = _neighbour_rows(h, _modulate(xp_ref[...], m), _modulate(xn_ref[...], m), has_prev, has_next)
    d = 0.5 * (prev + nxt) - h
    mu = mu_ref[...]

    def mix(p):
        return (h + d * mu[p:p + 1]).astype(BF16)

    r = _dot(mix(0), wr_ref[...])
    tw = jnp.tanh(_dot(mix(1), w1_ref[...])).astype(BF16)
    k = _dot(mix(2), wk_ref[...])
    v = _dot(mix(3), wv_ref[...])
    ta = _dot(mix(4), a1_ref[...]).astype(BF16)
    z = _dot(mix(5), wg_ref[...])
    r_ref[...] = r
    v_ref[...] = v
    sz_ref[...] = _silu(z)
    ones_bd = ones_ref[...]
    kk = k * kk_ref[...]
    kk = kk * lax.rsqrt(_head_sum(kk * kk, ones_bd) + 1e-12)
    nkk_ref[...] = -kk
    coef = jnp.zeros_like(r)
    for n in range(2):
        wl = w0_ref[n:n + 1, :] + _dot(tw, w2_ref[n])
        lw_ref[n] = -math.exp(-0.5) * _sigmoid(wl)
        a = _sigmoid(a0_ref[n:n + 1, :] + _dot(ta, a2_ref[n]))
        kd = k * (1.0 + (a - 1.0) * ka_ref[...])
        kd_ref[n] = kd
        bd_ref[n] = kk * a
        coef = coef + r * kd * rk_ref[...]
    bonus_ref[...] = _head_sum(coef, ones_bd) * v


def _pad_lora_up(w):
    z = jnp.zeros_like(w[0])
    return jnp.stack([jnp.concatenate([w[0], z], 0), jnp.concatenate([z, w[1]], 0)])


def _head_ones():
    h = np.arange(D) // RW_N
    return jnp.asarray(h[:, None] == h[None, :], dtype=BF16)


def _rw_in_proj(x, mod, layer, ones_bd, mu, w_in, w0, w1, w2, a0, a1, a2, k_k, k_a, r_k):
    mu8 = jnp.pad(mu, ((0, 2), (0, 0)))
    w1c = jnp.concatenate([w1[0], w1[1]], -1).astype(BF16)
    a1c = jnp.concatenate([a1[0], a1[1]], -1).astype(BF16)
    w2p = _pad_lora_up(w2).astype(BF16)
    a2p = _pad_lora_up(a2).astype(BF16)
    row = pl.BlockSpec((TM, D), lambda i: (i, 0))
    row2 = pl.BlockSpec((2, TM, D), lambda i: (0, i, 0))
    prev, nxt = _halo_specs(D, lambda i: 0)
    full = lambda a: pl.BlockSpec(a.shape, lambda i: (0,) * a.ndim)
    consts = [mu8, w_in[0].astype(BF16), w_in[1].astype(BF16), w_in[2].astype(BF16), w_in[3].astype(BF16),
              w1c, a1c, w2p, a2p, w0, a0, k_k.reshape(1, D), k_a.reshape(1, D), r_k.reshape(1, D), ones_bd]
    return pl.pallas_call(
        _rw_in_kernel,
        grid=(NT,),
        in_specs=[row, prev, nxt, _mod_spec(layer)] + [full(a) for a in consts],
        out_specs=[row] * 5 + [row2] * 3,
        out_shape=[jax.ShapeDtypeStruct((T, D), F32)] * 5 + [jax.ShapeDtypeStruct((2, T, D), F32)] * 3,
        compiler_params=_cparams(("arbitrary",)),
        name="rwkv_in_proj",
    )(x, x, x, mod, *consts)


def _rw_scan_kernel(rowblk_ref, dirn_ref, first_ref, unit_ref, r_ref, v_ref, nkk_ref, lw_ref, kd_ref, bd_ref,
                    s0_ref, y_ref, s_ref):
    step = pl.program_id(0)
    bwd = dirn_ref[step] == 1
    c = CHUNK

    @pl.when(first_ref[step] == 1)
    def _():
        s_ref[...] = s0_ref[...]

    sgn = jnp.where(bwd, -1, 1)
    ri = lax.broadcasted_iota(jnp.int32, (c, c), 0)
    ci = lax.broadcasted_iota(jnp.int32, (c, c), 1)
    incl = (ri - ci) * sgn >= 0
    strict = (ri - ci) * sgn > 0
    ri2 = lax.broadcasted_iota(jnp.int32, (c, 2 * c), 0)
    ci2 = lax.broadcasted_iota(jnp.int32, (c, 2 * c), 1)
    cm2 = jnp.where(ci2 >= c, ci2 - c, ci2)
    mask_2 = (ri2 - cm2) * sgn >= 0
    mask_k = jnp.logical_and((ri2 - cm2) * sgn > 0, ci2 >= c)
    eye_f = (ri == ci).astype(F32)
    eye = eye_f.astype(BF16)
    off_masks = []
    for lg in range(6):
        same_2m = (ri >> (lg + 1)) == (ci >> (lg + 1))
        diff_m = (ri >> lg) != (ci >> lg)
        off_masks.append(jnp.logical_and(strict, jnp.logical_and(same_2m, diff_m)))

    lw = lw_ref[...]
    g = _dot_hi(incl.astype(F32), lw)
    e_in = jnp.exp(g)
    e_ex = jnp.exp(g - lw)
    e_inv = jnp.exp(-g)
    gtot = jnp.where(bwd, g[0:1, :], g[c - 1:c, :])
    e_rem = jnp.exp(gtot - g)
    e_tot = jnp.exp(gtot)
    a_t = nkk_ref[...] * e_ex
    r_t = r_ref[...] * e_in
    kd, bd = kd_ref[...], bd_ref[...]
    b_t = bd * e_inv
    k_t = kd * e_inv
    b_e = bd * e_rem
    k_e = kd * e_rem
    v = v_ref[...]

    heads = range(RW_H)
    sls = [slice(h * RW_N, (h + 1) * RW_N) for h in heads]
    s_old = [s_ref[h] for h in heads]
    cat = lambda p, q, sl: jnp.concatenate([p[:, sl], q[:, sl]], axis=0).astype(BF16)
    ar = [cat(a_t, r_t, sl) for sl in sls]
    bk = [cat(b_t, k_t, sl) for sl in sls]
    bke = [cat(b_e, k_e, sl) for sl in sls]
    vh = [v[:, sl] for sl in sls]
    gm = [_dot_nt(ar[h], bk[h]) for h in heads]
    a_s = [_dot_nt(ar[h], s_old[h].astype(BF16)) for h in heads]
    lab = [gm[h][:c, :c] for h in heads]
    tinv = [eye_f + jnp.where(off_masks[0], lab[h], 0.0) for h in heads]
    for off in off_masks[1:]:
        tb = [t.astype(BF16) for t in tinv]
        lt = [_dot(jnp.where(off, lab[h], 0.0).astype(BF16), tb[h]) for h in heads]
        tinv = [tinv[h] + _dot(tb[h], lt[h].astype(BF16)) for h in heads]
    vv2 = [jnp.concatenate([vh[h], vh[h]], axis=0).astype(BF16) for h in heads]
    w1 = [a_s[h][:c] + _dot(jnp.where(mask_k, gm[h][:c], 0.0).astype(BF16), vv2[h]) for h in heads]
    u = [_dot(tinv[h].astype(BF16), w1[h].astype(BF16)) for h in heads]
    uv = [jnp.concatenate([u[h], vh[h]], axis=0).astype(BF16) for h in heads]
    y = [a_s[h][c:] + _dot(jnp.where(mask_2, gm[h][c:], 0.0).astype(BF16), uv[h]) for h in heads]
    y_ref[...] = jnp.concatenate(y, axis=1)
    for h in heads:
        uvt = jnp.concatenate([_dot_nt(eye, uv[h][:c]), _dot_nt(eye, uv[h][c:])], axis=1)
        s_ref[h] = s_old[h] * e_tot[:, sls[h]] + _dot(uvt.astype(BF16), bke[h])


def _rw_scan_tables():
    rowblk, dirn, first, unit = [], [], [], []
    u = 0
    for nb, L, blk0 in ((B_P, L_P, 0), (B_S, L_S, T_P // CHUNK)):
        nc = L // CHUNK
        for b in range(nb):
            for d in range(2):
                for j in range(nc):
                    cn = j if d == 0 else nc - 1 - j
                    rowblk.append(blk0 + b * nc + cn)
                    dirn.append(d)
                    first.append(1 if j == 0 else 0)
                    unit.append(u)
                u += 1
    as_i32 = lambda a: jnp.asarray(np.asarray(a, np.int32))
    return as_i32(rowblk), as_i32(dirn), as_i32(first), as_i32(unit), u


def _rw_scan(r, v, nkk, lw, kd, bd, s0_all):
    rowblk, dirn, first, unit, n_units = _rw_scan_tables()
    n_steps = rowblk.shape[0]
    tok = pl.BlockSpec((CHUNK, D), lambda s, rb, dr, fs, un: (rb[s], 0))
    tok2 = pl.BlockSpec((None, CHUNK, D), lambda s, rb, dr, fs, un: (dr[s], rb[s], 0))
    st = pl.BlockSpec((None, RW_H, RW_N, RW_N), lambda s, rb, dr, fs, un: (un[s], 0, 0, 0))
    return pl.pallas_call(
        _rw_scan_kernel,
        grid_spec=pltpu.PrefetchScalarGridSpec(
            num_scalar_prefetch=4,
            grid=(n_steps,),
            in_specs=[tok, tok, tok, tok2, tok2, tok2, st],
            out_specs=[tok2, st],
        ),
        out_shape=[jax.ShapeDtypeStruct((2, T, D), F32),
                   jax.ShapeDtypeStruct((n_units, RW_H, RW_N, RW_N), F32)],
        compiler_params=_cparams(("arbitrary",)),
        name="rwkv_scan",
    )(rowblk, dirn, first, unit, r, v, nkk, lw, kd, bd, s0_all)


def _rw_out_kernel(y_ref, bonus_ref, sz_ref, gg_ref, gb_ref, ones_ref, w_ref, x_ref, m_ref, lng_ref, lnb_ref, o_ref):
    ones_bd = ones_ref[...]
    y = y_ref[0] + y_ref[1]
    mean = _head_sum(y, ones_bd) * (1.0 / RW_N)
    yc = y - mean
    var = _head_sum(yc * yc, ones_bd) * (1.0 / RW_N)
    yn = yc * lax.rsqrt(var + RW_GN_EPS) * gg_ref[...] + gb_ref[...]
    g = (yn + bonus_ref[...]) * sz_ref[...]
    _out_ln_tail(g, w_ref, x_ref, m_ref, lng_ref, lnb_ref, o_ref)


def _rw_out_proj_ln(y, bonus, sz, gn_g, gn_b, ones_bd, w_out, x, mod, layer, ln_g, ln_b):
    row = pl.BlockSpec((TM, D), lambda i: (i, 0))
    vec = pl.BlockSpec((1, D), lambda i: (0, 0))
    mat = pl.BlockSpec((D, D), lambda i: (0, 0))
    return pl.pallas_call(
        _rw_out_kernel,
        grid=(NT,),
        in_specs=[pl.BlockSpec((2, TM, D), lambda i: (0, i, 0)), row, row, vec, vec, mat, mat, row,
                  _mod_spec(layer), vec, vec],
        out_specs=row,
        out_shape=jax.ShapeDtypeStruct((T, D), F32),
        compiler_params=_cparams(("arbitrary",)),
        name="rwkv_out_proj_ln",
    )(y, bonus, sz, gn_g.reshape(1, D), gn_b.reshape(1, D), ones_bd, w_out.astype(BF16), x, mod,
      ln_g.reshape(1, D), ln_b.reshape(1, D))


def _rwkv_layer(x, mod, layer, ln_g, ln_b, state, mu, w_in, w0, w1, w2, a0, a1, a2, k_k, k_a, r_k, gn_g, gn_b, w_out):
    ones_bd = _head_ones()
    r, v, sz, nkk, bonus, lw, kd, bd = _rw_in_proj(x, mod, layer, ones_bd, mu, w_in, w0, w1, w2, a0, a1, a2,
                                                  k_k, k_a, r_k)
    n_p = B_P * 2
    s0_all = jnp.concatenate([jnp.zeros((n_p, RW_H, RW_N, RW_N), F32),
                              state.astype(F32).reshape(B_S * 2, RW_H, RW_N, RW_N)], 0)
    y, s_fin = _rw_scan(r, v, nkk, lw, kd, bd, s0_all)
    x_new = _rw_out_proj_ln(y, bonus, sz, gn_g, gn_b, ones_bd, w_out, x, mod, layer, ln_g, ln_b)
    new_state = s_fin[:n_p].reshape(B_P, 1, 2, RW_H, RW_N, RW_N)
    return x_new, new_state


def kernel(x_prompt, x_sample, cache_mla_ckv, cache_mla_kpe, state_rwkv, c, c_ctx, mod_w, mod_b, ln_g, ln_b, hy_w_in, hy_conv_w, hy_conv_b, hy_ffn_w1, hy_ffn_b1, hy_ffn_w2, hy_ffn_b2, hy_ffn_w3, hy_ffn_b3, hy_freq, hy_decay, hy_skip, hy_w_out, mla_w_in, mla_q_norm, mla_kv_norm, mla_w_q_up, mla_w_kv_up, mla_w_out, rw_mu, rw_w_in, rw_w0, rw_w1, rw_w2, rw_a0, rw_a1, rw_a2, rw_k_k, rw_k_a, rw_r_k, rw_gn_g, rw_gn_b, rw_w_out):
    x = jnp.concatenate([x_prompt.reshape(T_P, D), x_sample.reshape(T_S, D)], 0)
    cond8 = jnp.concatenate([c_ctx[None, :], c, jnp.zeros((8 - 1 - B_S, D), F32)], 0)
    mod = _modulation_table(cond8, mod_w, mod_b).reshape(DEPTH * 8, 1, 3 * D)
    tabs_p = _dft_tables(L_P)
    tabs_s = _dft_tables(L_S)
    rope = _rope_tables()
    new_ckv = new_kpe = new_state = None
    for i in range(DEPTH):
        kind, j = i % 3, i // 3
        if kind == 0:
            x = _hyena_layer(x, mod, i, tabs_p, tabs_s, ln_g[i], ln_b[i], hy_w_in[j], hy_conv_w[j], hy_conv_b[j],
                             hy_ffn_w1[j], hy_ffn_b1[j], hy_ffn_w2[j], hy_ffn_b2[j], hy_ffn_w3[j], hy_ffn_b3[j],
                             hy_freq[j], hy_decay[j], hy_skip[j], hy_w_out[j])
        elif kind == 1:
            x, new_ckv, new_kpe = _mla_layer(x, mod, i, rope, ln_g[i], ln_b[i], cache_mla_ckv[:, j],
                                             cache_mla_kpe[:, j], mla_w_in[j], mla_q_norm[j], mla_kv_norm[j],
                                             mla_w_q_up[j], mla_w_kv_up[j], mla_w_out[j])
        else:
            x, new_state = _rwkv_layer(x, mod, i, ln_g[i], ln_b[i], state_rwkv[:, j], rw_mu[j], rw_w_in[j],
                                       rw_w0[j], rw_w1[j], rw_w2[j], rw_a0[j], rw_a1[j], rw_a2[j], rw_k_k[j],
                                       rw_k_a[j], rw_r_k[j], rw_gn_g[j], rw_gn_b[j], rw_w_out[j])
    return (x[:T_P].reshape(B_P, L_P, D), x[T_P:].reshape(B_S, L_S, D), new_ckv, new_kpe, new_state)
```

```python
import functools
import math

import numpy as np
import jax
import jax.numpy as jnp
from jax import lax
from jax.experimental import pallas as pl
from jax.experimental.pallas import tpu as pltpu

F32 = jnp.float32
BF16 = jnp.bfloat16
HIGHEST = lax.Precision.HIGHEST

D = 1024
B_P, L_P = 16, 256
B_S, L_S = 2, 2048
T_P = B_P * L_P
T_S = B_S * L_S
T = T_P + T_S
PAST = 512
DEPTH = 4
DEEPNORM_ALPHA = (2.0 * DEPTH) ** 0.25
LN_EPS = 1e-5
RMS_EPS = 1e-6
HY_BANDS = 16
HY_FFN = 64
MLA_HEADS = 16
MLA_Q_RANK = 256
MLA_KV_RANK = 128
MLA_NOPE = 64
MLA_ROPE = 32
MLA_V = 64
ROPE_BASE = 10000.0
GRID_W = 64
RW_N = 64
RW_H = D // RW_N
RW_LORA = 64
RW_GN_EPS = 64e-5

TM = 256
NT_P = T_P // TM
NT_S_SEQ = L_S // TM
NT = T // TM
HALO = 8
LANES = 128
CHUNK = 64
VMEM_LIMIT = 52 * 1024 * 1024


def _cparams(sem):
    return pltpu.CompilerParams(dimension_semantics=sem, vmem_limit_bytes=VMEM_LIMIT)


def _group(i):
    return jnp.where(i < NT_P, 0, 1 + (i - NT_P) // NT_S_SEQ)


def _sigmoid(x):
    return 1.0 / (1.0 + jnp.exp(-x))


def _silu(x):
    return x * _sigmoid(x)


def _dot(a, b):
    return jnp.dot(a, b, preferred_element_type=F32)


def _dot_nt(a, b):
    return lax.dot_general(a, b, (((1,), (1,)), ((), ())), preferred_element_type=F32)


def _dot_hi(a, b):
    return jnp.dot(a, b, preferred_element_type=F32, precision=HIGHEST)


def _split(x):
    hi = x.astype(BF16)
    lo = (x - hi.astype(F32)).astype(BF16)
    return hi, lo


def _dot3(ahi, alo, x):
    xhi, xlo = _split(x)
    return _dot(ahi, xhi) + _dot(ahi, xlo) + _dot(alo, xhi)


def _head_sum(x, ones_bd):
    hi, lo = _split(x)
    lanes = ones_bd.shape[0]
    parts = []
    for g in range(x.shape[1] // lanes):
        sl = slice(g * lanes, (g + 1) * lanes)
        parts.append(_dot(hi[:, sl], ones_bd) + _dot(lo[:, sl], ones_bd))
    return jnp.concatenate(parts, axis=1)


def _modulate(x, m):
    return x * (1.0 + m[:, D:2 * D]) + m[:, :D]


def _layer_norm_rows(y, g, b):
    mu = jnp.mean(y, axis=-1, keepdims=True)
    yc = y - mu
    var = jnp.mean(yc * yc, axis=-1, keepdims=True)
    return yc * lax.rsqrt(var + LN_EPS) * g + b


def _neighbour_rows(cur, prev_halo, next_halo, has_prev, has_next):
    rows = cur.shape[0]
    ridx = lax.broadcasted_iota(jnp.int32, cur.shape, 0)
    pr = jnp.where(has_prev, prev_halo[HALO - 1:HALO, :], 0.0)
    nx = jnp.where(has_next, next_halo[0:1, :], 0.0)
    prev = jnp.where(ridx == 0, pr, pltpu.roll(cur, 1, axis=0))
    nxt = jnp.where(ridx == rows - 1, nx, pltpu.roll(cur, rows - 1, axis=0))
    return prev, nxt


def _tile_has_neighbours(i):
    k = (i - NT_P) % NT_S_SEQ
    is_s = i >= NT_P
    return jnp.logical_and(is_s, k != 0), jnp.logical_and(is_s, k != NT_S_SEQ - 1)


def _halo_specs(width, col_of):
    r = TM // HALO
    prev = pl.BlockSpec((HALO, width), lambda i, *a: (jnp.maximum(i * r - 1, 0), col_of(i, *a)))
    nxt = pl.BlockSpec((HALO, width), lambda i, *a: (jnp.minimum((i + 1) * r, T // HALO - 1), col_of(i, *a)))
    return prev, nxt


def _mod_kernel(c_ref, w_ref, b_ref, o_ref):
    o_ref[...] = _dot_hi(_silu(c_ref[...]), w_ref[...]) + b_ref[...]


def _modulation_table(cond8, mod_w, mod_b):
    tn = 1024
    return pl.pallas_call(
        _mod_kernel,
        grid=(DEPTH, 3 * D // tn),
        in_specs=[pl.BlockSpec((8, D), lambda l, j: (0, 0)),
                  pl.BlockSpec((None, D, tn), lambda l, j: (l, 0, j)),
                  pl.BlockSpec((None, 1, tn), lambda l, j: (l, 0, j))],
        out_specs=pl.BlockSpec((None, 8, tn), lambda l, j: (l, 0, j)),
        out_shape=jax.ShapeDtypeStruct((DEPTH, 8, 3 * D), F32),
        compiler_params=_cparams(("arbitrary", "arbitrary")),
        name="modulation",
    )(cond8, mod_w, mod_b.reshape(DEPTH, 1, 3 * D))


def _mod_spec(layer):
    return pl.BlockSpec((None, 1, 3 * D), lambda i, *a: (layer * 8 + _group(i), 0, 0))


def _out_ln_tail(g, w_ref, x_ref, m_ref, lng_ref, lnb_ref, o_ref):
    mix = _dot(g.astype(BF16), w_ref[...])
    gate = m_ref[...][:, 2 * D:]
    y = DEEPNORM_ALPHA * x_ref[...] + gate * mix
    o_ref[...] = _layer_norm_rows(y, lng_ref[...], lnb_ref[...])


def _out_ln2_kernel(gp_ref, gs_ref, w_ref, x_ref, m_ref, lng_ref, lnb_ref, o_ref):
    g = jnp.where(pl.program_id(0) < NT_P, gp_ref[...], gs_ref[...])
    _out_ln_tail(g, w_ref, x_ref, m_ref, lng_ref, lnb_ref, o_ref)


def _out_proj_ln2(g_p, g_s, w_out, x, mod, layer, ln_g, ln_b):
    row = pl.BlockSpec((TM, D), lambda i: (i, 0))
    vec = pl.BlockSpec((1, D), lambda i: (0, 0))
    return pl.pallas_call(
        _out_ln2_kernel,
        grid=(NT,),
        in_specs=[pl.BlockSpec((TM, D), lambda i: (jnp.minimum(i, NT_P - 1), 0)),
                  pl.BlockSpec((TM, D), lambda i: (jnp.maximum(i - NT_P, 0), 0)),
                  pl.BlockSpec((D, D), lambda i: (0, 0)),
                  row, _mod_spec(layer), vec, vec],
        out_specs=row,
        out_shape=jax.ShapeDtypeStruct((T, D), F32),
        compiler_params=_cparams(("arbitrary",)),
        name="out_proj_ln",
    )(g_p, g_s, w_out.astype(BF16), x, mod, ln_g.reshape(1, D), ln_b.reshape(1, D))


def _hy_in_kernel(x_ref, m_ref, w_ref, o_ref):
    h = _modulate(x_ref[...], m_ref[...])
    o_ref[...] = _dot(h.astype(BF16), w_ref[...])


def _hy_in_proj(x, mod, layer, w_in):
    n = w_in.shape[1]
    return pl.pallas_call(
        _hy_in_kernel,
        grid=(NT,),
        in_specs=[pl.BlockSpec((TM, D), lambda i: (i, 0)), _mod_spec(layer),
                  pl.BlockSpec((D, n), lambda i: (0, 0))],
        out_specs=pl.BlockSpec((TM, n), lambda i: (i, 0)),
        out_shape=jax.ShapeDtypeStruct((T, n), F32),
        compiler_params=_cparams(("arbitrary",)),
        name="hyena_in_proj",
    )(x, mod, w_in.astype(BF16))


def _hy_conv3_kernel(x0_ref, x0p_ref, x0n_ref, x1_ref, x1p_ref, x1n_ref, v_ref, vp_ref, vn_ref, z_ref,
                     w0_ref, w1_ref, w2_ref, b0_ref, b1_ref, b2_ref, vv_ref, gate_ref):
    has_prev, has_next = _tile_has_neighbours(pl.program_id(0))

    def conv(c_ref, p_ref, n_ref, w_ref, b_ref):
        cur = c_ref[...]
        prev, nxt = _neighbour_rows(cur, p_ref[...], n_ref[...], has_prev, has_next)
        w = w_ref[...]
        return prev * w[0:1] + cur * w[1:2] + nxt * w[2:3] + b_ref[...]

    x0 = conv(x0_ref, x0p_ref, x0n_ref, w0_ref, b0_ref)
    x1 = conv(x1_ref, x1p_ref, x1n_ref, w1_ref, b1_ref)
    v = conv(v_ref, vp_ref, vn_ref, w2_ref, b2_ref)
    vv_ref[...] = v * x1
    gate_ref[...] = x0 * _silu(z_ref[...])


def _hy_conv3(u, conv_w, conv_b):
    ct = 512
    nct = D // ct
    in_specs = []
    for grp in range(3):
        in_specs.append(pl.BlockSpec((TM, ct), lambda i, j, grp=grp: (i, grp * nct + j)))
        in_specs.extend(_halo_specs(ct, lambda i, j, grp=grp: grp * nct + j))
    in_specs.append(pl.BlockSpec((TM, ct), lambda i, j: (i, 3 * nct + j)))
    for grp in range(3):
        in_specs.append(pl.BlockSpec((3, ct), lambda i, j, grp=grp: (0, grp * nct + j)))
    for grp in range(3):
        in_specs.append(pl.BlockSpec((1, ct), lambda i, j, grp=grp: (0, grp * nct + j)))
    out = pl.BlockSpec((TM, ct), lambda i, j: (i, j))
    args = [u] * 10 + [conv_w] * 3 + [conv_b.reshape(1, 3 * D)] * 3
    return pl.pallas_call(
        _hy_conv3_kernel,
        grid=(NT, nct),
        in_specs=in_specs,
        out_specs=[out, out],
        out_shape=[jax.ShapeDtypeStruct((T, D), F32)] * 2,
        compiler_params=_cparams(("arbitrary", "arbitrary")),
        name="hyena_conv3_gate",
    )(*args)


def _hy_filter_kernel(t_ref, bands_ref, wt_ref, wc_ref, ws_ref, b1_ref, w2_ref, b2_ref, w3_ref, b3_ref,
                      f0_ref, f1_ref, dec_ref, hs_ref, hd_ref, nyq_ref, *, L, tr):
    i = pl.program_id(0)
    ridx = lax.broadcasted_iota(jnp.int32, (tr, 1), 0) + i * tr
    pos = ridx.astype(F32)
    t = t_ref[...]
    ang = ((2.0 * math.pi / L) * pos) * bands_ref[...]
    pre = t * wt_ref[...] + _dot_hi(jnp.cos(ang), wc_ref[...]) + _dot_hi(jnp.sin(ang), ws_ref[...])
    hdn = jnp.sin(f0_ref[...] * (pre + b1_ref[...]))
    hdn = jnp.sin(f1_ref[...] * (_dot_hi(hdn, w2_ref[...]) + b2_ref[...]))
    hf = _dot_hi(hdn, w3_ref[...]) + b3_ref[...]
    h = hf * jnp.exp(-t * jnp.abs(dec_ref[...]))
    h0 = h[:, :D]
    h1 = jnp.where(ridx == 0, 0.0, h[:, D:])
    hsum = h0 + h1
    hs_ref[...] = hsum
    hd_ref[...] = h1 - h0
    alt = jnp.where((ridx & 1) == 0, 1.0, -1.0)
    part =jnp.broadcast_to(jnp.sum(alt * hsum, axis=0, keepdims=True), (8, D))

    @pl.when(i == 0)
    def _():
        nyq_ref[...] = part

    @pl.when(i > 0)
    def _():
        nyq_ref[...] += part


def _hy_filter(L, w1, b1, w2, b2, w3, b3, freq, decay):
    tr = 256
    t = jnp.linspace(0.0, 1.0, L, dtype=F32).reshape(L, 1)
    bands = jnp.linspace(1e-4, HY_BANDS - 1, HY_BANDS, dtype=F32)
    bands = jnp.pad(bands, (0, 128 - HY_BANDS)).reshape(1, 128)
    wt = w1[0:1]
    wc = jnp.pad(w1[1:1 + HY_BANDS], ((0, 128 - HY_BANDS), (0, 0)))
    ws = jnp.pad(-w1[1 + HY_BANDS:], ((0, 128 - HY_BANDS), (0, 0)))
    full = lambda shape: pl.BlockSpec(shape, lambda i: (0, 0))
    rows = pl.BlockSpec((tr, D), lambda i: (i, 0))
    return pl.pallas_call(
        functools.partial(_hy_filter_kernel, L=L, tr=tr),
        grid=(L // tr,),
        in_specs=[pl.BlockSpec((tr, 1), lambda i: (i, 0)), full((1, 128)), full((1, HY_FFN)),
                  full((128, HY_FFN)), full((128, HY_FFN)), full((1, HY_FFN)),
                  full((HY_FFN, HY_FFN)), full((1, HY_FFN)), full((HY_FFN, 2 * D)), full((1, 2 * D)),
                  full((1, HY_FFN)), full((1, HY_FFN)), full((1, 2 * D))],
        out_specs=[rows, rows, pl.BlockSpec((8, D), lambda i: (0, 0))],
        out_shape=[jax.ShapeDtypeStruct((L, D), F32), jax.ShapeDtypeStruct((L, D), F32),
                   jax.ShapeDtypeStruct((8, D), F32)],
        compiler_params=_cparams(("arbitrary",)),
        name="hyena_filter",
    )(t, bands, wt, wc, ws, b1.reshape(1, -1), w2, b2.reshape(1, -1), w3, b3.reshape(1, -1),
      freq[0:1], freq[1:2], decay.reshape(1, 2 * D))


def _dft_tables(L):
    n = 2 * L
    w = 64
    k = jnp.arange(L, dtype=jnp.int32)

    def cs(t):
        ang = ((k[:, None] * t[None, :]) % n).astype(F32) * (2.0 * math.pi / n)
        return jnp.cos(ang), jnp.sin(ang)

    ca, sa = cs(jnp.arange(L // w, dtype=jnp.int32) * w)
    cb, sb = cs(jnp.arange(w, dtype=jnp.int32))
    c = (ca[:, :, None] * cb[:, None, :] - sa[:, :, None] * sb[:, None, :]).reshape(L, L)
    s = (sa[:, :, None] * cb[:, None, :] + ca[:, :, None] * sb[:, None, :]).reshape(L, L)
    alt = jnp.where(k % 2 == 0, 1.0, -1.0).astype(F32)
    s_rows = jnp.where(k[:, None] == 0, alt[None, :], s)
    s_cols = jnp.where(k[None, :] == 0, alt[:, None], s)
    return _split(c) + _split(s_rows) + _split(s_cols)


def _dft_fwd_kernel(chi_ref, clo_ref, shi_ref, slo_ref, *refs):
    x1_ref, x2_ref = refs[0], refs[-3]
    oc_ref, os_ref = refs[-2:]
    oc_ref[...] = _dot3(chi_ref[...], clo_ref[...], x1_ref[...])
    os_ref[...] = _dot3(shi_ref[...], slo_ref[...], x2_ref[...])


def _dft_col_tile(L):
    return 512 if L <= 512 else 256


def _dft_fwd(tabs, xs, L, nb, row_blk0):
    chi, clo, shi, slo = tabs[:4]
    tk = min(L, 512)
    tn = _dft_col_tile(L)
    nk = L // tk
    a_spec = pl.BlockSpec((tk, L), lambda b, j, k: (k, 0))
    x_spec = pl.BlockSpec((L, tn), lambda b, j, k: (row_blk0 + b, j))
    o_spec = pl.BlockSpec((tk, tn), lambda b, j, k: (b * nk + k, j))
    return pl.pallas_call(
        _dft_fwd_kernel,
        grid=(nb, D // tn, nk),
        in_specs=[a_spec] * 4 + [x_spec] * len(xs),
        out_specs=[o_spec, o_spec],
        out_shape=[jax.ShapeDtypeStruct((nb * L, D), F32)] * 2,
        compiler_params=_cparams(("arbitrary", "arbitrary", "arbitrary")),
        name="hyena_dft_fwd",
    )(chi, clo, shi, slo, *xs)


def _dft_inv_kernel(chi_ref, clo_ref, sthi_ref, stlo_ref, vc_ref, vs_ref, kre_ref, kim_ref, nyq_ref,
                    vv_ref, skip_ref, gate_ref, o_ref, *, L, tk):
    k = pl.program_id(2)
    nk = pl.num_programs(2)
    vc, vs, kre = vc_ref[...], vs_ref[...], kre_ref[...]
    bin0 = jnp.logical_and(lax.broadcasted_iota(jnp.int32, vc.shape, 0) == 0, k == 0)
    kim = jnp.where(bin0, nyq_ref[0:1, :], kim_ref[...])
    inv_n = 1.0 / (2 * L)
    yre = jnp.where(bin0, vc * kre * inv_n, (vc * kre + vs * kim) * (2.0 * inv_n))
    yim = jnp.where(bin0, vs * kim * inv_n, (vs * kre - vc * kim) * (2.0 * inv_n))
    contrib = _dot3(chi_ref[...], clo_ref[...], yre) + _dot3(sthi_ref[...], stlo_ref[...], yim)

    @pl.when(k == 0)
    def _():
        o_ref[...] = contrib

    @pl.when(k > 0)
    def _():
        o_ref[...] += contrib

    @pl.when(k == nk - 1)
    def _():
        o_ref[...] = (o_ref[...] + vv_ref[...] * skip_ref[...]) * gate_ref[...]


def _dft_inv(tabs, vc, vs, kre, kim, nyq, vv, skip, gate, L, nb, row_blk0):
    chi, clo, _, _, sthi, stlo = tabs
    tk = min(L, 512)
    tn = _dft_col_tile(L)
    nk = L // tk
    a_spec = pl.BlockSpec((L, tk), lambda b, j, k: (0, k))
    v_spec = pl.BlockSpec((tk, tn), lambda b, j, k: (b * nk + k, j))
    k_spec = pl.BlockSpec((tk, tn), lambda b, j, k: (k, j))
    row_spec = pl.BlockSpec((L, tn), lambda b, j, k: (row_blk0 + b, j))
    return pl.pallas_call(
        functools.partial(_dft_inv_kernel, L=L, tk=tk),
        grid=(nb, D // tn, nk),
        in_specs=[a_spec] * 4 + [v_spec, v_spec, k_spec, k_spec,
                                 pl.BlockSpec((8, tn), lambda b, j, k: (0, j)),
                                 row_spec, pl.BlockSpec((1, tn), lambda b, j, k: (0, j)), row_spec],
        out_specs=pl.BlockSpec((L, tn), lambda b, j, k: (b, j)),
        out_shape=jax.ShapeDtypeStruct((nb * L, D), F32),
        compiler_params=_cparams(("arbitrary", "arbitrary", "arbitrary")),
        name="hyena_dft_inv_gate",
    )(chi, clo, sthi, stlo, vc, vs, kre, kim, nyq, vv, skip.reshape(1, D), gate)


def _hyena_layer(x, mod, layer, tabs_p, tabs_s, ln_g, ln_b, w_in, conv_w, conv_b, w1, b1, w2, b2, w3, b3,
                 freq, decay, skip, w_out):
    u = _hy_in_proj(x, mod, layer, w_in)
    vv, gate = _hy_conv3(u, conv_w, conv_b)
    gs = []
    for L, nb, blk0, tabs in ((L_P, B_P, 0, tabs_p), (L_S, B_S, T_P // L_S, tabs_s)):
        hsum, hdiff, nyq = _hy_filter(L, w1, b1, w2, b2, w3, b3, freq, decay)
        kre, kim = _dft_fwd(tabs, (hsum, hdiff), L, 1, 0)
        vc, vs = _dft_fwd(tabs, (vv,), L, nb, blk0)
        gs.append(_dft_inv(tabs, vc, vs, kre, kim, nyq, vv, skip, gate, L, nb, blk0))
    return _out_proj_ln2(gs[0], gs[1], w_out, x, mod, layer, ln_g, ln_b)


def _rope_tables():
    rows = L_S // GRID_W
    half = MLA_ROPE // 2
    inv = ROPE_BASE ** (-jnp.arange(0, half, 2, dtype=F32) / half)
    r = jnp.repeat(jnp.arange(rows, dtype=F32), GRID_W)
    col = jnp.tile(jnp.arange(GRID_W, dtype=F32), rows)
    ar, ac = r[:, None] * inv, col[:, None] * inv
    ang = jnp.concatenate([ar, ar, ac, ac], -1)
    cos, sin = jnp.cos(ang), jnp.sin(ang)
    cos = jnp.concatenate([jnp.ones((TM, MLA_ROPE), F32), cos], 0)
    sin = jnp.concatenate([jnp.zeros((TM, MLA_ROPE), F32), sin], 0)
    return cos, sin, jnp.tile(cos, (1, MLA_HEADS)), jnp.tile(sin, (1, MLA_HEADS))


def _rope_rot_cols(w):
    idx = np.concatenate([np.arange(8, 16), np.arange(0, 8), np.arange(24, 32), np.arange(16, 24)])
    sign = np.concatenate([-np.ones(8), np.ones(8), -np.ones(8), np.ones(8)]).astype(np.float32)
    return w[..., idx] * sign


def _mla_in_kernel(x_ref, m_ref, wq_ref, wkv_ref, wkp_ref, wz_ref, qn_ref, kvn_ref, wqn_ref, wqp_ref, wqr_ref,
                   c32_ref, s32_ref, c512_ref, s512_ref,
                   qno_ref, qpe_ref, ckv_ref, kpe_ref, kpr_ref, sz_ref):
    h = _modulate(x_ref[...], m_ref[...]).astype(BF16)
    q_c = _dot(h, wq_ref[...])
    kv_c = _dot(h, wkv_ref[...])
    kp2 = _dot(h, wkp_ref[...])
    z = _dot(h, wz_ref[...])

    def rms(v, g):
        return v * lax.rsqrt(jnp.mean(v * v, axis=-1, keepdims=True) + RMS_EPS) * g

    qn = rms(q_c, qn_ref[...]).astype(BF16)
    scale = (MLA_NOPE + MLA_ROPE) ** -0.5
    qno_ref[...] = (_dot(qn, wqn_ref[...]) * scale).astype(BF16)
    q_pe = _dot(qn, wqp_ref[...]) * c512_ref[...] + _dot(qn, wqr_ref[...]) * s512_ref[...]
    qpe_ref[...] = (q_pe * scale).astype(BF16)
    ckv_ref[...] = rms(kv_c, kvn_ref[...])
    kpe = kp2[:, :MLA_ROPE]
    kpe_ref[...] = kpe
    kpr_ref[...] = kpe * c32_ref[...] + kp2[:, MLA_ROPE:] * s32_ref[...]
    sz_ref[...] = _silu(z)


def _mla_in_proj(x, mod, layer, w_in, q_norm, kv_norm, w_q_up, rope):
    c32, s32, c512, s512 = rope
    o1, o2, o3 = MLA_Q_RANK, MLA_Q_RANK + MLA_KV_RANK, MLA_Q_RANK + MLA_KV_RANK + MLA_ROPE
    wq, wkv, wkp, wz = w_in[:, :o1], w_in[:, o1:o2], w_in[:, o2:o3], w_in[:, o3:]
    wkp2 = jnp.concatenate([wkp, _rope_rot_cols(wkp)], -1)
    wqu = w_q_up.reshape(MLA_Q_RANK, MLA_HEADS, MLA_NOPE + MLA_ROPE)
    wqn = wqu[:, :, :MLA_NOPE].reshape(MLA_Q_RANK, MLA_HEADS * MLA_NOPE)
    wqp = wqu[:, :, MLA_NOPE:]
    wqr = _rope_rot_cols(wqp).reshape(MLA_Q_RANK, MLA_HEADS * MLA_ROPE)
    wqp = wqp.reshape(MLA_Q_RANK, MLA_HEADS * MLA_ROPE)
    full = lambda a: pl.BlockSpec(a.shape, lambda i: (0,) * a.ndim)
    rope_idx = lambda i: jnp.where(i < NT_P, 0, 1 + (i - NT_P) % NT_S_SEQ)
    rows = lambda n: pl.BlockSpec((TM, n), lambda i: (i, 0))
    tab = lambda n: pl.BlockSpec((TM, n), lambda i: (rope_idx(i), 0))
    weights = [wq.astype(BF16), wkv.astype(BF16), wkp2.astype(BF16), wz.astype(BF16),
               q_norm.reshape(1, -1), kv_norm.reshape(1, -1),
               wqn.astype(BF16), wqp.astype(BF16), wqr.astype(BF16)]
    npe = MLA_HEADS * MLA_ROPE
    return pl.pallas_call(
        _mla_in_kernel,
        grid=(NT,),
        in_specs=[rows(D), _mod_spec(layer)] + [full(a) for a in weights]
                 + [tab(MLA_ROPE), tab(MLA_ROPE), tab(npe), tab(npe)],
        out_specs=[rows(D), rows(npe), rows(MLA_KV_RANK), rows(MLA_ROPE), rows(MLA_ROPE), rows(D)],
        out_shape=[jax.ShapeDtypeStruct((T, D), BF16), jax.ShapeDtypeStruct((T, npe), BF16),
                   jax.ShapeDtypeStruct((T, MLA_KV_RANK), F32), jax.ShapeDtypeStruct((T, MLA_ROPE), F32),
                   jax.ShapeDtypeStruct((T, MLA_ROPE), F32), jax.ShapeDtypeStruct((T, D), F32)],
        compiler_params=_cparams(("arbitrary",)),
        name="mla_in_proj",
    )(x, mod, *weights, c32, s32, c512, s512)


ATT_HG = 4


def _mla_attn_kernel(*refs, n_cache, n_new):
    if n_cache:
        (qn_ref, qp_ref, ckv_ref, kpr_ref, cckv_ref, ckpe_ref, wk_ref, wv_ref, sz_ref,
         o_ref, k_s, v_s, kp_s) = refs
    else:
        qn_ref, qp_ref, ckv_ref, kpr_ref, wk_ref, wv_ref, sz_ref, o_ref, k_s, v_s, kp_s = refs

    @pl.when(pl.program_id(2) == 0)
    def _():
        wk, wv = wk_ref[...], wv_ref[...]
        if n_cache:
            cc = cckv_ref[...].astype(BF16)
            k_s[0:n_cache, :] = _dot(cc, wk).astype(BF16)
            v_s[0:n_cache, :] = _dot(cc, wv).astype(BF16)
            kp_s[0:n_cache, :] = ckpe_ref[...].astype(BF16)
        cn = ckv_ref[...].astype(BF16)
        k_s[n_cache:n_cache + n_new, :] = _dot(cn, wk).astype(BF16)
        v_s[n_cache:n_cache + n_new, :] = _dot(cn, wv).astype(BF16)
        kp_s[n_cache:n_cache + n_new, :] = kpr_ref[...].astype(BF16)

    kp = kp_s[...]
    k_all, v_all = k_s[...], v_s[...]
    qn_all, qp_all = qn_ref[...], qp_ref[...]
    outs = []
    for hh in range(ATT_HG):
        qn = qn_all[:, hh * MLA_NOPE:(hh + 1) * MLA_NOPE]
        qp = qp_all[:, hh * MLA_ROPE:(hh + 1) * MLA_ROPE]
        s = _dot_nt(qn, k_all[:, hh * MLA_NOPE:(hh + 1) * MLA_NOPE]) + _dot_nt(qp, kp)
        p = jnp.exp(s - jnp.max(s, axis=-1, keepdims=True))
        l = jnp.sum(p, axis=-1, keepdims=True)
        o = _dot(p.astype(BF16), v_all[:, hh * MLA_V:(hh + 1) * MLA_V])
        outs.append(o / l)
    o_ref[...] = jnp.concatenate(outs, axis=-1) * sz_ref[...]


def _mla_attention(qno, qpe, ckv, kpr, sz, w_kv_up, cache_ckv, cache_kpe, *, nb, L, row_blk0):
    wkv = w_kv_up.reshape(MLA_KV_RANK, MLA_HEADS, MLA_NOPE + MLA_V)
    wk = wkv[:, :, :MLA_NOPE].reshape(MLA_KV_RANK, D).astype(BF16)
    wv = wkv[:, :, MLA_NOPE:].reshape(MLA_KV_RANK, D).astype(BF16)
    n_cache = 0 if cache_ckv is None else cache_ckv.shape[1]
    lk = n_cache + L
    nq = L // TM
    wq = ATT_HG * MLA_NOPE
    wp = ATT_HG * MLA_ROPE
    qrow = lambda w: pl.BlockSpec((TM, w), lambda b, g, q: (row_blk0 + b * nq + q, g))
    seq = lambda w: pl.BlockSpec((L, w), lambda b, g, q: (row_blk0 * TM // L + b, 0))
    wspec = pl.BlockSpec((MLA_KV_RANK, wq), lambda b, g, q: (0, g))
    in_specs = [qrow(wq), qrow(wp), seq(MLA_KV_RANK), seq(MLA_ROPE)]
    args = [qno, qpe, ckv, kpr]
    if n_cache:
        in_specs += [pl.BlockSpec((None, n_cache, MLA_KV_RANK), lambda b, g, q: (b, 0, 0)),
                     pl.BlockSpec((None, n_cache, MLA_ROPE), lambda b, g, q: (b, 0, 0))]
        args += [cache_ckv, cache_kpe]
    in_specs += [wspec, wspec, qrow(wq)]
    args += [wk, wv, sz]
    return pl.pallas_call(
        functools.partial(_mla_attn_kernel, n_cache=n_cache, n_new=L),
        grid=(nb, MLA_HEADS // ATT_HG, nq),
        in_specs=in_specs,
        out_specs=pl.BlockSpec((TM, wq), lambda b, g, q: (b * nq + q, g)),
        out_shape=jax.ShapeDtypeStruct((nb * L, D), F32),
        scratch_shapes=[pltpu.VMEM((lk, wq), BF16), pltpu.VMEM((lk, wq), BF16), pltpu.VMEM((lk, MLA_ROPE), BF16)],
        compiler_params=_cparams(("arbitrary", "arbitrary", "arbitrary")),
        name="mla_attention",
    )(*args)


def _mla_layer(x, mod, layer, rope, ln_g, ln_b, cache_ckv, cache_kpe, w_in, q_norm, kv_norm, w_q_up, w_kv_up, w_out):
    qno, qpe, ckv, kpe, kpr, sz = _mla_in_proj(x, mod, layer, w_in, q_norm, kv_norm, w_q_up, rope)
    g_p = _mla_attention(qno, qpe, ckv, kpr, sz, w_kv_up, None, None, nb=B_P, L=L_P, row_blk0=0)
    g_s = _mla_attention(qno, qpe, ckv, kpr, sz, w_kv_up, cache_ckv, cache_kpe, nb=B_S, L=L_S, row_blk0=NT_P)
    x_new = _out_proj_ln2(g_p, g_s, w_out, x, mod, layer, ln_g, ln_b)
    new_ckv = ckv[:T_P].reshape(B_P, 1, L_P, MLA_KV_RANK)
    new_kpe = kpe[:T_P].reshape(B_P, 1, L_P, MLA_ROPE)
    return x_new, new_ckv, new_kpe


def _rw_in_kernel(x_ref, xp_ref, xn_ref, m_ref, mu_ref, wr_ref, wk_ref, wv_ref, wg_ref, w1_ref, a1_ref,
                  w2_ref, a2_ref, w0_ref, a0_ref, kk_ref, ka_ref, rk_ref, ones_ref,
                  r_ref, v_ref, sz_ref, nkk_ref, bonus_ref, lw_ref, kd_ref, bd_ref):
    i = pl.program_id(0)
    has_prev, has_next = _tile_has_neighbours(i)
    m = m_ref[...]
    h = _modulate(x_ref[...], m)
    prev, nxt = _neighbour_rows(h, _modulate(xp_ref[...], m), _modulate(xn_ref[...], m), has_prev, has_next)
    d = 0.5 * (prev + nxt) - h
    mu = mu_ref[...]

    def mix(p):
        return (h + d * mu[p:p + 1]).astype(BF16)

    r = _dot(mix(0), wr_ref[...])
    tw = jnp.tanh(_dot(mix(1), w1_ref[...])).astype(BF16)
    k = _dot(mix(2), wk_ref[...])
    v = _dot(mix(3), wv_ref[...])
    ta = _dot(mix(4), a1_ref[...]).astype(BF16)
    z = _dot(mix(5), wg_ref[...])
    r_ref[...] = r
    v_ref[...] = v
    sz_ref[...] = _silu(z)
    ones_bd = ones_ref[...]
    kk = k * kk_ref[...]
    kk = kk * lax.rsqrt(_head_sum(kk * kk, ones_bd) + 1e-12)
    nkk_ref[...] = -kk
    coef = jnp.zeros_like(r)
    for n in range(2):
        wl = w0_ref[n:n + 1, :] + _dot(tw, w2_ref[n])
        lw_ref[n] = -math.exp(-0.5) * _sigmoid(wl)
        a = _sigmoid(a0_ref[n:n + 1, :] + _dot(ta, a2_ref[n]))
        kd = k * (1.0 + (a - 1.0) * ka_ref[...])
        kd_ref[n] = kd
        bd_ref[n] = kk * a
        coef = coef + r * kd * rk_ref[...]
    bonus_ref[...] = _head_sum(coef, ones_bd) * v


def _pad_lora_up(w):
    z = jnp.zeros_like(w[0])
    return jnp.stack([jnp.concatenate([w[0], z], 0), jnp.concatenate([z, w[1]], 0)])


def _head_ones():
    h = np.arange(LANES) // RW_N
    return jnp.asarray(h[:, None] == h[None, :], dtype=BF16)


def _rw_in_proj(x, mod, layer, ones_bd, mu, w_in, w0, w1, w2, a0, a1, a2, k_k, k_a, r_k):
    mu8 = jnp.pad(mu, ((0, 2), (0, 0)))
    w1c = jnp.concatenate([w1[0], w1[1]], -1).astype(BF16)
    a1c = jnp.concatenate([a1[0], a1[1]], -1).astype(BF16)
    w2p = _pad_lora_up(w2).astype(BF16)
    a2p = _pad_lora_up(a2).astype(BF16)
    row = pl.BlockSpec((TM, D), lambda i: (i, 0))
    row2 = pl.BlockSpec((2, TM, D), lambda i: (0, i, 0))
    prev, nxt = _halo_specs(D, lambda i: 0)
    full = lambda a: pl.BlockSpec(a.shape, lambda i: (0,) * a.ndim)
    consts = [mu8, w_in[0].astype(BF16), w_in[1].astype(BF16), w_in[2].astype(BF16), w_in[3].astype(BF16),
              w1c, a1c, w2p, a2p, w0, a0, k_k.reshape(1, D), k_a.reshape(1, D), r_k.reshape(1, D), ones_bd]
    return pl.pallas_call(
        _rw_in_kernel,
        grid=(NT,),
        in_specs=[row, prev, nxt, _mod_spec(layer)] + [full(a) for a in consts],
        out_specs=[row] * 5 + [row2] * 3,
        out_shape=[jax.ShapeDtypeStruct((T, D), F32)] * 5 + [jax.ShapeDtypeStruct((2, T, D), F32)] * 3,
        compiler_params=_cparams(("arbitrary",)),
        name="rwkv_in_proj",
    )(x, x, x, mod, *consts)


def _rw_scan_kernel(rowblk_ref, dirn_ref, first_ref, unit_ref, r_ref, v_ref, nkk_ref, lw_ref, kd_ref, bd_ref,
                    s0_ref, y_ref, s_ref):
    step = pl.program_id(0)
    bwd = dirn_ref[step] == 1
    c = CHUNK

    @pl.when(first_ref[step] == 1)
    def _():
        s_ref[...] = s0_ref[...]

    sgn = jnp.where(bwd, -1, 1)
    ri = lax.broadcasted_iota(jnp.int32, (c, c), 0)
    ci = lax.broadcasted_iota(jnp.int32, (c, c), 1)
    incl = (ri - ci) * sgn >= 0
    strict = (ri - ci) * sgn > 0
    ri2 = lax.broadcasted_iota(jnp.int32, (c, 2 * c), 0)
    ci2 = lax.broadcasted_iota(jnp.int32, (c, 2 * c), 1)
    cm2 = jnp.where(ci2 >= c, ci2 - c, ci2)
    mask_2 = (ri2 - cm2) * sgn >= 0
    mask_k = jnp.logical_and((ri2 - cm2) * sgn > 0, ci2 >= c)
    eye_f = (ri == ci).astype(F32)
    eye = eye_f.astype(BF16)
    off_masks = []
    for lg in range(6):
        same_2m = (ri >> (lg + 1)) == (ci >> (lg + 1))
        diff_m = (ri >> lg) != (ci >> lg)
        off_masks.append(jnp.logical_and(strict, jnp.logical_and(same_2m, diff_m)))

    lw = lw_ref[...]
    g = _dot_hi(incl.astype(F32), lw)
    e_in = jnp.exp(g)
    e_ex = jnp.exp(g - lw)
    e_inv = jnp.exp(-g)
    gtot = jnp.where(bwd, g[0:1, :], g[c - 1:c, :])
    e_rem = jnp.exp(gtot - g)
    e_tot = jnp.exp(gtot)
    a_t = nkk_ref[...] * e_ex
    r_t = r_ref[...] * e_in
    kd, bd = kd_ref[...], bd_ref[...]
    b_t = bd * e_inv
    k_t = kd * e_inv
    b_e = bd * e_rem
    k_e = kd * e_rem
    v = v_ref[...]

    heads = range(RW_H)
    sls = [slice(h * RW_N, (h + 1) * RW_N) for h in heads]
    s_old = [s_ref[h] for h in heads]
    cat = lambda p, q, sl: jnp.concatenate([p[:, sl], q[:, sl]], axis=0).astype(BF16)
    ar = [cat(a_t, r_t, sl) for sl in sls]
    bk = [cat(b_t, k_t, sl) for sl in sls]
    bke = [cat(b_e, k_e, sl) for sl in sls]
    vh = [v[:, sl] for sl in sls]
    gm = [_dot_nt(ar[h], bk[h]) for h in heads]
    a_s = [_dot_nt(ar[h], s_old[h].astype(BF16)) for h in heads]
    lab = [gm[h][:c, :c] for h in heads]
    tinv = [eye_f + jnp.where(off_masks[0], lab[h], 0.0) for h in heads]
    for off in off_masks[1:]:
        tb = [t.astype(BF16) for t in tinv]
        lt = [_dot(jnp.where(off, lab[h], 0.0).astype(BF16), tb[h]) for h in heads]
        tinv = [tinv[h] + _dot(tb[h], lt[h].astype(BF16)) for h in heads]
    vv2 = [jnp.concatenate([vh[h], vh[h]], axis=0).astype(BF16) for h in heads]
    w1 = [a_s[h][:c] + _dot(jnp.where(mask_k, gm[h][:c], 0.0).astype(BF16), vv2[h]) for h in heads]
    u = [_dot(tinv[h].astype(BF16), w1[h].astype(BF16)) for h in heads]
    uv = [jnp.concatenate([u[h], vh[h]], axis=0).astype(BF16) for h in heads]
    y = [a_s[h][c:] + _dot(jnp.where(mask_2, gm[h][c:], 0.0).astype(BF16), uv[h]) for h in heads]
    y_ref[...] = jnp.concatenate(y, axis=1)
    for h in heads:
        uvt = jnp.concatenate([_dot_nt(eye, uv[h][:c]), _dot_nt(eye, uv[h][c:])], axis=1)
        s_ref[h] = s_old[h] * e_tot[:, sls[h]] + _dot(uvt.astype(BF16), bke[h])


def _rw_scan_tables():
    rowblk, dirn, first, unit = [], [], [], []
    u = 0
    for nb, L, blk0 in ((B_P, L_P, 0), (B_S, L_S, T_P // CHUNK)):
        nc = L // CHUNK
        for b in range(nb):
            for d in range(2):
                for j in range(nc):
                    cn = j if d == 0 else nc - 1 - j
                    rowblk.append(blk0 + b * nc + cn)
                    dirn.append(d)
                    first.append(1 if j == 0 else 0)
                    unit.append(u)
                u += 1
    as_i32 = lambda a: jnp.asarray(np.asarray(a, np.int32))
    return as_i32(rowblk), as_i32(dirn), as_i32(first), as_i32(unit), u


def _rw_scan(r, v, nkk, lw, kd, bd, s0_all):
    rowblk, dirn, first, unit, n_units = _rw_scan_tables()
    n_steps = rowblk.shape[0]
    tok = pl.BlockSpec((CHUNK, D), lambda s, rb, dr, fs, un: (rb[s], 0))
    tok2 = pl.BlockSpec((None, CHUNK, D), lambda s, rb, dr, fs, un: (dr[s], rb[s], 0))
    st = pl.BlockSpec((None, RW_H, RW_N, RW_N), lambda s, rb, dr, fs, un: (un[s], 0, 0, 0))
    return pl.pallas_call(
        _rw_scan_kernel,
        grid_spec=pltpu.PrefetchScalarGridSpec(
            num_scalar_prefetch=4,
            grid=(n_steps,),
            in_specs=[tok, tok, tok, tok2, tok2, tok2, st],
            out_specs=[tok2, st],
        ),
        out_shape=[jax.ShapeDtypeStruct((2, T, D), F32),
                   jax.ShapeDtypeStruct((n_units, RW_H, RW_N, RW_N), F32)],
        compiler_params=_cparams(("arbitrary",)),
        name="rwkv_scan",
    )(rowblk, dirn, first, unit, r, v, nkk, lw, kd, bd, s0_all)


def _rw_out_kernel(y_ref, bonus_ref, sz_ref, gg_ref, gb_ref, ones_ref, w_ref, x_ref, m_ref, lng_ref, lnb_ref, o_ref):
    ones_bd = ones_ref[...]
    y = y_ref[0] + y_ref[1]
    mean = _head_sum(y, ones_bd) * (1.0 / RW_N)
    yc = y - mean
    var = _head_sum(yc * yc, ones_bd) * (1.0 / RW_N)
    yn = yc * lax.rsqrt(var + RW_GN_EPS) * gg_ref[...] + gb_ref[...]
    g = (yn + bonus_ref[...]) * sz_ref[...]
    _out_ln_tail(g, w_ref, x_ref, m_ref, lng_ref, lnb_ref, o_ref)


def _rw_out_proj_ln(y, bonus, sz, gn_g, gn_b, ones_bd, w_out, x, mod, layer, ln_g, ln_b):
    row = pl.BlockSpec((TM, D), lambda i: (i, 0))
    vec = pl.BlockSpec((1, D), lambda i: (0, 0))
    mat = pl.BlockSpec((D, D), lambda i: (0, 0))
    return pl.pallas_call(
        _rw_out_kernel,
        grid=(NT,),
        in_specs=[pl.BlockSpec((2, TM, D), lambda i: (0, i, 0)), row, row, vec, vec,
                  pl.BlockSpec((LANES, LANES), lambda i: (0, 0)), mat, row,
                  _mod_spec(layer), vec, vec],
        out_specs=row,
        out_shape=jax.ShapeDtypeStruct((T, D), F32),
        compiler_params=_cparams(("arbitrary",)),
        name="rwkv_out_proj_ln",
    )(y, bonus, sz, gn_g.reshape(1, D), gn_b.reshape(1, D), ones_bd, w_out.astype(BF16), x, mod,
      ln_g.reshape(1, D), ln_b.reshape(1, D))


def _rwkv_layer(x, mod, layer, ln_g, ln_b, state, mu, w_in, w0, w1, w2, a0, a1, a2, k_k, k_a, r_k, gn_g, gn_b, w_out):
    ones_bd = _head_ones()
    r, v, sz, nkk, bonus, lw, kd, bd = _rw_in_proj(x, mod, layer, ones_bd, mu, w_in, w0, w1, w2, a0, a1, a2,
                                                  k_k, k_a, r_k)
    n_p = B_P * 2
    s0_all = jnp.concatenate([jnp.zeros((n_p, RW_H, RW_N, RW_N), F32),
                              state.astype(F32).reshape(B_S * 2, RW_H, RW_N, RW_N)], 0)
    y, s_fin = _rw_scan(r, v, nkk, lw, kd, bd, s0_all)
    x_new = _rw_out_proj_ln(y, bonus, sz, gn_g, gn_b, ones_bd, w_out, x, mod, layer, ln_g, ln_b)
    new_state = s_fin[:n_p].reshape(B_P, 1, 2, RW_H, RW_N, RW_N)
    return x_new, new_state


def kernel(x_prompt, x_sample, cache_mla_ckv, cache_mla_kpe, state_rwkv, c, c_ctx, mod_w, mod_b, ln_g, ln_b, hy_w_in, hy_conv_w, hy_conv_b, hy_ffn_w1, hy_ffn_b1, hy_ffn_w2, hy_ffn_b2, hy_ffn_w3, hy_ffn_b3, hy_freq, hy_decay, hy_skip, hy_w_out, mla_w_in, mla_q_norm, mla_kv_norm, mla_w_q_up, mla_w_kv_up, mla_w_out, rw_mu, rw_w_in, rw_w0, rw_w1, rw_w2, rw_a0, rw_a1, rw_a2, rw_k_k, rw_k_a, rw_r_k, rw_gn_g, rw_gn_b, rw_w_out):
    x = jnp.concatenate([x_prompt.reshape(T_P, D), x_sample.reshape(T_S, D)], 0)
    cond8 = jnp.concatenate([c_ctx[None, :], c, jnp.zeros((8 - 1 - B_S, D), F32)], 0)
    mod = _modulation_table(cond8, mod_w, mod_b).reshape(DEPTH * 8, 1, 3 * D)
    tabs_p = _dft_tables(L_P)
    tabs_s = _dft_tables(L_S)
    rope = _rope_tables()
    new_ckv = new_kpe = new_state = None
    for i in range(DEPTH):
        kind, j = i % 3, i // 3
        if kind == 0:
            x = _hyena_layer(x, mod, i, tabs_p, tabs_s, ln_g[i], ln_b[i], hy_w_in[j], hy_conv_w[j], hy_conv_b[j],
                             hy_ffn_w1[j], hy_ffn_b1[j], hy_ffn_w2[j], hy_ffn_b2[j], hy_ffn_w3[j], hy_ffn_b3[j],
                             hy_freq[j], hy_decay[j], hy_skip[j], hy_w_out[j])
        elif kind == 1:
            x, new_ckv, new_kpe = _mla_layer(x, mod, i, rope, ln_g[i], ln_b[i], cache_mla_ckv[:, j],
                                             cache_mla_kpe[:, j], mla_w_in[j], mla_q_norm[j], mla_kv_norm[j],
                                             mla_w_q_up[j], mla_w_kv_up[j], mla_w_out[j])
        else:
            x, new_state = _rwkv_layer(x, mod, i, ln_g[i], ln_b[i], state_rwkv[:, j], rw_mu[j], rw_w_in[j],
                                       rw_w0[j], rw_w1[j], rw_w2[j], rw_a0[j], rw_a1[j], rw_a2[j], rw_k_k[j],
                                       rw_k_a[j], rw_r_k[j], rw_gn_g[j], rw_gn_b[j], rw_w_out[j])
    return (x[:T_P].reshape(B_P, L_P, D), x[T_P:].reshape(B_S, L_S, D), new_ckv, new_kpe, new_state)
```

```python
import functools
import math

import numpy as np
import jax
import jax.numpy as jnp
from jax import lax
from jax.experimental import pallas as pl
from jax.experimental.pallas import tpu as pltpu

F32 = jnp.float32
BF16 = jnp.bfloat16
HIGHEST = lax.Precision.HIGHEST

D = 1024
B_P, L_P = 16, 256
B_S, L_S = 2, 2048
T_P = B_P * L_P
T_S = B_S * L_S
T = T_P + T_S
PAST = 512
DEPTH = 4
DEEPNORM_ALPHA = (2.0 * DEPTH) ** 0.25
LN_EPS = 1e-5
RMS_EPS = 1e-6
HY_BANDS = 16
HY_FFN = 64
MLA_HEADS = 16
MLA_Q_RANK = 256
MLA_KV_RANK = 128
MLA_NOPE = 64
MLA_ROPE = 32
MLA_V = 64
ROPE_BASE = 10000.0
GRID_W = 64
RW_N = 64
RW_H = D // RW_N
RW_LORA = 64
RW_GN_EPS = 64e-5

TM = 256
NT_P = T_P // TM
NT_S_SEQ = L_S // TM
NT = T // TM
HALO = 8
LANES = 128
CHUNK = 64
VMEM_LIMIT = 52 * 1024 * 1024


def _cparams(sem):
    return pltpu.CompilerParams(dimension_semantics=sem, vmem_limit_bytes=VMEM_LIMIT)


def _group(i):
    return jnp.where(i < NT_P, 0, 1 + (i - NT_P) // NT_S_SEQ)


def _sigmoid(x):
    return 1.0 / (1.0 + jnp.exp(-x))


def _silu(x):
    return x * _sigmoid(x)


def _dot(a, b):
    return jnp.dot(a, b, preferred_element_type=F32)


def _dot_nt(a, b):
    return lax.dot_general(a, b, (((1,), (1,)), ((), ())), preferred_element_type=F32)


def _dot_hi(a, b):
    return jnp.dot(a, b, preferred_element_type=F32, precision=HIGHEST)


def _split(x):
    hi = x.astype(BF16)
    lo = (x - hi.astype(F32)).astype(BF16)
    return hi, lo


def _head_sum(x, ones_bd):
    hi, lo = _split(x)
    lanes = ones_bd.shape[0]
    parts = []
    for g in range(x.shape[1] // lanes):
        sl = slice(g * lanes, (g + 1) * lanes)
        parts.append(_dot(hi[:, sl], ones_bd) + _dot(lo[:, sl], ones_bd))
    return jnp.concatenate(parts, axis=1)


def _modulate(x, m):
    return x * (1.0 + m[:, D:2 * D]) + m[:, :D]


def _layer_norm_rows(y, g, b):
    mu = jnp.mean(y, axis=-1, keepdims=True)
    yc = y - mu
    var = jnp.mean(yc * yc, axis=-1, keepdims=True)
    return yc * lax.rsqrt(var + LN_EPS) * g + b


def _neighbour_rows(cur, prev_halo, next_halo, has_prev, has_next):
    rows = cur.shape[0]
    ridx = lax.broadcasted_iota(jnp.int32, cur.shape, 0)
    pr = jnp.where(has_prev, prev_halo[HALO - 1:HALO, :], 0.0)
    nx = jnp.where(has_next, next_halo[0:1, :], 0.0)
    prev = jnp.where(ridx == 0, pr, pltpu.roll(cur, 1, axis=0))
    nxt = jnp.where(ridx == rows - 1, nx, pltpu.roll(cur, rows - 1, axis=0))
    return prev, nxt


def _tile_has_neighbours(i):
    k = (i - NT_P) % NT_S_SEQ
    is_s = i >= NT_P
    return jnp.logical_and(is_s, k != 0), jnp.logical_and(is_s, k != NT_S_SEQ - 1)


def _halo_specs(width, col_of):
    r = TM // HALO
    prev = pl.BlockSpec((HALO, width), lambda i, *a: (jnp.maximum(i * r - 1, 0), col_of(i, *a)))
    nxt = pl.BlockSpec((HALO, width), lambda i, *a: (jnp.minimum((i + 1) * r, T // HALO - 1), col_of(i, *a)))
    return prev, nxt


def _mod_kernel(c_ref, w_ref, b_ref, o_ref):
    o_ref[...] = _dot_hi(_silu(c_ref[...]), w_ref[...]) + b_ref[...]


def _modulation_table(cond8, mod_w, mod_b):
    tn = 1024
    return pl.pallas_call(
        _mod_kernel,
        grid=(DEPTH, 3 * D // tn),
        in_specs=[pl.BlockSpec((8, D), lambda l, j: (0, 0)),
                  pl.BlockSpec((None, D, tn), lambda l, j: (l, 0, j)),
                  pl.BlockSpec((None, 1, tn), lambda l, j: (l, 0, j))],
        out_specs=pl.BlockSpec((None, 8, tn), lambda l, j: (l, 0, j)),
        out_shape=jax.ShapeDtypeStruct((DEPTH, 8, 3 * D), F32),
        compiler_params=_cparams(("arbitrary", "arbitrary")),
        name="modulation",
    )(cond8, mod_w, mod_b.reshape(DEPTH, 1, 3 * D))


def _mod_spec(layer):
    return pl.BlockSpec((None, 1, 3 * D), lambda i, *a: (layer * 8 + _group(i), 0, 0))


def _out_ln_tail(g, w_ref, x_ref, m_ref, lng_ref, lnb_ref, o_ref):
    mix = _dot(g.astype(BF16), w_ref[...])
    gate = m_ref[...][:, 2 * D:]
    y = DEEPNORM_ALPHA * x_ref[...] + gate * mix
    o_ref[...] = _layer_norm_rows(y, lng_ref[...], lnb_ref[...])


def _out_ln2_kernel(gp_ref, gs_ref, w_ref, x_ref, m_ref, lng_ref, lnb_ref, o_ref):
    g = jnp.where(pl.program_id(0) < NT_P, gp_ref[...], gs_ref[...])
    _out_ln_tail(g, w_ref, x_ref, m_ref, lng_ref, lnb_ref, o_ref)


def _out_proj_ln2(g_p, g_s, w_out, x, mod, layer, ln_g, ln_b):
    row = pl.BlockSpec((TM, D), lambda i: (i, 0))
    vec = pl.BlockSpec((1, D), lambda i: (0, 0))
    return pl.pallas_call(
        _out_ln2_kernel,
        grid=(NT,),
        in_specs=[pl.BlockSpec((TM, D), lambda i: (jnp.minimum(i, NT_P - 1), 0)),
                  pl.BlockSpec((TM, D), lambda i: (jnp.maximum(i - NT_P, 0), 0)),
                  pl.BlockSpec((D, D), lambda i: (0, 0)),
                  row, _mod_spec(layer), vec, vec],
        out_specs=row,
        out_shape=jax.ShapeDtypeStruct((T, D), F32),
        compiler_params=_cparams(("arbitrary",)),
        name="out_proj_ln",
    )(g_p, g_s, w_out.astype(BF16), x, mod, ln_g.reshape(1, D), ln_b.reshape(1, D))


def _hy_in_kernel(x_ref, m_ref, w_ref, o_ref):
    h = _modulate(x_ref[...], m_ref[...])
    o_ref[...] = _dot(h.astype(BF16), w_ref[...])


def _hy_in_proj(x, mod, layer, w_in):
    n = w_in.shape[1]
    return pl.pallas_call(
        _hy_in_kernel,
        grid=(NT,),
        in_specs=[pl.BlockSpec((TM, D), lambda i: (i, 0)), _mod_spec(layer),
                  pl.BlockSpec((D, n), lambda i: (0, 0))],
        out_specs=pl.BlockSpec((TM, n), lambda i: (i, 0)),
        out_shape=jax.ShapeDtypeStruct((T, n), F32),
        compiler_params=_cparams(("arbitrary",)),
        name="hyena_in_proj",
    )(x, mod, w_in.astype(BF16))


def _hy_conv3_kernel(x0_ref, x0p_ref, x0n_ref, x1_ref, x1p_ref, x1n_ref, v_ref, vp_ref, vn_ref, z_ref,
                     w0_ref, w1_ref, w2_ref, b0_ref, b1_ref, b2_ref, vv_ref, gate_ref):
    has_prev, has_next = _tile_has_neighbours(pl.program_id(0))

    def conv(c_ref, p_ref, n_ref, w_ref, b_ref):
        cur = c_ref[...]
        prev, nxt = _neighbour_rows(cur, p_ref[...], n_ref[...], has_prev, has_next)
        w = w_ref[...]
        return prev * w[0:1] + cur * w[1:2] + nxt * w[2:3] + b_ref[...]

    x0 = conv(x0_ref, x0p_ref, x0n_ref, w0_ref, b0_ref)
    x1 = conv(x1_ref, x1p_ref, x1n_ref, w1_ref, b1_ref)
    v = conv(v_ref, vp_ref, vn_ref, w2_ref, b2_ref)
    vv_ref[...] = v * x1
    gate_ref[...] = x0 * _silu(z_ref[...])


def _hy_conv3(u, conv_w, conv_b):
    ct = 512
    nct = D // ct
    in_specs = []
    for grp in range(3):
        in_specs.append(pl.BlockSpec((TM, ct), lambda i, j, grp=grp: (i, grp * nct + j)))
        in_specs.extend(_halo_specs(ct, lambda i, j, grp=grp: grp * nct + j))
    in_specs.append(pl.BlockSpec((TM, ct), lambda i, j: (i, 3 * nct + j)))
    for grp in range(3):
        in_specs.append(pl.BlockSpec((3, ct), lambda i, j, grp=grp: (0, grp * nct + j)))
    for grp in range(3):
        in_specs.append(pl.BlockSpec((1, ct), lambda i, j, grp=grp: (0, grp * nct + j)))
    out = pl.BlockSpec((TM, ct), lambda i, j: (i, j))
    args = [u] * 10 + [conv_w] * 3 + [conv_b.reshape(1, 3 * D)] * 3
    return pl.pallas_call(
        _hy_conv3_kernel,
        grid=(NT, nct),
        in_specs=in_specs,
        out_specs=[out, out],
        out_shape=[jax.ShapeDtypeStruct((T, D), F32)] * 2,
        compiler_params=_cparams(("arbitrary", "arbitrary")),
        name="hyena_conv3_gate",
    )(*args)


def _hy_filter_kernel(t_ref, bands_ref, wt_ref, wc_ref, ws_ref, b1_ref, w2_ref, b2_ref, w3_ref, b3_ref,
                      f0_ref, f1_ref, dec_ref, hs_ref, hd_ref, nyq_ref, *, L, tr):
    i = pl.program_id(0)
    ridx = lax.broadcasted_iota(jnp.int32, (tr, 1), 0) + i * tr
    pos = ridx.astype(F32)
    t = t_ref[...]
    ang = ((2.0 * math.pi / L) * pos) * bands_ref[...]
    pre = t * wt_ref[...] + _dot_hi(jnp.cos(ang), wc_ref[...]) + _dot_hi(jnp.sin(ang), ws_ref[...])
    hdn = jnp.sin(f0_ref[...] * (pre + b1_ref[...]))
    hdn = jnp.sin(f1_ref[...] * (_dot_hi(hdn, w2_ref[...]) + b2_ref[...]))
    hf = _dot_hi(hdn, w3_ref[...]) + b3_ref[...]
    h = hf * jnp.exp(-t * jnp.abs(dec_ref[...]))
    h0 = h[:, :D]
    h1 = jnp.where(ridx == 0, 0.0, h[:, D:])
    hsum = h0 + h1
    hs_ref[...] = hsum
    hd_ref[...] = h1 - h0
    alt = jnp.where((ridx & 1) == 0, 1.0, -1.0)
    part =jnp.broadcast_to(jnp.sum(alt * hsum, axis=0, keepdims=True), (8, D))

    @pl.when(i == 0)
    def _():
        nyq_ref[...] = part

    @pl.when(i > 0)
    def _():
        nyq_ref[...] += part


def _hy_filter(L, w1, b1, w2, b2, w3, b3, freq, decay):
    tr = 256
    t = jnp.linspace(0.0, 1.0, L, dtype=F32).reshape(L, 1)
    bands = jnp.linspace(1e-4, HY_BANDS - 1, HY_BANDS, dtype=F32)
    bands = jnp.pad(bands, (0, 128 - HY_BANDS)).reshape(1, 128)
    wt = w1[0:1]
    wc = jnp.pad(w1[1:1 + HY_BANDS], ((0, 128 - HY_BANDS), (0, 0)))
    ws = jnp.pad(-w1[1 + HY_BANDS:], ((0, 128 - HY_BANDS), (0, 0)))
    full = lambda shape: pl.BlockSpec(shape, lambda i: (0, 0))
    rows = pl.BlockSpec((tr, D), lambda i: (i, 0))
    return pl.pallas_call(
        functools.partial(_hy_filter_kernel, L=L, tr=tr),
        grid=(L // tr,),
        in_specs=[pl.BlockSpec((tr, 1), lambda i: (i, 0)), full((1, 128)), full((1, HY_FFN)),
                  full((128, HY_FFN)), full((128, HY_FFN)), full((1, HY_FFN)),
                  full((HY_FFN, HY_FFN)), full((1, HY_FFN)), full((HY_FFN, 2 * D)), full((1, 2 * D)),
                  full((1, HY_FFN)), full((1, HY_FFN)), full((1, 2 * D))],
        out_specs=[rows, rows, pl.BlockSpec((8, D), lambda i: (0, 0))],
        out_shape=[jax.ShapeDtypeStruct((L, D), F32), jax.ShapeDtypeStruct((L, D), F32),
                   jax.ShapeDtypeStruct((8, D), F32)],
        compiler_params=_cparams(("arbitrary",)),
        name="hyena_filter",
    )(t, bands, wt, wc, ws, b1.reshape(1, -1), w2, b2.reshape(1, -1), w3, b3.reshape(1, -1),
      freq[0:1], freq[1:2], decay.reshape(1, 2 * D))


def _dft_tables(L):
    n = 2 * L
    w = 64
    k = jnp.arange(L, dtype=jnp.int32)

    def cs(t):
        ang = ((k[:, None] * t[None, :]) % n).astype(F32) * (2.0 * math.pi / n)
        return jnp.cos(ang), jnp.sin(ang)

    ca, sa = cs(jnp.arange(L // w, dtype=jnp.int32) * w)
    cb, sb = cs(jnp.arange(w, dtype=jnp.int32))
    c = (ca[:, :, None] * cb[:, None, :] - sa[:, :, None] * sb[:, None, :]).reshape(L, L)
    s = (sa[:, :, None] * cb[:, None, :] + ca[:, :, None] * sb[:, None, :]).reshape(L, L)
    alt = jnp.where(k % 2 == 0, 1.0, -1.0).astype(F32)
    s_rows = jnp.where(k[:, None] == 0, alt[None, :], s)
    s_cols = jnp.where(k[None, :] == 0, alt[:, None], s)
    return c.astype(BF16), s_rows.astype(BF16), s_cols.astype(BF16)


def _dft_fwd_kernel(c_ref, s_ref, *refs):
    x1_ref, x2_ref = refs[0], refs[-3]
    oc_ref, os_ref = refs[-2:]
    oc_ref[...] = _dot(c_ref[...], x1_ref[...].astype(BF16))
    os_ref[...] = _dot(s_ref[...], x2_ref[...].astype(BF16))


def _dft_fwd(tabs, xs, L, nb, row_blk0):
    c, s_rows, _ = tabs
    tk = min(L, 512)
    tn = 512
    nk = L // tk
    a_spec = pl.BlockSpec((tk, L), lambda b, j, k: (k, 0))
    x_spec = pl.BlockSpec((L, tn), lambda b, j, k: (row_blk0 + b, j))
    o_spec = pl.BlockSpec((tk, tn), lambda b, j, k: (b * nk + k, j))
    return pl.pallas_call(
        _dft_fwd_kernel,
        grid=(nb, D // tn, nk),
        in_specs=[a_spec] * 2 + [x_spec] * len(xs),
        out_specs=[o_spec, o_spec],
        out_shape=[jax.ShapeDtypeStruct((nb * L, D), F32)] * 2,
        compiler_params=_cparams(("arbitrary", "arbitrary", "arbitrary")),
        name="hyena_dft_fwd",
    )(c, s_rows, *xs)


def _dft_inv_kernel(c_ref, st_ref, vc_ref, vs_ref, kre_ref, kim_ref, nyq_ref,
                    vv_ref, skip_ref, gate_ref, o_ref, *, L):
    k = pl.program_id(2)
    nk = pl.num_programs(2)
    vc, vs, kre = vc_ref[...], vs_ref[...], kre_ref[...]
    bin0 = jnp.logical_and(lax.broadcasted_iota(jnp.int32, vc.shape, 0) == 0, k == 0)
    kim = jnp.where(bin0, nyq_ref[0:1, :], kim_ref[...])
    inv_n = 1.0 / (2 * L)
    yre = jnp.where(bin0, vc * kre * inv_n, (vc * kre + vs * kim) * (2.0 * inv_n))
    yim = jnp.where(bin0, vs * kim * inv_n, (vs * kre - vc * kim) * (2.0 * inv_n))
    contrib = _dot(c_ref[...], yre.astype(BF16)) + _dot(st_ref[...], yim.astype(BF16))

    @pl.when(k == 0)
    def _():
        o_ref[...] = contrib

    @pl.when(k > 0)
    def _():
        o_ref[...] += contrib

    @pl.when(k == nk - 1)
    def _():
        o_ref[...] = (o_ref[...] + vv_ref[...] * skip_ref[...]) * gate_ref[...]


def _dft_inv(tabs, vc, vs, kre, kim, nyq, vv, skip, gate, L, nb, row_blk0):
    c, _, s_cols = tabs
    tk = min(L, 512)
    tn = 512 if L <= 512 else 256
    nk = L // tk
    a_spec = pl.BlockSpec((L, tk), lambda b, j, k: (0, k))
    v_spec = pl.BlockSpec((tk, tn), lambda b, j, k: (b * nk + k, j))
    k_spec = pl.BlockSpec((tk, tn), lambda b, j, k: (k, j))
    row_spec = pl.BlockSpec((L, tn), lambda b, j, k: (row_blk0 + b, j))
    return pl.pallas_call(
        functools.partial(_dft_inv_kernel, L=L),
        grid=(nb, D // tn, nk),
        in_specs=[a_spec] * 2 + [v_spec, v_spec, k_spec, k_spec,
                                 pl.BlockSpec((8, tn), lambda b, j, k: (0, j)),
                                 row_spec, pl.BlockSpec((1, tn), lambda b, j, k: (0, j)), row_spec],
        out_specs=pl.BlockSpec((L, tn), lambda b, j, k: (b, j)),
        out_shape=jax.ShapeDtypeStruct((nb * L, D), F32),
        compiler_params=_cparams(("arbitrary", "arbitrary", "arbitrary")),
        name="hyena_dft_inv_gate",
    )(c, s_cols, vc, vs, kre, kim, nyq, vv, skip.reshape(1, D), gate)


def _hyena_layer(x, mod, layer, tabs_p, tabs_s, ln_g, ln_b, w_in, conv_w, conv_b, w1, b1, w2, b2, w3, b3,
                 freq, decay, skip, w_out):
    u = _hy_in_proj(x, mod, layer, w_in)
    vv, gate = _hy_conv3(u, conv_w, conv_b)
    gs = []
    for L, nb, blk0, tabs in ((L_P, B_P, 0, tabs_p), (L_S, B_S, T_P // L_S, tabs_s)):
        hsum, hdiff, nyq = _hy_filter(L, w1, b1, w2, b2, w3, b3, freq, decay)
        kre, kim = _dft_fwd(tabs, (hsum, hdiff), L, 1, 0)
        vc, vs = _dft_fwd(tabs, (vv,), L, nb, blk0)
        gs.append(_dft_inv(tabs, vc, vs, kre, kim, nyq, vv, skip, gate, L, nb, blk0))
    return _out_proj_ln2(gs[0], gs[1], w_out, x, mod, layer, ln_g, ln_b)


def _rope_tables():
    rows = L_S // GRID_W
    half = MLA_ROPE // 2
    inv = ROPE_BASE ** (-jnp.arange(0, half, 2, dtype=F32) / half)
    r = jnp.repeat(jnp.arange(rows, dtype=F32), GRID_W)
    col = jnp.tile(jnp.arange(GRID_W, dtype=F32), rows)
    ar, ac = r[:, None] * inv, col[:, None] * inv
    ang = jnp.concatenate([ar, ar, ac, ac], -1)
    cos, sin = jnp.cos(ang), jnp.sin(ang)
    cos = jnp.concatenate([jnp.ones((TM, MLA_ROPE), F32), cos], 0)
    sin = jnp.concatenate([jnp.zeros((TM, MLA_ROPE), F32), sin], 0)
    return cos, sin, jnp.tile(cos, (1, MLA_HEADS)), jnp.tile(sin, (1, MLA_HEADS))


def _rope_rot_cols(w):
    idx = np.concatenate([np.arange(8, 16), np.arange(0, 8), np.arange(24, 32), np.arange(16, 24)])
    sign = np.concatenate([-np.ones(8), np.ones(8), -np.ones(8), np.ones(8)]).astype(np.float32)
    return w[..., idx] * sign


def _mla_in_kernel(x_ref, m_ref, wq_ref, wkv_ref, wkp_ref, wz_ref, qn_ref, kvn_ref, wqn_ref, wqp_ref, wqr_ref,
                   c32_ref, s32_ref, c512_ref, s512_ref,
                   qno_ref, qpe_ref, ckv_ref, kpe_ref, kpr_ref, sz_ref):
    h = _modulate(x_ref[...], m_ref[...]).astype(BF16)
    q_c = _dot(h, wq_ref[...])
    kv_c = _dot(h, wkv_ref[...])
    kp2 = _dot(h, wkp_ref[...])
    z = _dot(h, wz_ref[...])

    def rms(v, g):
        return v * lax.rsqrt(jnp.mean(v * v, axis=-1, keepdims=True) + RMS_EPS) * g

    qn = rms(q_c, qn_ref[...]).astype(BF16)
    scale = (MLA_NOPE + MLA_ROPE) ** -0.5
    qno_ref[...] = (_dot(qn, wqn_ref[...]) * scale).astype(BF16)
    q_pe = _dot(qn, wqp_ref[...]) * c512_ref[...] + _dot(qn, wqr_ref[...]) * s512_ref[...]
    qpe_ref[...] = (q_pe * scale).astype(BF16)
    ckv_ref[...] = rms(kv_c, kvn_ref[...])
    kpe = kp2[:, :MLA_ROPE]
    kpe_ref[...] = kpe
    kpr_ref[...] = kpe * c32_ref[...] + kp2[:, MLA_ROPE:] * s32_ref[...]
    sz_ref[...] = _silu(z)


def _mla_in_proj(x, mod, layer, w_in, q_norm, kv_norm, w_q_up, rope):
    c32, s32, c512, s512 = rope
    o1, o2, o3 = MLA_Q_RANK, MLA_Q_RANK + MLA_KV_RANK, MLA_Q_RANK + MLA_KV_RANK + MLA_ROPE
    wq, wkv, wkp, wz = w_in[:, :o1], w_in[:, o1:o2], w_in[:, o2:o3], w_in[:, o3:]
    wkp2 = jnp.concatenate([wkp, _rope_rot_cols(wkp)], -1)
    wqu = w_q_up.reshape(MLA_Q_RANK, MLA_HEADS, MLA_NOPE + MLA_ROPE)
    wqn = wqu[:, :, :MLA_NOPE].reshape(MLA_Q_RANK, MLA_HEADS * MLA_NOPE)
    wqp = wqu[:, :, MLA_NOPE:]
    wqr = _rope_rot_cols(wqp).reshape(MLA_Q_RANK, MLA_HEADS * MLA_ROPE)
    wqp = wqp.reshape(MLA_Q_RANK, MLA_HEADS * MLA_ROPE)
    full = lambda a: pl.BlockSpec(a.shape, lambda i: (0,) * a.ndim)
    rope_idx = lambda i: jnp.where(i < NT_P, 0, 1 + (i - NT_P) % NT_S_SEQ)
    rows = lambda n: pl.BlockSpec((TM, n), lambda i: (i, 0))
    tab = lambda n: pl.BlockSpec((TM, n), lambda i: (rope_idx(i), 0))
    weights = [wq.astype(BF16), wkv.astype(BF16), wkp2.astype(BF16), wz.astype(BF16),
               q_norm.reshape(1, -1), kv_norm.reshape(1, -1),
               wqn.astype(BF16), wqp.astype(BF16), wqr.astype(BF16)]
    npe = MLA_HEADS * MLA_ROPE
    return pl.pallas_call(
        _mla_in_kernel,
        grid=(NT,),
        in_specs=[rows(D), _mod_spec(layer)] + [full(a) for a in weights]
                 + [tab(MLA_ROPE), tab(MLA_ROPE), tab(npe), tab(npe)],
        out_specs=[rows(D), rows(npe), rows(MLA_KV_RANK), rows(MLA_ROPE), rows(MLA_ROPE), rows(D)],
        out_shape=[jax.ShapeDtypeStruct((T, D), BF16), jax.ShapeDtypeStruct((T, npe), BF16),
                   jax.ShapeDtypeStruct((T, MLA_KV_RANK), F32), jax.ShapeDtypeStruct((T, MLA_ROPE), F32),
                   jax.ShapeDtypeStruct((T, MLA_ROPE), F32), jax.ShapeDtypeStruct((T, D), F32)],
        compiler_params=_cparams(("arbitrary",)),
        name="mla_in_proj",
    )(x, mod, *weights, c32, s32, c512, s512)


ATT_HG = 4


def _mla_attn_kernel(*refs, n_cache, n_new):
    if n_cache:
        (qn_ref, qp_ref, ckv_ref, kpr_ref, cckv_ref, ckpe_ref, wk_ref, wv_ref, sz_ref,
         o_ref, k_s, v_s, kp_s) = refs
    else:
        qn_ref, qp_ref, ckv_ref, kpr_ref, wk_ref, wv_ref, sz_ref, o_ref, k_s, v_s, kp_s = refs

    @pl.when(pl.program_id(2) == 0)
    def _():
        wk, wv = wk_ref[...], wv_ref[...]
        if n_cache:
            cc = cckv_ref[...].astype(BF16)
            k_s[0:n_cache, :] = _dot(cc, wk).astype(BF16)
            v_s[0:n_cache, :] = _dot(cc, wv).astype(BF16)
            kp_s[0:n_cache, :] = ckpe_ref[...].astype(BF16)
        cn = ckv_ref[...].astype(BF16)
        k_s[n_cache:n_cache + n_new, :] = _dot(cn, wk).astype(BF16)
        v_s[n_cache:n_cache + n_new, :] = _dot(cn, wv).astype(BF16)
        kp_s[n_cache:n_cache + n_new, :] = kpr_ref[...].astype(BF16)

    kp = kp_s[...]
    k_all, v_all = k_s[...], v_s[...]
    qn_all, qp_all = qn_ref[...], qp_ref[...]
    outs = []
    for hh in range(ATT_HG):
        qn = qn_all[:, hh * MLA_NOPE:(hh + 1) * MLA_NOPE]
        qp = qp_all[:, hh * MLA_ROPE:(hh + 1) * MLA_ROPE]
        s = _dot_nt(qn, k_all[:, hh * MLA_NOPE:(hh + 1) * MLA_NOPE]) + _dot_nt(qp, kp)
        p = jnp.exp(s - jnp.max(s, axis=-1, keepdims=True))
        l = jnp.sum(p, axis=-1, keepdims=True)
        o = _dot(p.astype(BF16), v_all[:, hh * MLA_V:(hh + 1) * MLA_V])
        outs.append(o / l)
    o_ref[...] = jnp.concatenate(outs, axis=-1) * sz_ref[...]


def _mla_attention(qno, qpe, ckv, kpr, sz, w_kv_up, cache_ckv, cache_kpe, *, nb, L, row_blk0):
    wkv = w_kv_up.reshape(MLA_KV_RANK, MLA_HEADS, MLA_NOPE + MLA_V)
    wk = wkv[:, :, :MLA_NOPE].reshape(MLA_KV_RANK, D).astype(BF16)
    wv = wkv[:, :, MLA_NOPE:].reshape(MLA_KV_RANK, D).astype(BF16)
    n_cache = 0 if cache_ckv is None else cache_ckv.shape[1]
    lk = n_cache + L
    nq = L // TM
    wq = ATT_HG * MLA_NOPE
    wp = ATT_HG * MLA_ROPE
    qrow = lambda w: pl.BlockSpec((TM, w), lambda b, g, q: (row_blk0 + b * nq + q, g))
    seq = lambda w: pl.BlockSpec((L, w), lambda b, g, q: (row_blk0 * TM // L + b, 0))
    wspec = pl.BlockSpec((MLA_KV_RANK, wq), lambda b, g, q: (0, g))
    in_specs = [qrow(wq), qrow(wp), seq(MLA_KV_RANK), seq(MLA_ROPE)]
    args = [qno, qpe, ckv, kpr]
    if n_cache:
        in_specs += [pl.BlockSpec((None, n_cache, MLA_KV_RANK), lambda b, g, q: (b, 0, 0)),
                     pl.BlockSpec((None, n_cache, MLA_ROPE), lambda b, g, q: (b, 0, 0))]
        args += [cache_ckv, cache_kpe]
    in_specs += [wspec, wspec, qrow(wq)]
    args += [wk, wv, sz]
    return pl.pallas_call(
        functools.partial(_mla_attn_kernel, n_cache=n_cache, n_new=L),
        grid=(nb, MLA_HEADS // ATT_HG, nq),
        in_specs=in_specs,
        out_specs=pl.BlockSpec((TM, wq), lambda b, g, q: (b * nq + q, g)),
        out_shape=jax.ShapeDtypeStruct((nb * L, D), F32),
        scratch_shapes=[pltpu.VMEM((lk, wq), BF16), pltpu.VMEM((lk, wq), BF16), pltpu.VMEM((lk, MLA_ROPE), BF16)],
        compiler_params=_cparams(("arbitrary", "arbitrary", "arbitrary")),
        name="mla_attention",
    )(*args)


def _mla_layer(x, mod, layer, rope, ln_g, ln_b, cache_ckv, cache_kpe, w_in, q_norm, kv_norm, w_q_up, w_kv_up, w_out):
    qno, qpe, ckv, kpe, kpr, sz = _mla_in_proj(x, mod, layer, w_in, q_norm, kv_norm, w_q_up, rope)
    g_p = _mla_attention(qno, qpe, ckv, kpr, sz, w_kv_up, None, None, nb=B_P, L=L_P, row_blk0=0)
    g_s = _mla_attention(qno, qpe, ckv, kpr, sz, w_kv_up, cache_ckv, cache_kpe, nb=B_S, L=L_S, row_blk0=NT_P)
    x_new = _out_proj_ln2(g_p, g_s, w_out, x, mod, layer, ln_g, ln_b)
    new_ckv = ckv[:T_P].reshape(B_P, 1, L_P, MLA_KV_RANK)
    new_kpe = kpe[:T_P].reshape(B_P, 1, L_P, MLA_ROPE)
    return x_new, new_ckv, new_kpe


def _rw_in_kernel(x_ref, xp_ref, xn_ref, m_ref, mu_ref, wr_ref, wk_ref, wv_ref, wg_ref, w1_ref, a1_ref,
                  w2_ref, a2_ref, w0_ref, a0_ref, kk_ref, ka_ref, rk_ref, ones_ref,
                  r_ref, v_ref, sz_ref, nkk_ref, bonus_ref, lw_ref, kd_ref, bd_ref):
    i = pl.program_id(0)
    has_prev, has_next = _tile_has_neighbours(i)
    m = m_ref[...]
    h = _modulate(x_ref[...], m)
    prev, nxt = _neighbour_rows(h, _modulate(xp_ref[...], m), _modulate(xn_ref[...], m), has_prev, has_next)
    d = 0.5 * (prev + nxt) - h
    mu = mu_ref[...]

    def mix(p):
        return (h + d * mu[p:p + 1]).astype(BF16)

    r = _dot(mix(0), wr_ref[...])
    tw = jnp.tanh(_dot(mix(1), w1_ref[...])).astype(BF16)
    k = _dot(mix(2), wk_ref[...])
    v = _dot(mix(3), wv_ref[...])
    ta = _dot(mix(4), a1_ref[...]).astype(BF16)
    z = _dot(mix(5), wg_ref[...])
    r_ref[...] = r
    v_ref[...] = v
    sz_ref[...] = _silu(z)
    ones_bd = ones_ref[...]
    kk = k * kk_ref[...]
    kk = kk * lax.rsqrt(_head_sum(kk * kk, ones_bd) + 1e-12)
    nkk_ref[...] = -kk
    coef = jnp.zeros_like(r)
    for n in range(2):
        wl = w0_ref[n:n + 1, :] + _dot(tw, w2_ref[n])
        lw_ref[n] = -math.exp(-0.5) * _sigmoid(wl)
        a = _sigmoid(a0_ref[n:n + 1, :] + _dot(ta, a2_ref[n]))
        kd = k * (1.0 + (a - 1.0) * ka_ref[...])
        kd_ref[n] = kd
        bd_ref[n] = kk * a
        coef = coef + r * kd * rk_ref[...]
    bonus_ref[...] = _head_sum(coef, ones_bd) * v


def _pad_lora_up(w):
    z = jnp.zeros_like(w[0])
    return jnp.stack([jnp.concatenate([w[0], z], 0), jnp.concatenate([z, w[1]], 0)])


def _head_ones():
    h = np.arange(LANES) // RW_N
    return jnp.asarray(h[:, None] == h[None, :], dtype=BF16)


def _rw_in_proj(x, mod, layer, ones_bd, mu, w_in, w0, w1, w2, a0, a1, a2, k_k, k_a, r_k):
    mu8 = jnp.pad(mu, ((0, 2), (0, 0)))
    w1c = jnp.concatenate([w1[0], w1[1]], -1).astype(BF16)
    a1c = jnp.concatenate([a1[0], a1[1]], -1).astype(BF16)
    w2p = _pad_lora_up(w2).astype(BF16)
    a2p = _pad_lora_up(a2).astype(BF16)
    row = pl.BlockSpec((TM, D), lambda i: (i, 0))
    row2 = pl.BlockSpec((2, TM, D), lambda i: (0, i, 0))
    prev, nxt = _halo_specs(D, lambda i: 0)
    full = lambda a: pl.BlockSpec(a.shape, lambda i: (0,) * a.ndim)
    consts = [mu8, w_in[0].astype(BF16), w_in[1].astype(BF16), w_in[2].astype(BF16), w_in[3].astype(BF16),
              w1c, a1c, w2p, a2p, w0, a0, k_k.reshape(1, D), k_a.reshape(1, D), r_k.reshape(1, D), ones_bd]
    return pl.pallas_call(
        _rw_in_kernel,
        grid=(NT,),
        in_specs=[row, prev, nxt, _mod_spec(layer)] + [full(a) for a in consts],
        out_specs=[row] * 5 + [row2] * 3,
        out_shape=[jax.ShapeDtypeStruct((T, D), F32)] * 5 + [jax.ShapeDtypeStruct((2, T, D), F32)] * 3,
        compiler_params=_cparams(("arbitrary",)),
        name="rwkv_in_proj",
    )(x, x, x, mod, *consts)


def _rw_scan_kernel(rowblk_ref, dirn_ref, first_ref, unit_ref, r_ref, v_ref, nkk_ref, lw_ref, kd_ref, bd_ref,
                    s0_ref, y_ref, s_ref):
    step = pl.program_id(0)
    bwd = dirn_ref[step] == 1
    c = CHUNK

    @pl.when(first_ref[step] == 1)
    def _():
        s_ref[...] = s0_ref[...]

    sgn = jnp.where(bwd, -1, 1)
    ri = lax.broadcasted_iota(jnp.int32, (c, c), 0)
    ci = lax.broadcasted_iota(jnp.int32, (c, c), 1)
    incl = (ri - ci) * sgn >= 0
    strict = (ri - ci) * sgn > 0
    ri2 = lax.broadcasted_iota(jnp.int32, (c, 2 * c), 0)
    ci2 = lax.broadcasted_iota(jnp.int32, (c, 2 * c), 1)
    cm2 = jnp.where(ci2 >= c, ci2 - c, ci2)
    mask_2 = (ri2 - cm2) * sgn >= 0
    mask_k = jnp.logical_and((ri2 - cm2) * sgn > 0, ci2 >= c)
    eye_f = (ri == ci).astype(F32)
    eye = eye_f.astype(BF16)
    off_masks = []
    for lg in range(6):
        same_2m = (ri >> (lg + 1)) == (ci >> (lg + 1))
        diff_m = (ri >> lg) != (ci >> lg)
        off_masks.append(jnp.logical_and(strict, jnp.logical_and(same_2m, diff_m)))

    lw = lw_ref[...]
    g = _dot_hi(incl.astype(F32), lw)
    e_in = jnp.exp(g)
    e_ex = jnp.exp(g - lw)
    e_inv = jnp.exp(-g)
    gtot = jnp.where(bwd, g[0:1, :], g[c - 1:c, :])
    e_rem = jnp.exp(gtot - g)
    e_tot = jnp.exp(gtot)
    a_t = nkk_ref[...] * e_ex
    r_t = r_ref[...] * e_in
    kd, bd = kd_ref[...], bd_ref[...]
    b_t = bd * e_inv
    k_t = kd * e_inv
    b_e = bd * e_rem
    k_e = kd * e_rem
    v = v_ref[...]

    heads = range(RW_H)
    sls = [slice(h * RW_N, (h + 1) * RW_N) for h in heads]
    s_old = [s_ref[h] for h in heads]
    cat = lambda p, q, sl: jnp.concatenate([p[:, sl], q[:, sl]], axis=0).astype(BF16)
    ar = [cat(a_t, r_t, sl) for sl in sls]
    bk = [cat(b_t, k_t, sl) for sl in sls]
    bke = [cat(b_e, k_e, sl) for sl in sls]
    vh = [v[:, sl] for sl in sls]
    gm = [_dot_nt(ar[h], bk[h]) for h in heads]
    a_s = [_dot_nt(ar[h], s_old[h].astype(BF16)) for h in heads]
    lab = [gm[h][:c, :c] for h in heads]
    tinv = [eye_f + jnp.where(off_masks[0], lab[h], 0.0) for h in heads]
    for off in off_masks[1:]:
        tb = [t.astype(BF16) for t in tinv]
        lt = [_dot(jnp.where(off, lab[h], 0.0).astype(BF16), tb[h]) for h in heads]
        tinv = [tinv[h] + _dot(tb[h], lt[h].astype(BF16)) for h in heads]
    vv2 = [jnp.concatenate([vh[h], vh[h]], axis=0).astype(BF16) for h in heads]
    w1 = [a_s[h][:c] + _dot(jnp.where(mask_k, gm[h][:c], 0.0).astype(BF16), vv2[h]) for h in heads]
    u = [_dot(tinv[h].astype(BF16), w1[h].astype(BF16)) for h in heads]
    uv = [jnp.concatenate([u[h], vh[h]], axis=0).astype(BF16) for h in heads]
    y = [a_s[h][c:] + _dot(jnp.where(mask_2, gm[h][c:], 0.0).astype(BF16), uv[h]) for h in heads]
    y_ref[...] = jnp.concatenate(y, axis=1)
    for h in heads:
        uvt = jnp.concatenate([_dot_nt(eye, uv[h][:c]), _dot_nt(eye, uv[h][c:])], axis=1)
        s_ref[h] = s_old[h] * e_tot[:, sls[h]] + _dot(uvt.astype(BF16), bke[h])


def _rw_scan_tables():
    rowblk, dirn, first, unit = [], [], [], []
    u = 0
    for nb, L, blk0 in ((B_P, L_P, 0), (B_S, L_S, T_P // CHUNK)):
        nc = L // CHUNK
        for b in range(nb):
            for d in range(2):
                for j in range(nc):
                    cn = j if d == 0 else nc - 1 - j
                    rowblk.append(blk0 + b * nc + cn)
                    dirn.append(d)
                    first.append(1 if j == 0 else 0)
                    unit.append(u)
                u += 1
    as_i32 = lambda a: jnp.asarray(np.asarray(a, np.int32))
    return as_i32(rowblk), as_i32(dirn), as_i32(first), as_i32(unit), u


def _rw_scan(r, v, nkk, lw, kd, bd, s0_all):
    rowblk, dirn, first, unit, n_units = _rw_scan_tables()
    n_steps = rowblk.shape[0]
    tok = pl.BlockSpec((CHUNK, D), lambda s, rb, dr, fs, un: (rb[s], 0))
    tok2 = pl.BlockSpec((None, CHUNK, D), lambda s, rb, dr, fs, un: (dr[s], rb[s], 0))
    st = pl.BlockSpec((None, RW_H, RW_N, RW_N), lambda s, rb, dr, fs, un: (un[s], 0, 0, 0))
    return pl.pallas_call(
        _rw_scan_kernel,
        grid_spec=pltpu.PrefetchScalarGridSpec(
            num_scalar_prefetch=4,
            grid=(n_steps,),
            in_specs=[tok, tok, tok, tok2, tok2, tok2, st],
            out_specs=[tok2, st],
        ),
        out_shape=[jax.ShapeDtypeStruct((2, T, D), F32),
                   jax.ShapeDtypeStruct((n_units, RW_H, RW_N, RW_N), F32)],
        compiler_params=_cparams(("arbitrary",)),
        name="rwkv_scan",
    )(rowblk, dirn, first, unit, r, v, nkk, lw, kd, bd, s0_all)


def _rw_out_kernel(y_ref, bonus_ref, sz_ref, gg_ref, gb_ref, ones_ref, w_ref, x_ref, m_ref, lng_ref, lnb_ref, o_ref):
    ones_bd = ones_ref[...]
    y = y_ref[0] + y_ref[1]
    mean = _head_sum(y, ones_bd) * (1.0 / RW_N)
    yc = y - mean
    var = _head_sum(yc * yc, ones_bd) * (1.0 / RW_N)
    yn = yc * lax.rsqrt(var + RW_GN_EPS) * gg_ref[...] + gb_ref[...]
    g = (yn + bonus_ref[...]) * sz_ref[...]
    _out_ln_tail(g, w_ref, x_ref, m_ref, lng_ref, lnb_ref, o_ref)


def _rw_out_proj_ln(y, bonus, sz, gn_g, gn_b, ones_bd, w_out, x, mod, layer, ln_g, ln_b):
    row = pl.BlockSpec((TM, D), lambda i: (i, 0))
    vec = pl.BlockSpec((1, D), lambda i: (0, 0))
    mat = pl.BlockSpec((D, D), lambda i: (0, 0))
    return pl.pallas_call(
        _rw_out_kernel,
        grid=(NT,),
        in_specs=[pl.BlockSpec((2, TM, D), lambda i: (0, i, 0)), row, row, vec, vec,
                  pl.BlockSpec((LANES, LANES), lambda i: (0, 0)), mat, row,
                  _mod_spec(layer), vec, vec],
        out_specs=row,
        out_shape=jax.ShapeDtypeStruct((T, D), F32),
        compiler_params=_cparams(("arbitrary",)),
        name="rwkv_out_proj_ln",
    )(y, bonus, sz, gn_g.reshape(1, D), gn_b.reshape(1, D), ones_bd, w_out.astype(BF16), x, mod,
      ln_g.reshape(1, D), ln_b.reshape(1, D))


def _rwkv_layer(x, mod, layer, ln_g, ln_b, state, mu, w_in, w0, w1, w2, a0, a1, a2, k_k, k_a, r_k, gn_g, gn_b, w_out):
    ones_bd = _head_ones()
    r, v, sz, nkk, bonus, lw, kd, bd = _rw_in_proj(x, mod, layer, ones_bd, mu, w_in, w0, w1, w2, a0, a1, a2,
                                                  k_k, k_a, r_k)
    n_p = B_P * 2
    s0_all = jnp.concatenate([jnp.zeros((n_p, RW_H, RW_N, RW_N), F32),
                              state.astype(F32).reshape(B_S * 2, RW_H, RW_N, RW_N)], 0)
    y, s_fin = _rw_scan(r, v, nkk, lw, kd, bd, s0_all)
    x_new = _rw_out_proj_ln(y, bonus, sz, gn_g, gn_b, ones_bd, w_out, x, mod, layer, ln_g, ln_b)
    new_state = s_fin[:n_p].reshape(B_P, 1, 2, RW_H, RW_N, RW_N)
    return x_new, new_state


def kernel(x_prompt, x_sample, cache_mla_ckv, cache_mla_kpe, state_rwkv, c, c_ctx, mod_w, mod_b, ln_g, ln_b, hy_w_in, hy_conv_w, hy_conv_b, hy_ffn_w1, hy_ffn_b1, hy_ffn_w2, hy_ffn_b2, hy_ffn_w3, hy_ffn_b3, hy_freq, hy_decay, hy_skip, hy_w_out, mla_w_in, mla_q_norm, mla_kv_norm, mla_w_q_up, mla_w_kv_up, mla_w_out, rw_mu, rw_w_in, rw_w0, rw_w1, rw_w2, rw_a0, rw_a1, rw_a2, rw_k_k, rw_k_a, rw_r_k, rw_gn_g, rw_gn_b, rw_w_out):
    x = jnp.concatenate([x_prompt.reshape(T_P, D), x_sample.reshape(T_S, D)], 0)
    cond8 = jnp.concatenate([c_ctx[None, :], c, jnp.zeros((8 - 1 - B_S, D), F32)], 0)
    mod = _modulation_table(cond8, mod_w, mod_b).reshape(DEPTH * 8, 1, 3 * D)
    tabs_p = _dft_tables(L_P)
    tabs_s = _dft_tables(L_S)
    rope = _rope_tables()
    new_ckv = new_kpe = new_state = None
    for i in range(DEPTH):
        kind, j = i % 3, i // 3
        if kind == 0:
            x = _hyena_layer(x, mod, i, tabs_p, tabs_s, ln_g[i], ln_b[i], hy_w_in[j], hy_conv_w[j], hy_conv_b[j],
                             hy_ffn_w1[j], hy_ffn_b1[j], hy_ffn_w2[j], hy_ffn_b2[j], hy_ffn_w3[j], hy_ffn_b3[j],
                             hy_freq[j], hy_decay[j], hy_skip[j], hy_w_out[j])
        elif kind == 1:
            x, new_ckv, new_kpe = _mla_layer(x, mod, i, rope, ln_g[i], ln_b[i], cache_mla_ckv[:, j],
                                             cache_mla_kpe[:, j], mla_w_in[j], mla_q_norm[j], mla_kv_norm[j],
                                             mla_w_q_up[j], mla_w_kv_up[j], mla_w_out[j])
        else:
            x, new_state = _rwkv_layer(x, mod, i, ln_g[i], ln_b[i], state_rwkv[:, j], rw_mu[j], rw_w_in[j],
                                       rw_w0[j], rw_w1[j], rw_w2[j], rw_a0[j], rw_a1[j], rw_a2[j], rw_k_k[j],
                                       rw_k_a[j], rw_r_k[j], rw_gn_g[j], rw_gn_b[j], rw_w_out[j])
    return (x[:T_P].reshape(B_P, L_P, D), x[T_P:].reshape(B_S, L_S, D), new_ckv, new_kpe, new_state)
```

```python
import functools
import math

import numpy as np
import jax
import jax.numpy as jnp
from jax import lax
from jax.experimental import pallas as pl
from jax.experimental.pallas import tpu as pltpu

F32 = jnp.float32
BF16 = jnp.bfloat16
HIGHEST = lax.Precision.HIGHEST

D = 1024
B_P, L_P = 16, 256
B_S, L_S = 2, 2048
T_P = B_P * L_P
T_S = B_S * L_S
T = T_P + T_S
PAST = 512
DEPTH = 4
DEEPNORM_ALPHA = (2.0 * DEPTH) ** 0.25
LN_EPS = 1e-5
RMS_EPS = 1e-6
HY_BANDS = 16
HY_FFN = 64
MLA_HEADS = 16
MLA_Q_RANK = 256
MLA_KV_RANK = 128
MLA_NOPE = 64
MLA_ROPE = 32
MLA_V = 64
ROPE_BASE = 10000.0
GRID_W = 64
RW_N = 64
RW_H = D // RW_N
RW_LORA = 64
RW_GN_EPS = 64e-5

TM = 256
NT_P = T_P // TM
NT_S_SEQ = L_S // TM
NT = T // TM
HALO = 8
LANES = 128
ROWS_BF16 = 16
CHUNK = 64
VMEM_LIMIT = 52 * 1024 * 1024


def _cparams(sem):
    return pltpu.CompilerParams(dimension_semantics=sem, vmem_limit_bytes=VMEM_LIMIT)


def _group(i):
    return jnp.where(i < NT_P, 0, 1 + (i - NT_P) // NT_S_SEQ)


def _sigmoid(x):
    return 1.0 / (1.0 + jnp.exp(-x))


def _silu(x):
    return x * _sigmoid(x)


def _dot(a, b):
    return jnp.dot(a, b, preferred_element_type=F32)


def _dot_nt(a, b):
    return lax.dot_general(a, b, (((1,), (1,)), ((), ())), preferred_element_type=F32)


def _dot_hi(a, b):
    return jnp.dot(a, b, preferred_element_type=F32, precision=HIGHEST)


def _split(x):
    hi = x.astype(BF16)
    lo = (x - hi.astype(F32)).astype(BF16)
    return hi, lo


def _head_sum(x, ones_bd):
    hi, lo = _split(x)
    lanes = ones_bd.shape[0]
    parts = []
    for g in range(x.shape[1] // lanes):
        sl = slice(g * lanes, (g + 1) * lanes)
        parts.append(_dot(hi[:, sl], ones_bd) + _dot(lo[:, sl], ones_bd))
    return jnp.concatenate(parts, axis=1)


def _modulate(x, m):
    return x * (1.0 + m[:, D:2 * D]) + m[:, :D]


def _layer_norm_rows(y, g, b):
    mu = jnp.mean(y, axis=-1, keepdims=True)
    yc = y - mu
    var = jnp.mean(yc * yc, axis=-1, keepdims=True)
    return yc * lax.rsqrt(var + LN_EPS) * g + b


def _neighbour_rows(cur, prev_halo, next_halo, has_prev, has_next):
    rows = cur.shape[0]
    ridx = lax.broadcasted_iota(jnp.int32, cur.shape, 0)
    pr = jnp.where(has_prev, prev_halo[HALO - 1:HALO, :], 0.0)
    nx = jnp.where(has_next, next_halo[0:1, :], 0.0)
    prev = jnp.where(ridx == 0, pr, pltpu.roll(cur, 1, axis=0))
    nxt = jnp.where(ridx == rows - 1, nx, pltpu.roll(cur, rows - 1, axis=0))
    return prev, nxt


def _tile_has_neighbours(i):
    k = (i - NT_P) % NT_S_SEQ
    is_s = i >= NT_P
    return jnp.logical_and(is_s, k != 0), jnp.logical_and(is_s, k != NT_S_SEQ - 1)


def _halo_specs(width, col_of):
    r = TM // HALO
    prev = pl.BlockSpec((HALO, width), lambda i, *a: (jnp.maximum(i * r - 1, 0), col_of(i, *a)))
    nxt = pl.BlockSpec((HALO, width), lambda i, *a: (jnp.minimum((i + 1) * r, T // HALO - 1), col_of(i, *a)))
    return prev, nxt


def _mod_kernel(c_ref, w_ref, b_ref, o_ref):
    o_ref[...] = _dot_hi(_silu(c_ref[...]), w_ref[...]) + b_ref[...]


def _modulation_table(cond8, mod_w, mod_b):
    tn = 1024
    return pl.pallas_call(
        _mod_kernel,
        grid=(DEPTH, 3 * D // tn),
        in_specs=[pl.BlockSpec((8, D), lambda l, j: (0, 0)),
                  pl.BlockSpec((None, D, tn), lambda l, j: (l, 0, j)),
                  pl.BlockSpec((None, 1, tn), lambda l, j: (l, 0, j))],
        out_specs=pl.BlockSpec((None, 8, tn), lambda l, j: (l, 0, j)),
        out_shape=jax.ShapeDtypeStruct((DEPTH, 8, 3 * D), F32),
        compiler_params=_cparams(("arbitrary", "arbitrary")),
        name="modulation",
    )(cond8, mod_w, mod_b.reshape(DEPTH, 1, 3 * D))


def _mod_spec(layer):
    return pl.BlockSpec((None, 1, 3 * D), lambda i, *a: (layer * 8 + _group(i), 0, 0))


def _out_ln_tail(g, w_ref, x_ref, m_ref, lng_ref, lnb_ref, o_ref):
    mix = _dot(g.astype(BF16), w_ref[...])
    gate = m_ref[...][:, 2 * D:]
    y = DEEPNORM_ALPHA * x_ref[...] + gate * mix
    o_ref[...] = _layer_norm_rows(y, lng_ref[...], lnb_ref[...])


def _out_ln2_kernel(gp_ref, gs_ref, w_ref, x_ref, m_ref, lng_ref, lnb_ref, o_ref):
    g = jnp.where(pl.program_id(0) < NT_P, gp_ref[...], gs_ref[...])
    _out_ln_tail(g, w_ref, x_ref, m_ref, lng_ref, lnb_ref, o_ref)


def _out_proj_ln2(g_p, g_s, w_out, x, mod, layer, ln_g, ln_b):
    row = pl.BlockSpec((TM, D), lambda i: (i, 0))
    vec = pl.BlockSpec((1, D), lambda i: (0, 0))
    return pl.pallas_call(
        _out_ln2_kernel,
        grid=(NT,),
        in_specs=[pl.BlockSpec((TM, D), lambda i: (jnp.minimum(i, NT_P - 1), 0)),
                  pl.BlockSpec((TM, D), lambda i: (jnp.maximum(i - NT_P, 0), 0)),
                  pl.BlockSpec((D, D), lambda i: (0, 0)),
                  row, _mod_spec(layer), vec, vec],
        out_specs=row,
        out_shape=jax.ShapeDtypeStruct((T, D), F32),
        compiler_params=_cparams(("arbitrary",)),
        name="out_proj_ln",
    )(g_p, g_s, w_out.astype(BF16), x, mod, ln_g.reshape(1, D), ln_b.reshape(1, D))


def _hy_in_kernel(x_ref, m_ref, w_ref, o_ref):
    h = _modulate(x_ref[...], m_ref[...])
    o_ref[...] = _dot(h.astype(BF16), w_ref[...])


def _hy_in_proj(x, mod, layer, w_in):
    n = w_in.shape[1]
    return pl.pallas_call(
        _hy_in_kernel,
        grid=(NT,),
        in_specs=[pl.BlockSpec((TM, D), lambda i: (i, 0)), _mod_spec(layer),
                  pl.BlockSpec((D, n), lambda i: (0, 0))],
        out_specs=pl.BlockSpec((TM, n), lambda i: (i, 0)),
        out_shape=jax.ShapeDtypeStruct((T, n), F32),
        compiler_params=_cparams(("arbitrary",)),
        name="hyena_in_proj",
    )(x, mod, w_in.astype(BF16))


def _hy_conv3_kernel(x0_ref, x0p_ref, x0n_ref, x1_ref, x1p_ref, x1n_ref, v_ref, vp_ref, vn_ref, z_ref,
                     w0_ref, w1_ref, w2_ref, b0_ref, b1_ref, b2_ref, vv_ref, gate_ref):
    has_prev, has_next = _tile_has_neighbours(pl.program_id(0))

    def conv(c_ref, p_ref, n_ref, w_ref, b_ref):
        cur = c_ref[...]
        prev, nxt = _neighbour_rows(cur, p_ref[...], n_ref[...], has_prev, has_next)
        w = w_ref[...]
        return prev * w[0:1] + cur * w[1:2] + nxt * w[2:3] + b_ref[...]

    x0 = conv(x0_ref, x0p_ref, x0n_ref, w0_ref, b0_ref)
    x1 = conv(x1_ref, x1p_ref, x1n_ref, w1_ref, b1_ref)
    v = conv(v_ref, vp_ref, vn_ref, w2_ref, b2_ref)
    vv_ref[...] = v * x1
    gate_ref[...] = x0 * _silu(z_ref[...])


def _hy_conv3(u, conv_w, conv_b):
    ct = 512
    nct = D // ct
    in_specs = []
    for grp in range(3):
        in_specs.append(pl.BlockSpec((TM, ct), lambda i, j, grp=grp: (i, grp * nct + j)))
        in_specs.extend(_halo_specs(ct, lambda i, j, grp=grp: grp * nct + j))
    in_specs.append(pl.BlockSpec((TM, ct), lambda i, j: (i, 3 * nct + j)))
    for grp in range(3):
        in_specs.append(pl.BlockSpec((3, ct), lambda i, j, grp=grp: (0, grp * nct + j)))
    for grp in range(3):
        in_specs.append(pl.BlockSpec((1, ct), lambda i, j, grp=grp: (0, grp * nct + j)))
    out = pl.BlockSpec((TM, ct), lambda i, j: (i, j))
    args = [u] * 10 + [conv_w] * 3 + [conv_b.reshape(1, 3 * D)] * 3
    return pl.pallas_call(
        _hy_conv3_kernel,
        grid=(NT, nct),
        in_specs=in_specs,
        out_specs=[out, out],
        out_shape=[jax.ShapeDtypeStruct((T, D), F32)] * 2,
        compiler_params=_cparams(("arbitrary", "arbitrary")),
        name="hyena_conv3_gate",
    )(*args)


def _hy_filter_kernel(t_ref, bands_ref, wt_ref, wc_ref, ws_ref, b1_ref, w2_ref, b2_ref, w3_ref, b3_ref,
                      f0_ref, f1_ref, dec_ref, hs_ref, hd_ref, nyq_ref, *, L, tr):
    i = pl.program_id(0)
    ridx = lax.broadcasted_iota(jnp.int32, (tr, 1), 0) + i * tr
    pos = ridx.astype(F32)
    t = t_ref[...]
    ang = ((2.0 * math.pi / L) * pos) * bands_ref[...]
    pre = t * wt_ref[...] + _dot_hi(jnp.cos(ang), wc_ref[...]) + _dot_hi(jnp.sin(ang), ws_ref[...])
    hdn = jnp.sin(f0_ref[...] * (pre + b1_ref[...]))
    hdn = jnp.sin(f1_ref[...] * (_dot_hi(hdn, w2_ref[...]) + b2_ref[...]))
    hf = _dot_hi(hdn, w3_ref[...]) + b3_ref[...]
    h = hf * jnp.exp(-t * jnp.abs(dec_ref[...]))
    h0 = h[:, :D]
    h1 = jnp.where(ridx == 0, 0.0, h[:, D:])
    hsum = h0 + h1
    hs_ref[...] = hsum
    hd_ref[...] = h1 - h0
    alt = jnp.where((ridx & 1) == 0, 1.0, -1.0)
    part =jnp.broadcast_to(jnp.sum(alt * hsum, axis=0, keepdims=True), (8, D))

    @pl.when(i == 0)
    def _():
        nyq_ref[...] = part

    @pl.when(i > 0)
    def _():
        nyq_ref[...] += part


def _hy_filter(L, w1, b1, w2, b2, w3, b3, freq, decay):
    tr = 256
    t = jnp.linspace(0.0, 1.0, L, dtype=F32).reshape(L, 1)
    bands = jnp.linspace(1e-4, HY_BANDS - 1, HY_BANDS, dtype=F32)
    bands = jnp.pad(bands, (0, 128 - HY_BANDS)).reshape(1, 128)
    wt = w1[0:1]
    wc = jnp.pad(w1[1:1 + HY_BANDS], ((0, 128 - HY_BANDS), (0, 0)))
    ws = jnp.pad(-w1[1 + HY_BANDS:], ((0, 128 - HY_BANDS), (0, 0)))
    full = lambda shape: pl.BlockSpec(shape, lambda i: (0, 0))
    rows = pl.BlockSpec((tr, D), lambda i: (i, 0))
    return pl.pallas_call(
        functools.partial(_hy_filter_kernel, L=L, tr=tr),
        grid=(L // tr,),
        in_specs=[pl.BlockSpec((tr, 1), lambda i: (i, 0)), full((1, 128)), full((1, HY_FFN)),
                  full((128, HY_FFN)), full((128, HY_FFN)), full((1, HY_FFN)),
                  full((HY_FFN, HY_FFN)), full((1, HY_FFN)), full((HY_FFN, 2 * D)), full((1, 2 * D)),
                  full((1, HY_FFN)), full((1, HY_FFN)), full((1, 2 * D))],
        out_specs=[rows, rows, pl.BlockSpec((8, D), lambda i: (0, 0))],
        out_shape=[jax.ShapeDtypeStruct((L, D), F32), jax.ShapeDtypeStruct((L, D), F32),
                   jax.ShapeDtypeStruct((8, D), F32)],
        compiler_params=_cparams(("arbitrary",)),
        name="hyena_filter",
    )(t, bands, wt, wc, ws, b1.reshape(1, -1), w2, b2.reshape(1, -1), w3, b3.reshape(1, -1),
      freq[0:1], freq[1:2], decay.reshape(1, 2 * D))


def _dft_tables(L):
    n = 2 * L
    w = 64
    k = jnp.arange(L, dtype=jnp.int32)

    def cs(t):
        ang = ((k[:, None] * t[None, :]) % n).astype(F32) * (2.0 * math.pi / n)
        return jnp.cos(ang), jnp.sin(ang)

    ca, sa = cs(jnp.arange(L // w, dtype=jnp.int32) * w)
    cb, sb = cs(jnp.arange(w, dtype=jnp.int32))
    c = (ca[:, :, None] * cb[:, None, :] - sa[:, :, None] * sb[:, None, :]).reshape(L, L)
    s = (sa[:, :, None] * cb[:, None, :] + ca[:, :, None] * sb[:, None, :]).reshape(L, L)
    alt = jnp.where(k % 2 == 0, 1.0, -1.0).astype(F32)
    s_rows = jnp.where(k[:, None] == 0, alt[None, :], s)
    s_cols = jnp.where(k[None, :] == 0, alt[:, None], s)
    return c.astype(BF16), s_rows.astype(BF16), s_cols.astype(BF16)


def _dft_fwd_kernel(c_ref, s_ref, *refs):
    x1_ref, x2_ref = refs[0], refs[-3]
    oc_ref, os_ref = refs[-2:]
    oc_ref[...] = _dot(c_ref[...], x1_ref[...].astype(BF16))
    os_ref[...] = _dot(s_ref[...], x2_ref[...].astype(BF16))


def _dft_fwd(tabs, xs, L, nb, row_blk0):
    c, s_rows, _ = tabs
    tk = min(L, 512)
    tn = 512
    nk = L // tk
    a_spec = pl.BlockSpec((tk, L), lambda b, j, k: (k, 0))
    x_spec = pl.BlockSpec((L, tn), lambda b, j, k: (row_blk0 + b, j))
    o_spec = pl.BlockSpec((tk, tn), lambda b, j, k: (b * nk + k, j))
    return pl.pallas_call(
        _dft_fwd_kernel,
        grid=(nb, D // tn, nk),
        in_specs=[a_spec] * 2 + [x_spec] * len(xs),
        out_specs=[o_spec, o_spec],
        out_shape=[jax.ShapeDtypeStruct((nb * L, D), F32)] * 2,
        compiler_params=_cparams(("arbitrary", "arbitrary", "arbitrary")),
        name="hyena_dft_fwd",
    )(c, s_rows, *xs)


def _dft_inv_kernel(c_ref, st_ref, vc_ref, vs_ref, kre_ref, kim_ref, nyq_ref,
                    vv_ref, skip_ref, gate_ref, o_ref, *, L):
    k = pl.program_id(2)
    nk = pl.num_programs(2)
    vc, vs, kre = vc_ref[...], vs_ref[...], kre_ref[...]
    bin0 = jnp.logical_and(lax.broadcasted_iota(jnp.int32, vc.shape, 0) == 0, k == 0)
    kim = jnp.where(bin0, nyq_ref[0:1, :], kim_ref[...])
    inv_n = 1.0 / (2 * L)
    yre = jnp.where(bin0, vc * kre * inv_n, (vc * kre + vs * kim) * (2.0 * inv_n))
    yim = jnp.where(bin0, vs * kim * inv_n, (vs * kre - vc * kim) * (2.0 * inv_n))
    contrib = _dot(c_ref[...], yre.astype(BF16)) + _dot(st_ref[...], yim.astype(BF16))

    @pl.when(k == 0)
    def _():
        o_ref[...] = contrib

    @pl.when(k > 0)
    def _():
        o_ref[...] += contrib

    @pl.when(k == nk - 1)
    def _():
        o_ref[...] = (o_ref[...] + vv_ref[...] * skip_ref[...]) * gate_ref[...]


def _dft_inv(tabs, vc, vs, kre, kim, nyq, vv, skip, gate, L, nb, row_blk0):
    c, _, s_cols = tabs
    tk = min(L, 512)
    tn = 512 if L <= 512 else 256
    nk = L // tk
    a_spec = pl.BlockSpec((L, tk), lambda b, j, k: (0, k))
    v_spec = pl.BlockSpec((tk, tn), lambda b, j, k: (b * nk + k, j))
    k_spec = pl.BlockSpec((tk, tn), lambda b, j, k: (k, j))
    row_spec = pl.BlockSpec((L, tn), lambda b, j, k: (row_blk0 + b, j))
    return pl.pallas_call(
        functools.partial(_dft_inv_kernel, L=L),
        grid=(nb, D // tn, nk),
        in_specs=[a_spec] * 2 + [v_spec, v_spec, k_spec, k_spec,
                                 pl.BlockSpec((8, tn), lambda b, j, k: (0, j)),
                                 row_spec, pl.BlockSpec((1, tn), lambda b, j, k: (0, j)), row_spec],
        out_specs=pl.BlockSpec((L, tn), lambda b, j, k: (b, j)),
        out_shape=jax.ShapeDtypeStruct((nb * L, D), F32),
        compiler_params=_cparams(("arbitrary", "arbitrary", "arbitrary")),
        name="hyena_dft_inv_gate",
    )(c, s_cols, vc, vs, kre, kim, nyq, vv, skip.reshape(1, D), gate)


def _hyena_layer(x, mod, layer, tabs_p, tabs_s, ln_g, ln_b, w_in, conv_w, conv_b, w1, b1, w2, b2, w3, b3,
                 freq, decay, skip, w_out):
    u = _hy_in_proj(x, mod, layer, w_in)
    vv, gate = _hy_conv3(u, conv_w, conv_b)
    gs = []
    for L, nb, blk0, tabs in ((L_P, B_P, 0, tabs_p), (L_S, B_S, T_P // L_S, tabs_s)):
        hsum, hdiff, nyq = _hy_filter(L, w1, b1, w2, b2, w3, b3, freq, decay)
        kre, kim = _dft_fwd(tabs, (hsum, hdiff), L, 1, 0)
        vc, vs = _dft_fwd(tabs, (vv,), L, nb, blk0)
        gs.append(_dft_inv(tabs, vc, vs, kre, kim, nyq, vv, skip, gate, L, nb, blk0))
    return _out_proj_ln2(gs[0], gs[1], w_out, x, mod, layer, ln_g, ln_b)


def _rope_tables():
    rows = L_S // GRID_W
    half = MLA_ROPE // 2
    inv = ROPE_BASE ** (-jnp.arange(0, half, 2, dtype=F32) / half)
    r = jnp.repeat(jnp.arange(rows, dtype=F32), GRID_W)
    col = jnp.tile(jnp.arange(GRID_W, dtype=F32), rows)
    ar, ac = r[:, None] * inv, col[:, None] * inv
    ang = jnp.concatenate([ar, ar, ac, ac], -1)
    cos, sin = jnp.cos(ang), jnp.sin(ang)
    cos = jnp.concatenate([jnp.ones((TM, MLA_ROPE), F32), cos], 0)
    sin = jnp.concatenate([jnp.zeros((TM, MLA_ROPE), F32), sin], 0)
    return cos, sin, jnp.tile(cos, (1, MLA_HEADS)), jnp.tile(sin, (1, MLA_HEADS))


def _rope_rot_cols(w):
    idx = np.concatenate([np.arange(8, 16), np.arange(0, 8), np.arange(24, 32), np.arange(16, 24)])
    sign = np.concatenate([-np.ones(8), np.ones(8), -np.ones(8), np.ones(8)]).astype(np.float32)
    return w[..., idx] * sign


def _mla_in_kernel(x_ref, m_ref, wq_ref, wkv_ref, wkp_ref, wz_ref, qn_ref, kvn_ref, wqn_ref, wqp_ref, wqr_ref,
                   c32_ref, s32_ref, c512_ref, s512_ref,
                   qno_ref, qpe_ref, ckv_ref, kpe_ref, kpr_ref, sz_ref):
    h = _modulate(x_ref[...], m_ref[...]).astype(BF16)
    q_c = _dot(h, wq_ref[...])
    kv_c = _dot(h, wkv_ref[...])
    kp2 = _dot(h, wkp_ref[...])
    z = _dot(h, wz_ref[...])

    def rms(v, g):
        return v * lax.rsqrt(jnp.mean(v * v, axis=-1, keepdims=True) + RMS_EPS) * g

    qn = rms(q_c, qn_ref[...]).astype(BF16)
    scale = (MLA_NOPE + MLA_ROPE) ** -0.5
    qno_ref[...] = (_dot(qn, wqn_ref[...]) * scale).astype(BF16)
    q_pe = _dot(qn, wqp_ref[...]) * c512_ref[...] + _dot(qn, wqr_ref[...]) * s512_ref[...]
    qpe_ref[...] = (q_pe * scale).astype(BF16)
    ckv_ref[...] = rms(kv_c, kvn_ref[...])
    kpe = kp2[:, :MLA_ROPE]
    kpe_ref[...] = kpe
    kpr_ref[...] = kpe * c32_ref[...] + kp2[:, MLA_ROPE:] * s32_ref[...]
    sz_ref[...] = _silu(z)


def _mla_in_proj(x, mod, layer, w_in, q_norm, kv_norm, w_q_up, rope):
    c32, s32, c512, s512 = rope
    o1, o2, o3 = MLA_Q_RANK, MLA_Q_RANK + MLA_KV_RANK, MLA_Q_RANK + MLA_KV_RANK + MLA_ROPE
    wq, wkv, wkp, wz = w_in[:, :o1], w_in[:, o1:o2], w_in[:, o2:o3], w_in[:, o3:]
    wkp2 = jnp.concatenate([wkp, _rope_rot_cols(wkp)], -1)
    wqu = w_q_up.reshape(MLA_Q_RANK, MLA_HEADS, MLA_NOPE + MLA_ROPE)
    wqn = wqu[:, :, :MLA_NOPE].reshape(MLA_Q_RANK, MLA_HEADS * MLA_NOPE)
    wqp = wqu[:, :, MLA_NOPE:]
    wqr = _rope_rot_cols(wqp).reshape(MLA_Q_RANK, MLA_HEADS * MLA_ROPE)
    wqp = wqp.reshape(MLA_Q_RANK, MLA_HEADS * MLA_ROPE)
    full = lambda a: pl.BlockSpec(a.shape, lambda i: (0,) * a.ndim)
    rope_idx = lambda i: jnp.where(i < NT_P, 0, 1 + (i - NT_P) % NT_S_SEQ)
    rows = lambda n: pl.BlockSpec((TM, n), lambda i: (i, 0))
    tab = lambda n: pl.BlockSpec((TM, n), lambda i: (rope_idx(i), 0))
    weights = [wq.astype(BF16), wkv.astype(BF16), wkp2.astype(BF16), wz.astype(BF16),
               q_norm.reshape(1, -1), kv_norm.reshape(1, -1),
               wqn.astype(BF16), wqp.astype(BF16), wqr.astype(BF16)]
    npe = MLA_HEADS * MLA_ROPE
    return pl.pallas_call(
        _mla_in_kernel,
        grid=(NT,),
        in_specs=[rows(D), _mod_spec(layer)] + [full(a) for a in weights]
                 + [tab(MLA_ROPE), tab(MLA_ROPE), tab(npe), tab(npe)],
        out_specs=[rows(D), rows(npe), rows(MLA_KV_RANK), rows(MLA_ROPE), rows(MLA_ROPE), rows(D)],
        out_shape=[jax.ShapeDtypeStruct((T, D), BF16), jax.ShapeDtypeStruct((T, npe), BF16),
                   jax.ShapeDtypeStruct((T, MLA_KV_RANK), F32), jax.ShapeDtypeStruct((T, MLA_ROPE), F32),
                   jax.ShapeDtypeStruct((T, MLA_ROPE), F32), jax.ShapeDtypeStruct((T, D), F32)],
        compiler_params=_cparams(("arbitrary",)),
        name="mla_in_proj",
    )(x, mod, *weights, c32, s32, c512, s512)


ATT_HG = 4


def _mla_attn_kernel(*refs, n_cache, n_new):
    if n_cache:
        (qn_ref, qp_ref, ckv_ref, kpr_ref, cckv_ref, ckpe_ref, wk_ref, wv_ref, sz_ref,
         o_ref, k_s, v_s, kp_s) = refs
    else:
        qn_ref, qp_ref, ckv_ref, kpr_ref, wk_ref, wv_ref, sz_ref, o_ref, k_s, v_s, kp_s = refs

    @pl.when(pl.program_id(2) == 0)
    def _():
        wk, wv = wk_ref[...], wv_ref[...]
        if n_cache:
            cc = cckv_ref[...].astype(BF16)
            k_s[0:n_cache, :] = _dot(cc, wk).astype(BF16)
            v_s[0:n_cache, :] = _dot(cc, wv).astype(BF16)
            kp_s[0:n_cache, :] = ckpe_ref[...].astype(BF16)
        cn = ckv_ref[...].astype(BF16)
        k_s[n_cache:n_cache + n_new, :] = _dot(cn, wk).astype(BF16)
        v_s[n_cache:n_cache + n_new, :] = _dot(cn, wv).astype(BF16)
        kp_s[n_cache:n_cache + n_new, :] = kpr_ref[...].astype(BF16)

    kp = kp_s[...]
    k_all, v_all = k_s[...], v_s[...]
    qn_all, qp_all = qn_ref[...], qp_ref[...]
    outs = []
    for hh in range(ATT_HG):
        qn = qn_all[:, hh * MLA_NOPE:(hh + 1) * MLA_NOPE]
        qp = qp_all[:, hh * MLA_ROPE:(hh + 1) * MLA_ROPE]
        s = _dot_nt(qn, k_all[:, hh * MLA_NOPE:(hh + 1) * MLA_NOPE]) + _dot_nt(qp, kp)
        p = jnp.exp(s - jnp.max(s, axis=-1, keepdims=True))
        l = jnp.sum(p, axis=-1, keepdims=True)
        o = _dot(p.astype(BF16), v_all[:, hh * MLA_V:(hh + 1) * MLA_V])
        outs.append(o / l)
    o_ref[...] = jnp.concatenate(outs, axis=-1) * sz_ref[...]


def _mla_attention(qno, qpe, ckv, kpr, sz, w_kv_up, cache_ckv, cache_kpe, *, nb, L, row_blk0):
    wkv = w_kv_up.reshape(MLA_KV_RANK, MLA_HEADS, MLA_NOPE + MLA_V)
    wk = wkv[:, :, :MLA_NOPE].reshape(MLA_KV_RANK, D).astype(BF16)
    wv = wkv[:, :, MLA_NOPE:].reshape(MLA_KV_RANK, D).astype(BF16)
    n_cache = 0 if cache_ckv is None else cache_ckv.shape[1]
    lk = n_cache + L
    nq = L // TM
    wq = ATT_HG * MLA_NOPE
    wp = ATT_HG * MLA_ROPE
    qrow = lambda w: pl.BlockSpec((TM, w), lambda b, g, q: (row_blk0 + b * nq + q, g))
    seq = lambda w: pl.BlockSpec((L, w), lambda b, g, q: (row_blk0 * TM // L + b, 0))
    wspec = pl.BlockSpec((MLA_KV_RANK, wq), lambda b, g, q: (0, g))
    in_specs = [qrow(wq), qrow(wp), seq(MLA_KV_RANK), seq(MLA_ROPE)]
    args = [qno, qpe, ckv, kpr]
    if n_cache:
        in_specs += [pl.BlockSpec((None, n_cache, MLA_KV_RANK), lambda b, g, q: (b, 0, 0)),
                     pl.BlockSpec((None, n_cache, MLA_ROPE), lambda b, g, q: (b, 0, 0))]
        args += [cache_ckv, cache_kpe]
    in_specs += [wspec, wspec, qrow(wq)]
    args += [wk, wv, sz]
    return pl.pallas_call(
        functools.partial(_mla_attn_kernel, n_cache=n_cache, n_new=L),
        grid=(nb, MLA_HEADS // ATT_HG, nq),
        in_specs=in_specs,
        out_specs=pl.BlockSpec((TM, wq), lambda b, g, q: (b * nq + q, g)),
        out_shape=jax.ShapeDtypeStruct((nb * L, D), F32),
        scratch_shapes=[pltpu.VMEM((lk, wq), BF16), pltpu.VMEM((lk, wq), BF16), pltpu.VMEM((lk, MLA_ROPE), BF16)],
        compiler_params=_cparams(("arbitrary", "arbitrary", "arbitrary")),
        name="mla_attention",
    )(*args)


def _mla_layer(x, mod, layer, rope, ln_g, ln_b, cache_ckv, cache_kpe, w_in, q_norm, kv_norm, w_q_up, w_kv_up, w_out):
    qno, qpe, ckv, kpe, kpr, sz = _mla_in_proj(x, mod, layer, w_in, q_norm, kv_norm, w_q_up, rope)
    g_p = _mla_attention(qno, qpe, ckv, kpr, sz, w_kv_up, None, None, nb=B_P, L=L_P, row_blk0=0)
    g_s = _mla_attention(qno, qpe, ckv, kpr, sz, w_kv_up, cache_ckv, cache_kpe, nb=B_S, L=L_S, row_blk0=NT_P)
    x_new = _out_proj_ln2(g_p, g_s, w_out, x, mod, layer, ln_g, ln_b)
    new_ckv = ckv[:T_P].reshape(B_P, 1, L_P, MLA_KV_RANK)
    new_kpe = kpe[:T_P].reshape(B_P, 1, L_P, MLA_ROPE)
    return x_new, new_ckv, new_kpe


def _rw_in_kernel(x_ref, xp_ref, xn_ref, m_ref, mu_ref, wr_ref, wk_ref, wv_ref, wg_ref, w1_ref, a1_ref,
                  w2_ref, a2_ref, w0_ref, a0_ref, kk_ref, ka_ref, rk_ref, ones_ref,
                  r_ref, v_ref, sz_ref, nkk_ref, bonus_ref, lw_ref, kd_ref, bd_ref):
    i = pl.program_id(0)
    has_prev, has_next = _tile_has_neighbours(i)
    m = m_ref[...]
    h = _modulate(x_ref[...], m)
    prev, nxt = _neighbour_rows(h, _modulate(xp_ref[...], m), _modulate(xn_ref[...], m), has_prev, has_next)
    d = 0.5 * (prev + nxt) - h
    mu = mu_ref[...]

    def mix(p):
        return (h + d * mu[p:p + 1]).astype(BF16)

    r = _dot(mix(0), wr_ref[...])
    tw = jnp.tanh(_dot(mix(1), w1_ref[...])).astype(BF16)
    k = _dot(mix(2), wk_ref[...])
    v = _dot(mix(3), wv_ref[...])
    ta = _dot(mix(4), a1_ref[...]).astype(BF16)
    z = _dot(mix(5), wg_ref[...])
    r_ref[...] = r
    v_ref[...] = v
    sz_ref[...] = _silu(z)
    ones_bd = ones_ref[...]
    kk = k * kk_ref[...]
    kk = kk * lax.rsqrt(_head_sum(kk * kk, ones_bd) + 1e-12)
    nkk_ref[...] = -kk
    coef = jnp.zeros_like(r)
    for n in range(2):
        wl = w0_ref[n:n + 1, :] + _dot(tw, w2_ref[n])
        lw_ref[n] = -math.exp(-0.5) * _sigmoid(wl)
        a = _sigmoid(a0_ref[n:n + 1, :] + _dot(ta, a2_ref[n]))
        kd = k * (1.0 + (a - 1.0) * ka_ref[...])
        kd_ref[n] = kd
        bd_ref[n] = kk * a
        coef = coef + r * kd * rk_ref[...]
    bonus_ref[...] = _head_sum(coef, ones_bd) * v


def _pad_lora_up(w):
    z = jnp.zeros_like(w[0])
    return jnp.stack([jnp.concatenate([w[0], z], 0), jnp.concatenate([z, w[1]], 0)])


def _head_ones():
    h = np.arange(LANES) // RW_N
    return jnp.asarray(h[:, None] == h[None, :], dtype=BF16)


def _rw_in_proj(x, mod, layer, ones_bd, mu, w_in, w0, w1, w2, a0, a1, a2, k_k, k_a, r_k):
    mu8 = jnp.pad(mu, ((0, 2), (0, 0)))
    w1c = jnp.concatenate([w1[0], w1[1]], -1).astype(BF16)
    a1c = jnp.concatenate([a1[0], a1[1]], -1).astype(BF16)
    w2p = _pad_lora_up(w2).astype(BF16)
    a2p = _pad_lora_up(a2).astype(BF16)
    row = pl.BlockSpec((TM, D), lambda i: (i, 0))
    row2 = pl.BlockSpec((2, TM, D), lambda i: (0, i, 0))
    prev, nxt = _halo_specs(D, lambda i: 0)
    full = lambda a: pl.BlockSpec(a.shape, lambda i: (0,) * a.ndim)
    consts = [mu8, w_in[0].astype(BF16), w_in[1].astype(BF16), w_in[2].astype(BF16), w_in[3].astype(BF16),
              w1c, a1c, w2p, a2p, w0, a0, k_k.reshape(1, D), k_a.reshape(1, D), r_k.reshape(1, D), ones_bd]
    return pl.pallas_call(
        _rw_in_kernel,
        grid=(NT,),
        in_specs=[row, prev, nxt, _mod_spec(layer)] + [full(a) for a in consts],
        out_specs=[row] * 5 + [row2] * 3,
        out_shape=[jax.ShapeDtypeStruct((T, D), F32)] * 5 + [jax.ShapeDtypeStruct((2, T, D), F32)] * 3,
        compiler_params=_cparams(("arbitrary",)),
        name="rwkv_in_proj",
    )(x, x, x, mod, *consts)


SCAN_NB = 2
SCAN_NU = 2 * SCAN_NB


def _rw_scan_kernel(*refs):
    rb_refs = refs[:SCAN_NU]
    first_ref = refs[SCAN_NU]
    ins = refs[SCAN_NU + 1:SCAN_NU + 1 + 6 * SCAN_NU]
    s0_ref, y_ref, s_ref = refs[SCAN_NU + 1 + 6 * SCAN_NU:]
    del rb_refs
    step = pl.program_id(0)
    c = CHUNK

    @pl.when(first_ref[step] == 1)
    def _():
        s_ref[...] = s0_ref[...]

    ri = lax.broadcasted_iota(jnp.int32, (c, c), 0)
    ci = lax.broadcasted_iota(jnp.int32, (c, c), 1)
    ri2 = lax.broadcasted_iota(jnp.int32, (c, 2 * c), 0)
    ci2 = lax.broadcasted_iota(jnp.int32, (c, 2 * c), 1)
    cm2 = jnp.where(ci2 >= c, ci2 - c, ci2)
    eye_f = (ri == ci).astype(F32)
    eye = eye_f.astype(BF16)
    masks = []
    for sgn in (1, -1):
        incl = (ri - ci) * sgn >= 0
        strict = (ri - ci) * sgn > 0
        mask_2 = (ri2 - cm2) * sgn >= 0
        mask_k = jnp.logical_and((ri2 - cm2) * sgn > 0, ci2 >= c)
        off_masks = []
        for lg in range(6):
            same_2m = (ri >> (lg + 1)) == (ci >> (lg + 1))
            diff_m = (ri >> lg) != (ci >> lg)
            off_masks.append(jnp.logical_and(strict, jnp.logical_and(same_2m, diff_m)))
        masks.append((incl, mask_2, mask_k, off_masks))

    chains = [(q, h) for q in range(SCAN_NU) for h in range(RW_H)]
    ar, bk, bke, vh = {}, {}, {}, {}
    for q in range(SCAN_NU):
        r_ref, v_ref, nkk_ref, lw_ref, kd_ref, bd_ref = ins[6 * q:6 * q + 6]
        bwd = q % 2
        lw = lw_ref[...]
        g = _dot_hi(masks[bwd][0].astype(F32), lw)
        gtot = g[0:1, :] if bwd else g[c - 1:c, :]
        e_inv = jnp.exp(-g)
        e_rem = jnp.exp(gtot - g)
        e_hi, e_lo = _split(jnp.broadcast_to(jnp.exp(gtot), (ROWS_BF16, D)))
        a_t = nkk_ref[...] * jnp.exp(g - lw)
        r_t = r_ref[...] * jnp.exp(g)
        kd, bd = kd_ref[...], bd_ref[...]
        b_t, k_t, b_e, k_e = bd * e_inv, kd * e_inv, bd * e_rem, kd * e_rem
        v = v_ref[...]
        for h in range(RW_H):
            sl = slice(h * RW_N, (h + 1) * RW_N)
            cat = lambda p, r_: jnp.concatenate([p[:, sl], r_[:, sl]], axis=0).astype(BF16)
            ar[q, h], bk[q, h] = cat(a_t, r_t), cat(b_t, k_t)
            bke[q, h] = jnp.concatenate([cat(b_e, k_e), e_hi[:, sl], e_lo[:, sl]], axis=0)
            vh[q, h] = v[:, sl]
    h_old = {ch: s_ref[ch[0], ch[1]] for ch in chains}
    gm = {ch: _dot_nt(ar[ch], bk[ch]) for ch in chains}
    a_s = {ch: _dot(ar[ch], h_old[ch].astype(BF16)) for ch in chains}
    bke_t = {ch: _dot_nt(eye, bke[ch]) for ch in chains}
    lab = {ch: gm[ch][:c, :c] for ch in chains}
    offs = lambda ch: masks[ch[0] % 2][3]
    tinv = {ch: eye_f + jnp.where(offs(ch)[0], lab[ch], 0.0) for ch in chains}
    for lg in range(1, 6):
        tb = {ch: tinv[ch].astype(BF16) for ch in chains}
        lt = {ch: _dot(jnp.where(offs(ch)[lg], lab[ch], 0.0).astype(BF16), tb[ch]) for ch in chains}
        tinv = {ch: tinv[ch] + _dot(tb[ch], lt[ch].astype(BF16)) for ch in chains}
    vv2 = {ch: jnp.concatenate([vh[ch], vh[ch]], axis=0).astype(BF16) for ch in chains}
    w1 = {ch: a_s[ch][:c] + _dot(jnp.where(masks[ch[0] % 2][2], gm[ch][:c], 0.0).astype(BF16), vv2[ch])
          for ch in chains}
    u = {ch: _dot(tinv[ch].astype(BF16), w1[ch].astype(BF16)) for ch in chains}
    uv = {ch: jnp.concatenate([u[ch], vh[ch]], axis=0).astype(BF16) for ch in chains}
    y = {ch: a_s[ch][c:] + _dot(jnp.where(masks[ch[0] % 2][1], gm[ch][c:], 0.0).astype(BF16), uv[ch])
         for ch in chains}
    for q in range(SCAN_NU):
        y_ref[q] = jnp.concatenate([y[q, h] for h in range(RW_H)], axis=1)
    upd = {ch: _dot(bke_t[ch][:, :2 * c].astype(BF16), uv[ch]) for ch in chains}
    lo_col = 2 * c + ROWS_BF16
    h_new = {ch: h_old[ch] * (bke_t[ch][:, 2 * c:2 * c + 1] + bke_t[ch][:, lo_col:lo_col + 1]) + upd[ch]
             for ch in chains}
    for ch in chains:
        s_ref[ch[0], ch[1]] = h_new[ch]


def _rw_scan_tables():
    rb = [[] for _ in range(SCAN_NU)]
    first = []
    for nb, L, blk0 in ((B_P, L_P, 0), (B_S, L_S, T_P // CHUNK)):
        nc = L // CHUNK
        for grp in range(nb // SCAN_NB):
            for j in range(nc):
                for q in range(SCAN_NU):
                    b = grp * SCAN_NB + q // 2
                    cn = j if q % 2 == 0 else nc - 1 - j
                    rb[q].append(blk0 + b * nc + cn)
                first.append(1 if j == 0 else 0)
    as_i32 = lambda a: jnp.asarray(np.asarray(a, np.int32))
    return [as_i32(a) for a in rb], as_i32(first)


def _rw_scan(r, v, nkk, lw, kd, bd, s0_all):
    rb, first = _rw_scan_tables()
    n_steps = first.shape[0]
    n_groups = s0_all.shape[0] // SCAN_NU
    steps_p = (B_P // SCAN_NB) * (L_P // CHUNK)
    in_specs = []
    args = []
    for q in range(SCAN_NU):
        tok = pl.BlockSpec((CHUNK, D), lambda s, *t, q=q: (t[q][s], 0))
        tok2 = pl.BlockSpec((None, CHUNK, D), lambda s, *t, q=q: (q % 2, t[q][s], 0))
        in_specs += [tok, tok, tok, tok2, tok2, tok2]
        args += [r, v, nkk, lw, kd, bd]
    grp = lambda s: jnp.where(s < steps_p, s // (L_P // CHUNK),
                              B_P // SCAN_NB + (s - steps_p) // (L_S // CHUNK))
    st = pl.BlockSpec((None, SCAN_NU, RW_H, RW_N, RW_N), lambda s, *t: (grp(s), 0, 0, 0, 0))
    s0g = s0_all.reshape(n_groups, SCAN_NU, RW_H, RW_N, RW_N)
    y, s_fin = pl.pallas_call(
        _rw_scan_kernel,
        grid_spec=pltpu.PrefetchScalarGridSpec(
            num_scalar_prefetch=SCAN_NU + 1,
            grid=(n_steps,),
            in_specs=in_specs + [st],
            out_specs=[pl.BlockSpec((None, SCAN_NU, CHUNK, D), lambda s, *t: (s, 0, 0, 0)), st],
        ),
        out_shape=[jax.ShapeDtypeStruct((n_steps, SCAN_NU, CHUNK, D), F32),
                   jax.ShapeDtypeStruct(s0g.shape, F32)],
        compiler_params=_cparams(("arbitrary",)),
        name="rwkv_scan",
    )(*rb, first, *args, s0g)
    return y, s_fin.reshape(s0_all.shape)


def _scan_out_index(i, k, bwd):
    per_tile = TM // CHUNK
    nc_p, nc_s = L_P // CHUNK, L_S // CHUNK
    steps_p = (B_P // SCAN_NB) * nc_p
    cn_p = k
    b_s = (i - NT_P) // NT_S_SEQ
    cn_s = ((i - NT_P) % NT_S_SEQ) * per_tile + k
    step_p = (i // SCAN_NB) * nc_p + (nc_p - 1 - cn_p if bwd else cn_p)
    step_s = steps_p + (b_s // SCAN_NB) * nc_s + (nc_s - 1 - cn_s if bwd else cn_s)
    q_p = (i % SCAN_NB) * 2 + bwd
    q_s = (b_s % SCAN_NB) * 2 + bwd
    is_p = i < NT_P
    return jnp.where(is_p, step_p, step_s), jnp.where(is_p, q_p, q_s)


def _rw_out_kernel(*refs):
    per_tile = TM // CHUNK
    y_refs = refs[:2 * per_tile]
    bonus_ref, sz_ref, gg_ref, gb_ref, ones_ref, w_ref, x_ref, m_ref, lng_ref, lnb_ref, o_ref = refs[2 * per_tile:]
    ones_bd = ones_ref[...]
    y = jnp.concatenate([y_refs[k][...] + y_refs[per_tile + k][...] for k in range(per_tile)], axis=0)
    mean = _head_sum(y, ones_bd) * (1.0 / RW_N)
    yc = y - mean
    var = _head_sum(yc * yc, ones_bd) * (1.0 / RW_N)
    yn = yc * lax.rsqrt(var + RW_GN_EPS) * gg_ref[...] + gb_ref[...]
    g = (yn + bonus_ref[...]) * sz_ref[...]
    _out_ln_tail(g, w_ref, x_ref, m_ref, lng_ref, lnb_ref, o_ref)


def _rw_out_proj_ln(y, bonus, sz, gn_g, gn_b, ones_bd, w_out, x, mod, layer, ln_g, ln_b):
    row = pl.BlockSpec((TM, D), lambda i: (i, 0))
    vec = pl.BlockSpec((1, D), lambda i: (0, 0))
    mat = pl.BlockSpec((D, D), lambda i: (0, 0))
    y_specs = [pl.BlockSpec((None, None, CHUNK, D), lambda i, k=k, bwd=bwd: _scan_out_index(i, k, bwd) + (0, 0))
               for bwd in (0, 1) for k in range(TM // CHUNK)]
    return pl.pallas_call(
        _rw_out_kernel,
        grid=(NT,),
        in_specs=y_specs + [row, row, vec, vec,
                  pl.BlockSpec((LANES, LANES), lambda i: (0, 0)), mat, row,
                  _mod_spec(layer), vec, vec],
        out_specs=row,
        out_shape=jax.ShapeDtypeStruct((T, D), F32),
        compiler_params=_cparams(("arbitrary",)),
        name="rwkv_out_proj_ln",
    )(*[y] * len(y_specs), bonus, sz, gn_g.reshape(1, D), gn_b.reshape(1, D), ones_bd, w_out.astype(BF16), x, mod,
      ln_g.reshape(1, D), ln_b.reshape(1, D))


def _rwkv_layer(x, mod, layer, ln_g, ln_b, state, mu, w_in, w0, w1, w2, a0, a1, a2, k_k, k_a, r_k, gn_g, gn_b, w_out):
    ones_bd = _head_ones()
    r, v, sz, nkk, bonus, lw, kd, bd = _rw_in_proj(x, mod, layer, ones_bd, mu, w_in, w0, w1, w2, a0, a1, a2,
                                                  k_k, k_a, r_k)
    n_p = B_P * 2
    s0_all = jnp.concatenate([jnp.zeros((n_p, RW_H, RW_N, RW_N), F32),
                              state.astype(F32).reshape(B_S * 2, RW_H, RW_N, RW_N).swapaxes(-1, -2)], 0)
    y, h_fin = _rw_scan(r, v, nkk, lw, kd, bd, s0_all)
    x_new = _rw_out_proj_ln(y, bonus, sz, gn_g, gn_b, ones_bd, w_out, x, mod, layer, ln_g, ln_b)
    new_state = h_fin[:n_p].swapaxes(-1, -2).reshape(B_P, 1, 2, RW_H, RW_N, RW_N)
    return x_new, new_state


def kernel(x_prompt, x_sample, cache_mla_ckv, cache_mla_kpe, state_rwkv, c, c_ctx, mod_w, mod_b, ln_g, ln_b, hy_w_in, hy_conv_w, hy_conv_b, hy_ffn_w1, hy_ffn_b1, hy_ffn_w2, hy_ffn_b2, hy_ffn_w3, hy_ffn_b3, hy_freq, hy_decay, hy_skip, hy_w_out, mla_w_in, mla_q_norm, mla_kv_norm, mla_w_q_up, mla_w_kv_up, mla_w_out, rw_mu, rw_w_in, rw_w0, rw_w1, rw_w2, rw_a0, rw_a1, rw_a2, rw_k_k, rw_k_a, rw_r_k, rw_gn_g, rw_gn_b, rw_w_out):
    x = jnp.concatenate([x_prompt.reshape(T_P, D), x_sample.reshape(T_S, D)], 0)
    cond8 = jnp.concatenate([c_ctx[None, :], c, jnp.zeros((8 - 1 - B_S, D), F32)], 0)
    mod = _modulation_table(cond8, mod_w, mod_b).reshape(DEPTH * 8, 1, 3 * D)
    tabs_p = _dft_tables(L_P)
    tabs_s = _dft_tables(L_S)
    rope = _rope_tables()
    new_ckv = new_kpe = new_state = None
    for i in range(DEPTH):
        kind, j = i % 3, i // 3
        if kind == 0:
            x = _hyena_layer(x, mod, i, tabs_p, tabs_s, ln_g[i], ln_b[i], hy_w_in[j], hy_conv_w[j], hy_conv_b[j],
                             hy_ffn_w1[j], hy_ffn_b1[j], hy_ffn_w2[j], hy_ffn_b2[j], hy_ffn_w3[j], hy_ffn_b3[j],
                             hy_freq[j], hy_decay[j], hy_skip[j], hy_w_out[j])
        elif kind == 1:
            x, new_ckv, new_kpe = _mla_layer(x, mod, i, rope, ln_g[i], ln_b[i], cache_mla_ckv[:, j],
                                             cache_mla_kpe[:, j], mla_w_in[j], mla_q_norm[j], mla_kv_norm[j],
                                             mla_w_q_up[j], mla_w_kv_up[j], mla_w_out[j])
        else:
            x, new_state = _rwkv_layer(x, mod, i, ln_g[i], ln_b[i], state_rwkv[:, j], rw_mu[j], rw_w_in[j],
                                       rw_w0[j], rw_w1[j], rw_w2[j], rw_a0[j], rw_a1[j], rw_a2[j], rw_k_k[j],
                                       rw_k_a[j], rw_r_k[j], rw_gn_g[j], rw_gn_b[j], rw_w_out[j])
    return (x[:T_P].reshape(B_P, L_P, D), x[T_P:].reshape(B_S, L_S, D), new_ckv, new_kpe, new_state)
```

```python
import functools
import math

import numpy as np
import jax
import jax.numpy as jnp
from jax import lax
from jax.experimental import pallas as pl
from jax.experimental.pallas import tpu as pltpu

F32 = jnp.float32
BF16 = jnp.bfloat16
HIGHEST = lax.Precision.HIGHEST

D = 1024
B_P, L_P = 16, 256
B_S, L_S = 2, 2048
T_P = B_P * L_P
T_S = B_S * L_S
T = T_P + T_S
PAST = 512
DEPTH = 4
DEEPNORM_ALPHA = (2.0 * DEPTH) ** 0.25
LN_EPS = 1e-5
RMS_EPS = 1e-6
HY_BANDS = 16
HY_FFN = 64
MLA_HEADS = 16
MLA_Q_RANK = 256
MLA_KV_RANK = 128
MLA_NOPE = 64
MLA_ROPE = 32
MLA_V = 64
ROPE_BASE = 10000.0
GRID_W = 64
RW_N = 64
RW_H = D // RW_N
RW_LORA = 64
RW_GN_EPS = 64e-5

TM = 256
NT_P = T_P // TM
NT_S_SEQ = L_S // TM
NT = T // TM
HALO = 8
LANES = 128
ROWS_BF16 = 16
CHUNK = 64
VMEM_LIMIT = 52 * 1024 * 1024


def _cparams(sem):
    return pltpu.CompilerParams(dimension_semantics=sem, vmem_limit_bytes=VMEM_LIMIT)


def _group(i):
    return jnp.where(i < NT_P, 0, 1 + (i - NT_P) // NT_S_SEQ)


def _sigmoid(x):
    return 1.0 / (1.0 + jnp.exp(-x))


def _silu(x):
    return x * _sigmoid(x)


def _dot(a, b):
    return jnp.dot(a, b, preferred_element_type=F32)


def _dot_nt(a, b):
    return lax.dot_general(a, b, (((1,), (1,)), ((), ())), preferred_element_type=F32)


def _dot_hi(a, b):
    return jnp.dot(a, b, preferred_element_type=F32, precision=HIGHEST)


def _split(x):
    hi = x.astype(BF16)
    lo = (x - hi.astype(F32)).astype(BF16)
    return hi, lo


def _head_sum(x, ones_bd):
    hi, lo = _split(x)
    lanes = ones_bd.shape[0]
    parts = []
    for g in range(x.shape[1] // lanes):
        sl = slice(g * lanes, (g + 1) * lanes)
        parts.append(_dot(hi[:, sl], ones_bd) + _dot(lo[:, sl], ones_bd))
    return jnp.concatenate(parts, axis=1)


def _modulate(x, m):
    return x * (1.0 + m[:, D:2 * D]) + m[:, :D]


def _layer_norm_rows(y, g, b):
    mu = jnp.mean(y, axis=-1, keepdims=True)
    yc = y - mu
    var = jnp.mean(yc * yc, axis=-1, keepdims=True)
    return yc * lax.rsqrt(var + LN_EPS) * g + b


def _neighbour_rows(cur, prev_halo, next_halo, has_prev, has_next):
    rows = cur.shape[0]
    ridx = lax.broadcasted_iota(jnp.int32, cur.shape, 0)
    pr = jnp.where(has_prev, prev_halo[HALO - 1:HALO, :], 0.0)
    nx = jnp.where(has_next, next_halo[0:1, :], 0.0)
    prev = jnp.where(ridx == 0, pr, pltpu.roll(cur, 1, axis=0))
    nxt = jnp.where(ridx == rows - 1, nx, pltpu.roll(cur, rows - 1, axis=0))
    return prev, nxt


def _tile_has_neighbours(i):
    k = (i - NT_P) % NT_S_SEQ
    is_s = i >= NT_P
    return jnp.logical_and(is_s, k != 0), jnp.logical_and(is_s, k != NT_S_SEQ - 1)


def _halo_specs(width, col_of):
    r = TM // HALO
    prev = pl.BlockSpec((HALO, width), lambda i, *a: (jnp.maximum(i * r - 1, 0), col_of(i, *a)))
    nxt = pl.BlockSpec((HALO, width), lambda i, *a: (jnp.minimum((i + 1) * r, T // HALO - 1), col_of(i, *a)))
    return prev, nxt


def _mod_kernel(c_ref, w_ref, b_ref, o_ref):
    o_ref[...] = _dot_hi(_silu(c_ref[...]), w_ref[...]) + b_ref[...]


def _modulation_table(cond8, mod_w, mod_b):
    tn = 1024
    return pl.pallas_call(
        _mod_kernel,
        grid=(DEPTH, 3 * D // tn),
        in_specs=[pl.BlockSpec((8, D), lambda l, j: (0, 0)),
                  pl.BlockSpec((None, D, tn), lambda l, j: (l, 0, j)),
                  pl.BlockSpec((None, 1, tn), lambda l, j: (l, 0, j))],
        out_specs=pl.BlockSpec((None, 8, tn), lambda l, j: (l, 0, j)),
        out_shape=jax.ShapeDtypeStruct((DEPTH, 8, 3 * D), F32),
        compiler_params=_cparams(("arbitrary", "arbitrary")),
        name="modulation",
    )(cond8, mod_w, mod_b.reshape(DEPTH, 1, 3 * D))


def _mod_spec(layer):
    return pl.BlockSpec((None, 1, 3 * D), lambda i, *a: (layer * 8 + _group(i), 0, 0))


def _out_ln_tail(g, w_ref, x_ref, m_ref, lng_ref, lnb_ref, o_ref):
    mix = _dot(g.astype(BF16), w_ref[...])
    gate = m_ref[...][:, 2 * D:]
    y = DEEPNORM_ALPHA * x_ref[...] + gate * mix
    o_ref[...] = _layer_norm_rows(y, lng_ref[...], lnb_ref[...])


def _out_ln2_kernel(gp_ref, gs_ref, w_ref, x_ref, m_ref, lng_ref, lnb_ref, o_ref):
    g = jnp.where(pl.program_id(0) < NT_P, gp_ref[...], gs_ref[...])
    _out_ln_tail(g, w_ref, x_ref, m_ref, lng_ref, lnb_ref, o_ref)


def _out_proj_ln2(g_p, g_s, w_out, x, mod, layer, ln_g, ln_b):
    row = pl.BlockSpec((TM, D), lambda i: (i, 0))
    vec = pl.BlockSpec((1, D), lambda i: (0, 0))
    return pl.pallas_call(
        _out_ln2_kernel,
        grid=(NT,),
        in_specs=[pl.BlockSpec((TM, D), lambda i: (jnp.minimum(i, NT_P - 1), 0)),
                  pl.BlockSpec((TM, D), lambda i: (jnp.maximum(i - NT_P, 0), 0)),
                  pl.BlockSpec((D, D), lambda i: (0, 0)),
                  row, _mod_spec(layer), vec, vec],
        out_specs=row,
        out_shape=jax.ShapeDtypeStruct((T, D), F32),
        compiler_params=_cparams(("arbitrary",)),
        name="out_proj_ln",
    )(g_p, g_s, w_out.astype(BF16), x, mod, ln_g.reshape(1, D), ln_b.reshape(1, D))


def _hy_in_kernel(x_ref, m_ref, w_ref, o_ref):
    h = _modulate(x_ref[...], m_ref[...])
    o_ref[...] = _dot(h.astype(BF16), w_ref[...])


def _hy_in_proj(x, mod, layer, w_in):
    n = w_in.shape[1]
    return pl.pallas_call(
        _hy_in_kernel,
        grid=(NT,),
        in_specs=[pl.BlockSpec((TM, D), lambda i: (i, 0)), _mod_spec(layer),
                  pl.BlockSpec((D, n), lambda i: (0, 0))],
        out_specs=pl.BlockSpec((TM, n), lambda i: (i, 0)),
        out_shape=jax.ShapeDtypeStruct((T, n), F32),
        compiler_params=_cparams(("arbitrary",)),
        name="hyena_in_proj",
    )(x, mod, w_in.astype(BF16))


def _hy_conv3_kernel(x0_ref, x0p_ref, x0n_ref, x1_ref, x1p_ref, x1n_ref, v_ref, vp_ref, vn_ref, z_ref,
                     w0_ref, w1_ref, w2_ref, b0_ref, b1_ref, b2_ref, vv_ref, gate_ref):
    has_prev, has_next = _tile_has_neighbours(pl.program_id(0))

    def conv(c_ref, p_ref, n_ref, w_ref, b_ref):
        cur = c_ref[...]
        prev, nxt = _neighbour_rows(cur, p_ref[...], n_ref[...], has_prev, has_next)
        w = w_ref[...]
        return prev * w[0:1] + cur * w[1:2] + nxt * w[2:3] + b_ref[...]

    x0 = conv(x0_ref, x0p_ref, x0n_ref, w0_ref, b0_ref)
    x1 = conv(x1_ref, x1p_ref, x1n_ref, w1_ref, b1_ref)
    v = conv(v_ref, vp_ref, vn_ref, w2_ref, b2_ref)
    vv_ref[...] = v * x1
    gate_ref[...] = x0 * _silu(z_ref[...])


def _hy_conv3(u, conv_w, conv_b):
    ct = 512
    nct = D // ct
    in_specs = []
    for grp in range(3):
        in_specs.append(pl.BlockSpec((TM, ct), lambda i, j, grp=grp: (i, grp * nct + j)))
        in_specs.extend(_halo_specs(ct, lambda i, j, grp=grp: grp * nct + j))
    in_specs.append(pl.BlockSpec((TM, ct), lambda i, j: (i, 3 * nct + j)))
    for grp in range(3):
        in_specs.append(pl.BlockSpec((3, ct), lambda i, j, grp=grp: (0, grp * nct + j)))
    for grp in range(3):
        in_specs.append(pl.BlockSpec((1, ct), lambda i, j, grp=grp: (0, grp * nct + j)))
    out = pl.BlockSpec((TM, ct), lambda i, j: (i, j))
    args = [u] * 10 + [conv_w] * 3 + [conv_b.reshape(1, 3 * D)] * 3
    return pl.pallas_call(
        _hy_conv3_kernel,
        grid=(NT, nct),
        in_specs=in_specs,
        out_specs=[out, out],
        out_shape=[jax.ShapeDtypeStruct((T, D), F32)] * 2,
        compiler_params=_cparams(("arbitrary", "arbitrary")),
        name="hyena_conv3_gate",
    )(*args)


def _hy_filter_kernel(t_ref, bands_ref, wt_ref, wc_ref, ws_ref, b1_ref, w2_ref, b2_ref, w3_ref, b3_ref,
                      f0_ref, f1_ref, dec_ref, hs_ref, hd_ref, nyq_ref, *, L, tr):
    i = pl.program_id(0)
    ridx = lax.broadcasted_iota(jnp.int32, (tr, 1), 0) + i * tr
    pos = ridx.astype(F32)
    t = t_ref[...]
    ang = ((2.0 * math.pi / L) * pos) * bands_ref[...]
    pre = t * wt_ref[...] + _dot_hi(jnp.cos(ang), wc_ref[...]) + _dot_hi(jnp.sin(ang), ws_ref[...])
    hdn = jnp.sin(f0_ref[...] * (pre + b1_ref[...]))
    hdn = jnp.sin(f1_ref[...] * (_dot_hi(hdn, w2_ref[...]) + b2_ref[...]))
    hf = _dot_hi(hdn, w3_ref[...]) + b3_ref[...]
    h = hf * jnp.exp(-t * jnp.abs(dec_ref[...]))
    h0 = h[:, :D]
    h1 = jnp.where(ridx == 0, 0.0, h[:, D:])
    hsum = h0 + h1
    hs_ref[...] = hsum
    hd_ref[...] = h1 - h0
    alt = jnp.where((ridx & 1) == 0, 1.0, -1.0)
    part =jnp.broadcast_to(jnp.sum(alt * hsum, axis=0, keepdims=True), (8, D))

    @pl.when(i == 0)
    def _():
        nyq_ref[...] = part

    @pl.when(i > 0)
    def _():
        nyq_ref[...] += part


def _hy_filter(L, w1, b1, w2, b2, w3, b3, freq, decay):
    tr = 256
    t = jnp.linspace(0.0, 1.0, L, dtype=F32).reshape(L, 1)
    bands = jnp.linspace(1e-4, HY_BANDS - 1, HY_BANDS, dtype=F32)
    bands = jnp.pad(bands, (0, 128 - HY_BANDS)).reshape(1, 128)
    wt = w1[0:1]
    wc = jnp.pad(w1[1:1 + HY_BANDS], ((0, 128 - HY_BANDS), (0, 0)))
    ws = jnp.pad(-w1[1 + HY_BANDS:], ((0, 128 - HY_BANDS), (0, 0)))
    full = lambda shape: pl.BlockSpec(shape, lambda i: (0, 0))
    rows = pl.BlockSpec((tr, D), lambda i: (i, 0))
    return pl.pallas_call(
        functools.partial(_hy_filter_kernel, L=L, tr=tr),
        grid=(L // tr,),
        in_specs=[pl.BlockSpec((tr, 1), lambda i: (i, 0)), full((1, 128)), full((1, HY_FFN)),
                  full((128, HY_FFN)), full((128, HY_FFN)), full((1, HY_FFN)),
                  full((HY_FFN, HY_FFN)), full((1, HY_FFN)), full((HY_FFN, 2 * D)), full((1, 2 * D)),
                  full((1, HY_FFN)), full((1, HY_FFN)), full((1, 2 * D))],
        out_specs=[rows, rows, pl.BlockSpec((8, D), lambda i: (0, 0))],
        out_shape=[jax.ShapeDtypeStruct((L, D), F32), jax.ShapeDtypeStruct((L, D), F32),
                   jax.ShapeDtypeStruct((8, D), F32)],
        compiler_params=_cparams(("arbitrary",)),
        name="hyena_filter",
    )(t, bands, wt, wc, ws, b1.reshape(1, -1), w2, b2.reshape(1, -1), w3, b3.reshape(1, -1),
      freq[0:1], freq[1:2], decay.reshape(1, 2 * D))


def _dft_tables(L):
    n = 2 * L
    w = 64
    k = jnp.arange(L, dtype=jnp.int32)

    def cs(t):
        ang = ((k[:, None] * t[None, :]) % n).astype(F32) * (2.0 * math.pi / n)
        return jnp.cos(ang), jnp.sin(ang)

    ca, sa = cs(jnp.arange(L // w, dtype=jnp.int32) * w)
    cb, sb = cs(jnp.arange(w, dtype=jnp.int32))
    c = (ca[:, :, None] * cb[:, None, :] - sa[:, :, None] * sb[:, None, :]).reshape(L, L)
    s = (sa[:, :, None] * cb[:, None, :] + ca[:, :, None] * sb[:, None, :]).reshape(L, L)
    alt = jnp.where(k % 2 == 0, 1.0, -1.0).astype(F32)
    s_rows = jnp.where(k[:, None] == 0, alt[None, :], s)
    s_cols = jnp.where(k[None, :] == 0, alt[:, None], s)
    return c.astype(BF16), s_rows.astype(BF16), s_cols.astype(BF16)


def _dft_fwd_kernel(c_ref, s_ref, *refs):
    x1_ref, x2_ref = refs[0], refs[-3]
    oc_ref, os_ref = refs[-2:]
    oc_ref[...] = _dot(c_ref[...], x1_ref[...].astype(BF16))
    os_ref[...] = _dot(s_ref[...], x2_ref[...].astype(BF16))


def _dft_fwd(tabs, xs, L, nb, row_blk0):
    c, s_rows, _ = tabs
    tk = min(L, 512)
    tn = 512
    nk = L // tk
    a_spec = pl.BlockSpec((tk, L), lambda b, j, k: (k, 0))
    x_spec = pl.BlockSpec((L, tn), lambda b, j, k: (row_blk0 + b, j))
    o_spec = pl.BlockSpec((tk, tn), lambda b, j, k: (b * nk + k, j))
    return pl.pallas_call(
        _dft_fwd_kernel,
        grid=(nb, D // tn, nk),
        in_specs=[a_spec] * 2 + [x_spec] * len(xs),
        out_specs=[o_spec, o_spec],
        out_shape=[jax.ShapeDtypeStruct((nb * L, D), F32)] * 2,
        compiler_params=_cparams(("arbitrary", "arbitrary", "arbitrary")),
        name="hyena_dft_fwd",
    )(c, s_rows, *xs)


def _dft_inv_kernel(c_ref, st_ref, vc_ref, vs_ref, kre_ref, kim_ref, nyq_ref,
                    vv_ref, skip_ref, gate_ref, o_ref, *, L):
    k = pl.program_id(2)
    nk = pl.num_programs(2)
    vc, vs, kre = vc_ref[...], vs_ref[...], kre_ref[...]
    bin0 = jnp.logical_and(lax.broadcasted_iota(jnp.int32, vc.shape, 0) == 0, k == 0)
    kim = jnp.where(bin0, nyq_ref[0:1, :], kim_ref[...])
    inv_n = 1.0 / (2 * L)
    yre = jnp.where(bin0, vc * kre * inv_n, (vc * kre + vs * kim) * (2.0 * inv_n))
    yim = jnp.where(bin0, vs * kim * inv_n, (vs * kre - vc * kim) * (2.0 * inv_n))
    contrib = _dot(c_ref[...], yre.astype(BF16)) + _dot(st_ref[...], yim.astype(BF16))

    @pl.when(k == 0)
    def _():
        o_ref[...] = contrib

    @pl.when(k > 0)
    def _():
        o_ref[...] += contrib

    @pl.when(k == nk - 1)
    def _():
        o_ref[...] = (o_ref[...] + vv_ref[...] * skip_ref[...]) * gate_ref[...]


def _dft_inv(tabs, vc, vs, kre, kim, nyq, vv, skip, gate, L, nb, row_blk0):
    c, _, s_cols = tabs
    tk = min(L, 512)
    tn = 512 if L <= 512 else 256
    nk = L // tk
    a_spec = pl.BlockSpec((L, tk), lambda b, j, k: (0, k))
    v_spec = pl.BlockSpec((tk, tn), lambda b, j, k: (b * nk + k, j))
    k_spec = pl.BlockSpec((tk, tn), lambda b, j, k: (k, j))
    row_spec = pl.BlockSpec((L, tn), lambda b, j, k: (row_blk0 + b, j))
    return pl.pallas_call(
        functools.partial(_dft_inv_kernel, L=L),
        grid=(nb, D // tn, nk),
        in_specs=[a_spec] * 2 + [v_spec, v_spec, k_spec, k_spec,
                                 pl.BlockSpec((8, tn), lambda b, j, k: (0, j)),
                                 row_spec, pl.BlockSpec((1, tn), lambda b, j, k: (0, j)), row_spec],
        out_specs=pl.BlockSpec((L, tn), lambda b, j, k: (b, j)),
        out_shape=jax.ShapeDtypeStruct((nb * L, D), F32),
        compiler_params=_cparams(("arbitrary", "arbitrary", "arbitrary")),
        name="hyena_dft_inv_gate",
    )(c, s_cols, vc, vs, kre, kim, nyq, vv, skip.reshape(1, D), gate)


def _hyena_layer(x, mod, layer, tabs_p, tabs_s, ln_g, ln_b, w_in, conv_w, conv_b, w1, b1, w2, b2, w3, b3,
                 freq, decay, skip, w_out):
    u = _hy_in_proj(x, mod, layer, w_in)
    vv, gate = _hy_conv3(u, conv_w, conv_b)
    gs = []
    for L, nb, blk0, tabs in ((L_P, B_P, 0, tabs_p), (L_S, B_S, T_P // L_S, tabs_s)):
        hsum, hdiff, nyq = _hy_filter(L, w1, b1, w2, b2, w3, b3, freq, decay)
        kre, kim = _dft_fwd(tabs, (hsum, hdiff), L, 1, 0)
        vc, vs = _dft_fwd(tabs, (vv,), L, nb, blk0)
        gs.append(_dft_inv(tabs, vc, vs, kre, kim, nyq, vv, skip, gate, L, nb, blk0))
    return _out_proj_ln2(gs[0], gs[1], w_out, x, mod, layer, ln_g, ln_b)


def _rope_tables():
    rows = L_S // GRID_W
    half = MLA_ROPE // 2
    inv = ROPE_BASE ** (-jnp.arange(0, half, 2, dtype=F32) / half)
    r = jnp.repeat(jnp.arange(rows, dtype=F32), GRID_W)
    col = jnp.tile(jnp.arange(GRID_W, dtype=F32), rows)
    ar, ac = r[:, None] * inv, col[:, None] * inv
    ang = jnp.concatenate([ar, ar, ac, ac], -1)
    cos, sin = jnp.cos(ang), jnp.sin(ang)
    cos = jnp.concatenate([jnp.ones((TM, MLA_ROPE), F32), cos], 0)
    sin = jnp.concatenate([jnp.zeros((TM, MLA_ROPE), F32), sin], 0)
    return cos, sin, jnp.tile(cos, (1, MLA_HEADS)), jnp.tile(sin, (1, MLA_HEADS))


def _rope_rot_cols(w):
    idx = np.concatenate([np.arange(8, 16), np.arange(0, 8), np.arange(24, 32), np.arange(16, 24)])
    sign = np.concatenate([-np.ones(8), np.ones(8), -np.ones(8), np.ones(8)]).astype(np.float32)
    return w[..., idx] * sign


def _mla_in_kernel(x_ref, m_ref, wq_ref, wkv_ref, wkp_ref, wz_ref, qn_ref, kvn_ref, wqn_ref, wqp_ref, wqr_ref,
                   c32_ref, s32_ref, c512_ref, s512_ref,
                   qno_ref, qpe_ref, ckv_ref, kpe_ref, kpr_ref, sz_ref):
    h = _modulate(x_ref[...], m_ref[...]).astype(BF16)
    q_c = _dot(h, wq_ref[...])
    kv_c = _dot(h, wkv_ref[...])
    kp2 = _dot(h, wkp_ref[...])
    z = _dot(h, wz_ref[...])

    def rms(v, g):
        return v * lax.rsqrt(jnp.mean(v * v, axis=-1, keepdims=True) + RMS_EPS) * g

    qn = rms(q_c, qn_ref[...]).astype(BF16)
    scale = (MLA_NOPE + MLA_ROPE) ** -0.5
    qno_ref[...] = (_dot(qn, wqn_ref[...]) * scale).astype(BF16)
    q_pe = _dot(qn, wqp_ref[...]) * c512_ref[...] + _dot(qn, wqr_ref[...]) * s512_ref[...]
    qpe_ref[...] = (q_pe * scale).astype(BF16)
    ckv_ref[...] = rms(kv_c, kvn_ref[...])
    kpe = kp2[:, :MLA_ROPE]
    kpe_ref[...] = kpe
    kpr_ref[...] = kpe * c32_ref[...] + kp2[:, MLA_ROPE:] * s32_ref[...]
    sz_ref[...] = _silu(z)


def _mla_in_proj(x, mod, layer, w_in, q_norm, kv_norm, w_q_up, rope):
    c32, s32, c512, s512 = rope
    o1, o2, o3 = MLA_Q_RANK, MLA_Q_RANK + MLA_KV_RANK, MLA_Q_RANK + MLA_KV_RANK + MLA_ROPE
    wq, wkv, wkp, wz = w_in[:, :o1], w_in[:, o1:o2], w_in[:, o2:o3], w_in[:, o3:]
    wkp2 = jnp.concatenate([wkp, _rope_rot_cols(wkp)], -1)
    wqu = w_q_up.reshape(MLA_Q_RANK, MLA_HEADS, MLA_NOPE + MLA_ROPE)
    wqn = wqu[:, :, :MLA_NOPE].reshape(MLA_Q_RANK, MLA_HEADS * MLA_NOPE)
    wqp = wqu[:, :, MLA_NOPE:]
    wqr = _rope_rot_cols(wqp).reshape(MLA_Q_RANK, MLA_HEADS * MLA_ROPE)
    wqp = wqp.reshape(MLA_Q_RANK, MLA_HEADS * MLA_ROPE)
    full = lambda a: pl.BlockSpec(a.shape, lambda i: (0,) * a.ndim)
    rope_idx = lambda i: jnp.where(i < NT_P, 0, 1 + (i - NT_P) % NT_S_SEQ)
    rows = lambda n: pl.BlockSpec((TM, n), lambda i: (i, 0))
    tab = lambda n: pl.BlockSpec((TM, n), lambda i: (rope_idx(i), 0))
    weights = [wq.astype(BF16), wkv.astype(BF16), wkp2.astype(BF16), wz.astype(BF16),
               q_norm.reshape(1, -1), kv_norm.reshape(1, -1),
               wqn.astype(BF16), wqp.astype(BF16), wqr.astype(BF16)]
    npe = MLA_HEADS * MLA_ROPE
    return pl.pallas_call(
        _mla_in_kernel,
        grid=(NT,),
        in_specs=[rows(D), _mod_spec(layer)] + [full(a) for a in weights]
                 + [tab(MLA_ROPE), tab(MLA_ROPE), tab(npe), tab(npe)],
        out_specs=[rows(D), rows(npe), rows(MLA_KV_RANK), rows(MLA_ROPE), rows(MLA_ROPE), rows(D)],
        out_shape=[jax.ShapeDtypeStruct((T, D), BF16), jax.ShapeDtypeStruct((T, npe), BF16),
                   jax.ShapeDtypeStruct((T, MLA_KV_RANK), F32), jax.ShapeDtypeStruct((T, MLA_ROPE), F32),
                   jax.ShapeDtypeStruct((T, MLA_ROPE), F32), jax.ShapeDtypeStruct((T, D), F32)],
        compiler_params=_cparams(("arbitrary",)),
        name="mla_in_proj",
    )(x, mod, *weights, c32, s32, c512, s512)


def _mla_attn_kernel(*refs, n_cache, hg):
    if n_cache:
        (qn_ref, qp_ref, ckv_ref, kpr_ref, cckv_ref, ckpe_ref, wk_ref, wvt_ref, sz_ref,
         o_ref, kcat_s, vt_s) = refs
    else:
        qn_ref, qp_ref, ckv_ref, kpr_ref, wk_ref, wvt_ref, sz_ref, o_ref, kcat_s, vt_s = refs

    @pl.when(pl.program_id(2) == 0)
    def _():
        cc = ckv_ref[...].astype(BF16)
        kp = kpr_ref[...].astype(BF16)
        if n_cache:
            cc = jnp.concatenate([cckv_ref[...].astype(BF16), cc], axis=0)
            kp = jnp.concatenate([ckpe_ref[...].astype(BF16), kp], axis=0)
        kn = _dot(cc, wk_ref[...]).astype(BF16)
        for hh in range(hg):
            kcat_s[hh] = jnp.concatenate([kn[:, hh * MLA_NOPE:(hh + 1) * MLA_NOPE], kp], axis=1)
            vt_s[hh] = _dot_nt(wvt_ref[hh], cc).astype(BF16)

    qn_all, qp_all = qn_ref[...], qp_ref[...]

    def scores(hh):
        qcat = jnp.concatenate([qn_all[:, hh * MLA_NOPE:(hh + 1) * MLA_NOPE],
                                qp_all[:, hh * MLA_ROPE:(hh + 1) * MLA_ROPE]], axis=1)
        return _dot_nt(kcat_s[hh], qcat)

    outs = []
    ahead = 2
    pending = [scores(hh) for hh in range(min(ahead, hg))]
    for hh in range(hg):
        s = pending.pop(0)
        if hh + ahead < hg:
            pending.append(scores(hh + ahead))
        p = jnp.exp(s - jnp.max(s, axis=0, keepdims=True))
        l = jnp.sum(p, axis=0, keepdims=True)
        outs.append(_dot(vt_s[hh], p.astype(BF16)) / l)
    o_ref[...] = jnp.concatenate(outs, axis=0).T * sz_ref[...]


def _mla_attention(qno, qpe, ckv, kpr, sz, w_kv_up, cache_ckv, cache_kpe, *, nb, L, row_blk0, hg):
    wkv = w_kv_up.reshape(MLA_KV_RANK, MLA_HEADS, MLA_NOPE + MLA_V)
    wk = wkv[:, :, :MLA_NOPE].reshape(MLA_KV_RANK, D).astype(BF16)
    wvt = wkv[:, :, MLA_NOPE:].transpose(1, 2, 0).astype(BF16)
    n_cache = 0 if cache_ckv is None else cache_ckv.shape[1]
    lk = n_cache + L
    nq = L // TM
    wq = hg * MLA_NOPE
    wp = hg * MLA_ROPE
    qrow = lambda w: pl.BlockSpec((TM, w), lambda b, g, q: (row_blk0 + b * nq + q, g))
    seq = lambda w: pl.BlockSpec((L, w), lambda b, g, q: (row_blk0 * TM // L + b, 0))
    in_specs = [qrow(wq), qrow(wp), seq(MLA_KV_RANK), seq(MLA_ROPE)]
    args = [qno, qpe, ckv, kpr]
    if n_cache:
        in_specs += [pl.BlockSpec((None, n_cache, MLA_KV_RANK), lambda b, g, q: (b, 0, 0)),
                     pl.BlockSpec((None, n_cache, MLA_ROPE), lambda b, g, q: (b, 0, 0))]
        args += [cache_ckv, cache_kpe]
    in_specs += [pl.BlockSpec((MLA_KV_RANK, wq), lambda b, g, q: (0, g)),
                 pl.BlockSpec((hg, MLA_V, MLA_KV_RANK), lambda b, g, q: (g, 0, 0)), qrow(wq)]
    args += [wk, wvt, sz]
    return pl.pallas_call(
        functools.partial(_mla_attn_kernel, n_cache=n_cache, hg=hg),
        grid=(nb, MLA_HEADS // hg, nq),
        in_specs=in_specs,
        out_specs=pl.BlockSpec((TM, wq), lambda b, g, q: (b * nq + q, g)),
        out_shape=jax.ShapeDtypeStruct((nb * L, D), F32),
        scratch_shapes=[pltpu.VMEM((hg, lk, MLA_NOPE + MLA_ROPE), BF16), pltpu.VMEM((hg, MLA_V, lk), BF16)],
        compiler_params=_cparams(("arbitrary", "arbitrary", "arbitrary")),
        name="mla_attention",
    )(*args)


def _mla_layer(x, mod, layer, rope, ln_g, ln_b, cache_ckv, cache_kpe, w_in, q_norm, kv_norm, w_q_up, w_kv_up, w_out):
    qno, qpe, ckv, kpe, kpr, sz = _mla_in_proj(x, mod, layer, w_in, q_norm, kv_norm, w_q_up, rope)
    g_p = _mla_attention(qno, qpe, ckv, kpr, sz, w_kv_up, None, None, nb=B_P, L=L_P, row_blk0=0,
                         hg=MLA_HEADS)
    g_s = _mla_attention(qno, qpe, ckv, kpr, sz, w_kv_up, cache_ckv, cache_kpe, nb=B_S, L=L_S, row_blk0=NT_P,
                         hg=MLA_HEADS // 2)
    x_new = _out_proj_ln2(g_p, g_s, w_out, x, mod, layer, ln_g, ln_b)
    new_ckv = ckv[:T_P].reshape(B_P, 1, L_P, MLA_KV_RANK)
    new_kpe = kpe[:T_P].reshape(B_P, 1, L_P, MLA_ROPE)
    return x_new, new_ckv, new_kpe


def _rw_in_kernel(x_ref, xp_ref, xn_ref, m_ref, mu_ref, wr_ref, wk_ref, wv_ref, wg_ref, w1_ref, a1_ref,
                  w2_ref, a2_ref, w0_ref, a0_ref, kk_ref, ka_ref, rk_ref, ones_ref,
                  r_ref, v_ref, sz_ref, nkk_ref, bonus_ref, lw_ref, kd_ref, bd_ref):
    i = pl.program_id(0)
    has_prev, has_next = _tile_has_neighbours(i)
    m = m_ref[...]
    h = _modulate(x_ref[...], m)
    prev, nxt = _neighbour_rows(h, _modulate(xp_ref[...], m), _modulate(xn_ref[...], m), has_prev, has_next)
    d = 0.5 * (prev + nxt) - h
    mu = mu_ref[...]

    def mix(p):
        return (h + d * mu[p:p + 1]).astype(BF16)

    r = _dot(mix(0), wr_ref[...])
    tw = jnp.tanh(_dot(mix(1), w1_ref[...])).astype(BF16)
    k = _dot(mix(2), wk_ref[...])
    v = _dot(mix(3), wv_ref[...])
    ta = _dot(mix(4), a1_ref[...]).astype(BF16)
    z = _dot(mix(5), wg_ref[...])
    r_ref[...] = r
    v_ref[...] = v
    sz_ref[...] = _silu(z)
    ones_bd = ones_ref[...]
    kk = k * kk_ref[...]
    kk = kk * lax.rsqrt(_head_sum(kk * kk, ones_bd) + 1e-12)
    nkk_ref[...] = -kk
    coef = jnp.zeros_like(r)
    for n in range(2):
        wl = w0_ref[n:n + 1, :] + _dot(tw, w2_ref[n])
        lw_ref[n] = -math.exp(-0.5) * _sigmoid(wl)
        a = _sigmoid(a0_ref[n:n + 1, :] + _dot(ta, a2_ref[n]))
        kd = k * (1.0 + (a - 1.0) * ka_ref[...])
        kd_ref[n] = kd
        bd_ref[n] = kk * a
        coef = coef + r * kd * rk_ref[...]
    bonus_ref[...] = _head_sum(coef, ones_bd) * v


def _pad_lora_up(w):
    z = jnp.zeros_like(w[0])
    return jnp.stack([jnp.concatenate([w[0], z], 0), jnp.concatenate([z, w[1]], 0)])


def _head_ones():
    h = np.arange(LANES) // RW_N
    return jnp.asarray(h[:, None] == h[None, :], dtype=BF16)


def _rw_in_proj(x, mod, layer, ones_bd, mu, w_in, w0, w1, w2, a0, a1, a2, k_k, k_a, r_k):
    mu8 = jnp.pad(mu, ((0, 2), (0, 0)))
    w1c = jnp.concatenate([w1[0], w1[1]], -1).astype(BF16)
    a1c = jnp.concatenate([a1[0], a1[1]], -1).astype(BF16)
    w2p = _pad_lora_up(w2).astype(BF16)
    a2p = _pad_lora_up(a2).astype(BF16)
    row = pl.BlockSpec((TM, D), lambda i: (i, 0))
    row2 = pl.BlockSpec((2, TM, D), lambda i: (0, i, 0))
    prev, nxt = _halo_specs(D, lambda i: 0)
    full = lambda a: pl.BlockSpec(a.shape, lambda i: (0,) * a.ndim)
    consts = [mu8, w_in[0].astype(BF16), w_in[1].astype(BF16), w_in[2].astype(BF16), w_in[3].astype(BF16),
              w1c, a1c, w2p, a2p, w0, a0, k_k.reshape(1, D), k_a.reshape(1, D), r_k.reshape(1, D), ones_bd]
    return pl.pallas_call(
        _rw_in_kernel,
        grid=(NT,),
        in_specs=[row, prev, nxt, _mod_spec(layer)] + [full(a) for a in consts],
        out_specs=[row] * 5 + [row2] * 3,
        out_shape=[jax.ShapeDtypeStruct((T, D), F32)] * 5 + [jax.ShapeDtypeStruct((2, T, D), F32)] * 3,
        compiler_params=_cparams(("arbitrary",)),
        name="rwkv_in_proj",
    )(x, x, x, mod, *consts)


SCAN_NB = 2
SCAN_NU = 2 * SCAN_NB


def _rw_scan_kernel(*refs):
    rb_refs = refs[:SCAN_NU]
    first_ref = refs[SCAN_NU]
    ins = refs[SCAN_NU + 1:SCAN_NU + 1 + 6 * SCAN_NU]
    s0_ref, y_ref, s_ref = refs[SCAN_NU + 1 + 6 * SCAN_NU:]
    del rb_refs
    step = pl.program_id(0)
    c = CHUNK

    @pl.when(first_ref[step] == 1)
    def _():
        s_ref[...] = s0_ref[...]

    ri = lax.broadcasted_iota(jnp.int32, (c, c), 0)
    ci = lax.broadcasted_iota(jnp.int32, (c, c), 1)
    ri2 = lax.broadcasted_iota(jnp.int32, (c, 2 * c), 0)
    ci2 = lax.broadcasted_iota(jnp.int32, (c, 2 * c), 1)
    cm2 = jnp.where(ci2 >= c, ci2 - c, ci2)
    eye_f = (ri == ci).astype(F32)
    eye = eye_f.astype(BF16)
    masks = []
    for sgn in (1, -1):
        incl = (ri - ci) * sgn >= 0
        strict = (ri - ci) * sgn > 0
        mask_2 = (ri2 - cm2) * sgn >= 0
        mask_k = jnp.logical_and((ri2 - cm2) * sgn > 0, ci2 >= c)
        off_masks = []
        for lg in range(6):
            same_2m = (ri >> (lg + 1)) == (ci >> (lg + 1))
            diff_m = (ri >> lg) != (ci >> lg)
            off_masks.append(jnp.logical_and(strict, jnp.logical_and(same_2m, diff_m)))
        masks.append((incl, mask_2, mask_k, off_masks))

    chains = [(q, h) for q in range(SCAN_NU) for h in range(RW_H)]
    ar, bk, bke, vh = {}, {}, {}, {}
    for q in range(SCAN_NU):
        r_ref, v_ref, nkk_ref, lw_ref, kd_ref, bd_ref = ins[6 * q:6 * q + 6]
        bwd = q % 2
        lw = lw_ref[...]
        g = _dot_hi(masks[bwd][0].astype(F32), lw)
        gtot = g[0:1, :] if bwd else g[c - 1:c, :]
        e_inv = jnp.exp(-g)
        e_rem = jnp.exp(gtot - g)
        e_hi, e_lo = _split(jnp.broadcast_to(jnp.exp(gtot), (ROWS_BF16, D)))
        a_t = nkk_ref[...] * jnp.exp(g - lw)
        r_t = r_ref[...] * jnp.exp(g)
        kd, bd = kd_ref[...], bd_ref[...]
        b_t, k_t, b_e, k_e = bd * e_inv, kd * e_inv, bd * e_rem, kd * e_rem
        v = v_ref[...]
        for h in range(RW_H):
            sl = slice(h * RW_N, (h + 1) * RW_N)
            cat = lambda p, r_: jnp.concatenate([p[:, sl], r_[:, sl]], axis=0).astype(BF16)
            ar[q, h], bk[q, h] = cat(a_t, r_t), cat(b_t, k_t)
            bke[q, h] = jnp.concatenate([cat(b_e, k_e), e_hi[:, sl], e_lo[:, sl]], axis=0)
            vh[q, h] = v[:, sl]
    h_old = {ch: s_ref[ch[0], ch[1]] for ch in chains}
    gm = {ch: _dot_nt(ar[ch], bk[ch]) for ch in chains}
    a_s = {ch: _dot(ar[ch], h_old[ch].astype(BF16)) for ch in chains}
    bke_t = {ch: _dot_nt(eye, bke[ch]) for ch in chains}
    lab = {ch: gm[ch][:c, :c] for ch in chains}
    offs = lambda ch: masks[ch[0] % 2][3]
    tinv = {ch: eye_f + jnp.where(offs(ch)[0], lab[ch], 0.0) for ch in chains}
    for lg in range(1, 6):
        tb = {ch: tinv[ch].astype(BF16) for ch in chains}
        lt = {ch: _dot(jnp.where(offs(ch)[lg], lab[ch], 0.0).astype(BF16), tb[ch]) for ch in chains}
        tinv = {ch: tinv[ch] + _dot(tb[ch], lt[ch].astype(BF16)) for ch in chains}
    vv2 = {ch: jnp.concatenate([vh[ch], vh[ch]], axis=0).astype(BF16) for ch in chains}
    w1 = {ch: a_s[ch][:c] + _dot(jnp.where(masks[ch[0] % 2][2], gm[ch][:c], 0.0).astype(BF16), vv2[ch])
          for ch in chains}
    u = {ch: _dot(tinv[ch].astype(BF16), w1[ch].astype(BF16)) for ch in chains}
    uv = {ch: jnp.concatenate([u[ch], vh[ch]], axis=0).astype(BF16) for ch in chains}
    y = {ch: a_s[ch][c:] + _dot(jnp.where(masks[ch[0] % 2][1], gm[ch][c:], 0.0).astype(BF16), uv[ch])
         for ch in chains}
    for q in range(SCAN_NU):
        y_ref[q] = jnp.concatenate([y[q, h] for h in range(RW_H)], axis=1)
    upd = {ch: _dot(bke_t[ch][:, :2 * c].astype(BF16), uv[ch]) for ch in chains}
    lo_col = 2 * c + ROWS_BF16
    h_new = {ch: h_old[ch] * (bke_t[ch][:, 2 * c:2 * c + 1] + bke_t[ch][:, lo_col:lo_col + 1]) + upd[ch]
             for ch in chains}
    for ch in chains:
        s_ref[ch[0], ch[1]] = h_new[ch]


def _rw_scan_tables():
    rb = [[] for _ in range(SCAN_NU)]
    first = []
    for nb, L, blk0 in ((B_P, L_P, 0), (B_S, L_S, T_P // CHUNK)):
        nc = L // CHUNK
        for grp in range(nb // SCAN_NB):
            for j in range(nc):
                for q in range(SCAN_NU):
                    b = grp * SCAN_NB + q // 2
                    cn = j if q % 2 == 0 else nc - 1 - j
                    rb[q].append(blk0 + b * nc + cn)
                first.append(1 if j == 0 else 0)
    as_i32 = lambda a: jnp.asarray(np.asarray(a, np.int32))
    return [as_i32(a) for a in rb], as_i32(first)


def _rw_scan(r, v, nkk, lw, kd, bd, s0_all):
    rb, first = _rw_scan_tables()
    n_steps = first.shape[0]
    n_groups = s0_all.shape[0] // SCAN_NU
    steps_p = (B_P // SCAN_NB) * (L_P // CHUNK)
    in_specs = []
    args = []
    for q in range(SCAN_NU):
        tok = pl.BlockSpec((CHUNK, D), lambda s, *t, q=q: (t[q][s], 0))
        tok2 = pl.BlockSpec((None, CHUNK, D), lambda s, *t, q=q: (q % 2, t[q][s], 0))
        in_specs += [tok, tok, tok, tok2, tok2, tok2]
        args += [r, v, nkk, lw, kd, bd]
    grp = lambda s: jnp.where(s < steps_p, s // (L_P // CHUNK),
                              B_P // SCAN_NB + (s - steps_p) // (L_S // CHUNK))
    st = pl.BlockSpec((None, SCAN_NU, RW_H, RW_N, RW_N), lambda s, *t: (grp(s), 0, 0, 0, 0))
    s0g = s0_all.reshape(n_groups, SCAN_NU, RW_H, RW_N, RW_N)
    y, s_fin = pl.pallas_call(
        _rw_scan_kernel,
        grid_spec=pltpu.PrefetchScalarGridSpec(
            num_scalar_prefetch=SCAN_NU + 1,
            grid=(n_steps,),
            in_specs=in_specs + [st],
            out_specs=[pl.BlockSpec((None, SCAN_NU, CHUNK, D), lambda s, *t: (s, 0, 0, 0)), st],
        ),
        out_shape=[jax.ShapeDtypeStruct((n_steps, SCAN_NU, CHUNK, D), F32),
                   jax.ShapeDtypeStruct(s0g.shape, F32)],
        compiler_params=_cparams(("arbitrary",)),
        name="rwkv_scan",
    )(*rb, first, *args, s0g)
    return y, s_fin.reshape(s0_all.shape)


def _scan_out_index(i, k, bwd):
    per_tile = TM // CHUNK
    nc_p, nc_s = L_P // CHUNK, L_S // CHUNK
    steps_p = (B_P // SCAN_NB) * nc_p
    cn_p = k
    b_s = (i - NT_P) // NT_S_SEQ
    cn_s = ((i - NT_P) % NT_S_SEQ) * per_tile + k
    step_p = (i // SCAN_NB) * nc_p + (nc_p - 1 - cn_p if bwd else cn_p)
    step_s = steps_p + (b_s // SCAN_NB) * nc_s + (nc_s - 1 - cn_s if bwd else cn_s)
    q_p = (i % SCAN_NB) * 2 + bwd
    q_s = (b_s % SCAN_NB) * 2 + bwd
    is_p = i < NT_P
    return jnp.where(is_p, step_p, step_s), jnp.where(is_p, q_p, q_s)


def _rw_out_kernel(*refs):
    per_tile = TM // CHUNK
    y_refs = refs[:2 * per_tile]
    bonus_ref, sz_ref, gg_ref, gb_ref, ones_ref, w_ref, x_ref, m_ref, lng_ref, lnb_ref, o_ref = refs[2 * per_tile:]
    ones_bd = ones_ref[...]
    y = jnp.concatenate([y_refs[k][...] + y_refs[per_tile + k][...] for k in range(per_tile)], axis=0)
    mean = _head_sum(y, ones_bd) * (1.0 / RW_N)
    yc = y - mean
    var = _head_sum(yc * yc, ones_bd) * (1.0 / RW_N)
    yn = yc * lax.rsqrt(var + RW_GN_EPS) * gg_ref[...] + gb_ref[...]
    g = (yn + bonus_ref[...]) * sz_ref[...]
    _out_ln_tail(g, w_ref, x_ref, m_ref, lng_ref, lnb_ref, o_ref)


def _rw_out_proj_ln(y, bonus, sz, gn_g, gn_b, ones_bd, w_out, x, mod, layer, ln_g, ln_b):
    row = pl.BlockSpec((TM, D), lambda i: (i, 0))
    vec = pl.BlockSpec((1, D), lambda i: (0, 0))
    mat = pl.BlockSpec((D, D), lambda i: (0, 0))
    y_specs = [pl.BlockSpec((None, None, CHUNK, D), lambda i, k=k, bwd=bwd: _scan_out_index(i, k, bwd) + (0, 0))
               for bwd in (0, 1) for k in range(TM // CHUNK)]
    return pl.pallas_call(
        _rw_out_kernel,
        grid=(NT,),
        in_specs=y_specs + [row, row, vec, vec,
                  pl.BlockSpec((LANES, LANES), lambda i: (0, 0)), mat, row,
                  _mod_spec(layer), vec, vec],
        out_specs=row,
        out_shape=jax.ShapeDtypeStruct((T, D), F32),
        compiler_params=_cparams(("arbitrary",)),
        name="rwkv_out_proj_ln",
    )(*[y] * len(y_specs), bonus, sz, gn_g.reshape(1, D), gn_b.reshape(1, D), ones_bd, w_out.astype(BF16), x, mod,
      ln_g.reshape(1, D), ln_b.reshape(1, D))


def _rwkv_layer(x, mod, layer, ln_g, ln_b, state, mu, w_in, w0, w1, w2, a0, a1, a2, k_k, k_a, r_k, gn_g, gn_b, w_out):
    ones_bd = _head_ones()
    r, v, sz, nkk, bonus, lw, kd, bd = _rw_in_proj(x, mod, layer, ones_bd, mu, w_in, w0, w1, w2, a0, a1, a2,
                                                  k_k, k_a, r_k)
    n_p = B_P * 2
    s0_all = jnp.concatenate([jnp.zeros((n_p, RW_H, RW_N, RW_N), F32),
                              state.astype(F32).reshape(B_S * 2, RW_H, RW_N, RW_N).swapaxes(-1, -2)], 0)
    y, h_fin = _rw_scan(r, v, nkk, lw, kd, bd, s0_all)
    x_new = _rw_out_proj_ln(y, bonus, sz, gn_g, gn_b, ones_bd, w_out, x, mod, layer, ln_g, ln_b)
    new_state = h_fin[:n_p].swapaxes(-1, -2).reshape(B_P, 1, 2, RW_H, RW_N, RW_N)
    return x_new, new_state


def kernel(x_prompt, x_sample, cache_mla_ckv, cache_mla_kpe, state_rwkv, c, c_ctx, mod_w, mod_b, ln_g, ln_b, hy_w_in, hy_conv_w, hy_conv_b, hy_ffn_w1, hy_ffn_b1, hy_ffn_w2, hy_ffn_b2, hy_ffn_w3, hy_ffn_b3, hy_freq, hy_decay, hy_skip, hy_w_out, mla_w_in, mla_q_norm, mla_kv_norm, mla_w_q_up, mla_w_kv_up, mla_w_out, rw_mu, rw_w_in, rw_w0, rw_w1, rw_w2, rw_a0, rw_a1, rw_a2, rw_k_k, rw_k_a, rw_r_k, rw_gn_g, rw_gn_b, rw_w_out):
    x = jnp.concatenate([x_prompt.reshape(T_P, D), x_sample.reshape(T_S, D)], 0)
    cond8 = jnp.concatenate([c_ctx[None, :], c, jnp.zeros((8 - 1 - B_S, D), F32)], 0)
    mod = _modulation_table(cond8, mod_w, mod_b).reshape(DEPTH * 8, 1, 3 * D)
    tabs_p = _dft_tables(L_P)
    tabs_s = _dft_tables(L_S)
    rope = _rope_tables()
    new_ckv = new_kpe = new_state = None
    for i in range(DEPTH):
        kind, j = i % 3, i // 3
        if kind == 0:
            x = _hyena_layer(x, mod, i, tabs_p, tabs_s, ln_g[i], ln_b[i], hy_w_in[j], hy_conv_w[j], hy_conv_b[j],
                             hy_ffn_w1[j], hy_ffn_b1[j], hy_ffn_w2[j], hy_ffn_b2[j], hy_ffn_w3[j], hy_ffn_b3[j],
                             hy_freq[j], hy_decay[j], hy_skip[j], hy_w_out[j])
        elif kind == 1:
            x, new_ckv, new_kpe = _mla_layer(x, mod, i, rope, ln_g[i], ln_b[i], cache_mla_ckv[:, j],
                                             cache_mla_kpe[:, j], mla_w_in[j], mla_q_norm[j], mla_kv_norm[j],
                                             mla_w_q_up[j], mla_w_kv_up[j], mla_w_out[j])
        else:
            x, new_state = _rwkv_layer(x, mod, i, ln_g[i], ln_b[i], state_rwkv[:, j], rw_mu[j], rw_w_in[j],
                                       rw_w0[j], rw_w1[j], rw_w2[j], rw_a0[j], rw_a1[j], rw_a2[j], rw_k_k[j],
                                       rw_k_a[j], rw_r_k[j], rw_gn_g[j], rw_gn_b[j], rw_w_out[j])
    return (x[:T_P].reshape(B_P, L_P, D), x[T_P:].reshape(B_S, L_S, D), new_ckv, new_kpe, new_state)
```

```python
import functools
import math

import numpy as np
import jax
import jax.numpy as jnp
from jax import lax
from jax.experimental import pallas as pl
from jax.experimental.pallas import tpu as pltpu

F32 = jnp.float32
BF16 = jnp.bfloat16
HIGHEST = lax.Precision.HIGHEST

D = 1024
B_P, L_P = 16, 256
B_S, L_S = 2, 2048
T_P = B_P * L_P
T_S = B_S * L_S
T = T_P + T_S
PAST = 512
DEPTH = 4
DEEPNORM_ALPHA = (2.0 * DEPTH) ** 0.25
LN_EPS = 1e-5
RMS_EPS = 1e-6
HY_BANDS = 16
HY_FFN = 64
MLA_HEADS = 16
MLA_Q_RANK = 256
MLA_KV_RANK = 128
MLA_NOPE = 64
MLA_ROPE = 32
MLA_V = 64
ROPE_BASE = 10000.0
GRID_W = 64
RW_N = 64
RW_H = D // RW_N
RW_LORA = 64
RW_GN_EPS = 64e-5

TM = 256
NT_P = T_P // TM
NT_S_SEQ = L_S // TM
NT = T // TM
HALO = 8
LANES = 128
ROWS_BF16 = 16
CHUNK = 64
DFT_TK = 512
VMEM_LIMIT = 52 * 1024 * 1024


def _cparams(sem):
    return pltpu.CompilerParams(dimension_semantics=sem, vmem_limit_bytes=VMEM_LIMIT)


def _group(i):
    return jnp.where(i < NT_P, 0, 1 + (i - NT_P) // NT_S_SEQ)


def _sigmoid(x):
    return 1.0 / (1.0 + jnp.exp(-x))


def _silu(x):
    return x * _sigmoid(x)


def _dot(a, b):
    return jnp.dot(a, b, preferred_element_type=F32)


def _dot_nt(a, b):
    return lax.dot_general(a, b, (((1,), (1,)), ((), ())), preferred_element_type=F32)


def _dot_hi(a, b):
    return jnp.dot(a, b, preferred_element_type=F32, precision=HIGHEST)


def _split(x):
    hi = x.astype(BF16)
    lo = (x - hi.astype(F32)).astype(BF16)
    return hi, lo


def _head_sum(x, ones_bd):
    hi, lo = _split(x)
    lanes = ones_bd.shape[0]
    parts = []
    for g in range(x.shape[1] // lanes):
        sl = slice(g * lanes, (g + 1) * lanes)
        parts.append(_dot(hi[:, sl], ones_bd) + _dot(lo[:, sl], ones_bd))
    return jnp.concatenate(parts, axis=1)


def _modulate(x, m):
    return x * (1.0 + m[:, D:2 * D]) + m[:, :D]


def _layer_norm_rows(y, g, b):
    mu = jnp.mean(y, axis=-1, keepdims=True)
    yc = y - mu
    var = jnp.mean(yc * yc, axis=-1, keepdims=True)
    return yc * lax.rsqrt(var + LN_EPS) * g + b


def _neighbour_rows(cur, prev_halo, next_halo, has_prev, has_next):
    rows = cur.shape[0]
    ridx = lax.broadcasted_iota(jnp.int32, cur.shape, 0)
    pr = jnp.where(has_prev, prev_halo[HALO - 1:HALO, :], 0.0)
    nx = jnp.where(has_next, next_halo[0:1, :], 0.0)
    prev = jnp.where(ridx == 0, pr, pltpu.roll(cur, 1, axis=0))
    nxt = jnp.where(ridx == rows - 1, nx, pltpu.roll(cur, rows - 1, axis=0))
    return prev, nxt


def _tile_has_neighbours(i):
    k = (i - NT_P) % NT_S_SEQ
    is_s = i >= NT_P
    return jnp.logical_and(is_s, k != 0), jnp.logical_and(is_s, k != NT_S_SEQ - 1)


def _halo_specs(width, col_of):
    r = TM // HALO
    prev = pl.BlockSpec((HALO, width), lambda i, *a: (jnp.maximum(i * r - 1, 0), col_of(i, *a)))
    nxt = pl.BlockSpec((HALO, width), lambda i, *a: (jnp.minimum((i + 1) * r, T // HALO - 1), col_of(i, *a)))
    return prev, nxt


def _mod_kernel(c_ref, w_ref, b_ref, o_ref):
    o_ref[...] = _dot_hi(_silu(c_ref[...]), w_ref[...]) + b_ref[...]


def _modulation_table(cond8, mod_w, mod_b):
    tn = 1024
    return pl.pallas_call(
        _mod_kernel,
        grid=(DEPTH, 3 * D // tn),
        in_specs=[pl.BlockSpec((8, D), lambda l, j: (0, 0)),
                  pl.BlockSpec((None, D, tn), lambda l, j: (l, 0, j)),
                  pl.BlockSpec((None, 1, tn), lambda l, j: (l, 0, j))],
        out_specs=pl.BlockSpec((None, 8, tn), lambda l, j: (l, 0, j)),
        out_shape=jax.ShapeDtypeStruct((DEPTH, 8, 3 * D), F32),
        compiler_params=_cparams(("arbitrary", "arbitrary")),
        name="modulation",
    )(cond8, mod_w, mod_b.reshape(DEPTH, 1, 3 * D))


def _mod_spec(layer):
    return pl.BlockSpec((None, 1, 3 * D), lambda i, *a: (layer * 8 + _group(i), 0, 0))


def _out_ln_tail(g, w_ref, x_ref, m_ref, lng_ref, lnb_ref, o_ref):
    mix = _dot(g.astype(BF16), w_ref[...])
    gate = m_ref[...][:, 2 * D:]
    y = DEEPNORM_ALPHA * x_ref[...] + gate * mix
    o_ref[...] = _layer_norm_rows(y, lng_ref[...], lnb_ref[...])


def _out_ln2_kernel(gp_ref, gs_ref, w_ref, x_ref, m_ref, lng_ref, lnb_ref, o_ref):
    g = jnp.where(pl.program_id(0) < NT_P, gp_ref[...], gs_ref[...])
    _out_ln_tail(g, w_ref, x_ref, m_ref, lng_ref, lnb_ref, o_ref)


def _out_proj_ln2(g_p, g_s, w_out, x, mod, layer, ln_g, ln_b):
    row = pl.BlockSpec((TM, D), lambda i: (i, 0))
    vec = pl.BlockSpec((1, D), lambda i: (0, 0))
    return pl.pallas_call(
        _out_ln2_kernel,
        grid=(NT,),
        in_specs=[pl.BlockSpec((TM, D), lambda i: (jnp.minimum(i, NT_P - 1), 0)),
                  pl.BlockSpec((TM, D), lambda i: (jnp.maximum(i - NT_P, 0), 0)),
                  pl.BlockSpec((D, D), lambda i: (0, 0)),
                  row, _mod_spec(layer), vec, vec],
        out_specs=row,
        out_shape=jax.ShapeDtypeStruct((T, D), F32),
        compiler_params=_cparams(("arbitrary",)),
        name="out_proj_ln",
    )(g_p, g_s, w_out.astype(BF16), x, mod, ln_g.reshape(1, D), ln_b.reshape(1, D))


def _hy_in_conv_kernel(x_ref, xp_ref, xn_ref, m_ref, w_ref, cw_ref, cb_ref, vv_ref, gate_ref):
    has_prev, has_next = _tile_has_neighbours(pl.program_id(0))
    x_all = jnp.concatenate([xp_ref[...], x_ref[...], xn_ref[...]], axis=0)
    u = _dot(_modulate(x_all, m_ref[...]).astype(BF16), w_ref[...])
    rows = TM + 2 * HALO
    ridx = lax.broadcasted_iota(jnp.int32, (TM, 1), 0)
    no_prev = jnp.logical_and(ridx == 0, jnp.logical_not(has_prev))
    no_next = jnp.logical_and(ridx == TM - 1, jnp.logical_not(has_next))
    cw = cw_ref[...]
    cb = cb_ref[...]

    def conv(grp):
        sl = slice(grp * D, (grp + 1) * D)
        ug = u[:, sl]
        cur = ug[HALO:HALO + TM]
        prev = jnp.where(no_prev, 0.0, pltpu.roll(ug, 1, axis=0)[HALO:HALO + TM])
        nxt = jnp.where(no_next, 0.0, pltpu.roll(ug, rows - 1, axis=0)[HALO:HALO + TM])
        return prev * cw[0:1, sl] + cur * cw[1:2, sl] + nxt * cw[2:3, sl] + cb[:, sl]

    x0, x1, v = conv(0), conv(1), conv(2)
    vv_ref[...] = v * x1
    gate_ref[...] = x0 * _silu(u[HALO:HALO + TM, 3 * D:])


def _hy_in_conv(x, mod, layer, w_in, conv_w, conv_b):
    row = pl.BlockSpec((TM, D), lambda i: (i, 0))
    prev, nxt = _halo_specs(D, lambda i: 0)
    return pl.pallas_call(
        _hy_in_conv_kernel,
        grid=(NT,),
        in_specs=[row, prev, nxt, _mod_spec(layer), _resident(w_in.shape),
                  pl.BlockSpec((3, 3 * D), lambda i: (0, 0)), pl.BlockSpec((1, 3 * D), lambda i: (0, 0))],
        out_specs=[row, row],
        out_shape=[jax.ShapeDtypeStruct((T, D), F32)] * 2,
        compiler_params=_cparams(("arbitrary",)),
        name="hyena_in_proj_conv3",
    )(x, x, x, mod, w_in.astype(BF16), conv_w, conv_b.reshape(1, 3 * D))


def _hy_filter_kernel(t_ref, bands_ref, wt_ref, wc_ref, ws_ref, b1_ref, w2_ref, b2_ref, w3_ref, b3_ref,
                      f0_ref, f1_ref, dec_ref, hs_ref, hd_ref, nyq_ref, *, L, tr):
    i = pl.program_id(0)
    ridx = lax.broadcasted_iota(jnp.int32, (tr, 1), 0) + i * tr
    pos = ridx.astype(F32)
    t = t_ref[...]
    ang = ((2.0 * math.pi / L) * pos) * bands_ref[...]
    pre = t * wt_ref[...] + _dot_hi(jnp.cos(ang), wc_ref[...]) + _dot_hi(jnp.sin(ang), ws_ref[...])
    hdn = jnp.sin(f0_ref[...] * (pre + b1_ref[...]))
    hdn = jnp.sin(f1_ref[...] * (_dot_hi(hdn, w2_ref[...]) + b2_ref[...]))
    hf = _dot_hi(hdn, w3_ref[...]) + b3_ref[...]
    h = hf * jnp.exp(-t * jnp.abs(dec_ref[...]))
    h0 = h[:, :D]
    h1 = jnp.where(ridx == 0, 0.0, h[:, D:])
    hsum = h0 + h1
    hs_ref[...] = hsum
    hd_ref[...] = h1 - h0
    alt = jnp.where((ridx & 1) == 0, 1.0, -1.0)
    part =jnp.broadcast_to(jnp.sum(alt * hsum, axis=0, keepdims=True), (8, D))

    @pl.when(i == 0)
    def _():
        nyq_ref[...] = part

    @pl.when(i > 0)
    def _():
        nyq_ref[...] += part


def _hy_filter(L, w1, b1, w2, b2, w3, b3, freq, decay):
    tr = 256
    t = jnp.linspace(0.0, 1.0, L, dtype=F32).reshape(L, 1)
    bands = jnp.linspace(1e-4, HY_BANDS - 1, HY_BANDS, dtype=F32)
    bands = jnp.pad(bands, (0, 128 - HY_BANDS)).reshape(1, 128)
    wt = w1[0:1]
    wc = jnp.pad(w1[1:1 + HY_BANDS], ((0, 128 - HY_BANDS), (0, 0)))
    ws = jnp.pad(-w1[1 + HY_BANDS:], ((0, 128 - HY_BANDS), (0, 0)))
    full = lambda shape: pl.BlockSpec(shape, lambda i: (0, 0))
    rows = pl.BlockSpec((tr, D), lambda i: (i, 0))
    return pl.pallas_call(
        functools.partial(_hy_filter_kernel, L=L, tr=tr),
        grid=(L // tr,),
        in_specs=[pl.BlockSpec((tr, 1), lambda i: (i, 0)), full((1, 128)), full((1, HY_FFN)),
                  full((128, HY_FFN)), full((128, HY_FFN)), full((1, HY_FFN)),
                  full((HY_FFN, HY_FFN)), full((1, HY_FFN)), full((HY_FFN, 2 * D)), full((1, 2 * D)),
                  full((1, HY_FFN)), full((1, HY_FFN)), full((1, 2 * D))],
        out_specs=[rows, rows, pl.BlockSpec((8, D), lambda i: (0, 0))],
        out_shape=[jax.ShapeDtypeStruct((L, D), F32), jax.ShapeDtypeStruct((L, D), F32),
                   jax.ShapeDtypeStruct((8, D), F32)],
        compiler_params=_cparams(("arbitrary",)),
        name="hyena_filter",
    )(t, bands, wt, wc, ws, b1.reshape(1, -1), w2, b2.reshape(1, -1), w3, b3.reshape(1, -1),
      freq[0:1], freq[1:2], decay.reshape(1, 2 * D))


def _dft_tables(L):
    n = 2 * L
    w = 64
    k = jnp.arange(L, dtype=jnp.int32)

    def cs(t):
        ang = ((k[:, None] * t[None, :]) % n).astype(F32) * (2.0 * math.pi / n)
        return jnp.cos(ang), jnp.sin(ang)

    ca, sa = cs(jnp.arange(L // w, dtype=jnp.int32) * w)
    cb, sb = cs(jnp.arange(w, dtype=jnp.int32))
    c = (ca[:, :, None] * cb[:, None, :] - sa[:, :, None] * sb[:, None, :]).reshape(L, L)
    s = (sa[:, :, None] * cb[:, None, :] + ca[:, :, None] * sb[:, None, :]).reshape(L, L)
    alt = jnp.where(k % 2 == 0, 1.0, -1.0).astype(F32)
    s_rows = jnp.where(k[:, None] == 0, alt[None, :], s)
    s_cols = jnp.where(k[None, :] == 0, alt[:, None], s)
    tk = min(L, DFT_TK)
    nk = L // tk
    row_tiles = lambda a: a.astype(BF16).reshape(nk, tk, L)
    col_tiles = lambda a: a.astype(BF16).reshape(L, nk, tk).transpose(1, 0, 2)
    return row_tiles(c), row_tiles(s_rows), col_tiles(c), col_tiles(s_cols)


def _dft_fwd_kernel(c_ref, s_ref, *refs):
    x1_ref, x2_ref = refs[0], refs[-3]
    oc_ref, os_ref = refs[-2:]
    k = pl.program_id(2)
    oc_ref[...] = _dot(c_ref[k], x1_ref[...].astype(BF16))
    os_ref[...] = _dot(s_ref[k], x2_ref[...].astype(BF16))


def _resident(shape):
    return pl.BlockSpec(shape, lambda *_: (0,) * len(shape), pipeline_mode=pl.Buffered(1))


def _dft_fwd(tabs, xs, L, nb, row_blk0):
    c, s_rows = tabs[:2]
    nk, tk, _ = c.shape
    tn = 512
    a_spec = _resident(c.shape)
    x_spec = pl.BlockSpec((L, tn), lambda b, j, k: (row_blk0 + b, j))
    o_spec = pl.BlockSpec((tk, tn), lambda b, j, k: (b * nk + k, j))
    return pl.pallas_call(
        _dft_fwd_kernel,
        grid=(nb, D // tn, nk),
        in_specs=[a_spec] * 2 + [x_spec] * len(xs),
        out_specs=[o_spec, o_spec],
        out_shape=[jax.ShapeDtypeStruct((nb * L, D), F32)] * 2,
        compiler_params=_cparams(("arbitrary", "arbitrary", "arbitrary")),
        name="hyena_dft_fwd",
    )(c, s_rows, *xs)


def _dft_inv_kernel(c_ref, st_ref, vc_ref, vs_ref, kre_ref, kim_ref, nyq_ref,
                    vv_ref, skip_ref, gate_ref, o_ref, *, L):
    k = pl.program_id(2)
    nk = pl.num_programs(2)
    vc, vs, kre = vc_ref[...], vs_ref[...], kre_ref[...]
    bin0 = jnp.logical_and(lax.broadcasted_iota(jnp.int32, vc.shape, 0) == 0, k == 0)
    kim = jnp.where(bin0, nyq_ref[0:1, :], kim_ref[...])
    inv_n = 1.0 / (2 * L)
    yre = jnp.where(bin0, vc * kre * inv_n, (vc * kre + vs * kim) * (2.0 * inv_n))
    yim = jnp.where(bin0, vs * kim * inv_n, (vs * kre - vc * kim) * (2.0 * inv_n))
    contrib = _dot(c_ref[k], yre.astype(BF16)) + _dot(st_ref[k], yim.astype(BF16))

    @pl.when(k == 0)
    def _():
        o_ref[...] = contrib

    @pl.when(k > 0)
    def _():
        o_ref[...] += contrib

    @pl.when(k == nk - 1)
    def _():
        o_ref[...] = (o_ref[...] + vv_ref[...] * skip_ref[...]) * gate_ref[...]


def _dft_inv(tabs, vc, vs, kre, kim, nyq, vv, skip, gate, L, nb, row_blk0):
    c, s_cols = tabs[2:]
    nk, _, tk = c.shape
    tn = 512 if L <= 512 else 256
    a_spec = _resident(c.shape)
    v_spec = pl.BlockSpec((tk, tn), lambda b, j, k: (b * nk + k, j))
    k_spec = pl.BlockSpec((tk, tn), lambda b, j, k: (k, j))
    row_spec = pl.BlockSpec((L, tn), lambda b, j, k: (row_blk0 + b, j))
    return pl.pallas_call(
        functools.partial(_dft_inv_kernel, L=L),
        grid=(nb, D // tn, nk),
        in_specs=[a_spec] * 2 + [v_spec, v_spec, k_spec, k_spec,
                                 pl.BlockSpec((8, tn), lambda b, j, k: (0, j)),
                                 row_spec, pl.BlockSpec((1, tn), lambda b, j, k: (0, j)), row_spec],
        out_specs=pl.BlockSpec((L, tn), lambda b, j, k: (b, j)),
        out_shape=jax.ShapeDtypeStruct((nb * L, D), F32),
        compiler_params=_cparams(("arbitrary", "arbitrary", "arbitrary")),
        name="hyena_dft_inv_gate",
    )(c, s_cols, vc, vs, kre, kim, nyq, vv, skip.reshape(1, D), gate)


def _hyena_layer(x, mod, layer, tabs_p, tabs_s, ln_g, ln_b, w_in, conv_w, conv_b, w1, b1, w2, b2, w3, b3,
                 freq, decay, skip, w_out):
    vv, gate = _hy_in_conv(x, mod, layer, w_in, conv_w, conv_b)
    gs = []
    for L, nb, blk0, tabs in ((L_P, B_P, 0, tabs_p), (L_S, B_S, T_P // L_S, tabs_s)):
        hsum, hdiff, nyq = _hy_filter(L, w1, b1, w2, b2, w3, b3, freq, decay)
        kre, kim = _dft_fwd(tabs, (hsum, hdiff), L, 1, 0)
        vc, vs = _dft_fwd(tabs, (vv,), L, nb, blk0)
        gs.append(_dft_inv(tabs, vc, vs, kre, kim, nyq, vv, skip, gate, L, nb, blk0))
    return _out_proj_ln2(gs[0], gs[1], w_out, x, mod, layer, ln_g, ln_b)


def _rope_tables():
    rows = L_S // GRID_W
    half = MLA_ROPE // 2
    inv = ROPE_BASE ** (-jnp.arange(0, half, 2, dtype=F32) / half)
    r = jnp.repeat(jnp.arange(rows, dtype=F32), GRID_W)
    col = jnp.tile(jnp.arange(GRID_W, dtype=F32), rows)
    ar, ac = r[:, None] * inv, col[:, None] * inv
    ang = jnp.concatenate([ar, ar, ac, ac], -1)
    cos, sin = jnp.cos(ang), jnp.sin(ang)
    cos = jnp.concatenate([jnp.ones((TM, MLA_ROPE), F32), cos], 0)
    sin = jnp.concatenate([jnp.zeros((TM, MLA_ROPE), F32), sin], 0)
    return cos, sin, jnp.tile(cos, (1, MLA_HEADS)), jnp.tile(sin, (1, MLA_HEADS))


def _rope_rot_cols(w):
    idx = np.concatenate([np.arange(8, 16), np.arange(0, 8), np.arange(24, 32), np.arange(16, 24)])
    sign = np.concatenate([-np.ones(8), np.ones(8), -np.ones(8), np.ones(8)]).astype(np.float32)
    return w[..., idx] * sign


def _mla_in_kernel(x_ref, m_ref, wq_ref, wkv_ref, wkp_ref, wz_ref, qn_ref, kvn_ref, wqn_ref, wqp_ref, wqr_ref,
                   c32_ref, s32_ref, c512_ref, s512_ref,
                   qno_ref, qpe_ref, ckv_ref, kpe_ref, kpr_ref, sz_ref):
    h = _modulate(x_ref[...], m_ref[...]).astype(BF16)
    q_c = _dot(h, wq_ref[...])
    kv_c = _dot(h, wkv_ref[...])
    kp2 = _dot(h, wkp_ref[...])
    z = _dot(h, wz_ref[...])

    def rms(v, g):
        return v * lax.rsqrt(jnp.mean(v * v, axis=-1, keepdims=True) + RMS_EPS) * g

    qn = rms(q_c, qn_ref[...]).astype(BF16)
    scale = (MLA_NOPE + MLA_ROPE) ** -0.5
    qno_ref[...] = (_dot(qn, wqn_ref[...]) * scale).astype(BF16)
    q_pe = _dot(qn, wqp_ref[...]) * c512_ref[...] + _dot(qn, wqr_ref[...]) * s512_ref[...]
    qpe_ref[...] = (q_pe * scale).astype(BF16)
    ckv_ref[...] = rms(kv_c, kvn_ref[...])
    kpe = kp2[:, :MLA_ROPE]
    kpe_ref[...] = kpe
    kpr_ref[...] = kpe * c32_ref[...] + kp2[:, MLA_ROPE:] * s32_ref[...]
    sz_ref[...] = _silu(z)


def _mla_in_proj(x, mod, layer, w_in, q_norm, kv_norm, w_q_up, rope):
    c32, s32, c512, s512 = rope
    o1, o2, o3 = MLA_Q_RANK, MLA_Q_RANK + MLA_KV_RANK, MLA_Q_RANK + MLA_KV_RANK + MLA_ROPE
    wq, wkv, wkp, wz = w_in[:, :o1], w_in[:, o1:o2], w_in[:, o2:o3], w_in[:, o3:]
    wkp2 = jnp.concatenate([wkp, _rope_rot_cols(wkp)], -1)
    wqu = w_q_up.reshape(MLA_Q_RANK, MLA_HEADS, MLA_NOPE + MLA_ROPE)
    wqn = wqu[:, :, :MLA_NOPE].reshape(MLA_Q_RANK, MLA_HEADS * MLA_NOPE)
    wqp = wqu[:, :, MLA_NOPE:]
    wqr = _rope_rot_cols(wqp).reshape(MLA_Q_RANK, MLA_HEADS * MLA_ROPE)
    wqp = wqp.reshape(MLA_Q_RANK, MLA_HEADS * MLA_ROPE)
    full = lambda a: pl.BlockSpec(a.shape, lambda i: (0,) * a.ndim)
    rope_idx = lambda i: jnp.where(i < NT_P, 0, 1 + (i - NT_P) % NT_S_SEQ)
    rows = lambda n: pl.BlockSpec((TM, n), lambda i: (i, 0))
    tab = lambda n: pl.BlockSpec((TM, n), lambda i: (rope_idx(i), 0))
    weights = [wq.astype(BF16), wkv.astype(BF16), wkp2.astype(BF16), wz.astype(BF16),
               q_norm.reshape(1, -1), kv_norm.reshape(1, -1),
               wqn.astype(BF16), wqp.astype(BF16), wqr.astype(BF16)]
    npe = MLA_HEADS * MLA_ROPE
    return pl.pallas_call(
        _mla_in_kernel,
        grid=(NT,),
        in_specs=[rows(D), _mod_spec(layer)] + [full(a) for a in weights]
                 + [tab(MLA_ROPE), tab(MLA_ROPE), tab(npe), tab(npe)],
        out_specs=[rows(D), rows(npe), rows(MLA_KV_RANK), rows(MLA_ROPE), rows(MLA_ROPE), rows(D)],
        out_shape=[jax.ShapeDtypeStruct((T, D), BF16), jax.ShapeDtypeStruct((T, npe), BF16),
                   jax.ShapeDtypeStruct((T, MLA_KV_RANK), F32), jax.ShapeDtypeStruct((T, MLA_ROPE), F32),
                   jax.ShapeDtypeStruct((T, MLA_ROPE), F32), jax.ShapeDtypeStruct((T, D), F32)],
        compiler_params=_cparams(("arbitrary",)),
        name="mla_in_proj",
    )(x, mod, *weights, c32, s32, c512, s512)


def _mla_attn_kernel(*refs, n_cache, hg):
    if n_cache:
        (qn_ref, qp_ref, ckv_ref, kpr_ref, cckv_ref, ckpe_ref, wk_ref, wvt_ref, sz_ref,
         o_ref, kcat_s, vt_s) = refs
    else:
        qn_ref, qp_ref, ckv_ref, kpr_ref, wk_ref, wvt_ref, sz_ref, o_ref, kcat_s, vt_s = refs

    @pl.when(pl.program_id(2) == 0)
    def _():
        cc = ckv_ref[...].astype(BF16)
        kp = kpr_ref[...].astype(BF16)
        if n_cache:
            cc = jnp.concatenate([cckv_ref[...].astype(BF16), cc], axis=0)
            kp = jnp.concatenate([ckpe_ref[...].astype(BF16), kp], axis=0)
        kn = _dot(cc, wk_ref[...]).astype(BF16)
        for hh in range(hg):
            kcat_s[hh] = jnp.concatenate([kn[:, hh * MLA_NOPE:(hh + 1) * MLA_NOPE], kp], axis=1)
            vt_s[hh] = _dot_nt(wvt_ref[hh], cc).astype(BF16)

    qn_all, qp_all = qn_ref[...], qp_ref[...]

    def scores(hh):
        qcat = jnp.concatenate([qn_all[:, hh * MLA_NOPE:(hh + 1) * MLA_NOPE],
                                qp_all[:, hh * MLA_ROPE:(hh + 1) * MLA_ROPE]], axis=1)
        return _dot_nt(kcat_s[hh], qcat)

    outs = []
    ahead = 2
    pending = [scores(hh) for hh in range(min(ahead, hg))]
    for hh in range(hg):
        s = pending.pop(0)
        if hh + ahead < hg:
            pending.append(scores(hh + ahead))
        p = jnp.exp(s - jnp.max(s, axis=0, keepdims=True))
        l = jnp.sum(p, axis=0, keepdims=True)
        outs.append(_dot(vt_s[hh], p.astype(BF16)) / l)
    o_ref[...] = jnp.concatenate(outs, axis=0).T * sz_ref[...]


def _mla_attention(qno, qpe, ckv, kpr, sz, w_kv_up, cache_ckv, cache_kpe, *, nb, L, row_blk0, hg):
    wkv = w_kv_up.reshape(MLA_KV_RANK, MLA_HEADS, MLA_NOPE + MLA_V)
    wk = wkv[:, :, :MLA_NOPE].reshape(MLA_KV_RANK, D).astype(BF16)
    wvt = wkv[:, :, MLA_NOPE:].transpose(1, 2, 0).astype(BF16)
    n_cache = 0 if cache_ckv is None else cache_ckv.shape[1]
    lk = n_cache + L
    nq = L // TM
    wq = hg * MLA_NOPE
    wp = hg * MLA_ROPE
    qrow = lambda w: pl.BlockSpec((TM, w), lambda b, g, q: (row_blk0 + b * nq + q, g))
    seq = lambda w: pl.BlockSpec((L, w), lambda b, g, q: (row_blk0 * TM // L + b, 0))
    in_specs = [qrow(wq), qrow(wp), seq(MLA_KV_RANK), seq(MLA_ROPE)]
    args = [qno, qpe, ckv, kpr]
    if n_cache:
        in_specs += [pl.BlockSpec((None, n_cache, MLA_KV_RANK), lambda b, g, q: (b, 0, 0)),
                     pl.BlockSpec((None, n_cache, MLA_ROPE), lambda b, g, q: (b, 0, 0))]
        args += [cache_ckv, cache_kpe]
    in_specs += [pl.BlockSpec((MLA_KV_RANK, wq), lambda b, g, q: (0, g)),
                 pl.BlockSpec((hg, MLA_V, MLA_KV_RANK), lambda b, g, q: (g, 0, 0)), qrow(wq)]
    args += [wk, wvt, sz]
    return pl.pallas_call(
        functools.partial(_mla_attn_kernel, n_cache=n_cache, hg=hg),
        grid=(nb, MLA_HEADS // hg, nq),
        in_specs=in_specs,
        out_specs=pl.BlockSpec((TM, wq), lambda b, g, q: (b * nq + q, g)),
        out_shape=jax.ShapeDtypeStruct((nb * L, D), F32),
        scratch_shapes=[pltpu.VMEM((hg, lk, MLA_NOPE + MLA_ROPE), BF16), pltpu.VMEM((hg, MLA_V, lk), BF16)],
        compiler_params=_cparams(("arbitrary", "arbitrary", "arbitrary")),
        name="mla_attention",
    )(*args)


def _mla_layer(x, mod, layer, rope, ln_g, ln_b, cache_ckv, cache_kpe, w_in, q_norm, kv_norm, w_q_up, w_kv_up, w_out):
    qno, qpe, ckv, kpe, kpr, sz = _mla_in_proj(x, mod, layer, w_in, q_norm, kv_norm, w_q_up, rope)
    g_p = _mla_attention(qno, qpe, ckv, kpr, sz, w_kv_up, None, None, nb=B_P, L=L_P, row_blk0=0,
                         hg=MLA_HEADS)
    g_s = _mla_attention(qno, qpe, ckv, kpr, sz, w_kv_up, cache_ckv, cache_kpe, nb=B_S, L=L_S, row_blk0=NT_P,
                         hg=MLA_HEADS // 2)
    x_new = _out_proj_ln2(g_p, g_s, w_out, x, mod, layer, ln_g, ln_b)
    new_ckv = ckv[:T_P].reshape(B_P, 1, L_P, MLA_KV_RANK)
    new_kpe = kpe[:T_P].reshape(B_P, 1, L_P, MLA_ROPE)
    return x_new, new_ckv, new_kpe


def _rw_in_kernel(x_ref, xp_ref, xn_ref, m_ref, mu_ref, wr_ref, wk_ref, wv_ref, wg_ref, w1_ref, a1_ref,
                  w2_ref, a2_ref, w0_ref, a0_ref, kk_ref, ka_ref, rk_ref, ones_ref,
                  r_ref, v_ref, sz_ref, nkk_ref, bonus_ref, lw_ref, kd_ref, bd_ref):
    i = pl.program_id(0)
    has_prev, has_next = _tile_has_neighbours(i)
    m = m_ref[...]
    h = _modulate(x_ref[...], m)
    prev, nxt = _neighbour_rows(h, _modulate(xp_ref[...], m), _modulate(xn_ref[...], m), has_prev, has_next)
    d = 0.5 * (prev + nxt) - h
    mu = mu_ref[...]

    def mix(p):
        return (h + d * mu[p:p + 1]).astype(BF16)

    r = _dot(mix(0), wr_ref[...])
    tw = jnp.tanh(_dot(mix(1), w1_ref[...])).astype(BF16)
    k = _dot(mix(2), wk_ref[...])
    v = _dot(mix(3), wv_ref[...])
    ta = _dot(mix(4), a1_ref[...]).astype(BF16)
    z = _dot(mix(5), wg_ref[...])
    r_ref[...] = r
    v_ref[...] = v
    sz_ref[...] = _silu(z)
    ones_bd = ones_ref[...]
    kk = k * kk_ref[...]
    kk = kk * lax.rsqrt(_head_sum(kk * kk, ones_bd) + 1e-12)
    nkk_ref[...] = -kk
    coef = jnp.zeros_like(r)
    for n in range(2):
        wl = w0_ref[n:n + 1, :] + _dot(tw, w2_ref[n])
        lw_ref[n] = -math.exp(-0.5) * _sigmoid(wl)
        a = _sigmoid(a0_ref[n:n + 1, :] + _dot(ta, a2_ref[n]))
        kd = k * (1.0 + (a - 1.0) * ka_ref[...])
        kd_ref[n] = kd
        bd_ref[n] = kk * a
        coef = coef + r * kd * rk_ref[...]
    bonus_ref[...] = _head_sum(coef, ones_bd) * v


def _pad_lora_up(w):
    z = jnp.zeros_like(w[0])
    return jnp.stack([jnp.concatenate([w[0], z], 0), jnp.concatenate([z, w[1]], 0)])


def _head_ones():
    h = np.arange(LANES) // RW_N
    return jnp.asarray(h[:, None] == h[None, :], dtype=BF16)


def _rw_in_proj(x, mod, layer, ones_bd, mu, w_in, w0, w1, w2, a0, a1, a2, k_k, k_a, r_k):
    mu8 = jnp.pad(mu, ((0, 2), (0, 0)))
    w1c = jnp.concatenate([w1[0], w1[1]], -1).astype(BF16)
    a1c = jnp.concatenate([a1[0], a1[1]], -1).astype(BF16)
    w2p = _pad_lora_up(w2).astype(BF16)
    a2p = _pad_lora_up(a2).astype(BF16)
    row = pl.BlockSpec((TM, D), lambda i: (i, 0))
    row2 = pl.BlockSpec((2, TM, D), lambda i: (0, i, 0))
    prev, nxt = _halo_specs(D, lambda i: 0)
    full = lambda a: pl.BlockSpec(a.shape, lambda i: (0,) * a.ndim)
    consts = [mu8, w_in[0].astype(BF16), w_in[1].astype(BF16), w_in[2].astype(BF16), w_in[3].astype(BF16),
              w1c, a1c, w2p, a2p, w0, a0, k_k.reshape(1, D), k_a.reshape(1, D), r_k.reshape(1, D), ones_bd]
    return pl.pallas_call(
        _rw_in_kernel,
        grid=(NT,),
        in_specs=[row, prev, nxt, _mod_spec(layer)] + [full(a) for a in consts],
        out_specs=[row] * 5 + [row2] * 3,
        out_shape=[jax.ShapeDtypeStruct((T, D), F32)] * 5 + [jax.ShapeDtypeStruct((2, T, D), F32)] * 3,
        compiler_params=_cparams(("arbitrary",)),
        name="rwkv_in_proj",
    )(x, x, x, mod, *consts)


SCAN_NB = 2
SCAN_NU = 2 * SCAN_NB


def _rw_scan_kernel(*refs):
    rb_refs = refs[:SCAN_NU]
    first_ref = refs[SCAN_NU]
    ins = refs[SCAN_NU + 1:SCAN_NU + 1 + 6 * SCAN_NU]
    s0_ref, y_ref, s_ref = refs[SCAN_NU + 1 + 6 * SCAN_NU:]
    del rb_refs
    step = pl.program_id(0)
    c = CHUNK

    @pl.when(first_ref[step] == 1)
    def _():
        s_ref[...] = s0_ref[...]

    ri = lax.broadcasted_iota(jnp.int32, (c, c), 0)
    ci = lax.broadcasted_iota(jnp.int32, (c, c), 1)
    ri2 = lax.broadcasted_iota(jnp.int32, (c, 2 * c), 0)
    ci2 = lax.broadcasted_iota(jnp.int32, (c, 2 * c), 1)
    cm2 = jnp.where(ci2 >= c, ci2 - c, ci2)
    eye_f = (ri == ci).astype(F32)
    eye = eye_f.astype(BF16)
    masks = []
    for sgn in (1, -1):
        incl = (ri - ci) * sgn >= 0
        strict = (ri - ci) * sgn > 0
        mask_2 = (ri2 - cm2) * sgn >= 0
        mask_k = jnp.logical_and((ri2 - cm2) * sgn > 0, ci2 >= c)
        off_masks = []
        for lg in range(6):
            same_2m = (ri >> (lg + 1)) == (ci >> (lg + 1))
            diff_m = (ri >> lg) != (ci >> lg)
            off_masks.append(jnp.logical_and(strict, jnp.logical_and(same_2m, diff_m)))
        masks.append((incl, mask_2, mask_k, off_masks))

    chains = [(q, h) for q in range(SCAN_NU) for h in range(RW_H)]
    ar, bk, bke, vh = {}, {}, {}, {}
    for q in range(SCAN_NU):
        r_ref, v_ref, nkk_ref, lw_ref, kd_ref, bd_ref = ins[6 * q:6 * q + 6]
        bwd = q % 2
        lw = lw_ref[...]
        g = _dot_hi(masks[bwd][0].astype(F32), lw)
        gtot = g[0:1, :] if bwd else g[c - 1:c, :]
        e_inv = jnp.exp(-g)
        e_rem = jnp.exp(gtot - g)
        e_hi, e_lo = _split(jnp.broadcast_to(jnp.exp(gtot), (ROWS_BF16, D)))
        a_t = nkk_ref[...] * jnp.exp(g - lw)
        r_t = r_ref[...] * jnp.exp(g)
        kd, bd = kd_ref[...], bd_ref[...]
        b_t, k_t, b_e, k_e = bd * e_inv, kd * e_inv, bd * e_rem, kd * e_rem
        v = v_ref[...]
        for h in range(RW_H):
            sl = slice(h * RW_N, (h + 1) * RW_N)
            cat = lambda p, r_: jnp.concatenate([p[:, sl], r_[:, sl]], axis=0).astype(BF16)
            ar[q, h], bk[q, h] = cat(a_t, r_t), cat(b_t, k_t)
            bke[q, h] = jnp.concatenate([cat(b_e, k_e), e_hi[:, sl], e_lo[:, sl]], axis=0)
            vh[q, h] = v[:, sl]
    h_old = {ch: s_ref[ch[0], ch[1]] for ch in chains}
    gm = {ch: _dot_nt(ar[ch], bk[ch]) for ch in chains}
    a_s = {ch: _dot(ar[ch], h_old[ch].astype(BF16)) for ch in chains}
    bke_t = {ch: _dot_nt(eye, bke[ch]) for ch in chains}
    lab = {ch: gm[ch][:c, :c] for ch in chains}
    offs = lambda ch: masks[ch[0] % 2][3]
    tinv = {ch: eye_f + jnp.where(offs(ch)[0], lab[ch], 0.0) for ch in chains}
    for lg in range(1, 6):
        tb = {ch: tinv[ch].astype(BF16) for ch in chains}
        lt = {ch: _dot(jnp.where(offs(ch)[lg], lab[ch], 0.0).astype(BF16), tb[ch]) for ch in chains}
        tinv = {ch: tinv[ch] + _dot(tb[ch], lt[ch].astype(BF16)) for ch in chains}
    vv2 = {ch: jnp.concatenate([vh[ch], vh[ch]], axis=0).astype(BF16) for ch in chains}
    w1 = {ch: a_s[ch][:c] + _dot(jnp.where(masks[ch[0] % 2][2], gm[ch][:c], 0.0).astype(BF16), vv2[ch])
          for ch in chains}
    u = {ch: _dot(tinv[ch].astype(BF16), w1[ch].astype(BF16)) for ch in chains}
    uv = {ch: jnp.concatenate([u[ch], vh[ch]], axis=0).astype(BF16) for ch in chains}
    y = {ch: a_s[ch][c:] + _dot(jnp.where(masks[ch[0] % 2][1], gm[ch][c:], 0.0).astype(BF16), uv[ch])
         for ch in chains}
    for q in range(SCAN_NU):
        y_ref[q] = jnp.concatenate([y[q, h] for h in range(RW_H)], axis=1)
    upd = {ch: _dot(bke_t[ch][:, :2 * c].astype(BF16), uv[ch]) for ch in chains}
    lo_col = 2 * c + ROWS_BF16
    h_new = {ch: h_old[ch] * (bke_t[ch][:, 2 * c:2 * c + 1] + bke_t[ch][:, lo_col:lo_col + 1]) + upd[ch]
             for ch in chains}
    for ch in chains:
        s_ref[ch[0], ch[1]] = h_new[ch]


def _rw_scan_tables():
    rb = [[] for _ in range(SCAN_NU)]
    first = []
    for nb, L, blk0 in ((B_P, L_P, 0), (B_S, L_S, T_P // CHUNK)):
        nc = L // CHUNK
        for grp in range(nb // SCAN_NB):
            for j in range(nc):
                for q in range(SCAN_NU):
                    b = grp * SCAN_NB + q // 2
                    cn = j if q % 2 == 0 else nc - 1 - j
                    rb[q].append(blk0 + b * nc + cn)
                first.append(1 if j == 0 else 0)
    as_i32 = lambda a: jnp.asarray(np.asarray(a, np.int32))
    return [as_i32(a) for a in rb], as_i32(first)


def _rw_scan(r, v, nkk, lw, kd, bd, s0_all):
    rb, first = _rw_scan_tables()
    n_steps = first.shape[0]
    n_groups = s0_all.shape[0] // SCAN_NU
    steps_p = (B_P // SCAN_NB) * (L_P // CHUNK)
    in_specs = []
    args = []
    for q in range(SCAN_NU):
        tok = pl.BlockSpec((CHUNK, D), lambda s, *t, q=q: (t[q][s], 0))
        tok2 = pl.BlockSpec((None, CHUNK, D), lambda s, *t, q=q: (q % 2, t[q][s], 0))
        in_specs += [tok, tok, tok, tok2, tok2, tok2]
        args += [r, v, nkk, lw, kd, bd]
    grp = lambda s: jnp.where(s < steps_p, s // (L_P // CHUNK),
                              B_P // SCAN_NB + (s - steps_p) // (L_S // CHUNK))
    st = pl.BlockSpec((None, SCAN_NU, RW_H, RW_N, RW_N), lambda s, *t: (grp(s), 0, 0, 0, 0))
    s0g = s0_all.reshape(n_groups, SCAN_NU, RW_H, RW_N, RW_N)
    y, s_fin = pl.pallas_call(
        _rw_scan_kernel,
        grid_spec=pltpu.PrefetchScalarGridSpec(
            num_scalar_prefetch=SCAN_NU + 1,
            grid=(n_steps,),
            in_specs=in_specs + [st],
            out_specs=[pl.BlockSpec((None, SCAN_NU, CHUNK, D), lambda s, *t: (s, 0, 0, 0)), st],
        ),
        out_shape=[jax.ShapeDtypeStruct((n_steps, SCAN_NU, CHUNK, D), F32),
                   jax.ShapeDtypeStruct(s0g.shape, F32)],
        compiler_params=_cparams(("arbitrary",)),
        name="rwkv_scan",
    )(*rb, first, *args, s0g)
    return y, s_fin.reshape(s0_all.shape)


def _scan_out_index(i, k, bwd):
    per_tile = TM // CHUNK
    nc_p, nc_s = L_P // CHUNK, L_S // CHUNK
    steps_p = (B_P // SCAN_NB) * nc_p
    cn_p = k
    b_s = (i - NT_P) // NT_S_SEQ
    cn_s = ((i - NT_P) % NT_S_SEQ) * per_tile + k
    step_p = (i // SCAN_NB) * nc_p + (nc_p - 1 - cn_p if bwd else cn_p)
    step_s = steps_p + (b_s // SCAN_NB) * nc_s + (nc_s - 1 - cn_s if bwd else cn_s)
    q_p = (i % SCAN_NB) * 2 + bwd
    q_s = (b_s % SCAN_NB) * 2 + bwd
    is_p = i < NT_P
    return jnp.where(is_p, step_p, step_s), jnp.where(is_p, q_p, q_s)


def _rw_out_kernel(*refs):
    per_tile = TM // CHUNK
    y_refs = refs[:2 * per_tile]
    bonus_ref, sz_ref, gg_ref, gb_ref, ones_ref, w_ref, x_ref, m_ref, lng_ref, lnb_ref, o_ref = refs[2 * per_tile:]
    ones_bd = ones_ref[...]
    y = jnp.concatenate([y_refs[k][...] + y_refs[per_tile + k][...] for k in range(per_tile)], axis=0)
    mean = _head_sum(y, ones_bd) * (1.0 / RW_N)
    yc = y - mean
    var = _head_sum(yc * yc, ones_bd) * (1.0 / RW_N)
    yn = yc * lax.rsqrt(var + RW_GN_EPS) * gg_ref[...] + gb_ref[...]
    g = (yn + bonus_ref[...]) * sz_ref[...]
    _out_ln_tail(g, w_ref, x_ref, m_ref, lng_ref, lnb_ref, o_ref)


def _rw_out_proj_ln(y, bonus, sz, gn_g, gn_b, ones_bd, w_out, x, mod, layer, ln_g, ln_b):
    row = pl.BlockSpec((TM, D), lambda i: (i, 0))
    vec = pl.BlockSpec((1, D), lambda i: (0, 0))
    mat = pl.BlockSpec((D, D), lambda i: (0, 0))
    y_specs = [pl.BlockSpec((None, None, CHUNK, D), lambda i, k=k, bwd=bwd: _scan_out_index(i, k, bwd) + (0, 0))
               for bwd in (0, 1) for k in range(TM // CHUNK)]
    return pl.pallas_call(
        _rw_out_kernel,
        grid=(NT,),
        in_specs=y_specs + [row, row, vec, vec,
                  pl.BlockSpec((LANES, LANES), lambda i: (0, 0)), mat, row,
                  _mod_spec(layer), vec, vec],
        out_specs=row,
        out_shape=jax.ShapeDtypeStruct((T, D), F32),
        compiler_params=_cparams(("arbitrary",)),
        name="rwkv_out_proj_ln",
    )(*[y] * len(y_specs), bonus, sz, gn_g.reshape(1, D), gn_b.reshape(1, D), ones_bd, w_out.astype(BF16), x, mod,
      ln_g.reshape(1, D), ln_b.reshape(1, D))


def _rwkv_layer(x, mod, layer, ln_g, ln_b, state, mu, w_in, w0, w1, w2, a0, a1, a2, k_k, k_a, r_k, gn_g, gn_b, w_out):
    ones_bd = _head_ones()
    r, v, sz, nkk, bonus, lw, kd, bd = _rw_in_proj(x, mod, layer, ones_bd, mu, w_in, w0, w1, w2, a0, a1, a2,
                                                  k_k, k_a, r_k)
    n_p = B_P * 2
    s0_all = jnp.concatenate([jnp.zeros((n_p, RW_H, RW_N, RW_N), F32),
                              state.astype(F32).reshape(B_S * 2, RW_H, RW_N, RW_N).swapaxes(-1, -2)], 0)
    y, h_fin = _rw_scan(r, v, nkk, lw, kd, bd, s0_all)
    x_new = _rw_out_proj_ln(y, bonus, sz, gn_g, gn_b, ones_bd, w_out, x, mod, layer, ln_g, ln_b)
    new_state = h_fin[:n_p].swapaxes(-1, -2).reshape(B_P, 1, 2, RW_H, RW_N, RW_N)
    return x_new, new_state


def kernel(x_prompt, x_sample, cache_mla_ckv, cache_mla_kpe, state_rwkv, c, c_ctx, mod_w, mod_b, ln_g, ln_b, hy_w_in, hy_conv_w, hy_conv_b, hy_ffn_w1, hy_ffn_b1, hy_ffn_w2, hy_ffn_b2, hy_ffn_w3, hy_ffn_b3, hy_freq, hy_decay, hy_skip, hy_w_out, mla_w_in, mla_q_norm, mla_kv_norm, mla_w_q_up, mla_w_kv_up, mla_w_out, rw_mu, rw_w_in, rw_w0, rw_w1, rw_w2, rw_a0, rw_a1, rw_a2, rw_k_k, rw_k_a, rw_r_k, rw_gn_g, rw_gn_b, rw_w_out):
    x = jnp.concatenate([x_prompt.reshape(T_P, D), x_sample.reshape(T_S, D)], 0)
    cond8 = jnp.concatenate([c_ctx[None, :], c, jnp.zeros((8 - 1 - B_S, D), F32)], 0)
    mod = _modulation_table(cond8, mod_w, mod_b).reshape(DEPTH * 8, 1, 3 * D)
    tabs_p = _dft_tables(L_P)
    tabs_s = _dft_tables(L_S)
    rope = _rope_tables()
    new_ckv = new_kpe = new_state = None
    for i in range(DEPTH):
        kind, j = i % 3, i // 3
        if kind == 0:
            x = _hyena_layer(x, mod, i, tabs_p, tabs_s, ln_g[i], ln_b[i], hy_w_in[j], hy_conv_w[j], hy_conv_b[j],
                             hy_ffn_w1[j], hy_ffn_b1[j], hy_ffn_w2[j], hy_ffn_b2[j], hy_ffn_w3[j], hy_ffn_b3[j],
                             hy_freq[j], hy_decay[j], hy_skip[j], hy_w_out[j])
        elif kind == 1:
            x, new_ckv, new_kpe = _mla_layer(x, mod, i, rope, ln_g[i], ln_b[i], cache_mla_ckv[:, j],
                                             cache_mla_kpe[:, j], mla_w_in[j], mla_q_norm[j], mla_kv_norm[j],
                                             mla_w_q_up[j], mla_w_kv_up[j], mla_w_out[j])
        else:
            x, new_state = _rwkv_layer(x, mod, i, ln_g[i], ln_b[i], state_rwkv[:, j], rw_mu[j], rw_w_in[j],
                                       rw_w0[j], rw_w1[j], rw_w2[j], rw_a0[j], rw_a1[j], rw_a2[j], rw_k_k[j],
                                       rw_k_a[j], rw_r_k[j], rw_gn_g[j], rw_gn_b[j], rw_w_out[j])
    return (x[:T_P].reshape(B_P, L_P, D), x[T_P:].reshape(B_S, L_S, D), new_ckv, new_kpe, new_state)
```

```python
import functools
import math

import numpy as np
import jax
import jax.numpy as jnp
from jax import lax
from jax.experimental import pallas as pl
from jax.experimental.pallas import tpu as pltpu

F32 = jnp.float32
BF16 = jnp.bfloat16
HIGHEST = lax.Precision.HIGHEST

D = 1024
B_P, L_P = 16, 256
B_S, L_S = 2, 2048
T_P = B_P * L_P
T_S = B_S * L_S
T = T_P + T_S
PAST = 512
DEPTH = 4
DEEPNORM_ALPHA = (2.0 * DEPTH) ** 0.25
LN_EPS = 1e-5
RMS_EPS = 1e-6
HY_BANDS = 16
HY_FFN = 64
MLA_HEADS = 16
MLA_Q_RANK = 256
MLA_KV_RANK = 128
MLA_NOPE = 64
MLA_ROPE = 32
MLA_V = 64
ROPE_BASE = 10000.0
GRID_W = 64
RW_N = 64
RW_H = D // RW_N
RW_LORA = 64
RW_GN_EPS = 64e-5

TM = 256
NT_P = T_P // TM
NT_S_SEQ = L_S // TM
NT = T // TM
HALO = 8
LANES = 128
ROWS_BF16 = 16
CHUNK = 64
DFT_TK = 512
VMEM_LIMIT = 52 * 1024 * 1024


def _cparams(sem):
    return pltpu.CompilerParams(dimension_semantics=sem, vmem_limit_bytes=VMEM_LIMIT)


def _group(i):
    return jnp.where(i < NT_P, 0, 1 + (i - NT_P) // NT_S_SEQ)


def _sigmoid(x):
    return 1.0 / (1.0 + jnp.exp(-x))


def _silu(x):
    return x * _sigmoid(x)


def _dot(a, b):
    return jnp.dot(a, b, preferred_element_type=F32)


def _dot_nt(a, b):
    return lax.dot_general(a, b, (((1,), (1,)), ((), ())), preferred_element_type=F32)


def _dot_hi(a, b):
    return jnp.dot(a, b, preferred_element_type=F32, precision=HIGHEST)


def _split(x):
    hi = x.astype(BF16)
    lo = (x - hi.astype(F32)).astype(BF16)
    return hi, lo


def _head_sum(x, ones_bd):
    hi, lo = _split(x)
    lanes = ones_bd.shape[0]
    parts = []
    for g in range(x.shape[1] // lanes):
        sl = slice(g * lanes, (g + 1) * lanes)
        parts.append(_dot(hi[:, sl], ones_bd) + _dot(lo[:, sl], ones_bd))
    return jnp.concatenate(parts, axis=1)


def _modulate(x, m):
    return x * (1.0 + m[:, D:2 * D]) + m[:, :D]


def _layer_norm_rows(y, g, b):
    mu = jnp.mean(y, axis=-1, keepdims=True)
    yc = y - mu
    var = jnp.mean(yc * yc, axis=-1, keepdims=True)
    return yc * lax.rsqrt(var + LN_EPS) * g + b


def _neighbour_rows(cur, prev_halo, next_halo, has_prev, has_next):
    rows = cur.shape[0]
    ridx = lax.broadcasted_iota(jnp.int32, cur.shape, 0)
    pr = jnp.where(has_prev, prev_halo[HALO - 1:HALO, :], 0.0)
    nx = jnp.where(has_next, next_halo[0:1, :], 0.0)
    prev = jnp.where(ridx == 0, pr, pltpu.roll(cur, 1, axis=0))
    nxt = jnp.where(ridx == rows - 1, nx, pltpu.roll(cur, rows - 1, axis=0))
    return prev, nxt


def _tile_has_neighbours(i):
    k = (i - NT_P) % NT_S_SEQ
    is_s = i >= NT_P
    return jnp.logical_and(is_s, k != 0), jnp.logical_and(is_s, k != NT_S_SEQ - 1)


def _halo_specs(width, col_of):
    r = TM // HALO
    prev = pl.BlockSpec((HALO, width), lambda i, *a: (jnp.maximum(i * r - 1, 0), col_of(i, *a)))
    nxt = pl.BlockSpec((HALO, width), lambda i, *a: (jnp.minimum((i + 1) * r, T // HALO - 1), col_of(i, *a)))
    return prev, nxt


def _mod_kernel(c_ref, w_ref, b_ref, o_ref):
    o_ref[...] = _dot_hi(_silu(c_ref[...]), w_ref[...]) + b_ref[...]


def _modulation_table(cond8, mod_w, mod_b):
    tn = 1024
    return pl.pallas_call(
        _mod_kernel,
        grid=(DEPTH, 3 * D // tn),
        in_specs=[pl.BlockSpec((8, D), lambda l, j: (0, 0)),
                  pl.BlockSpec((None, D, tn), lambda l, j: (l, 0, j)),
                  pl.BlockSpec((None, 1, tn), lambda l, j: (l, 0, j))],
        out_specs=pl.BlockSpec((None, 8, tn), lambda l, j: (l, 0, j)),
        out_shape=jax.ShapeDtypeStruct((DEPTH, 8, 3 * D), F32),
        compiler_params=_cparams(("arbitrary", "arbitrary")),
        name="modulation",
    )(cond8, mod_w, mod_b.reshape(DEPTH, 1, 3 * D))


def _mod_spec(layer):
    return pl.BlockSpec((None, 1, 3 * D), lambda i, *a: (layer * 8 + _group(i), 0, 0))


def _out_ln_tail(g, w_ref, x_ref, m_ref, lng_ref, lnb_ref, o_ref):
    mix = _dot(g.astype(BF16), w_ref[...])
    gate = m_ref[...][:, 2 * D:]
    y = DEEPNORM_ALPHA * x_ref[...] + gate * mix
    o_ref[...] = _layer_norm_rows(y, lng_ref[...], lnb_ref[...])


def _out_ln2_kernel(gp_ref, gs_ref, w_ref, x_ref, m_ref, lng_ref, lnb_ref, o_ref):
    g = jnp.where(pl.program_id(0) < NT_P, gp_ref[...], gs_ref[...])
    _out_ln_tail(g, w_ref, x_ref, m_ref, lng_ref, lnb_ref, o_ref)


def _out_proj_ln2(g_p, g_s, w_out, x, mod, layer, ln_g, ln_b):
    row = pl.BlockSpec((TM, D), lambda i: (i, 0))
    vec = pl.BlockSpec((1, D), lambda i: (0, 0))
    return pl.pallas_call(
        _out_ln2_kernel,
        grid=(NT,),
        in_specs=[pl.BlockSpec((TM, D), lambda i: (jnp.minimum(i, NT_P - 1), 0)),
                  pl.BlockSpec((TM, D), lambda i: (jnp.maximum(i - NT_P, 0), 0)),
                  pl.BlockSpec((D, D), lambda i: (0, 0)),
                  row, _mod_spec(layer), vec, vec],
        out_specs=row,
        out_shape=jax.ShapeDtypeStruct((T, D), F32),
        compiler_params=_cparams(("arbitrary",)),
        name="out_proj_ln",
    )(g_p, g_s, w_out.astype(BF16), x, mod, ln_g.reshape(1, D), ln_b.reshape(1, D))


def _hy_in_conv_kernel(x_ref, xp_ref, xn_ref, m_ref, w_ref, cw_ref, cb_ref, vv_ref, gate_ref):
    has_prev, has_next = _tile_has_neighbours(pl.program_id(0))
    x_all = jnp.concatenate([xp_ref[...], x_ref[...], xn_ref[...]], axis=0)
    u = _dot(_modulate(x_all, m_ref[...]).astype(BF16), w_ref[...])
    rows = TM + 2 * HALO
    ridx = lax.broadcasted_iota(jnp.int32, (TM, 1), 0)
    no_prev = jnp.logical_and(ridx == 0, jnp.logical_not(has_prev))
    no_next = jnp.logical_and(ridx == TM - 1, jnp.logical_not(has_next))
    cw = cw_ref[...]
    cb = cb_ref[...]

    def conv(grp):
        sl = slice(grp * D, (grp + 1) * D)
        ug = u[:, sl]
        cur = ug[HALO:HALO + TM]
        prev = jnp.where(no_prev, 0.0, pltpu.roll(ug, 1, axis=0)[HALO:HALO + TM])
        nxt = jnp.where(no_next, 0.0, pltpu.roll(ug, rows - 1, axis=0)[HALO:HALO + TM])
        return prev * cw[0:1, sl] + cur * cw[1:2, sl] + nxt * cw[2:3, sl] + cb[:, sl]

    x0, x1, v = conv(0), conv(1), conv(2)
    vv_ref[...] = v * x1
    gate_ref[...] = x0 * _silu(u[HALO:HALO + TM, 3 * D:])


def _hy_in_conv(x, mod, layer, w_in, conv_w, conv_b):
    row = pl.BlockSpec((TM, D), lambda i: (i, 0))
    prev, nxt = _halo_specs(D, lambda i: 0)
    return pl.pallas_call(
        _hy_in_conv_kernel,
        grid=(NT,),
        in_specs=[row, prev, nxt, _mod_spec(layer), _resident(w_in.shape),
                  pl.BlockSpec((3, 3 * D), lambda i: (0, 0)), pl.BlockSpec((1, 3 * D), lambda i: (0, 0))],
        out_specs=[row, row],
        out_shape=[jax.ShapeDtypeStruct((T, D), F32)] * 2,
        compiler_params=_cparams(("arbitrary",)),
        name="hyena_in_proj_conv3",
    )(x, x, x, mod, w_in.astype(BF16), conv_w, conv_b.reshape(1, 3 * D))


def _hy_filter_kernel(t_ref, bands_ref, wt_ref, wc_ref, ws_ref, b1_ref, w2_ref, b2_ref, w3_ref, b3_ref,
                      f0_ref, f1_ref, dec_ref, hs_ref, hd_ref, nyq_ref, *, L, tr):
    i = pl.program_id(0)
    ridx = lax.broadcasted_iota(jnp.int32, (tr, 1), 0) + i * tr
    pos = ridx.astype(F32)
    t = t_ref[...]
    ang = ((2.0 * math.pi / L) * pos) * bands_ref[...]
    pre = t * wt_ref[...] + _dot_hi(jnp.cos(ang), wc_ref[...]) + _dot_hi(jnp.sin(ang), ws_ref[...])
    hdn = jnp.sin(f0_ref[...] * (pre + b1_ref[...]))
    hdn = jnp.sin(f1_ref[...] * (_dot_hi(hdn, w2_ref[...]) + b2_ref[...]))
    hf = _dot_hi(hdn, w3_ref[...]) + b3_ref[...]
    h = hf * jnp.exp(-t * jnp.abs(dec_ref[...]))
    h0 = h[:, :D]
    h1 = jnp.where(ridx == 0, 0.0, h[:, D:])
    hsum = h0 + h1
    hs_ref[...] = hsum
    hd_ref[...] = h1 - h0
    alt = jnp.where((ridx & 1) == 0, 1.0, -1.0)
    part =jnp.broadcast_to(jnp.sum(alt * hsum, axis=0, keepdims=True), (8, D))

    @pl.when(i == 0)
    def _():
        nyq_ref[...] = part

    @pl.when(i > 0)
    def _():
        nyq_ref[...] += part


def _hy_filter(L, w1, b1, w2, b2, w3, b3, freq, decay):
    tr = 256
    t = jnp.linspace(0.0, 1.0, L, dtype=F32).reshape(L, 1)
    bands = jnp.linspace(1e-4, HY_BANDS - 1, HY_BANDS, dtype=F32)
    bands = jnp.pad(bands, (0, 128 - HY_BANDS)).reshape(1, 128)
    wt = w1[0:1]
    wc = jnp.pad(w1[1:1 + HY_BANDS], ((0, 128 - HY_BANDS), (0, 0)))
    ws = jnp.pad(-w1[1 + HY_BANDS:], ((0, 128 - HY_BANDS), (0, 0)))
    full = lambda shape: pl.BlockSpec(shape, lambda i: (0, 0))
    rows = pl.BlockSpec((tr, D), lambda i: (i, 0))
    return pl.pallas_call(
        functools.partial(_hy_filter_kernel, L=L, tr=tr),
        grid=(L // tr,),
        in_specs=[pl.BlockSpec((tr, 1), lambda i: (i, 0)), full((1, 128)), full((1, HY_FFN)),
                  full((128, HY_FFN)), full((128, HY_FFN)), full((1, HY_FFN)),
                  full((HY_FFN, HY_FFN)), full((1, HY_FFN)), full((HY_FFN, 2 * D)), full((1, 2 * D)),
                  full((1, HY_FFN)), full((1, HY_FFN)), full((1, 2 * D))],
        out_specs=[rows, rows, pl.BlockSpec((8, D), lambda i: (0, 0))],
        out_shape=[jax.ShapeDtypeStruct((L, D), F32), jax.ShapeDtypeStruct((L, D), F32),
                   jax.ShapeDtypeStruct((8, D), F32)],
        compiler_params=_cparams(("arbitrary",)),
        name="hyena_filter",
    )(t, bands, wt, wc, ws, b1.reshape(1, -1), w2, b2.reshape(1, -1), w3, b3.reshape(1, -1),
      freq[0:1], freq[1:2], decay.reshape(1, 2 * D))


def _dft_tables(L):
    n = 2 * L
    w = 64
    k = jnp.arange(L, dtype=jnp.int32)

    def cs(t):
        ang = ((k[:, None] * t[None, :]) % n).astype(F32) * (2.0 * math.pi / n)
        return jnp.cos(ang), jnp.sin(ang)

    tk = min(L, DFT_TK)
    nk = L // tk
    ca, sa = cs(jnp.arange(L // w, dtype=jnp.int32) * w)
    cb, sb = cs(jnp.arange(w, dtype=jnp.int32))
    alt = jnp.where(k % 2 == 0, 1.0, -1.0).astype(F32)

    def row_tiles(p, q, r, s, sgn):
        tab = p[:, :, None] * q[:, None, :] + sgn * r[:, :, None] * s[:, None, :]
        return tab.reshape(nk, tk, L)

    def col_tiles(p, q, r, s, sgn):
        tr = lambda a: a.reshape(nk, tk, -1).transpose(0, 2, 1)
        tab = tr(p)[:, :, None, :] * tr(q)[:, None, :, :] + sgn * tr(r)[:, :, None, :] * tr(s)[:, None, :, :]
        return tab.reshape(nk, L, tk)

    bins = k.reshape(nk, tk)
    c_rows = row_tiles(ca, cb, sa, sb, -1.0)
    c_cols = col_tiles(ca, cb, sa, sb, -1.0)
    s_rows = jnp.where(bins[:, :, None] == 0, alt[None, None, :], row_tiles(sa, cb, ca, sb, 1.0))
    s_cols = jnp.where(bins[:, None, :] == 0, alt[None, :, None], col_tiles(sa, cb, ca, sb, 1.0))
    return tuple(a.astype(BF16) for a in (c_rows, s_rows, c_cols, s_cols))


def _dft_fwd_kernel(c_ref, s_ref, *refs):
    x1_ref, x2_ref = refs[0], refs[-3]
    oc_ref, os_ref = refs[-2:]
    k = pl.program_id(2)
    oc_ref[...] = _dot(c_ref[k], x1_ref[...].astype(BF16))
    os_ref[...] = _dot(s_ref[k], x2_ref[...].astype(BF16))


def _resident(shape):
    return pl.BlockSpec(shape, lambda *_: (0,) * len(shape), pipeline_mode=pl.Buffered(1))


def _dft_fwd(tabs, xs, L, nb, row_blk0):
    c, s_rows = tabs[:2]
    nk, tk, _ = c.shape
    tn = 512
    a_spec = _resident(c.shape)
    x_spec = pl.BlockSpec((L, tn), lambda b, j, k: (row_blk0 + b, j))
    o_spec = pl.BlockSpec((tk, tn), lambda b, j, k: (b * nk + k, j))
    return pl.pallas_call(
        _dft_fwd_kernel,
        grid=(nb, D // tn, nk),
        in_specs=[a_spec] * 2 + [x_spec] * len(xs),
        out_specs=[o_spec, o_spec],
        out_shape=[jax.ShapeDtypeStruct((nb * L, D), F32)] * 2,
        compiler_params=_cparams(("arbitrary", "arbitrary", "arbitrary")),
        name="hyena_dft_fwd",
    )(c, s_rows, *xs)


def _dft_inv_kernel(c_ref, st_ref, vc_ref, vs_ref, kre_ref, kim_ref, nyq_ref,
                    vv_ref, skip_ref, gate_ref, o_ref, *, L):
    k = pl.program_id(2)
    nk = pl.num_programs(2)
    vc, vs, kre = vc_ref[...], vs_ref[...], kre_ref[...]
    bin0 = jnp.logical_and(lax.broadcasted_iota(jnp.int32, vc.shape, 0) == 0, k == 0)
    kim = jnp.where(bin0, nyq_ref[0:1, :], kim_ref[...])
    inv_n = 1.0 / (2 * L)
    yre = jnp.where(bin0, vc * kre * inv_n, (vc * kre + vs * kim) * (2.0 * inv_n))
    yim = jnp.where(bin0, vs * kim * inv_n, (vs * kre - vc * kim) * (2.0 * inv_n))
    contrib = _dot(c_ref[k], yre.astype(BF16)) + _dot(st_ref[k], yim.astype(BF16))

    @pl.when(k == 0)
    def _():
        o_ref[...] = contrib

    @pl.when(k > 0)
    def _():
        o_ref[...] += contrib

    @pl.when(k == nk - 1)
    def _():
        o_ref[...] = (o_ref[...] + vv_ref[...] * skip_ref[...]) * gate_ref[...]


def _dft_inv(tabs, vc, vs, kre, kim, nyq, vv, skip, gate, L, nb, row_blk0):
    c, s_cols = tabs[2:]
    nk, _, tk = c.shape
    tn = 512 if L <= 512 else 256
    a_spec = _resident(c.shape)
    v_spec = pl.BlockSpec((tk, tn), lambda b, j, k: (b * nk + k, j))
    k_spec = pl.BlockSpec((tk, tn), lambda b, j, k: (k, j))
    row_spec = pl.BlockSpec((L, tn), lambda b, j, k: (row_blk0 + b, j))
    return pl.pallas_call(
        functools.partial(_dft_inv_kernel, L=L),
        grid=(nb, D // tn, nk),
        in_specs=[a_spec] * 2 + [v_spec, v_spec, k_spec, k_spec,
                                 pl.BlockSpec((8, tn), lambda b, j, k: (0, j)),
                                 row_spec, pl.BlockSpec((1, tn), lambda b, j, k: (0, j)), row_spec],
        out_specs=pl.BlockSpec((L, tn), lambda b, j, k: (b, j)),
        out_shape=jax.ShapeDtypeStruct((nb * L, D), F32),
        compiler_params=_cparams(("arbitrary", "arbitrary", "arbitrary")),
        name="hyena_dft_inv_gate",
    )(c, s_cols, vc, vs, kre, kim, nyq, vv, skip.reshape(1, D), gate)


def _hyena_layer(x, mod, layer, tabs_p, tabs_s, ln_g, ln_b, w_in, conv_w, conv_b, w1, b1, w2, b2, w3, b3,
                 freq, decay, skip, w_out):
    vv, gate = _hy_in_conv(x, mod, layer, w_in, conv_w, conv_b)
    gs = []
    for L, nb, blk0, tabs in ((L_P, B_P, 0, tabs_p), (L_S, B_S, T_P // L_S, tabs_s)):
        hsum, hdiff, nyq = _hy_filter(L, w1, b1, w2, b2, w3, b3, freq, decay)
        kre, kim = _dft_fwd(tabs, (hsum, hdiff), L, 1, 0)
        vc, vs = _dft_fwd(tabs, (vv,), L, nb, blk0)
        gs.append(_dft_inv(tabs, vc, vs, kre, kim, nyq, vv, skip, gate, L, nb, blk0))
    return _out_proj_ln2(gs[0], gs[1], w_out, x, mod, layer, ln_g, ln_b)


def _rope_tables():
    rows = L_S // GRID_W
    half = MLA_ROPE // 2
    inv = ROPE_BASE ** (-jnp.arange(0, half, 2, dtype=F32) / half)
    r = jnp.repeat(jnp.arange(rows, dtype=F32), GRID_W)
    col = jnp.tile(jnp.arange(GRID_W, dtype=F32), rows)
    ar, ac = r[:, None] * inv, col[:, None] * inv
    ang = jnp.concatenate([ar, ar, ac, ac], -1)
    cos, sin = jnp.cos(ang), jnp.sin(ang)
    cos = jnp.concatenate([jnp.ones((TM, MLA_ROPE), F32), cos], 0)
    sin = jnp.concatenate([jnp.zeros((TM, MLA_ROPE), F32), sin], 0)
    return cos, sin, jnp.tile(cos, (1, MLA_HEADS)), jnp.tile(sin, (1, MLA_HEADS))


def _rope_rot_cols(w):
    idx = np.concatenate([np.arange(8, 16), np.arange(0, 8), np.arange(24, 32), np.arange(16, 24)])
    sign = np.concatenate([-np.ones(8), np.ones(8), -np.ones(8), np.ones(8)]).astype(np.float32)
    return w[..., idx] * sign


def _mla_in_kernel(x_ref, m_ref, wq_ref, wkv_ref, wkp_ref, wz_ref, qn_ref, kvn_ref, wqn_ref, wqp_ref, wqr_ref,
                   c32_ref, s32_ref, c512_ref, s512_ref,
                   qno_ref, qpe_ref, ckv_ref, kpe_ref, kpr_ref, sz_ref):
    h = _modulate(x_ref[...], m_ref[...]).astype(BF16)
    q_c = _dot(h, wq_ref[...])
    kv_c = _dot(h, wkv_ref[...])
    kp2 = _dot(h, wkp_ref[...])
    z = _dot(h, wz_ref[...])

    def rms(v, g):
        return v * lax.rsqrt(jnp.mean(v * v, axis=-1, keepdims=True) + RMS_EPS) * g

    qn = rms(q_c, qn_ref[...]).astype(BF16)
    scale = (MLA_NOPE + MLA_ROPE) ** -0.5
    qno_ref[...] = (_dot(qn, wqn_ref[...]) * scale).astype(BF16)
    q_pe = _dot(qn, wqp_ref[...]) * c512_ref[...] + _dot(qn, wqr_ref[...]) * s512_ref[...]
    qpe_ref[...] = (q_pe * scale).astype(BF16)
    ckv_ref[...] = rms(kv_c, kvn_ref[...])
    kpe = kp2[:, :MLA_ROPE]
    kpe_ref[...] = kpe
    kpr_ref[...] = kpe * c32_ref[...] + kp2[:, MLA_ROPE:] * s32_ref[...]
    sz_ref[...] = _silu(z)


def _mla_in_proj(x, mod, layer, w_in, q_norm, kv_norm, w_q_up, rope):
    c32, s32, c512, s512 = rope
    o1, o2, o3 = MLA_Q_RANK, MLA_Q_RANK + MLA_KV_RANK, MLA_Q_RANK + MLA_KV_RANK + MLA_ROPE
    wq, wkv, wkp, wz = w_in[:, :o1], w_in[:, o1:o2], w_in[:, o2:o3], w_in[:, o3:]
    wkp2 = jnp.concatenate([wkp, _rope_rot_cols(wkp)], -1)
    wqu = w_q_up.reshape(MLA_Q_RANK, MLA_HEADS, MLA_NOPE + MLA_ROPE)
    wqn = wqu[:, :, :MLA_NOPE].reshape(MLA_Q_RANK, MLA_HEADS * MLA_NOPE)
    wqp = wqu[:, :, MLA_NOPE:]
    wqr = _rope_rot_cols(wqp).reshape(MLA_Q_RANK, MLA_HEADS * MLA_ROPE)
    wqp = wqp.reshape(MLA_Q_RANK, MLA_HEADS * MLA_ROPE)
    full = lambda a: pl.BlockSpec(a.shape, lambda i: (0,) * a.ndim)
    rope_idx = lambda i: jnp.where(i < NT_P, 0, 1 + (i - NT_P) % NT_S_SEQ)
    rows = lambda n: pl.BlockSpec((TM, n), lambda i: (i, 0))
    tab = lambda n: pl.BlockSpec((TM, n), lambda i: (rope_idx(i), 0))
    weights = [wq.astype(BF16), wkv.astype(BF16), wkp2.astype(BF16), wz.astype(BF16),
               q_norm.reshape(1, -1), kv_norm.reshape(1, -1),
               wqn.astype(BF16), wqp.astype(BF16), wqr.astype(BF16)]
    npe = MLA_HEADS * MLA_ROPE
    return pl.pallas_call(
        _mla_in_kernel,
        grid=(NT,),
        in_specs=[rows(D), _mod_spec(layer)] + [full(a) for a in weights]
                 + [tab(MLA_ROPE), tab(MLA_ROPE), tab(npe), tab(npe)],
        out_specs=[rows(D), rows(npe), rows(MLA_KV_RANK), rows(MLA_ROPE), rows(MLA_ROPE), rows(D)],
        out_shape=[jax.ShapeDtypeStruct((T, D), BF16), jax.ShapeDtypeStruct((T, npe), BF16),
                   jax.ShapeDtypeStruct((T, MLA_KV_RANK), F32), jax.ShapeDtypeStruct((T, MLA_ROPE), F32),
                   jax.ShapeDtypeStruct((T, MLA_ROPE), F32), jax.ShapeDtypeStruct((T, D), F32)],
        compiler_params=_cparams(("arbitrary",)),
        name="mla_in_proj",
    )(x, mod, *weights, c32, s32, c512, s512)


def _mla_attn_kernel(*refs, n_cache, hg):
    if n_cache:
        (qn_ref, qp_ref, ckv_ref, kpr_ref, cckv_ref, ckpe_ref, wk_ref, wvt_ref, sz_ref,
         o_ref, kcat_s, vt_s) = refs
    else:
        qn_ref, qp_ref, ckv_ref, kpr_ref, wk_ref, wvt_ref, sz_ref, o_ref, kcat_s, vt_s = refs

    @pl.when(pl.program_id(2) == 0)
    def _():
        cc = ckv_ref[...].astype(BF16)
        kp = kpr_ref[...].astype(BF16)
        if n_cache:
            cc = jnp.concatenate([cckv_ref[...].astype(BF16), cc], axis=0)
            kp = jnp.concatenate([ckpe_ref[...].astype(BF16), kp], axis=0)
        kn = _dot(cc, wk_ref[...]).astype(BF16)
        for hh in range(hg):
            kcat_s[hh] = jnp.concatenate([kn[:, hh * MLA_NOPE:(hh + 1) * MLA_NOPE], kp], axis=1)
            vt_s[hh] = _dot_nt(wvt_ref[hh], cc).astype(BF16)

    qn_all, qp_all = qn_ref[...], qp_ref[...]

    def scores(hh):
        qcat = jnp.concatenate([qn_all[:, hh * MLA_NOPE:(hh + 1) * MLA_NOPE],
                                qp_all[:, hh * MLA_ROPE:(hh + 1) * MLA_ROPE]], axis=1)
        return _dot_nt(kcat_s[hh], qcat)

    outs = []
    ahead = 2
    pending = [scores(hh) for hh in range(min(ahead, hg))]
    for hh in range(hg):
        s = pending.pop(0)
        if hh + ahead < hg:
            pending.append(scores(hh + ahead))
        p = jnp.exp(s - jnp.max(s, axis=0, keepdims=True))
        l = jnp.sum(p, axis=0, keepdims=True)
        outs.append(_dot(vt_s[hh], p.astype(BF16)) / l)
    o_ref[...] = jnp.concatenate(outs, axis=0).T * sz_ref[...]


def _mla_attention(qno, qpe, ckv, kpr, sz, w_kv_up, cache_ckv, cache_kpe, *, nb, L, row_blk0, hg):
    wkv = w_kv_up.reshape(MLA_KV_RANK, MLA_HEADS, MLA_NOPE + MLA_V)
    wk = wkv[:, :, :MLA_NOPE].reshape(MLA_KV_RANK, D).astype(BF16)
    wvt = wkv[:, :, MLA_NOPE:].transpose(1, 2, 0).astype(BF16)
    n_cache = 0 if cache_ckv is None else cache_ckv.shape[1]
    lk = n_cache + L
    nq = L // TM
    wq = hg * MLA_NOPE
    wp = hg * MLA_ROPE
    qrow = lambda w: pl.BlockSpec((TM, w), lambda b, g, q: (row_blk0 + b * nq + q, g))
    seq = lambda w: pl.BlockSpec((L, w), lambda b, g, q: (row_blk0 * TM // L + b, 0))
    in_specs = [qrow(wq), qrow(wp), seq(MLA_KV_RANK), seq(MLA_ROPE)]
    args = [qno, qpe, ckv, kpr]
    if n_cache:
        in_specs += [pl.BlockSpec((None, n_cache, MLA_KV_RANK), lambda b, g, q: (b, 0, 0)),
                     pl.BlockSpec((None, n_cache, MLA_ROPE), lambda b, g, q: (b, 0, 0))]
        args += [cache_ckv, cache_kpe]
    in_specs += [pl.BlockSpec((MLA_KV_RANK, wq), lambda b, g, q: (0, g)),
                 pl.BlockSpec((hg, MLA_V, MLA_KV_RANK), lambda b, g, q: (g, 0, 0)), qrow(wq)]
    args += [wk, wvt, sz]
    return pl.pallas_call(
        functools.partial(_mla_attn_kernel, n_cache=n_cache, hg=hg),
        grid=(nb, MLA_HEADS // hg, nq),
        in_specs=in_specs,
        out_specs=pl.BlockSpec((TM, wq), lambda b, g, q: (b * nq + q, g)),
        out_shape=jax.ShapeDtypeStruct((nb * L, D), F32),
        scratch_shapes=[pltpu.VMEM((hg, lk, MLA_NOPE + MLA_ROPE), BF16), pltpu.VMEM((hg, MLA_V, lk), BF16)],
        compiler_params=_cparams(("arbitrary", "arbitrary", "arbitrary")),
        name="mla_attention",
    )(*args)


def _mla_layer(x, mod, layer, rope, ln_g, ln_b, cache_ckv, cache_kpe, w_in, q_norm, kv_norm, w_q_up, w_kv_up, w_out):
    qno, qpe, ckv, kpe, kpr, sz = _mla_in_proj(x, mod, layer, w_in, q_norm, kv_norm, w_q_up, rope)
    g_p = _mla_attention(qno, qpe, ckv, kpr, sz, w_kv_up, None, None, nb=B_P, L=L_P, row_blk0=0,
                         hg=MLA_HEADS)
    g_s = _mla_attention(qno, qpe, ckv, kpr, sz, w_kv_up, cache_ckv, cache_kpe, nb=B_S, L=L_S, row_blk0=NT_P,
                         hg=MLA_HEADS // 2)
    x_new = _out_proj_ln2(g_p, g_s, w_out, x, mod, layer, ln_g, ln_b)
    new_ckv = ckv[:T_P].reshape(B_P, 1, L_P, MLA_KV_RANK)
    new_kpe = kpe[:T_P].reshape(B_P, 1, L_P, MLA_ROPE)
    return x_new, new_ckv, new_kpe


def _rw_in_kernel(x_ref, xp_ref, xn_ref, m_ref, mu_ref, wr_ref, wk_ref, wv_ref, wg_ref, w1_ref, a1_ref,
                  w2_ref, a2_ref, w0_ref, a0_ref, kk_ref, ka_ref, rk_ref, ones_ref,
                  r_ref, v_ref, sz_ref, nkk_ref, bonus_ref, lw_ref, kd_ref, bd_ref):
    i = pl.program_id(0)
    has_prev, has_next = _tile_has_neighbours(i)
    m = m_ref[...]
    h = _modulate(x_ref[...], m)
    prev, nxt = _neighbour_rows(h, _modulate(xp_ref[...], m), _modulate(xn_ref[...], m), has_prev, has_next)
    d = 0.5 * (prev + nxt) - h
    mu = mu_ref[...]

    def mix(p):
        return (h + d * mu[p:p + 1]).astype(BF16)

    r = _dot(mix(0), wr_ref[...])
    tw = jnp.tanh(_dot(mix(1), w1_ref[...])).astype(BF16)
    k = _dot(mix(2), wk_ref[...])
    v = _dot(mix(3), wv_ref[...])
    ta = _dot(mix(4), a1_ref[...]).astype(BF16)
    z = _dot(mix(5), wg_ref[...])
    r_ref[...] = r.astype(r_ref.dtype)
    v_ref[...] = v.astype(v_ref.dtype)
    sz_ref[...] = _silu(z)
    ones_bd = ones_ref[...]
    kk = k * kk_ref[...]
    kk = kk * lax.rsqrt(_head_sum(kk * kk, ones_bd) + 1e-12)
    nkk_ref[...] = (-kk).astype(nkk_ref.dtype)
    coef = jnp.zeros_like(r)
    for n in range(2):
        wl = w0_ref[n:n + 1, :] + _dot(tw, w2_ref[n])
        lw_ref[n] = -math.exp(-0.5) * _sigmoid(wl)
        a = _sigmoid(a0_ref[n:n + 1, :] + _dot(ta, a2_ref[n]))
        kd = k * (1.0 + (a - 1.0) * ka_ref[...])
        kd_ref[n] = kd.astype(kd_ref.dtype)
        bd_ref[n] = (kk * a).astype(bd_ref.dtype)
        coef = coef + r * kd * rk_ref[...]
    bonus_ref[...] = _head_sum(coef, ones_bd) * v


def _pad_lora_up(w):
    z = jnp.zeros_like(w[0])
    return jnp.stack([jnp.concatenate([w[0], z], 0), jnp.concatenate([z, w[1]], 0)])


def _head_ones():
    h = np.arange(LANES) // RW_N
    return jnp.asarray(h[:, None] == h[None, :], dtype=BF16)


def _rw_in_proj(x, mod, layer, ones_bd, mu, w_in, w0, w1, w2, a0, a1, a2, k_k, k_a, r_k):
    mu8 = jnp.pad(mu, ((0, 2), (0, 0)))
    w1c = jnp.concatenate([w1[0], w1[1]], -1).astype(BF16)
    a1c = jnp.concatenate([a1[0], a1[1]], -1).astype(BF16)
    w2p = _pad_lora_up(w2).astype(BF16)
    a2p = _pad_lora_up(a2).astype(BF16)
    row = pl.BlockSpec((TM, D), lambda i: (i, 0))
    row2 = pl.BlockSpec((2, TM, D), lambda i: (0, i, 0))
    prev, nxt = _halo_specs(D, lambda i: 0)
    full = lambda a: pl.BlockSpec(a.shape, lambda i: (0,) * a.ndim)
    consts = [mu8, w_in[0].astype(BF16), w_in[1].astype(BF16), w_in[2].astype(BF16), w_in[3].astype(BF16),
              w1c, a1c, w2p, a2p, w0, a0, k_k.reshape(1, D), k_a.reshape(1, D), r_k.reshape(1, D), ones_bd]
    return pl.pallas_call(
        _rw_in_kernel,
        grid=(NT,),
        in_specs=[row, prev, nxt, _mod_spec(layer)] + [full(a) for a in consts],
        out_specs=[row] * 5 + [row2] * 3,
        out_shape=[jax.ShapeDtypeStruct((T, D), dt) for dt in (BF16, BF16, F32, BF16, F32)]
                  + [jax.ShapeDtypeStruct((2, T, D), dt) for dt in (F32, BF16, BF16)],
        compiler_params=_cparams(("arbitrary",)),
        name="rwkv_in_proj",
    )(x, x, x, mod, *consts)


SCAN_NB = 2
SCAN_NU = 2 * SCAN_NB


def _rw_scan_kernel(*refs):
    rb_refs = refs[:SCAN_NU]
    first_ref = refs[SCAN_NU]
    ins = refs[SCAN_NU + 1:SCAN_NU + 1 + 6 * SCAN_NU]
    s0_ref, y_ref, s_ref = refs[SCAN_NU + 1 + 6 * SCAN_NU:]
    del rb_refs
    step = pl.program_id(0)
    c = CHUNK

    @pl.when(first_ref[step] == 1)
    def _():
        s_ref[...] = s0_ref[...]

    ri = lax.broadcasted_iota(jnp.int32, (c, c), 0)
    ci = lax.broadcasted_iota(jnp.int32, (c, c), 1)
    ri2 = lax.broadcasted_iota(jnp.int32, (c, 2 * c), 0)
    ci2 = lax.broadcasted_iota(jnp.int32, (c, 2 * c), 1)
    cm2 = jnp.where(ci2 >= c, ci2 - c, ci2)
    eye_f = (ri == ci).astype(F32)
    eye = eye_f.astype(BF16)
    masks = []
    for sgn in (1, -1):
        incl = (ri - ci) * sgn >= 0
        strict = (ri - ci) * sgn > 0
        mask_2 = (ri2 - cm2) * sgn >= 0
        mask_k = jnp.logical_and((ri2 - cm2) * sgn > 0, ci2 >= c)
        off_masks = []
        for lg in range(6):
            same_2m = (ri >> (lg + 1)) == (ci >> (lg + 1))
            diff_m = (ri >> lg) != (ci >> lg)
            off_masks.append(jnp.logical_and(strict, jnp.logical_and(same_2m, diff_m)))
        masks.append((incl, mask_2, mask_k, off_masks))

    chains = [(q, h) for q in range(SCAN_NU) for h in range(RW_H)]
    ar, bk, bke, vh = {}, {}, {}, {}
    for q in range(SCAN_NU):
        r_ref, v_ref, nkk_ref, lw_ref, kd_ref, bd_ref = ins[6 * q:6 * q + 6]
        bwd = q % 2
        lw = lw_ref[...]
        g = _dot_hi(masks[bwd][0].astype(F32), lw)
        gtot = g[0:1, :] if bwd else g[c - 1:c, :]
        e_inv = jnp.exp(-g)
        e_rem = jnp.exp(gtot - g)
        e_hi, e_lo = _split(jnp.broadcast_to(jnp.exp(gtot), (ROWS_BF16, D)))
        a_t = nkk_ref[...].astype(F32) * jnp.exp(g - lw)
        r_t = r_ref[...].astype(F32) * jnp.exp(g)
        kd, bd = kd_ref[...].astype(F32), bd_ref[...].astype(F32)
        b_t, k_t, b_e, k_e = bd * e_inv, kd * e_inv, bd * e_rem, kd * e_rem
        v = v_ref[...]
        for h in range(RW_H):
            sl = slice(h * RW_N, (h + 1) * RW_N)
            cat = lambda p, r_: jnp.concatenate([p[:, sl], r_[:, sl]], axis=0).astype(BF16)
            ar[q, h], bk[q, h] = cat(a_t, r_t), cat(b_t, k_t)
            bke[q, h] = jnp.concatenate([cat(b_e, k_e), e_hi[:, sl], e_lo[:, sl]], axis=0)
            vh[q, h] = v[:, sl]
    h_old = {ch: s_ref[ch[0], ch[1]] for ch in chains}
    gm = {ch: _dot_nt(ar[ch], bk[ch]) for ch in chains}
    a_s = {ch: _dot(ar[ch], h_old[ch].astype(BF16)) for ch in chains}
    bke_t = {ch: _dot_nt(eye, bke[ch]) for ch in chains}
    lab = {ch: gm[ch][:c, :c] for ch in chains}
    offs = lambda ch: masks[ch[0] % 2][3]
    tinv = {ch: eye_f + jnp.where(offs(ch)[0], lab[ch], 0.0) for ch in chains}
    for lg in range(1, 6):
        tb = {ch: tinv[ch].astype(BF16) for ch in chains}
        lt = {ch: _dot(jnp.where(offs(ch)[lg], lab[ch], 0.0).astype(BF16), tb[ch]) for ch in chains}
        tinv = {ch: tinv[ch] + _dot(tb[ch], lt[ch].astype(BF16)) for ch in chains}
    vv2 = {ch: jnp.concatenate([vh[ch], vh[ch]], axis=0).astype(BF16) for ch in chains}
    w1 = {ch: a_s[ch][:c] + _dot(jnp.where(masks[ch[0] % 2][2], gm[ch][:c], 0.0).astype(BF16), vv2[ch])
          for ch in chains}
    u = {ch: _dot(tinv[ch].astype(BF16), w1[ch].astype(BF16)) for ch in chains}
    uv = {ch: jnp.concatenate([u[ch], vh[ch]], axis=0).astype(BF16) for ch in chains}
    y = {ch: a_s[ch][c:] + _dot(jnp.where(masks[ch[0] % 2][1], gm[ch][c:], 0.0).astype(BF16), uv[ch])
         for ch in chains}
    for q in range(SCAN_NU):
        y_ref[q] = jnp.concatenate([y[q, h] for h in range(RW_H)], axis=1)
    upd = {ch: _dot(bke_t[ch][:, :2 * c].astype(BF16), uv[ch]) for ch in chains}
    lo_col = 2 * c + ROWS_BF16
    h_new = {ch: h_old[ch] * (bke_t[ch][:, 2 * c:2 * c + 1] + bke_t[ch][:, lo_col:lo_col + 1]) + upd[ch]
             for ch in chains}
    for ch in chains:
        s_ref[ch[0], ch[1]] = h_new[ch]


def _rw_scan_tables():
    rb = [[] for _ in range(SCAN_NU)]
    first = []
    for nb, L, blk0 in ((B_P, L_P, 0), (B_S, L_S, T_P // CHUNK)):
        nc = L // CHUNK
        for grp in range(nb // SCAN_NB):
            for j in range(nc):
                for q in range(SCAN_NU):
                    b = grp * SCAN_NB + q // 2
                    cn = j if q % 2 == 0 else nc - 1 - j
                    rb[q].append(blk0 + b * nc + cn)
                first.append(1 if j == 0 else 0)
    as_i32 = lambda a: jnp.asarray(np.asarray(a, np.int32))
    return [as_i32(a) for a in rb], as_i32(first)


def _rw_scan(r, v, nkk, lw, kd, bd, s0_all):
    rb, first = _rw_scan_tables()
    n_steps = first.shape[0]
    n_groups = s0_all.shape[0] // SCAN_NU
    steps_p = (B_P // SCAN_NB) * (L_P // CHUNK)
    in_specs = []
    args = []
    for q in range(SCAN_NU):
        tok = pl.BlockSpec((CHUNK, D), lambda s, *t, q=q: (t[q][s], 0))
        tok2 = pl.BlockSpec((None, CHUNK, D), lambda s, *t, q=q: (q % 2, t[q][s], 0))
        in_specs += [tok, tok, tok, tok2, tok2, tok2]
        args += [r, v, nkk, lw, kd, bd]
    grp = lambda s: jnp.where(s < steps_p, s // (L_P // CHUNK),
                              B_P // SCAN_NB + (s - steps_p) // (L_S // CHUNK))
    st = pl.BlockSpec((None, SCAN_NU, RW_H, RW_N, RW_N), lambda s, *t: (grp(s), 0, 0, 0, 0))
    s0g = s0_all.reshape(n_groups, SCAN_NU, RW_H, RW_N, RW_N)
    y, s_fin = pl.pallas_call(
        _rw_scan_kernel,
        grid_spec=pltpu.PrefetchScalarGridSpec(
            num_scalar_prefetch=SCAN_NU + 1,
            grid=(n_steps,),
            in_specs=in_specs + [st],
            out_specs=[pl.BlockSpec((None, SCAN_NU, CHUNK, D), lambda s, *t: (s, 0, 0, 0)), st],
        ),
        out_shape=[jax.ShapeDtypeStruct((n_steps, SCAN_NU, CHUNK, D), F32),
                   jax.ShapeDtypeStruct(s0g.shape, F32)],
        compiler_params=_cparams(("arbitrary",)),
        name="rwkv_scan",
    )(*rb, first, *args, s0g)
    return y, s_fin.reshape(s0_all.shape)


def _scan_out_index(i, k, bwd):
    per_tile = TM // CHUNK
    nc_p, nc_s = L_P // CHUNK, L_S // CHUNK
    steps_p = (B_P // SCAN_NB) * nc_p
    cn_p = k
    b_s = (i - NT_P) // NT_S_SEQ
    cn_s = ((i - NT_P) % NT_S_SEQ) * per_tile + k
    step_p = (i // SCAN_NB) * nc_p + (nc_p - 1 - cn_p if bwd else cn_p)
    step_s = steps_p + (b_s // SCAN_NB) * nc_s + (nc_s - 1 - cn_s if bwd else cn_s)
    q_p = (i % SCAN_NB) * 2 + bwd
    q_s = (b_s % SCAN_NB) * 2 + bwd
    is_p = i < NT_P
    return jnp.where(is_p, step_p, step_s), jnp.where(is_p, q_p, q_s)


def _rw_out_kernel(*refs):
    per_tile = TM // CHUNK
    y_refs = refs[:2 * per_tile]
    bonus_ref, sz_ref, gg_ref, gb_ref, ones_ref, w_ref, x_ref, m_ref, lng_ref, lnb_ref, o_ref = refs[2 * per_tile:]
    ones_bd = ones_ref[...]
    y = jnp.concatenate([y_refs[k][...] + y_refs[per_tile + k][...] for k in range(per_tile)], axis=0)
    mean = _head_sum(y, ones_bd) * (1.0 / RW_N)
    yc = y - mean
    var = _head_sum(yc * yc, ones_bd) * (1.0 / RW_N)
    yn = yc * lax.rsqrt(var + RW_GN_EPS) * gg_ref[...] + gb_ref[...]
    g = (yn + bonus_ref[...]) * sz_ref[...]
    _out_ln_tail(g, w_ref, x_ref, m_ref, lng_ref, lnb_ref, o_ref)


def _rw_out_proj_ln(y, bonus, sz, gn_g, gn_b, ones_bd, w_out, x, mod, layer, ln_g, ln_b):
    row = pl.BlockSpec((TM, D), lambda i: (i, 0))
    vec = pl.BlockSpec((1, D), lambda i: (0, 0))
    mat = pl.BlockSpec((D, D), lambda i: (0, 0))
    y_specs = [pl.BlockSpec((None, None, CHUNK, D), lambda i, k=k, bwd=bwd: _scan_out_index(i, k, bwd) + (0, 0))
               for bwd in (0, 1) for k in range(TM // CHUNK)]
    return pl.pallas_call(
        _rw_out_kernel,
        grid=(NT,),
        in_specs=y_specs + [row, row, vec, vec,
                  pl.BlockSpec((LANES, LANES), lambda i: (0, 0)), mat, row,
                  _mod_spec(layer), vec, vec],
        out_specs=row,
        out_shape=jax.ShapeDtypeStruct((T, D), F32),
        compiler_params=_cparams(("arbitrary",)),
        name="rwkv_out_proj_ln",
    )(*[y] * len(y_specs), bonus, sz, gn_g.reshape(1, D), gn_b.reshape(1, D), ones_bd, w_out.astype(BF16), x, mod,
      ln_g.reshape(1, D), ln_b.reshape(1, D))


def _rwkv_layer(x, mod, layer, ln_g, ln_b, state, mu, w_in, w0, w1, w2, a0, a1, a2, k_k, k_a, r_k, gn_g, gn_b, w_out):
    ones_bd = _head_ones()
    r, v, sz, nkk, bonus, lw, kd, bd = _rw_in_proj(x, mod, layer, ones_bd, mu, w_in, w0, w1, w2, a0, a1, a2,
                                                  k_k, k_a, r_k)
    n_p = B_P * 2
    s0_all = jnp.concatenate([jnp.zeros((n_p, RW_H, RW_N, RW_N), F32),
                              state.astype(F32).reshape(B_S * 2, RW_H, RW_N, RW_N).swapaxes(-1, -2)], 0)
    y, h_fin = _rw_scan(r, v, nkk, lw, kd, bd, s0_all)
    x_new = _rw_out_proj_ln(y, bonus, sz, gn_g, gn_b, ones_bd, w_out, x, mod, layer, ln_g, ln_b)
    new_state = h_fin[:n_p].swapaxes(-1, -2).reshape(B_P, 1, 2, RW_H, RW_N, RW_N)
    return x_new, new_state


def kernel(x_prompt, x_sample, cache_mla_ckv, cache_mla_kpe, state_rwkv, c, c_ctx, mod_w, mod_b, ln_g, ln_b, hy_w_in, hy_conv_w, hy_conv_b, hy_ffn_w1, hy_ffn_b1, hy_ffn_w2, hy_ffn_b2, hy_ffn_w3, hy_ffn_b3, hy_freq, hy_decay, hy_skip, hy_w_out, mla_w_in, mla_q_norm, mla_kv_norm, mla_w_q_up, mla_w_kv_up, mla_w_out, rw_mu, rw_w_in, rw_w0, rw_w1, rw_w2, rw_a0, rw_a1, rw_a2, rw_k_k, rw_k_a, rw_r_k, rw_gn_g, rw_gn_b, rw_w_out):
    x = jnp.concatenate([x_prompt.reshape(T_P, D), x_sample.reshape(T_S, D)], 0)
    cond8 = jnp.concatenate([c_ctx[None, :], c, jnp.zeros((8 - 1 - B_S, D), F32)], 0)
    mod = _modulation_table(cond8, mod_w, mod_b).reshape(DEPTH * 8, 1, 3 * D)
    tabs_p = _dft_tables(L_P)
    tabs_s = _dft_tables(L_S)
    rope = _rope_tables()
    new_ckv = new_kpe = new_state = None
    for i in range(DEPTH):
        kind, j = i % 3, i // 3
        if kind == 0:
            x = _hyena_layer(x, mod, i, tabs_p, tabs_s, ln_g[i], ln_b[i], hy_w_in[j], hy_conv_w[j], hy_conv_b[j],
                             hy_ffn_w1[j], hy_ffn_b1[j], hy_ffn_w2[j], hy_ffn_b2[j], hy_ffn_w3[j], hy_ffn_b3[j],
                             hy_freq[j], hy_decay[j], hy_skip[j], hy_w_out[j])
        elif kind == 1:
            x, new_ckv, new_kpe = _mla_layer(x, mod, i, rope, ln_g[i], ln_b[i], cache_mla_ckv[:, j],
                                             cache_mla_kpe[:, j], mla_w_in[j], mla_q_norm[j], mla_kv_norm[j],
                                             mla_w_q_up[j], mla_w_kv_up[j], mla_w_out[j])
        else:
            x, new_state = _rwkv_layer(x, mod, i, ln_g[i], ln_b[i], state_rwkv[:, j], rw_mu[j], rw_w_in[j],
                                       rw_w0[j], rw_w1[j], rw_w2[j], rw_a0[j], rw_a1[j], rw_a2[j], rw_k_k[j],
                                       rw_k_a[j], rw_r_k[j], rw_gn_g[j], rw_gn_b[j], rw_w_out[j])
    return (x[:T_P].reshape(B_P, L_P, D), x[T_P:].reshape(B_S, L_S, D), new_ckv, new_kpe, new_state)
```

```python
import functools
import math

import numpy as np
import jax
import jax.numpy as jnp
from jax import lax
from jax.experimental import pallas as pl
from jax.experimental.pallas import tpu as pltpu

F32 = jnp.float32
BF16 = jnp.bfloat16
HIGHEST = lax.Precision.HIGHEST

D = 1024
B_P, L_P = 16, 256
B_S, L_S = 2, 2048
T_P = B_P * L_P
T_S = B_S * L_S
T = T_P + T_S
PAST = 512
DEPTH = 4
DEEPNORM_ALPHA = (2.0 * DEPTH) ** 0.25
LN_EPS = 1e-5
RMS_EPS = 1e-6
HY_BANDS = 16
HY_FFN = 64
MLA_HEADS = 16
MLA_Q_RANK = 256
MLA_KV_RANK = 128
MLA_NOPE = 64
MLA_ROPE = 32
MLA_V = 64
ROPE_BASE = 10000.0
GRID_W = 64
RW_N = 64
RW_H = D // RW_N
RW_LORA = 64
RW_GN_EPS = 64e-5

TM = 256
NT_P = T_P // TM
NT_S_SEQ = L_S // TM
NT = T // TM
HALO = 8
LANES = 128
ROWS_BF16 = 16
CHUNK = 64
DFT_TK = 512
VMEM_LIMIT = 52 * 1024 * 1024


def _cparams(sem):
    return pltpu.CompilerParams(dimension_semantics=sem, vmem_limit_bytes=VMEM_LIMIT)


def _group(i):
    return jnp.where(i < NT_P, 0, 1 + (i - NT_P) // NT_S_SEQ)


def _sigmoid(x):
    return 1.0 / (1.0 + jnp.exp(-x))


def _silu(x):
    return x * _sigmoid(x)


def _dot(a, b):
    return jnp.dot(a, b, preferred_element_type=F32)


def _dot_nt(a, b):
    return lax.dot_general(a, b, (((1,), (1,)), ((), ())), preferred_element_type=F32)


def _dot_hi(a, b):
    return jnp.dot(a, b, preferred_element_type=F32, precision=HIGHEST)


def _split(x):
    hi = x.astype(BF16)
    lo = (x - hi.astype(F32)).astype(BF16)
    return hi, lo


def _head_sum(x, ones_bd):
    hi, lo = _split(x)
    lanes = ones_bd.shape[0]
    parts = []
    for g in range(x.shape[1] // lanes):
        sl = slice(g * lanes, (g + 1) * lanes)
        parts.append(_dot(hi[:, sl], ones_bd) + _dot(lo[:, sl], ones_bd))
    return jnp.concatenate(parts, axis=1)


def _modulate(x, m):
    return x * (1.0 + m[:, D:2 * D]) + m[:, :D]


def _layer_norm_rows(y, g, b):
    mu = jnp.mean(y, axis=-1, keepdims=True)
    yc = y - mu
    var = jnp.mean(yc * yc, axis=-1, keepdims=True)
    return yc * lax.rsqrt(var + LN_EPS) * g + b


def _neighbour_rows(cur, prev_halo, next_halo, has_prev, has_next):
    rows = cur.shape[0]
    ridx = lax.broadcasted_iota(jnp.int32, cur.shape, 0)
    pr = jnp.where(has_prev, prev_halo[HALO - 1:HALO, :], 0.0)
    nx = jnp.where(has_next, next_halo[0:1, :], 0.0)
    prev = jnp.where(ridx == 0, pr, pltpu.roll(cur, 1, axis=0))
    nxt = jnp.where(ridx == rows - 1, nx, pltpu.roll(cur, rows - 1, axis=0))
    return prev, nxt


def _tile_has_neighbours(i):
    k = (i - NT_P) % NT_S_SEQ
    is_s = i >= NT_P
    return jnp.logical_and(is_s, k != 0), jnp.logical_and(is_s, k != NT_S_SEQ - 1)


def _halo_specs(width, col_of):
    r = TM // HALO
    prev = pl.BlockSpec((HALO, width), lambda i, *a: (jnp.maximum(i * r - 1, 0), col_of(i, *a)))
    nxt = pl.BlockSpec((HALO, width), lambda i, *a: (jnp.minimum((i + 1) * r, T // HALO - 1), col_of(i, *a)))
    return prev, nxt


def _mod_kernel(c_ref, w_ref, b_ref, o_ref):
    o_ref[...] = _dot_hi(_silu(c_ref[...]), w_ref[...]) + b_ref[...]


def _modulation_table(cond8, mod_w, mod_b):
    tn = 1024
    return pl.pallas_call(
        _mod_kernel,
        grid=(DEPTH, 3 * D // tn),
        in_specs=[pl.BlockSpec((8, D), lambda l, j: (0, 0)),
                  pl.BlockSpec((None, D, tn), lambda l, j: (l, 0, j)),
                  pl.BlockSpec((None, 1, tn), lambda l, j: (l, 0, j))],
        out_specs=pl.BlockSpec((None, 8, tn), lambda l, j: (l, 0, j)),
        out_shape=jax.ShapeDtypeStruct((DEPTH, 8, 3 * D), F32),
        compiler_params=_cparams(("arbitrary", "arbitrary")),
        name="modulation",
    )(cond8, mod_w, mod_b.reshape(DEPTH, 1, 3 * D))


def _mod_spec(layer):
    return pl.BlockSpec((None, 1, 3 * D), lambda i, *a: (layer * 8 + _group(i), 0, 0))


def _out_ln_tail(g, w_ref, x_ref, m_ref, lng_ref, lnb_ref, o_ref):
    mix = _dot(g.astype(BF16), w_ref[...])
    gate = m_ref[...][:, 2 * D:]
    y = DEEPNORM_ALPHA * x_ref[...] + gate * mix
    o_ref[...] = _layer_norm_rows(y, lng_ref[...], lnb_ref[...])


def _out_ln2_kernel(gp_ref, gs_ref, w_ref, x_ref, m_ref, lng_ref, lnb_ref, o_ref):
    g = jnp.where(pl.program_id(0) < NT_P, gp_ref[...], gs_ref[...])
    _out_ln_tail(g, w_ref, x_ref, m_ref, lng_ref, lnb_ref, o_ref)


def _out_proj_ln2(g_p, g_s, w_out, x, mod, layer, ln_g, ln_b):
    row = pl.BlockSpec((TM, D), lambda i: (i, 0))
    vec = pl.BlockSpec((1, D), lambda i: (0, 0))
    return pl.pallas_call(
        _out_ln2_kernel,
        grid=(NT,),
        in_specs=[pl.BlockSpec((TM, D), lambda i: (jnp.minimum(i, NT_P - 1), 0)),
                  pl.BlockSpec((TM, D), lambda i: (jnp.maximum(i - NT_P, 0), 0)),
                  pl.BlockSpec((D, D), lambda i: (0, 0)),
                  row, _mod_spec(layer), vec, vec],
        out_specs=row,
        out_shape=jax.ShapeDtypeStruct((T, D), F32),
        compiler_params=_cparams(("arbitrary",)),
        name="out_proj_ln",
    )(g_p, g_s, w_out.astype(BF16), x, mod, ln_g.reshape(1, D), ln_b.reshape(1, D))


def _hy_in_conv_kernel(x_ref, xp_ref, xn_ref, m_ref, w_ref, cw_ref, cb_ref, vv_ref, gate_ref):
    has_prev, has_next = _tile_has_neighbours(pl.program_id(0))
    x_all = jnp.concatenate([xp_ref[...], x_ref[...], xn_ref[...]], axis=0)
    u = _dot(_modulate(x_all, m_ref[...]).astype(BF16), w_ref[...])
    rows = TM + 2 * HALO
    ridx = lax.broadcasted_iota(jnp.int32, (TM, 1), 0)
    no_prev = jnp.logical_and(ridx == 0, jnp.logical_not(has_prev))
    no_next = jnp.logical_and(ridx == TM - 1, jnp.logical_not(has_next))
    cw = cw_ref[...]
    cb = cb_ref[...]

    def conv(grp):
        sl = slice(grp * D, (grp + 1) * D)
        ug = u[:, sl]
        cur = ug[HALO:HALO + TM]
        prev = jnp.where(no_prev, 0.0, pltpu.roll(ug, 1, axis=0)[HALO:HALO + TM])
        nxt = jnp.where(no_next, 0.0, pltpu.roll(ug, rows - 1, axis=0)[HALO:HALO + TM])
        return prev * cw[0:1, sl] + cur * cw[1:2, sl] + nxt * cw[2:3, sl] + cb[:, sl]

    x0, x1, v = conv(0), conv(1), conv(2)
    vv_ref[...] = v * x1
    gate_ref[...] = x0 * _silu(u[HALO:HALO + TM, 3 * D:])


def _hy_in_conv(x, mod, layer, w_in, conv_w, conv_b):
    row = pl.BlockSpec((TM, D), lambda i: (i, 0))
    prev, nxt = _halo_specs(D, lambda i: 0)
    return pl.pallas_call(
        _hy_in_conv_kernel,
        grid=(NT,),
        in_specs=[row, prev, nxt, _mod_spec(layer), _resident(w_in.shape),
                  pl.BlockSpec((3, 3 * D), lambda i: (0, 0)), pl.BlockSpec((1, 3 * D), lambda i: (0, 0))],
        out_specs=[row, row],
        out_shape=[jax.ShapeDtypeStruct((T, D), F32)] * 2,
        compiler_params=_cparams(("arbitrary",)),
        name="hyena_in_proj_conv3",
    )(x, x, x, mod, w_in.astype(BF16), conv_w, conv_b.reshape(1, 3 * D))


def _hy_filter_kernel(t_ref, bands_ref, wt_ref, wc_ref, ws_ref, b1_ref, w2_ref, b2_ref, w3_ref, b3_ref,
                      f0_ref, f1_ref, dec_ref, hs_ref, hd_ref, nyq_ref, *, L, tr):
    i = pl.program_id(0)
    ridx = lax.broadcasted_iota(jnp.int32, (tr, 1), 0) + i * tr
    pos = ridx.astype(F32)
    t = t_ref[...]
    ang = ((2.0 * math.pi / L) * pos) * bands_ref[...]
    pre = t * wt_ref[...] + _dot_hi(jnp.cos(ang), wc_ref[...]) + _dot_hi(jnp.sin(ang), ws_ref[...])
    hdn = jnp.sin(f0_ref[...] * (pre + b1_ref[...]))
    hdn = jnp.sin(f1_ref[...] * (_dot_hi(hdn, w2_ref[...]) + b2_ref[...]))
    hf = _dot_hi(hdn, w3_ref[...]) + b3_ref[...]
    h = hf * jnp.exp(-t * jnp.abs(dec_ref[...]))
    h0 = h[:, :D]
    h1 = jnp.where(ridx == 0, 0.0, h[:, D:])
    hsum = h0 + h1
    hs_ref[...] = hsum
    hd_ref[...] = h1 - h0
    alt = jnp.where((ridx & 1) == 0, 1.0, -1.0)
    part =jnp.broadcast_to(jnp.sum(alt * hsum, axis=0, keepdims=True), (8, D))

    @pl.when(i == 0)
    def _():
        nyq_ref[...] = part

    @pl.when(i > 0)
    def _():
        nyq_ref[...] += part


def _hy_filter(L, w1, b1, w2, b2, w3, b3, freq, decay):
    tr = 256
    t = jnp.linspace(0.0, 1.0, L, dtype=F32).reshape(L, 1)
    bands = jnp.linspace(1e-4, HY_BANDS - 1, HY_BANDS, dtype=F32)
    bands = jnp.pad(bands, (0, 128 - HY_BANDS)).reshape(1, 128)
    wt = w1[0:1]
    wc = jnp.pad(w1[1:1 + HY_BANDS], ((0, 128 - HY_BANDS), (0, 0)))
    ws = jnp.pad(-w1[1 + HY_BANDS:], ((0, 128 - HY_BANDS), (0, 0)))
    full = lambda shape: pl.BlockSpec(shape, lambda i: (0, 0))
    rows = pl.BlockSpec((tr, D), lambda i: (i, 0))
    return pl.pallas_call(
        functools.partial(_hy_filter_kernel, L=L, tr=tr),
        grid=(L // tr,),
        in_specs=[pl.BlockSpec((tr, 1), lambda i: (i, 0)), full((1, 128)), full((1, HY_FFN)),
                  full((128, HY_FFN)), full((128, HY_FFN)), full((1, HY_FFN)),
                  full((HY_FFN, HY_FFN)), full((1, HY_FFN)), full((HY_FFN, 2 * D)), full((1, 2 * D)),
                  full((1, HY_FFN)), full((1, HY_FFN)), full((1, 2 * D))],
        out_specs=[rows, rows, pl.BlockSpec((8, D), lambda i: (0, 0))],
        out_shape=[jax.ShapeDtypeStruct((L, D), F32), jax.ShapeDtypeStruct((L, D), F32),
                   jax.ShapeDtypeStruct((8, D), F32)],
        compiler_params=_cparams(("arbitrary",)),
        name="hyena_filter",
    )(t, bands, wt, wc, ws, b1.reshape(1, -1), w2, b2.reshape(1, -1), w3, b3.reshape(1, -1),
      freq[0:1], freq[1:2], decay.reshape(1, 2 * D))


def _dft_tables(L):
    n = 2 * L
    w = 64
    k = jnp.arange(L, dtype=jnp.int32)

    def cs(t):
        ang = ((k[:, None] * t[None, :]) % n).astype(F32) * (2.0 * math.pi / n)
        return jnp.cos(ang), jnp.sin(ang)

    tk = min(L, DFT_TK)
    nk = L // tk
    ca, sa = cs(jnp.arange(L // w, dtype=jnp.int32) * w)
    cb, sb = cs(jnp.arange(w, dtype=jnp.int32))
    alt = jnp.where(k % 2 == 0, 1.0, -1.0).astype(F32)

    def row_tiles(p, q, r, s, sgn):
        tab = p[:, :, None] * q[:, None, :] + sgn * r[:, :, None] * s[:, None, :]
        return tab.reshape(nk, tk, L)

    def col_tiles(p, q, r, s, sgn):
        tr = lambda a: a.reshape(nk, tk, -1).transpose(0, 2, 1)
        tab = tr(p)[:, :, None, :] * tr(q)[:, None, :, :] + sgn * tr(r)[:, :, None, :] * tr(s)[:, None, :, :]
        return tab.reshape(nk, L, tk)

    bins = k.reshape(nk, tk)
    c_rows = row_tiles(ca, cb, sa, sb, -1.0)
    c_cols = col_tiles(ca, cb, sa, sb, -1.0)
    s_rows = jnp.where(bins[:, :, None] == 0, alt[None, None, :], row_tiles(sa, cb, ca, sb, 1.0))
    s_cols = jnp.where(bins[:, None, :] == 0, alt[None, :, None], col_tiles(sa, cb, ca, sb, 1.0))
    return tuple(a.astype(BF16) for a in (c_rows, s_rows, c_cols, s_cols))


def _dft_fwd_kernel(c_ref, s_ref, *refs):
    x1_ref, x2_ref = refs[0], refs[-3]
    oc_ref, os_ref = refs[-2:]
    k = pl.program_id(2)
    oc_ref[...] = _dot(c_ref[k], x1_ref[...].astype(BF16))
    os_ref[...] = _dot(s_ref[k], x2_ref[...].astype(BF16))


def _resident(shape):
    return pl.BlockSpec(shape, lambda *_: (0,) * len(shape), pipeline_mode=pl.Buffered(1))


def _dft_fwd(tabs, xs, L, nb, row_blk0):
    c, s_rows = tabs[:2]
    nk, tk, _ = c.shape
    tn = 512
    a_spec = _resident(c.shape)
    x_spec = pl.BlockSpec((L, tn), lambda b, j, k: (row_blk0 + b, j))
    o_spec = pl.BlockSpec((tk, tn), lambda b, j, k: (b * nk + k, j))
    return pl.pallas_call(
        _dft_fwd_kernel,
        grid=(nb, D // tn, nk),
        in_specs=[a_spec] * 2 + [x_spec] * len(xs),
        out_specs=[o_spec, o_spec],
        out_shape=[jax.ShapeDtypeStruct((nb * L, D), F32)] * 2,
        compiler_params=_cparams(("arbitrary", "arbitrary", "arbitrary")),
        name="hyena_dft_fwd",
    )(c, s_rows, *xs)


def _dft_inv_kernel(c_ref, st_ref, vc_ref, vs_ref, kre_ref, kim_ref, nyq_ref,
                    vv_ref, skip_ref, gate_ref, o_ref, *, L):
    k = pl.program_id(2)
    nk = pl.num_programs(2)
    vc, vs, kre = vc_ref[...], vs_ref[...], kre_ref[...]
    bin0 = jnp.logical_and(lax.broadcasted_iota(jnp.int32, vc.shape, 0) == 0, k == 0)
    kim = jnp.where(bin0, nyq_ref[0:1, :], kim_ref[...])
    inv_n = 1.0 / (2 * L)
    yre = jnp.where(bin0, vc * kre * inv_n, (vc * kre + vs * kim) * (2.0 * inv_n))
    yim = jnp.where(bin0, vs * kim * inv_n, (vs * kre - vc * kim) * (2.0 * inv_n))
    contrib = _dot(c_ref[k], yre.astype(BF16)) + _dot(st_ref[k], yim.astype(BF16))

    @pl.when(k == 0)
    def _():
        o_ref[...] = contrib

    @pl.when(k > 0)
    def _():
        o_ref[...] += contrib

    @pl.when(k == nk - 1)
    def _():
        o_ref[...] = (o_ref[...] + vv_ref[...] * skip_ref[...]) * gate_ref[...]


def _dft_inv(tabs, vc, vs, kre, kim, nyq, vv, skip, gate, L, nb, row_blk0):
    c, s_cols = tabs[2:]
    nk, _, tk = c.shape
    tn = 512 if L <= 512 else 256
    a_spec = _resident(c.shape)
    v_spec = pl.BlockSpec((tk, tn), lambda b, j, k: (b * nk + k, j))
    k_spec = pl.BlockSpec((tk, tn), lambda b, j, k: (k, j))
    row_spec = pl.BlockSpec((L, tn), lambda b, j, k: (row_blk0 + b, j))
    return pl.pallas_call(
        functools.partial(_dft_inv_kernel, L=L),
        grid=(nb, D // tn, nk),
        in_specs=[a_spec] * 2 + [v_spec, v_spec, k_spec, k_spec,
                                 pl.BlockSpec((8, tn), lambda b, j, k: (0, j)),
                                 row_spec, pl.BlockSpec((1, tn), lambda b, j, k: (0, j)), row_spec],
        out_specs=pl.BlockSpec((L, tn), lambda b, j, k: (b, j)),
        out_shape=jax.ShapeDtypeStruct((nb * L, D), F32),
        compiler_params=_cparams(("arbitrary", "arbitrary", "arbitrary")),
        name="hyena_dft_inv_gate",
    )(c, s_cols, vc, vs, kre, kim, nyq, vv, skip.reshape(1, D), gate)


def _hyena_layer(x, mod, layer, tabs_p, tabs_s, ln_g, ln_b, w_in, conv_w, conv_b, w1, b1, w2, b2, w3, b3,
                 freq, decay, skip, w_out):
    vv, gate = _hy_in_conv(x, mod, layer, w_in, conv_w, conv_b)
    gs = []
    for L, nb, blk0, tabs in ((L_P, B_P, 0, tabs_p), (L_S, B_S, T_P // L_S, tabs_s)):
        hsum, hdiff, nyq = _hy_filter(L, w1, b1, w2, b2, w3, b3, freq, decay)
        kre, kim = _dft_fwd(tabs, (hsum, hdiff), L, 1, 0)
        vc, vs = _dft_fwd(tabs, (vv,), L, nb, blk0)
        gs.append(_dft_inv(tabs, vc, vs, kre, kim, nyq, vv, skip, gate, L, nb, blk0))
    return _out_proj_ln2(gs[0], gs[1], w_out, x, mod, layer, ln_g, ln_b)


def _rope_tables():
    rows = L_S // GRID_W
    half = MLA_ROPE // 2
    inv = ROPE_BASE ** (-jnp.arange(0, half, 2, dtype=F32) / half)
    r = jnp.repeat(jnp.arange(rows, dtype=F32), GRID_W)
    col = jnp.tile(jnp.arange(GRID_W, dtype=F32), rows)
    ar, ac = r[:, None] * inv, col[:, None] * inv
    ang = jnp.concatenate([ar, ar, ac, ac], -1)
    cos, sin = jnp.cos(ang), jnp.sin(ang)
    cos = jnp.concatenate([jnp.ones((TM, MLA_ROPE), F32), cos], 0)
    sin = jnp.concatenate([jnp.zeros((TM, MLA_ROPE), F32), sin], 0)
    return cos, sin, jnp.tile(cos, (1, MLA_HEADS)), jnp.tile(sin, (1, MLA_HEADS))


def _rope_rot_cols(w):
    idx = np.concatenate([np.arange(8, 16), np.arange(0, 8), np.arange(24, 32), np.arange(16, 24)])
    sign = np.concatenate([-np.ones(8), np.ones(8), -np.ones(8), np.ones(8)]).astype(np.float32)
    return w[..., idx] * sign


def _mla_in_kernel(x_ref, m_ref, wq_ref, wkv_ref, wkp_ref, wz_ref, qn_ref, kvn_ref, wqn_ref, wqp_ref, wqr_ref,
                   c32_ref, s32_ref, c512_ref, s512_ref,
                   qno_ref, qpe_ref, ckv_ref, kpe_ref, kpr_ref, sz_ref):
    h = _modulate(x_ref[...], m_ref[...]).astype(BF16)
    q_c = _dot(h, wq_ref[...])
    kv_c = _dot(h, wkv_ref[...])
    kp2 = _dot(h, wkp_ref[...])
    z = _dot(h, wz_ref[...])

    def rms(v, g):
        return v * lax.rsqrt(jnp.mean(v * v, axis=-1, keepdims=True) + RMS_EPS) * g

    qn = rms(q_c, qn_ref[...]).astype(BF16)
    scale = (MLA_NOPE + MLA_ROPE) ** -0.5
    qno_ref[...] = (_dot(qn, wqn_ref[...]) * scale).astype(BF16)
    q_pe = _dot(qn, wqp_ref[...]) * c512_ref[...] + _dot(qn, wqr_ref[...]) * s512_ref[...]
    qpe_ref[...] = (q_pe * scale).astype(BF16)
    ckv_ref[...] = rms(kv_c, kvn_ref[...])
    kpe = kp2[:, :MLA_ROPE]
    kpe_ref[...] = kpe
    kpr_ref[...] = kpe * c32_ref[...] + kp2[:, MLA_ROPE:] * s32_ref[...]
    sz_ref[...] = _silu(z)


def _mla_in_proj(x, mod, layer, w_in, q_norm, kv_norm, w_q_up, rope):
    c32, s32, c512, s512 = rope
    o1, o2, o3 = MLA_Q_RANK, MLA_Q_RANK + MLA_KV_RANK, MLA_Q_RANK + MLA_KV_RANK + MLA_ROPE
    wq, wkv, wkp, wz = w_in[:, :o1], w_in[:, o1:o2], w_in[:, o2:o3], w_in[:, o3:]
    wkp2 = jnp.concatenate([wkp, _rope_rot_cols(wkp)], -1)
    wqu = w_q_up.reshape(MLA_Q_RANK, MLA_HEADS, MLA_NOPE + MLA_ROPE)
    wqn = wqu[:, :, :MLA_NOPE].reshape(MLA_Q_RANK, MLA_HEADS * MLA_NOPE)
    wqp = wqu[:, :, MLA_NOPE:]
    wqr = _rope_rot_cols(wqp).reshape(MLA_Q_RANK, MLA_HEADS * MLA_ROPE)
    wqp = wqp.reshape(MLA_Q_RANK, MLA_HEADS * MLA_ROPE)
    full = lambda a: pl.BlockSpec(a.shape, lambda i: (0,) * a.ndim)
    rope_idx = lambda i: jnp.where(i < NT_P, 0, 1 + (i - NT_P) % NT_S_SEQ)
    rows = lambda n: pl.BlockSpec((TM, n), lambda i: (i, 0))
    tab = lambda n: pl.BlockSpec((TM, n), lambda i: (rope_idx(i), 0))
    weights = [wq.astype(BF16), wkv.astype(BF16), wkp2.astype(BF16), wz.astype(BF16),
               q_norm.reshape(1, -1), kv_norm.reshape(1, -1),
               wqn.astype(BF16), wqp.astype(BF16), wqr.astype(BF16)]
    npe = MLA_HEADS * MLA_ROPE
    return pl.pallas_call(
        _mla_in_kernel,
        grid=(NT,),
        in_specs=[rows(D), _mod_spec(layer)] + [full(a) for a in weights]
                 + [tab(MLA_ROPE), tab(MLA_ROPE), tab(npe), tab(npe)],
        out_specs=[rows(D), rows(npe), rows(MLA_KV_RANK), rows(MLA_ROPE), rows(MLA_ROPE), rows(D)],
        out_shape=[jax.ShapeDtypeStruct((T, D), BF16), jax.ShapeDtypeStruct((T, npe), BF16),
                   jax.ShapeDtypeStruct((T, MLA_KV_RANK), F32), jax.ShapeDtypeStruct((T, MLA_ROPE), F32),
                   jax.ShapeDtypeStruct((T, MLA_ROPE), F32), jax.ShapeDtypeStruct((T, D), F32)],
        compiler_params=_cparams(("arbitrary",)),
        name="mla_in_proj",
    )(x, mod, *weights, c32, s32, c512, s512)


def _mla_attn_kernel(*refs, n_cache, hg):
    if n_cache:
        (qn_ref, qp_ref, ckv_ref, kpr_ref, cckv_ref, ckpe_ref, wk_ref, wvt_ref, sz_ref,
         o_ref, kcat_s, vt_s) = refs
    else:
        qn_ref, qp_ref, ckv_ref, kpr_ref, wk_ref, wvt_ref, sz_ref, o_ref, kcat_s, vt_s = refs

    @pl.when(pl.program_id(2) == 0)
    def _():
        cc = ckv_ref[...].astype(BF16)
        kp = kpr_ref[...].astype(BF16)
        if n_cache:
            cc = jnp.concatenate([cckv_ref[...].astype(BF16), cc], axis=0)
            kp = jnp.concatenate([ckpe_ref[...].astype(BF16), kp], axis=0)
        kn = _dot(cc, wk_ref[...]).astype(BF16)
        for hh in range(hg):
            kcat_s[hh] = jnp.concatenate([kn[:, hh * MLA_NOPE:(hh + 1) * MLA_NOPE], kp], axis=1)
            vt_s[hh] = _dot_nt(wvt_ref[hh], cc).astype(BF16)

    qn_all, qp_all = qn_ref[...], qp_ref[...]

    def scores(hh):
        qcat = jnp.concatenate([qn_all[:, hh * MLA_NOPE:(hh + 1) * MLA_NOPE],
                                qp_all[:, hh * MLA_ROPE:(hh + 1) * MLA_ROPE]], axis=1)
        return _dot_nt(kcat_s[hh], qcat)

    outs = []
    ahead = 2
    pending = [scores(hh) for hh in range(min(ahead, hg))]
    for hh in range(hg):
        s = pending.pop(0)
        if hh + ahead < hg:
            pending.append(scores(hh + ahead))
        p = jnp.exp(s - jnp.max(s, axis=0, keepdims=True))
        l = jnp.sum(p, axis=0, keepdims=True)
        outs.append(_dot(vt_s[hh], p.astype(BF16)) / l)
    o_ref[...] = jnp.concatenate(outs, axis=0).T * sz_ref[...]


def _mla_attention(qno, qpe, ckv, kpr, sz, w_kv_up, cache_ckv, cache_kpe, *, nb, L, row_blk0, hg):
    wkv = w_kv_up.reshape(MLA_KV_RANK, MLA_HEADS, MLA_NOPE + MLA_V)
    wk = wkv[:, :, :MLA_NOPE].reshape(MLA_KV_RANK, D).astype(BF16)
    wvt = wkv[:, :, MLA_NOPE:].transpose(1, 2, 0).astype(BF16)
    n_cache = 0 if cache_ckv is None else cache_ckv.shape[1]
    lk = n_cache + L
    nq = L // TM
    wq = hg * MLA_NOPE
    wp = hg * MLA_ROPE
    qrow = lambda w: pl.BlockSpec((TM, w), lambda b, g, q: (row_blk0 + b * nq + q, g))
    seq = lambda w: pl.BlockSpec((L, w), lambda b, g, q: (row_blk0 * TM // L + b, 0))
    in_specs = [qrow(wq), qrow(wp), seq(MLA_KV_RANK), seq(MLA_ROPE)]
    args = [qno, qpe, ckv, kpr]
    if n_cache:
        in_specs += [pl.BlockSpec((None, n_cache, MLA_KV_RANK), lambda b, g, q: (b, 0, 0)),
                     pl.BlockSpec((None, n_cache, MLA_ROPE), lambda b, g, q: (b, 0, 0))]
        args += [cache_ckv, cache_kpe]
    in_specs += [pl.BlockSpec((MLA_KV_RANK, wq), lambda b, g, q: (0, g)),
                 pl.BlockSpec((hg, MLA_V, MLA_KV_RANK), lambda b, g, q: (g, 0, 0)), qrow(wq)]
    args += [wk, wvt, sz]
    return pl.pallas_call(
        functools.partial(_mla_attn_kernel, n_cache=n_cache, hg=hg),
        grid=(nb, MLA_HEADS // hg, nq),
        in_specs=in_specs,
        out_specs=pl.BlockSpec((TM, wq), lambda b, g, q: (b * nq + q, g)),
        out_shape=jax.ShapeDtypeStruct((nb * L, D), F32),
        scratch_shapes=[pltpu.VMEM((hg, lk, MLA_NOPE + MLA_ROPE), BF16), pltpu.VMEM((hg, MLA_V, lk), BF16)],
        compiler_params=_cparams(("arbitrary", "arbitrary", "arbitrary")),
        name="mla_attention",
    )(*args)


def _mla_layer(x, mod, layer, rope, ln_g, ln_b, cache_ckv, cache_kpe, w_in, q_norm, kv_norm, w_q_up, w_kv_up, w_out):
    qno, qpe, ckv, kpe, kpr, sz = _mla_in_proj(x, mod, layer, w_in, q_norm, kv_norm, w_q_up, rope)
    g_p = _mla_attention(qno, qpe, ckv, kpr, sz, w_kv_up, None, None, nb=B_P, L=L_P, row_blk0=0,
                         hg=MLA_HEADS)
    g_s = _mla_attention(qno, qpe, ckv, kpr, sz, w_kv_up, cache_ckv, cache_kpe, nb=B_S, L=L_S, row_blk0=NT_P,
                         hg=MLA_HEADS // 2)
    x_new = _out_proj_ln2(g_p, g_s, w_out, x, mod, layer, ln_g, ln_b)
    new_ckv = ckv[:T_P].reshape(B_P, 1, L_P, MLA_KV_RANK)
    new_kpe = kpe[:T_P].reshape(B_P, 1, L_P, MLA_ROPE)
    return x_new, new_ckv, new_kpe


def _rw_in_kernel(x_ref, xp_ref, xn_ref, m_ref, mu_ref, wr_ref, wk_ref, wv_ref, wg_ref, w1_ref, a1_ref,
                  w2_ref, a2_ref, w0_ref, a0_ref, kk_ref, ka_ref, rk_ref, ones_ref,
                  r_ref, v_ref, sz_ref, nkk_ref, bonus_ref, lw_ref, kd_ref, bd_ref):
    i = pl.program_id(0)
    has_prev, has_next = _tile_has_neighbours(i)
    m = m_ref[...]
    h = _modulate(x_ref[...], m)
    prev, nxt = _neighbour_rows(h, _modulate(xp_ref[...], m), _modulate(xn_ref[...], m), has_prev, has_next)
    d = 0.5 * (prev + nxt) - h
    mu = mu_ref[...]

    def mix(p):
        return (h + d * mu[p:p + 1]).astype(BF16)

    r = _dot(mix(0), wr_ref[...])
    tw = jnp.tanh(_dot(mix(1), w1_ref[...])).astype(BF16)
    k = _dot(mix(2), wk_ref[...])
    v = _dot(mix(3), wv_ref[...])
    ta = _dot(mix(4), a1_ref[...]).astype(BF16)
    z = _dot(mix(5), wg_ref[...])
    r_ref[...] = r.astype(r_ref.dtype)
    v_ref[...] = v.astype(v_ref.dtype)
    sz_ref[...] = _silu(z)
    ones_bd = ones_ref[...]
    kk = k * kk_ref[...]
    kk = kk * lax.rsqrt(_head_sum(kk * kk, ones_bd) + 1e-12)
    nkk_ref[...] = (-kk).astype(nkk_ref.dtype)
    coef = jnp.zeros_like(r)
    for n in range(2):
        wl = w0_ref[n:n + 1, :] + _dot(tw, w2_ref[n])
        lw_ref[n] = -math.exp(-0.5) * _sigmoid(wl)
        a = _sigmoid(a0_ref[n:n + 1, :] + _dot(ta, a2_ref[n]))
        kd = k * (1.0 + (a - 1.0) * ka_ref[...])
        kd_ref[n] = kd.astype(kd_ref.dtype)
        bd_ref[n] = (kk * a).astype(bd_ref.dtype)
        coef = coef + r * kd * rk_ref[...]
    bonus_ref[...] = _head_sum(coef, ones_bd) * v


def _pad_lora_up(w):
    z = jnp.zeros_like(w[0])
    return jnp.stack([jnp.concatenate([w[0], z], 0), jnp.concatenate([z, w[1]], 0)])


def _head_ones():
    h = np.arange(LANES) // RW_N
    return jnp.asarray(h[:, None] == h[None, :], dtype=BF16)


def _rw_in_proj(x, mod, layer, ones_bd, mu, w_in, w0, w1, w2, a0, a1, a2, k_k, k_a, r_k):
    mu8 = jnp.pad(mu, ((0, 2), (0, 0)))
    w1c = jnp.concatenate([w1[0], w1[1]], -1).astype(BF16)
    a1c = jnp.concatenate([a1[0], a1[1]], -1).astype(BF16)
    w2p = _pad_lora_up(w2).astype(BF16)
    a2p = _pad_lora_up(a2).astype(BF16)
    row = pl.BlockSpec((TM, D), lambda i: (i, 0))
    row2 = pl.BlockSpec((2, TM, D), lambda i: (0, i, 0))
    prev, nxt = _halo_specs(D, lambda i: 0)
    full = lambda a: pl.BlockSpec(a.shape, lambda i: (0,) * a.ndim)
    consts = [mu8, w_in[0].astype(BF16), w_in[1].astype(BF16), w_in[2].astype(BF16), w_in[3].astype(BF16),
              w1c, a1c, w2p, a2p, w0, a0, k_k.reshape(1, D), k_a.reshape(1, D), r_k.reshape(1, D), ones_bd]
    return pl.pallas_call(
        _rw_in_kernel,
        grid=(NT,),
        in_specs=[row, prev, nxt, _mod_spec(layer)] + [full(a) for a in consts],
        out_specs=[row] * 5 + [row2] * 3,
        out_shape=[jax.ShapeDtypeStruct((T, D), dt) for dt in (BF16, BF16, F32, BF16, F32)]
                  + [jax.ShapeDtypeStruct((2, T, D), dt) for dt in (F32, BF16, BF16)],
        compiler_params=_cparams(("arbitrary",)),
        name="rwkv_in_proj",
    )(x, x, x, mod, *consts)


SCAN_NB = 2
SCAN_NU = 2 * SCAN_NB


SCAN_PACK = 4
SCAN_GW = SCAN_PACK * RW_N


def _rw_scan_kernel(*refs):
    rb_refs = refs[:SCAN_NU]
    first_ref = refs[SCAN_NU]
    ins = refs[SCAN_NU + 1:SCAN_NU + 1 + 6 * SCAN_NU]
    s0_ref, y_ref, s_ref = refs[SCAN_NU + 1 + 6 * SCAN_NU:]
    del rb_refs
    step = pl.program_id(0)
    c, gw = CHUNK, SCAN_GW

    @pl.when(first_ref[step] == 1)
    def _():
        s_ref[...] = s0_ref[...]

    rc = lax.broadcasted_iota(jnp.int32, (c, c), 0)
    cc = lax.broadcasted_iota(jnp.int32, (c, c), 1)
    ri = lax.broadcasted_iota(jnp.int32, (c, gw), 0)
    lane = lax.broadcasted_iota(jnp.int32, (c, gw), 1)
    pos = lane & (RW_N - 1)
    ri8 = lax.broadcasted_iota(jnp.int32, (c, 2 * gw), 0)
    pos8 = lax.broadcasted_iota(jnp.int32, (c, 2 * gw), 1) & (RW_N - 1)
    head_of = [(lane >> 6) == hh for hh in range(SCAN_PACK)]
    eye_f = (ri == pos).astype(F32)
    eye = eye_f.astype(BF16)
    rdec = lax.broadcasted_iota(jnp.int32, (LANES, gw), 0)
    ldec = lax.broadcasted_iota(jnp.int32, (LANES, gw), 1)
    dec_keep = jnp.logical_and((rdec & (ROWS_BF16 - 1)) == (ldec >> 6), rdec < 2 * ROWS_BF16)
    dec_keep_f = dec_keep.astype(F32)
    dec_sum = dec_keep.astype(BF16)
    dec_hi = jnp.logical_and(dec_keep, rdec < ROWS_BF16).astype(BF16)
    dec_lo = dec_sum - dec_hi
    masks = []
    for sgn in (1, -1):
        incl8 = (ri8 - pos8) * sgn >= 0
        strict = (ri - pos) * sgn > 0
        off_masks = []
        for lg in range(6):
            same_2m = (ri >> (lg + 1)) == (pos >> (lg + 1))
            diff_m = (ri >> lg) != (pos >> lg)
            off_masks.append(jnp.logical_and(strict, jnp.logical_and(same_2m, diff_m)))
        masks.append((incl8, strict, off_masks))

    head_bf = [m.astype(BF16) for m in head_of]

    def keep(x, hh):
        return x * head_bf[hh]

    def bdiag(x):
        return jnp.concatenate([keep(x, hh) for hh in range(SCAN_PACK)], axis=0)

    groups = range(D // gw)
    chains = [(q, g) for q in range(SCAN_NU) for g in groups]
    ar, rb, rbe, v4 = {}, {}, {}, {}
    for q in range(SCAN_NU):
        r_ref, v_ref, nkk_ref, lw_ref, kd_ref, bd_ref = ins[6 * q:6 * q + 6]
        bwd = q % 2
        lw = lw_ref[...]
        cum = (rc - cc) * (-1 if bwd else 1) >= 0
        lw_hi = lw.astype(BF16)
        lw_mid, lw_lo = _split(lw - lw_hi.astype(F32))
        cum_b = cum.astype(BF16)
        g_in = _dot(cum_b, lw_hi) + _dot(cum_b, lw_mid) + _dot(cum_b, lw_lo)
        gtot = g_in[0:1, :] if bwd else g_in[c - 1:c, :]
        e_inv = jnp.exp(-g_in)
        e_rem = jnp.exp(gtot - g_in)
        e_tot = jnp.exp(gtot)
        a_t = nkk_ref[...].astype(F32) * jnp.exp(g_in - lw)
        r_t = r_ref[...].astype(F32) * jnp.exp(g_in)
        kd, bd = kd_ref[...].astype(F32), bd_ref[...].astype(F32)
        b_t, k_t, b_e, k_e = bd * e_inv, kd * e_inv, bd * e_rem, kd * e_rem
        v = v_ref[...]
        for g in groups:
            sl = slice(g * gw, (g + 1) * gw)
            ar[q, g] = jnp.concatenate([a_t[:, sl], r_t[:, sl]], axis=0).astype(BF16)
            bt4, kt4 = b_t[:, sl].astype(BF16), k_t[:, sl].astype(BF16)
            rb[q, g] = jnp.concatenate([bdiag(bt4), bdiag(kt4)], axis=0)
            e_hi, e_lo = _split(e_tot[:, sl] * dec_keep_f)
            dec = e_hi * dec_hi + e_lo * dec_lo
            rbe[q, g] = jnp.concatenate([bdiag(b_e[:, sl].astype(BF16)), bdiag(k_e[:, sl].astype(BF16)), dec],
                                        axis=0)
            v4[q, g] = v[:, sl]
    h_old = {ch: s_ref[ch[0], ch[1]] for ch in chains}
    gm = {ch: _dot_nt(ar[ch], rb[ch]) for ch in chains}
    a_s = {ch: _dot(ar[ch], bdiag(h_old[ch].astype(BF16))) for ch in chains}
    bke_t = {ch: _dot_nt(eye, rbe[ch]) for ch in chains}
    lab = {ch: gm[ch][:c, :gw] for ch in chains}
    offs = lambda ch: masks[ch[0] % 2][2]
    tinv = {ch: eye_f + jnp.where(offs(ch)[0], lab[ch], 0.0) for ch in chains}
    for lg in range(1, 6):
        tb = {ch: tinv[ch].astype(BF16) for ch in chains}
        lt = {ch: _dot(jnp.where(offs(ch)[lg], lab[ch], 0.0).astype(BF16), bdiag(tb[ch])) for ch in chains}
        tinv = {ch: tinv[ch] + _dot(tb[ch], bdiag(lt[ch].astype(BF16))) for ch in chains}
    v_bd = {ch: bdiag(v4[ch]) for ch in chains}
    w1 = {ch: a_s[ch][:c] + _dot(jnp.where(masks[ch[0] % 2][1], gm[ch][:c, gw:], 0.0).astype(BF16), v_bd[ch])
          for ch in chains}
    u = {ch: _dot(tinv[ch].astype(BF16), bdiag(w1[ch].astype(BF16))) for ch in chains}
    lhs = {ch: jnp.concatenate([jnp.where(masks[ch[0] % 2][0], gm[ch][c:], 0.0), bke_t[ch][:, :2 * gw]],
                               axis=0).astype(BF16) for ch in chains}
    yu = {ch: _dot(lhs[ch], jnp.concatenate([bdiag(u[ch].astype(BF16)), v_bd[ch]], axis=0)) for ch in chains}
    e_col = {ch: _dot(bke_t[ch][:, 2 * gw:].astype(BF16), dec_sum) for ch in chains}
    for q in range(SCAN_NU):
        y_ref[q] = jnp.concatenate([a_s[q, g][c:] + yu[q, g][:c] for g in groups], axis=1)
    for ch in chains:
        s_ref[ch[0], ch[1]] = h_old[ch] * e_col[ch] + yu[ch][c:]


def _rw_scan_tables():
    rb = [[] for _ in range(SCAN_NU)]
    first = []
    for nb, L, blk0 in ((B_P, L_P, 0), (B_S, L_S, T_P // CHUNK)):
        nc = L // CHUNK
        for grp in range(nb // SCAN_NB):
            for j in range(nc):
                for q in range(SCAN_NU):
                    b = grp * SCAN_NB + q // 2
                    cn = j if q % 2 == 0 else nc - 1 - j
                    rb[q].append(blk0 + b * nc + cn)
                first.append(1 if j == 0 else 0)
    as_i32 = lambda a: jnp.asarray(np.asarray(a, np.int32))
    return [as_i32(a) for a in rb], as_i32(first)


def _rw_scan(r, v, nkk, lw, kd, bd, s0_all):
    rb, first = _rw_scan_tables()
    n_steps = first.shape[0]
    n_groups = s0_all.shape[0] // SCAN_NU
    steps_p = (B_P // SCAN_NB) * (L_P // CHUNK)
    in_specs = []
    args = []
    for q in range(SCAN_NU):
        tok = pl.BlockSpec((CHUNK, D), lambda s, *t, q=q: (t[q][s], 0))
        tok2 = pl.BlockSpec((None, CHUNK, D), lambda s, *t, q=q: (q % 2, t[q][s], 0))
        in_specs += [tok, tok, tok, tok2, tok2, tok2]
        args += [r, v, nkk, lw, kd, bd]
    grp = lambda s: jnp.where(s < steps_p, s // (L_P // CHUNK),
                              B_P // SCAN_NB + (s - steps_p) // (L_S // CHUNK))
    n_hg = RW_H // SCAN_PACK
    st = pl.BlockSpec((None, SCAN_NU, n_hg, RW_N, SCAN_GW), lambda s, *t: (grp(s), 0, 0, 0, 0))
    s0g = (s0_all.reshape(n_groups, SCAN_NU, n_hg, SCAN_PACK, RW_N, RW_N).transpose(0, 1, 2, 4, 3, 5)
           .reshape(n_groups, SCAN_NU, n_hg, RW_N, SCAN_GW))
    y, s_fin = pl.pallas_call(
        _rw_scan_kernel,
        grid_spec=pltpu.PrefetchScalarGridSpec(
            num_scalar_prefetch=SCAN_NU + 1,
            grid=(n_steps,),
            in_specs=in_specs + [st],
            out_specs=[pl.BlockSpec((None, SCAN_NU, CHUNK, D), lambda s, *t: (s, 0, 0, 0)), st],
        ),
        out_shape=[jax.ShapeDtypeStruct((n_steps, SCAN_NU, CHUNK, D), F32),
                   jax.ShapeDtypeStruct(s0g.shape, F32)],
        compiler_params=_cparams(("arbitrary",)),
        name="rwkv_scan",
    )(*rb, first, *args, s0g)
    s_fin = (s_fin.reshape(n_groups, SCAN_NU, n_hg, RW_N, SCAN_PACK, RW_N).transpose(0, 1, 2, 4, 3, 5)
             .reshape(s0_all.shape))
    return y, s_fin


def _scan_out_index(i, k, bwd):
    per_tile = TM // CHUNK
    nc_p, nc_s = L_P // CHUNK, L_S // CHUNK
    steps_p = (B_P // SCAN_NB) * nc_p
    cn_p = k
    b_s = (i - NT_P) // NT_S_SEQ
    cn_s = ((i - NT_P) % NT_S_SEQ) * per_tile + k
    step_p = (i // SCAN_NB) * nc_p + (nc_p - 1 - cn_p if bwd else cn_p)
    step_s = steps_p + (b_s // SCAN_NB) * nc_s + (nc_s - 1 - cn_s if bwd else cn_s)
    q_p = (i % SCAN_NB) * 2 + bwd
    q_s = (b_s % SCAN_NB) * 2 + bwd
    is_p = i < NT_P
    return jnp.where(is_p, step_p, step_s), jnp.where(is_p, q_p, q_s)


def _rw_out_kernel(*refs):
    per_tile = TM // CHUNK
    y_refs = refs[:2 * per_tile]
    bonus_ref, sz_ref, gg_ref, gb_ref, ones_ref, w_ref, x_ref, m_ref, lng_ref, lnb_ref, o_ref = refs[2 * per_tile:]
    ones_bd = ones_ref[...]
    y = jnp.concatenate([y_refs[k][...] + y_refs[per_tile + k][...] for k in range(per_tile)], axis=0)
    mean = _head_sum(y, ones_bd) * (1.0 / RW_N)
    yc = y - mean
    var = _head_sum(yc * yc, ones_bd) * (1.0 / RW_N)
    yn = yc * lax.rsqrt(var + RW_GN_EPS) * gg_ref[...] + gb_ref[...]
    g = (yn + bonus_ref[...]) * sz_ref[...]
    _out_ln_tail(g, w_ref, x_ref, m_ref, lng_ref, lnb_ref, o_ref)


def _rw_out_proj_ln(y, bonus, sz, gn_g, gn_b, ones_bd, w_out, x, mod, layer, ln_g, ln_b):
    row = pl.BlockSpec((TM, D), lambda i: (i, 0))
    vec = pl.BlockSpec((1, D), lambda i: (0, 0))
    mat = pl.BlockSpec((D, D), lambda i: (0, 0))
    y_specs = [pl.BlockSpec((None, None, CHUNK, D), lambda i, k=k, bwd=bwd: _scan_out_index(i, k, bwd) + (0, 0))
               for bwd in (0, 1) for k in range(TM // CHUNK)]
    return pl.pallas_call(
        _rw_out_kernel,
        grid=(NT,),
        in_specs=y_specs + [row, row, vec, vec,
                  pl.BlockSpec((LANES, LANES), lambda i: (0, 0)), mat, row,
                  _mod_spec(layer), vec, vec],
        out_specs=row,
        out_shape=jax.ShapeDtypeStruct((T, D), F32),
        compiler_params=_cparams(("arbitrary",)),
        name="rwkv_out_proj_ln",
    )(*[y] * len(y_specs), bonus, sz, gn_g.reshape(1, D), gn_b.reshape(1, D), ones_bd, w_out.astype(BF16), x, mod,
      ln_g.reshape(1, D), ln_b.reshape(1, D))


def _rwkv_layer(x, mod, layer, ln_g, ln_b, state, mu, w_in, w0, w1, w2, a0, a1, a2, k_k, k_a, r_k, gn_g, gn_b, w_out):
    ones_bd = _head_ones()
    r, v, sz, nkk, bonus, lw, kd, bd = _rw_in_proj(x, mod, layer, ones_bd, mu, w_in, w0, w1, w2, a0, a1, a2,
                                                  k_k, k_a, r_k)
    n_p = B_P * 2
    s0_all = jnp.concatenate([jnp.zeros((n_p, RW_H, RW_N, RW_N), F32),
                              state.astype(F32).reshape(B_S * 2, RW_H, RW_N, RW_N).swapaxes(-1, -2)], 0)
    y, h_fin = _rw_scan(r, v, nkk, lw, kd, bd, s0_all)
    x_new = _rw_out_proj_ln(y, bonus, sz, gn_g, gn_b, ones_bd, w_out, x, mod, layer, ln_g, ln_b)
    new_state = h_fin[:n_p].swapaxes(-1, -2).reshape(B_P, 1, 2, RW_H, RW_N, RW_N)
    return x_new, new_state


def kernel(x_prompt, x_sample, cache_mla_ckv, cache_mla_kpe, state_rwkv, c, c_ctx, mod_w, mod_b, ln_g, ln_b, hy_w_in, hy_conv_w, hy_conv_b, hy_ffn_w1, hy_ffn_b1, hy_ffn_w2, hy_ffn_b2, hy_ffn_w3, hy_ffn_b3, hy_freq, hy_decay, hy_skip, hy_w_out, mla_w_in, mla_q_norm, mla_kv_norm, mla_w_q_up, mla_w_kv_up, mla_w_out, rw_mu, rw_w_in, rw_w0, rw_w1, rw_w2, rw_a0, rw_a1, rw_a2, rw_k_k, rw_k_a, rw_r_k, rw_gn_g, rw_gn_b, rw_w_out):
    x = jnp.concatenate([x_prompt.reshape(T_P, D), x_sample.reshape(T_S, D)], 0)
    cond8 = jnp.concatenate([c_ctx[None, :], c, jnp.zeros((8 - 1 - B_S, D), F32)], 0)
    mod = _modulation_table(cond8, mod_w, mod_b).reshape(DEPTH * 8, 1, 3 * D)
    tabs_p = _dft_tables(L_P)
    tabs_s = _dft_tables(L_S)
    rope = _rope_tables()
    new_ckv = new_kpe = new_state = None
    for i in range(DEPTH):
        kind, j = i % 3, i // 3
        if kind == 0:
            x = _hyena_layer(x, mod, i, tabs_p, tabs_s, ln_g[i], ln_b[i], hy_w_in[j], hy_conv_w[j], hy_conv_b[j],
                             hy_ffn_w1[j], hy_ffn_b1[j], hy_ffn_w2[j], hy_ffn_b2[j], hy_ffn_w3[j], hy_ffn_b3[j],
                             hy_freq[j], hy_decay[j], hy_skip[j], hy_w_out[j])
        elif kind == 1:
            x, new_ckv, new_kpe = _mla_layer(x, mod, i, rope, ln_g[i], ln_b[i], cache_mla_ckv[:, j],
                                             cache_mla_kpe[:, j], mla_w_in[j], mla_q_norm[j], mla_kv_norm[j],
                                             mla_w_q_up[j], mla_w_kv_up[j], mla_w_out[j])
        else:
            x, new_state = _rwkv_layer(x, mod, i, ln_g[i], ln_b[i], state_rwkv[:, j], rw_mu[j], rw_w_in[j],
                                       rw_w0[j], rw_w1[j], rw_w2[j], rw_a0[j], rw_a1[j], rw_a2[j], rw_k_k[j],
                                       rw_k_a[j], rw_r_k[j], rw_gn_g[j], rw_gn_b[j], rw_w_out[j])
    return (x[:T_P].reshape(B_P, L_P, D), x[T_P:].reshape(B_S, L_S, D), new_ckv, new_kpe, new_state)
```

```python
import functools
import math

import numpy as np
import jax
import jax.numpy as jnp
from jax import lax
from jax.experimental import pallas as pl
from jax.experimental.pallas import tpu as pltpu

F32 = jnp.float32
BF16 = jnp.bfloat16
HIGHEST = lax.Precision.HIGHEST

D = 1024
B_P, L_P = 16, 256
B_S, L_S = 2, 2048
T_P = B_P * L_P
T_S = B_S * L_S
T = T_P + T_S
PAST = 512
DEPTH = 4
DEEPNORM_ALPHA = (2.0 * DEPTH) ** 0.25
LN_EPS = 1e-5
RMS_EPS = 1e-6
HY_BANDS = 16
HY_FFN = 64
MLA_HEADS = 16
MLA_Q_RANK = 256
MLA_KV_RANK = 128
MLA_NOPE = 64
MLA_ROPE = 32
MLA_V = 64
ROPE_BASE = 10000.0
GRID_W = 64
RW_N = 64
RW_H = D // RW_N
RW_LORA = 64
RW_GN_EPS = 64e-5

TM = 256
NT_P = T_P // TM
NT_S_SEQ = L_S // TM
NT = T // TM
HALO = 8
LANES = 128
ROWS_BF16 = 16
CHUNK = 64
DFT_TK = 512
VMEM_LIMIT = 52 * 1024 * 1024


def _cparams(sem):
    return pltpu.CompilerParams(dimension_semantics=sem, vmem_limit_bytes=VMEM_LIMIT)


def _group(i, tm=TM):
    return jnp.where(i < T_P // tm, 0, 1 + (i - T_P // tm) // (L_S // tm))


def _sigmoid(x):
    return 1.0 / (1.0 + jnp.exp(-x))


def _silu(x):
    return x * _sigmoid(x)


def _dot(a, b):
    return jnp.dot(a, b, preferred_element_type=F32)


def _dot_nt(a, b):
    return lax.dot_general(a, b, (((1,), (1,)), ((), ())), preferred_element_type=F32)


def _dot_hi(a, b):
    return jnp.dot(a, b, preferred_element_type=F32, precision=HIGHEST)


def _split(x):
    hi = x.astype(BF16)
    lo = (x - hi.astype(F32)).astype(BF16)
    return hi, lo


def _head_sum(x, ones_bd):
    hi, lo = _split(x)
    lanes = ones_bd.shape[0]
    parts = []
    for g in range(x.shape[1] // lanes):
        sl = slice(g * lanes, (g + 1) * lanes)
        parts.append(_dot(hi[:, sl], ones_bd) + _dot(lo[:, sl], ones_bd))
    return jnp.concatenate(parts, axis=1)


def _modulate(x, m):
    return x * (1.0 + m[:, D:2 * D]) + m[:, :D]


def _layer_norm_rows(y, g, b):
    mu = jnp.mean(y, axis=-1, keepdims=True)
    yc = y - mu
    var = jnp.mean(yc * yc, axis=-1, keepdims=True)
    return yc * lax.rsqrt(var + LN_EPS) * g + b


def _neighbour_rows(cur, prev_halo, next_halo, has_prev, has_next):
    rows = cur.shape[0]
    ridx = lax.broadcasted_iota(jnp.int32, cur.shape, 0)
    pr = jnp.where(has_prev, prev_halo[HALO - 1:HALO, :], 0.0)
    nx = jnp.where(has_next, next_halo[0:1, :], 0.0)
    prev = jnp.where(ridx == 0, pr, pltpu.roll(cur, 1, axis=0))
    nxt = jnp.where(ridx == rows - 1, nx, pltpu.roll(cur, rows - 1, axis=0))
    return prev, nxt


def _tile_has_neighbours(i):
    k = (i - NT_P) % NT_S_SEQ
    is_s = i >= NT_P
    return jnp.logical_and(is_s, k != 0), jnp.logical_and(is_s, k != NT_S_SEQ - 1)


def _halo_specs(width, col_of):
    r = TM // HALO
    prev = pl.BlockSpec((HALO, width), lambda i, *a: (jnp.maximum(i * r - 1, 0), col_of(i, *a)))
    nxt = pl.BlockSpec((HALO, width), lambda i, *a: (jnp.minimum((i + 1) * r, T // HALO - 1), col_of(i, *a)))
    return prev, nxt


def _mod_kernel(c_ref, w_ref, b_ref, o_ref):
    o_ref[...] = _dot_hi(_silu(c_ref[...]), w_ref[...]) + b_ref[...]


def _modulation_table(cond8, mod_w, mod_b):
    tn = 1024
    return pl.pallas_call(
        _mod_kernel,
        grid=(DEPTH, 3 * D // tn),
        in_specs=[pl.BlockSpec((8, D), lambda l, j: (0, 0)),
                  pl.BlockSpec((None, D, tn), lambda l, j: (l, 0, j)),
                  pl.BlockSpec((None, 1, tn), lambda l, j: (l, 0, j))],
        out_specs=pl.BlockSpec((None, 8, tn), lambda l, j: (l, 0, j)),
        out_shape=jax.ShapeDtypeStruct((DEPTH, 8, 3 * D), F32),
        compiler_params=_cparams(("arbitrary", "arbitrary")),
        name="modulation",
    )(cond8, mod_w, mod_b.reshape(DEPTH, 1, 3 * D))


def _mod_spec(layer, tm=TM):
    return pl.BlockSpec((None, 1, 3 * D), lambda i, *a: (layer * 8 + _group(i, tm), 0, 0))


def _out_ln_tail(g, w_ref, x_ref, m_ref, lng_ref, lnb_ref, o_ref):
    mix = _dot(g.astype(BF16), w_ref[...])
    gate = m_ref[...][:, 2 * D:]
    y = DEEPNORM_ALPHA * x_ref[...] + gate * mix
    o_ref[...] = _layer_norm_rows(y, lng_ref[...], lnb_ref[...])


TM_OUT = 512


def _out_ln2_kernel(gp_ref, gs_ref, w_ref, x_ref, m_ref, lng_ref, lnb_ref, o_ref):
    g = jnp.where(pl.program_id(0) < T_P // TM_OUT, gp_ref[...], gs_ref[...])
    _out_ln_tail(g, w_ref, x_ref, m_ref, lng_ref, lnb_ref, o_ref)


def _out_proj_ln2(g_p, g_s, w_out, x, mod, layer, ln_g, ln_b):
    nt_p = T_P // TM_OUT
    row = pl.BlockSpec((TM_OUT, D), lambda i: (i, 0))
    vec = pl.BlockSpec((1, D), lambda i: (0, 0))
    return pl.pallas_call(
        _out_ln2_kernel,
        grid=(T // TM_OUT,),
        in_specs=[pl.BlockSpec((TM_OUT, D), lambda i: (jnp.minimum(i, nt_p - 1), 0)),
                  pl.BlockSpec((TM_OUT, D), lambda i: (jnp.maximum(i - nt_p, 0), 0)),
                  pl.BlockSpec((D, D), lambda i: (0, 0)),
                  row, _mod_spec(layer, TM_OUT), vec, vec],
        out_specs=row,
        out_shape=jax.ShapeDtypeStruct((T, D), F32),
        compiler_params=_cparams(("arbitrary",)),
        name="out_proj_ln",
    )(g_p, g_s, w_out.astype(BF16), x, mod, ln_g.reshape(1, D), ln_b.reshape(1, D))


def _hy_in_conv_kernel(x_ref, xp_ref, xn_ref, m_ref, w_ref, cw_ref, cb_ref, vv_ref, gate_ref):
    has_prev, has_next = _tile_has_neighbours(pl.program_id(0))
    x_all = jnp.concatenate([xp_ref[...], x_ref[...], xn_ref[...]], axis=0)
    u = _dot(_modulate(x_all, m_ref[...]).astype(BF16), w_ref[...])
    rows = TM + 2 * HALO
    ridx = lax.broadcasted_iota(jnp.int32, (TM, 1), 0)
    no_prev = jnp.logical_and(ridx == 0, jnp.logical_not(has_prev))
    no_next = jnp.logical_and(ridx == TM - 1, jnp.logical_not(has_next))
    cw = cw_ref[...]
    cb = cb_ref[...]

    def conv(grp):
        sl = slice(grp * D, (grp + 1) * D)
        ug = u[:, sl]
        cur = ug[HALO:HALO + TM]
        prev = jnp.where(no_prev, 0.0, pltpu.roll(ug, 1, axis=0)[HALO:HALO + TM])
        nxt = jnp.where(no_next, 0.0, pltpu.roll(ug, rows - 1, axis=0)[HALO:HALO + TM])
        return prev * cw[0:1, sl] + cur * cw[1:2, sl] + nxt * cw[2:3, sl] + cb[:, sl]

    x0, x1, v = conv(0), conv(1), conv(2)
    vv_ref[...] = v * x1
    gate_ref[...] = x0 * _silu(u[HALO:HALO + TM, 3 * D:])


def _hy_in_conv(x, mod, layer, w_in, conv_w, conv_b):
    row = pl.BlockSpec((TM, D), lambda i: (i, 0))
    prev, nxt = _halo_specs(D, lambda i: 0)
    return pl.pallas_call(
        _hy_in_conv_kernel,
        grid=(NT,),
        in_specs=[row, prev, nxt, _mod_spec(layer), _resident(w_in.shape),
                  pl.BlockSpec((3, 3 * D), lambda i: (0, 0)), pl.BlockSpec((1, 3 * D), lambda i: (0, 0))],
        out_specs=[row, row],
        out_shape=[jax.ShapeDtypeStruct((T, D), F32)] * 2,
        compiler_params=_cparams(("arbitrary",)),
        name="hyena_in_proj_conv3",
    )(x, x, x, mod, w_in.astype(BF16), conv_w, conv_b.reshape(1, 3 * D))


def _hy_filter_kernel(t_ref, bands_ref, wt_ref, wc_ref, ws_ref, b1_ref, w2_ref, b2_ref, w3_ref, b3_ref,
                      f0_ref, f1_ref, dec_ref, hs_ref, hd_ref, nyq_ref, *, L, tr):
    i = pl.program_id(0)
    ridx = lax.broadcasted_iota(jnp.int32, (tr, 1), 0) + i * tr
    pos = ridx.astype(F32)
    t = t_ref[...]
    ang = ((2.0 * math.pi / L) * pos) * bands_ref[...]
    pre = t * wt_ref[...] + _dot_hi(jnp.cos(ang), wc_ref[...]) + _dot_hi(jnp.sin(ang), ws_ref[...])
    hdn = jnp.sin(f0_ref[...] * (pre + b1_ref[...]))
    hdn = jnp.sin(f1_ref[...] * (_dot_hi(hdn, w2_ref[...]) + b2_ref[...]))
    hf = _dot_hi(hdn, w3_ref[...]) + b3_ref[...]
    h = hf * jnp.exp(-t * jnp.abs(dec_ref[...]))
    h0 = h[:, :D]
    h1 = jnp.where(ridx == 0, 0.0, h[:, D:])
    hsum = h0 + h1
    hs_ref[...] = hsum
    hd_ref[...] = h1 - h0
    alt = jnp.where((ridx & 1) == 0, 1.0, -1.0)
    part =jnp.broadcast_to(jnp.sum(alt * hsum, axis=0, keepdims=True), (8, D))

    @pl.when(i == 0)
    def _():
        nyq_ref[...] = part

    @pl.when(i > 0)
    def _():
        nyq_ref[...] += part


def _hy_filter(L, w1, b1, w2, b2, w3, b3, freq, decay):
    tr = 256
    t = jnp.linspace(0.0, 1.0, L, dtype=F32).reshape(L, 1)
    bands = jnp.linspace(1e-4, HY_BANDS - 1, HY_BANDS, dtype=F32)
    bands = jnp.pad(bands, (0, 128 - HY_BANDS)).reshape(1, 128)
    wt = w1[0:1]
    wc = jnp.pad(w1[1:1 + HY_BANDS], ((0, 128 - HY_BANDS), (0, 0)))
    ws = jnp.pad(-w1[1 + HY_BANDS:], ((0, 128 - HY_BANDS), (0, 0)))
    full = lambda shape: pl.BlockSpec(shape, lambda i: (0, 0))
    rows = pl.BlockSpec((tr, D), lambda i: (i, 0))
    return pl.pallas_call(
        functools.partial(_hy_filter_kernel, L=L, tr=tr),
        grid=(L // tr,),
        in_specs=[pl.BlockSpec((tr, 1), lambda i: (i, 0)), full((1, 128)), full((1, HY_FFN)),
                  full((128, HY_FFN)), full((128, HY_FFN)), full((1, HY_FFN)),
                  full((HY_FFN, HY_FFN)), full((1, HY_FFN)), full((HY_FFN, 2 * D)), full((1, 2 * D)),
                  full((1, HY_FFN)), full((1, HY_FFN)), full((1, 2 * D))],
        out_specs=[rows, rows, pl.BlockSpec((8, D), lambda i: (0, 0))],
        out_shape=[jax.ShapeDtypeStruct((L, D), F32), jax.ShapeDtypeStruct((L, D), F32),
                   jax.ShapeDtypeStruct((8, D), F32)],
        compiler_params=_cparams(("arbitrary",)),
        name="hyena_filter",
    )(t, bands, wt, wc, ws, b1.reshape(1, -1), w2, b2.reshape(1, -1), w3, b3.reshape(1, -1),
      freq[0:1], freq[1:2], decay.reshape(1, 2 * D))


def _dft_tables(L):
    n = 2 * L
    w = 64
    k = jnp.arange(L, dtype=jnp.int32)

    def cs(t):
        ang = ((k[:, None] * t[None, :]) % n).astype(F32) * (2.0 * math.pi / n)
        return jnp.cos(ang), jnp.sin(ang)

    tk = min(L, DFT_TK)
    nk = L // tk
    ca, sa = cs(jnp.arange(L // w, dtype=jnp.int32) * w)
    cb, sb = cs(jnp.arange(w, dtype=jnp.int32))
    alt = jnp.where(k % 2 == 0, 1.0, -1.0).astype(F32)

    def row_tiles(p, q, r, s, sgn):
        tab = p[:, :, None] * q[:, None, :] + sgn * r[:, :, None] * s[:, None, :]
        return tab.reshape(nk, tk, L)

    def col_tiles(p, q, r, s, sgn):
        tr = lambda a: a.reshape(nk, tk, -1).transpose(0, 2, 1)
        tab = tr(p)[:, :, None, :] * tr(q)[:, None, :, :] + sgn * tr(r)[:, :, None, :] * tr(s)[:, None, :, :]
        return tab.reshape(nk, L, tk)

    bins = k.reshape(nk, tk)
    c_rows = row_tiles(ca, cb, sa, sb, -1.0)
    c_cols = col_tiles(ca, cb, sa, sb, -1.0)
    s_rows = jnp.where(bins[:, :, None] == 0, alt[None, None, :], row_tiles(sa, cb, ca, sb, 1.0))
    s_cols = jnp.where(bins[:, None, :] == 0, alt[None, :, None], col_tiles(sa, cb, ca, sb, 1.0))
    return tuple(a.astype(BF16) for a in (c_rows, s_rows, c_cols, s_cols))


def _dft_fwd_kernel(c_ref, s_ref, *refs):
    x1_ref, x2_ref = refs[0], refs[-3]
    oc_ref, os_ref = refs[-2:]
    k = pl.program_id(2)
    oc_ref[...] = _dot(c_ref[k], x1_ref[...].astype(BF16))
    os_ref[...] = _dot(s_ref[k], x2_ref[...].astype(BF16))


def _resident(shape):
    return pl.BlockSpec(shape, lambda *_: (0,) * len(shape), pipeline_mode=pl.Buffered(1))


def _dft_fwd(tabs, xs, L, nb, row_blk0):
    c, s_rows = tabs[:2]
    nk, tk, _ = c.shape
    tn = 512
    a_spec = _resident(c.shape)
    x_spec = pl.BlockSpec((L, tn), lambda b, j, k: (row_blk0 + b, j))
    o_spec = pl.BlockSpec((tk, tn), lambda b, j, k: (b * nk + k, j))
    return pl.pallas_call(
        _dft_fwd_kernel,
        grid=(nb, D // tn, nk),
        in_specs=[a_spec] * 2 + [x_spec] * len(xs),
        out_specs=[o_spec, o_spec],
        out_shape=[jax.ShapeDtypeStruct((nb * L, D), F32)] * 2,
        compiler_params=_cparams(("arbitrary", "arbitrary", "arbitrary")),
        name="hyena_dft_fwd",
    )(c, s_rows, *xs)


def _spectrum_product(vc, vs, kre, kim, nyq, first_tile, L):
    bin0 = jnp.logical_and(lax.broadcasted_iota(jnp.int32, vc.shape, 0) == 0, first_tile)
    kim = jnp.where(bin0, nyq, kim)
    inv_n = 1.0 / (2 * L)
    yre = jnp.where(bin0, vc * kre * inv_n, (vc * kre + vs * kim) * (2.0 * inv_n))
    yim = jnp.where(bin0, vs * kim * inv_n, (vs * kre - vc * kim) * (2.0 * inv_n))
    return yre.astype(BF16), yim.astype(BF16)


def _dft_conv_kernel(c_ref, s_ref, st_ref, v_ref, kre_ref, kim_ref, nyq_ref, skip_ref, gate_ref, o_ref, *, L):
    v = v_ref[...]
    vb = v.astype(BF16)
    c = c_ref[0]
    yre, yim = _spectrum_product(_dot(c, vb), _dot(s_ref[0], vb), kre_ref[...], kim_ref[...], nyq_ref[0:1, :],
                                 True, L)
    y = _dot(c, yre) + _dot(st_ref[0], yim)
    o_ref[...] = (y + v * skip_ref[...]) * gate_ref[...]


def _dft_conv(tabs, kre, kim, nyq, vv, skip, gate, L, nb, row_blk0):
    c, s_rows, _, s_cols = tabs
    assert c.shape[0] == 1, "single frequency tile only"
    tn = 512
    seq = lambda b, j: (row_blk0 + b, j)
    spec = pl.BlockSpec((L, tn), lambda b, j: (0, j))
    return pl.pallas_call(
        functools.partial(_dft_conv_kernel, L=L),
        grid=(nb, D // tn),
        in_specs=[_resident(c.shape)] * 3 + [pl.BlockSpec((L, tn), seq), spec, spec,
                                             pl.BlockSpec((8, tn), lambda b, j: (0, j)),
                                             pl.BlockSpec((1, tn), lambda b, j: (0, j)), pl.BlockSpec((L, tn), seq)],
        out_specs=pl.BlockSpec((L, tn), lambda b, j: (b, j)),
        out_shape=jax.ShapeDtypeStruct((nb * L, D), F32),
        compiler_params=_cparams(("arbitrary", "arbitrary")),
        name="hyena_dft_conv_gate",
    )(c, s_rows, s_cols, vv, kre, kim, nyq, skip.reshape(1, D), gate)


def _dft_inv_kernel(c_ref, st_ref, vc_ref, vs_ref, kre_ref, kim_ref, nyq_ref,
                    vv_ref, skip_ref, gate_ref, o_ref, *, L):
    k = pl.program_id(2)
    nk = pl.num_programs(2)
    yre, yim = _spectrum_product(vc_ref[...], vs_ref[...], kre_ref[...], kim_ref[...], nyq_ref[0:1, :], k == 0, L)
    contrib = _dot(c_ref[k], yre) + _dot(st_ref[k], yim)

    @pl.when(k == 0)
    def _():
        o_ref[...] = contrib

    @pl.when(k > 0)
    def _():
        o_ref[...] += contrib

    @pl.when(k == nk - 1)
    def _():
        o_ref[...] = (o_ref[...] + vv_ref[...] * skip_ref[...]) * gate_ref[...]


def _dft_inv(tabs, vc, vs, kre, kim, nyq, vv, skip, gate, L, nb, row_blk0):
    c, s_cols = tabs[2:]
    nk, _, tk = c.shape
    tn = 512 if L <= 512 else 256
    a_spec = _resident(c.shape)
    v_spec = pl.BlockSpec((tk, tn), lambda b, j, k: (b * nk + k, j))
    k_spec = pl.BlockSpec((tk, tn), lambda b, j, k: (k, j))
    row_spec = pl.BlockSpec((L, tn), lambda b, j, k: (row_blk0 + b, j))
    return pl.pallas_call(
        functools.partial(_dft_inv_kernel, L=L),
        grid=(nb, D // tn, nk),
        in_specs=[a_spec] * 2 + [v_spec, v_spec, k_spec, k_spec,
                                 pl.BlockSpec((8, tn), lambda b, j, k: (0, j)),
                                 row_spec, pl.BlockSpec((1, tn), lambda b, j, k: (0, j)), row_spec],
        out_specs=pl.BlockSpec((L, tn), lambda b, j, k: (b, j)),
        out_shape=jax.ShapeDtypeStruct((nb * L, D), F32),
        compiler_params=_cparams(("arbitrary", "arbitrary", "arbitrary")),
        name="hyena_dft_inv_gate",
    )(c, s_cols, vc, vs, kre, kim, nyq, vv, skip.reshape(1, D), gate)


def _hyena_layer(x, mod, layer, tabs_p, tabs_s, ln_g, ln_b, w_in, conv_w, conv_b, w1, b1, w2, b2, w3, b3,
                 freq, decay, skip, w_out):
    vv, gate = _hy_in_conv(x, mod, layer, w_in, conv_w, conv_b)
    gs = []
    for L, nb, blk0, tabs in ((L_P, B_P, 0, tabs_p), (L_S, B_S, T_P // L_S, tabs_s)):
        hsum, hdiff, nyq = _hy_filter(L, w1, b1, w2, b2, w3, b3, freq, decay)
        kre, kim = _dft_fwd(tabs, (hsum, hdiff), L, 1, 0)
        if L <= DFT_TK:
            gs.append(_dft_conv(tabs, kre, kim, nyq, vv, skip, gate, L, nb, blk0))
        else:
            vc, vs = _dft_fwd(tabs, (vv,), L, nb, blk0)
            gs.append(_dft_inv(tabs, vc, vs, kre, kim, nyq, vv, skip, gate, L, nb, blk0))
    return _out_proj_ln2(gs[0], gs[1], w_out, x, mod, layer, ln_g, ln_b)


def _rope_tables():
    rows = L_S // GRID_W
    half = MLA_ROPE // 2
    inv = ROPE_BASE ** (-jnp.arange(0, half, 2, dtype=F32) / half)
    r = jnp.repeat(jnp.arange(rows, dtype=F32), GRID_W)
    col = jnp.tile(jnp.arange(GRID_W, dtype=F32), rows)
    ar, ac = r[:, None] * inv, col[:, None] * inv
    ang = jnp.concatenate([ar, ar, ac, ac], -1)
    cos, sin = jnp.cos(ang), jnp.sin(ang)
    cos = jnp.concatenate([jnp.ones((TM, MLA_ROPE), F32), cos], 0)
    sin = jnp.concatenate([jnp.zeros((TM, MLA_ROPE), F32), sin], 0)
    return cos, sin, jnp.tile(cos, (1, MLA_HEADS)), jnp.tile(sin, (1, MLA_HEADS))


def _rope_rot_cols(w):
    idx = np.concatenate([np.arange(8, 16), np.arange(0, 8), np.arange(24, 32), np.arange(16, 24)])
    sign = np.concatenate([-np.ones(8), np.ones(8), -np.ones(8), np.ones(8)]).astype(np.float32)
    return w[..., idx] * sign


def _mla_in_kernel(x_ref, m_ref, wq_ref, wkv_ref, wkp_ref, wz_ref, qn_ref, kvn_ref, wqn_ref, wqp_ref, wqr_ref,
                   c32_ref, s32_ref, c512_ref, s512_ref,
                   qno_ref, qpe_ref, ckv_ref, kpe_ref, kpr_ref, sz_ref):
    h = _modulate(x_ref[...], m_ref[...]).astype(BF16)
    q_c = _dot(h, wq_ref[...])
    kv_c = _dot(h, wkv_ref[...])
    kp2 = _dot(h, wkp_ref[...])
    z = _dot(h, wz_ref[...])

    def rms(v, g):
        return v * lax.rsqrt(jnp.mean(v * v, axis=-1, keepdims=True) + RMS_EPS) * g

    qn = rms(q_c, qn_ref[...]).astype(BF16)
    scale = (MLA_NOPE + MLA_ROPE) ** -0.5
    qno_ref[...] = (_dot(qn, wqn_ref[...]) * scale).astype(BF16)
    q_pe = _dot(qn, wqp_ref[...]) * c512_ref[...] + _dot(qn, wqr_ref[...]) * s512_ref[...]
    qpe_ref[...] = (q_pe * scale).astype(BF16)
    ckv_ref[...] = rms(kv_c, kvn_ref[...])
    kpe = kp2[:, :MLA_ROPE]
    kpe_ref[...] = kpe
    kpr_ref[...] = kpe * c32_ref[...] + kp2[:, MLA_ROPE:] * s32_ref[...]
    sz_ref[...] = _silu(z)


def _mla_in_proj(x, mod, layer, w_in, q_norm, kv_norm, w_q_up, rope):
    c32, s32, c512, s512 = rope
    o1, o2, o3 = MLA_Q_RANK, MLA_Q_RANK + MLA_KV_RANK, MLA_Q_RANK + MLA_KV_RANK + MLA_ROPE
    wq, wkv, wkp, wz = w_in[:, :o1], w_in[:, o1:o2], w_in[:, o2:o3], w_in[:, o3:]
    wkp2 = jnp.concatenate([wkp, _rope_rot_cols(wkp)], -1)
    wqu = w_q_up.reshape(MLA_Q_RANK, MLA_HEADS, MLA_NOPE + MLA_ROPE)
    wqn = wqu[:, :, :MLA_NOPE].reshape(MLA_Q_RANK, MLA_HEADS * MLA_NOPE)
    wqp = wqu[:, :, MLA_NOPE:]
    wqr = _rope_rot_cols(wqp).reshape(MLA_Q_RANK, MLA_HEADS * MLA_ROPE)
    wqp = wqp.reshape(MLA_Q_RANK, MLA_HEADS * MLA_ROPE)
    full = lambda a: pl.BlockSpec(a.shape, lambda i: (0,) * a.ndim)
    rope_idx = lambda i: jnp.where(i < NT_P, 0, 1 + (i - NT_P) % NT_S_SEQ)
    rows = lambda n: pl.BlockSpec((TM, n), lambda i: (i, 0))
    tab = lambda n: pl.BlockSpec((TM, n), lambda i: (rope_idx(i), 0))
    weights = [wq.astype(BF16), wkv.astype(BF16), wkp2.astype(BF16), wz.astype(BF16),
               q_norm.reshape(1, -1), kv_norm.reshape(1, -1),
               wqn.astype(BF16), wqp.astype(BF16), wqr.astype(BF16)]
    npe = MLA_HEADS * MLA_ROPE
    return pl.pallas_call(
        _mla_in_kernel,
        grid=(NT,),
        in_specs=[rows(D), _mod_spec(layer)] + [full(a) for a in weights]
                 + [tab(MLA_ROPE), tab(MLA_ROPE), tab(npe), tab(npe)],
        out_specs=[rows(D), rows(npe), rows(MLA_KV_RANK), rows(MLA_ROPE), rows(MLA_ROPE), rows(D)],
        out_shape=[jax.ShapeDtypeStruct((T, D), BF16), jax.ShapeDtypeStruct((T, npe), BF16),
                   jax.ShapeDtypeStruct((T, MLA_KV_RANK), F32), jax.ShapeDtypeStruct((T, MLA_ROPE), F32),
                   jax.ShapeDtypeStruct((T, MLA_ROPE), F32), jax.ShapeDtypeStruct((T, D), F32)],
        compiler_params=_cparams(("arbitrary",)),
        name="mla_in_proj",
    )(x, mod, *weights, c32, s32, c512, s512)


def _mla_attn_kernel(*refs, n_cache, hg):
    if n_cache:
        (qn_ref, qp_ref, ckv_ref, kpr_ref, cckv_ref, ckpe_ref, wk_ref, wvt_ref, sz_ref,
         o_ref, kcat_s, vt_s) = refs
    else:
        qn_ref, qp_ref, ckv_ref, kpr_ref, wk_ref, wvt_ref, sz_ref, o_ref, kcat_s, vt_s = refs

    @pl.when(pl.program_id(2) == 0)
    def _():
        cc = ckv_ref[...].astype(BF16)
        kp = kpr_ref[...].astype(BF16)
        if n_cache:
            cc = jnp.concatenate([cckv_ref[...].astype(BF16), cc], axis=0)
            kp = jnp.concatenate([ckpe_ref[...].astype(BF16), kp], axis=0)
        kn = _dot(cc, wk_ref[...]).astype(BF16)
        for hh in range(hg):
            kcat_s[hh] = jnp.concatenate([kn[:, hh * MLA_NOPE:(hh + 1) * MLA_NOPE], kp], axis=1)
            vt_s[hh] = _dot_nt(wvt_ref[hh], cc).astype(BF16)

    qn_all, qp_all = qn_ref[...], qp_ref[...]

    def scores(hh):
        qcat = jnp.concatenate([qn_all[:, hh * MLA_NOPE:(hh + 1) * MLA_NOPE],
                                qp_all[:, hh * MLA_ROPE:(hh + 1) * MLA_ROPE]], axis=1)
        return _dot_nt(kcat_s[hh], qcat)

    outs = []
    ahead = 2
    pending = [scores(hh) for hh in range(min(ahead, hg))]
    for hh in range(hg):
        s = pending.pop(0)
        if hh + ahead < hg:
            pending.append(scores(hh + ahead))
        p = jnp.exp(s - jnp.max(s, axis=0, keepdims=True))
        l = jnp.sum(p, axis=0, keepdims=True)
        outs.append(_dot(vt_s[hh], p.astype(BF16)) / l)
    o_ref[...] = jnp.concatenate(outs, axis=0).T * sz_ref[...]


def _mla_attention(qno, qpe, ckv, kpr, sz, w_kv_up, cache_ckv, cache_kpe, *, nb, L, row_blk0, hg):
    wkv = w_kv_up.reshape(MLA_KV_RANK, MLA_HEADS, MLA_NOPE + MLA_V)
    wk = wkv[:, :, :MLA_NOPE].reshape(MLA_KV_RANK, D).astype(BF16)
    wvt = wkv[:, :, MLA_NOPE:].transpose(1, 2, 0).astype(BF16)
    n_cache = 0 if cache_ckv is None else cache_ckv.shape[1]
    lk = n_cache + L
    nq = L // TM
    wq = hg * MLA_NOPE
    wp = hg * MLA_ROPE
    qrow = lambda w: pl.BlockSpec((TM, w), lambda b, g, q: (row_blk0 + b * nq + q, g))
    seq = lambda w: pl.BlockSpec((L, w), lambda b, g, q: (row_blk0 * TM // L + b, 0))
    in_specs = [qrow(wq), qrow(wp), seq(MLA_KV_RANK), seq(MLA_ROPE)]
    args = [qno, qpe, ckv, kpr]
    if n_cache:
        in_specs += [pl.BlockSpec((None, n_cache, MLA_KV_RANK), lambda b, g, q: (b, 0, 0)),
                     pl.BlockSpec((None, n_cache, MLA_ROPE), lambda b, g, q: (b, 0, 0))]
        args += [cache_ckv, cache_kpe]
    in_specs += [pl.BlockSpec((MLA_KV_RANK, wq), lambda b, g, q: (0, g)),
                 pl.BlockSpec((hg, MLA_V, MLA_KV_RANK), lambda b, g, q: (g, 0, 0)), qrow(wq)]
    args += [wk, wvt, sz]
    return pl.pallas_call(
        functools.partial(_mla_attn_kernel, n_cache=n_cache, hg=hg),
        grid=(nb, MLA_HEADS // hg, nq),
        in_specs=in_specs,
        out_specs=pl.BlockSpec((TM, wq), lambda b, g, q: (b * nq + q, g)),
        out_shape=jax.ShapeDtypeStruct((nb * L, D), F32),
        scratch_shapes=[pltpu.VMEM((hg, lk, MLA_NOPE + MLA_ROPE), BF16), pltpu.VMEM((hg, MLA_V, lk), BF16)],
        compiler_params=_cparams(("arbitrary", "arbitrary", "arbitrary")),
        name="mla_attention",
    )(*args)


def _mla_layer(x, mod, layer, rope, ln_g, ln_b, cache_ckv, cache_kpe, w_in, q_norm, kv_norm, w_q_up, w_kv_up, w_out):
    qno, qpe, ckv, kpe, kpr, sz = _mla_in_proj(x, mod, layer, w_in, q_norm, kv_norm, w_q_up, rope)
    g_p = _mla_attention(qno, qpe, ckv, kpr, sz, w_kv_up, None, None, nb=B_P, L=L_P, row_blk0=0,
                         hg=MLA_HEADS)
    g_s = _mla_attention(qno, qpe, ckv, kpr, sz, w_kv_up, cache_ckv, cache_kpe, nb=B_S, L=L_S, row_blk0=NT_P,
                         hg=MLA_HEADS // 2)
    x_new = _out_proj_ln2(g_p, g_s, w_out, x, mod, layer, ln_g, ln_b)
    new_ckv = ckv[:T_P].reshape(B_P, 1, L_P, MLA_KV_RANK)
    new_kpe = kpe[:T_P].reshape(B_P, 1, L_P, MLA_ROPE)
    return x_new, new_ckv, new_kpe


def _rw_in_kernel(x_ref, xp_ref, xn_ref, m_ref, mu_ref, wr_ref, wk_ref, wv_ref, wg_ref, w1_ref, a1_ref,
                  w2_ref, a2_ref, w0_ref, a0_ref, kk_ref, ka_ref, rk_ref, ones_ref,
                  r_ref, v_ref, sz_ref, nkk_ref, bonus_ref, lw_ref, kd_ref, bd_ref):
    i = pl.program_id(0)
    has_prev, has_next = _tile_has_neighbours(i)
    m = m_ref[...]
    h = _modulate(x_ref[...], m)
    prev, nxt = _neighbour_rows(h, _modulate(xp_ref[...], m), _modulate(xn_ref[...], m), has_prev, has_next)
    d = 0.5 * (prev + nxt) - h
    mu = mu_ref[...]

    def mix(p):
        return (h + d * mu[p:p + 1]).astype(BF16)

    r = _dot(mix(0), wr_ref[...])
    tw = jnp.tanh(_dot(mix(1), w1_ref[...])).astype(BF16)
    k = _dot(mix(2), wk_ref[...])
    v = _dot(mix(3), wv_ref[...])
    ta = _dot(mix(4), a1_ref[...]).astype(BF16)
    z = _dot(mix(5), wg_ref[...])
    r_ref[...] = r.astype(r_ref.dtype)
    v_ref[...] = v.astype(v_ref.dtype)
    sz_ref[...] = _silu(z)
    ones_bd = ones_ref[...]
    kk = k * kk_ref[...]
    kk = kk * lax.rsqrt(_head_sum(kk * kk, ones_bd) + 1e-12)
    nkk_ref[...] = (-kk).astype(nkk_ref.dtype)
    coef = jnp.zeros_like(r)
    for n in range(2):
        wl = w0_ref[n:n + 1, :] + _dot(tw, w2_ref[n])
        lw_ref[n] = -math.exp(-0.5) * _sigmoid(wl)
        a = _sigmoid(a0_ref[n:n + 1, :] + _dot(ta, a2_ref[n]))
        kd = k * (1.0 + (a - 1.0) * ka_ref[...])
        kd_ref[n] = kd.astype(kd_ref.dtype)
        bd_ref[n] = (kk * a).astype(bd_ref.dtype)
        coef = coef + r * kd * rk_ref[...]
    bonus_ref[...] = _head_sum(coef, ones_bd) * v


def _pad_lora_up(w):
    z = jnp.zeros_like(w[0])
    return jnp.stack([jnp.concatenate([w[0], z], 0), jnp.concatenate([z, w[1]], 0)])


def _head_ones():
    h = np.arange(LANES) // RW_N
    return jnp.asarray(h[:, None] == h[None, :], dtype=BF16)


def _rw_in_proj(x, mod, layer, ones_bd, mu, w_in, w0, w1, w2, a0, a1, a2, k_k, k_a, r_k):
    mu8 = jnp.pad(mu, ((0, 2), (0, 0)))
    w1c = jnp.concatenate([w1[0], w1[1]], -1).astype(BF16)
    a1c = jnp.concatenate([a1[0], a1[1]], -1).astype(BF16)
    w2p = _pad_lora_up(w2).astype(BF16)
    a2p = _pad_lora_up(a2).astype(BF16)
    row = pl.BlockSpec((TM, D), lambda i: (i, 0))
    row2 = pl.BlockSpec((2, TM, D), lambda i: (0, i, 0))
    prev, nxt = _halo_specs(D, lambda i: 0)
    full = lambda a: pl.BlockSpec(a.shape, lambda i: (0,) * a.ndim)
    consts = [mu8, w_in[0].astype(BF16), w_in[1].astype(BF16), w_in[2].astype(BF16), w_in[3].astype(BF16),
              w1c, a1c, w2p, a2p, w0, a0, k_k.reshape(1, D), k_a.reshape(1, D), r_k.reshape(1, D), ones_bd]
    return pl.pallas_call(
        _rw_in_kernel,
        grid=(NT,),
        in_specs=[row, prev, nxt, _mod_spec(layer)] + [full(a) for a in consts],
        out_specs=[row] * 5 + [row2] * 3,
        out_shape=[jax.ShapeDtypeStruct((T, D), dt) for dt in (BF16, BF16, F32, BF16, F32)]
                  + [jax.ShapeDtypeStruct((2, T, D), dt) for dt in (F32, BF16, BF16)],
        compiler_params=_cparams(("arbitrary",)),
        name="rwkv_in_proj",
    )(x, x, x, mod, *consts)


SCAN_NB = 2
SCAN_NU = 2 * SCAN_NB


SCAN_PACK = 4
SCAN_GW = SCAN_PACK * RW_N


def _rw_scan_kernel(*refs):
    rb_refs = refs[:SCAN_NU]
    first_ref = refs[SCAN_NU]
    ins = refs[SCAN_NU + 1:SCAN_NU + 1 + 6 * SCAN_NU]
    s0_ref, y_ref, s_ref = refs[SCAN_NU + 1 + 6 * SCAN_NU:]
    del rb_refs
    step = pl.program_id(0)
    c, gw = CHUNK, SCAN_GW

    @pl.when(first_ref[step] == 1)
    def _():
        s_ref[...] = s0_ref[...]

    rc = lax.broadcasted_iota(jnp.int32, (c, c), 0)
    cc = lax.broadcasted_iota(jnp.int32, (c, c), 1)
    ri = lax.broadcasted_iota(jnp.int32, (c, gw), 0)
    lane = lax.broadcasted_iota(jnp.int32, (c, gw), 1)
    pos = lane & (RW_N - 1)
    ri8 = lax.broadcasted_iota(jnp.int32, (c, 2 * gw), 0)
    pos8 = lax.broadcasted_iota(jnp.int32, (c, 2 * gw), 1) & (RW_N - 1)
    head_of = [(lane >> 6) == hh for hh in range(SCAN_PACK)]
    eye_f = (ri == pos).astype(F32)
    eye = eye_f.astype(BF16)
    rdec = lax.broadcasted_iota(jnp.int32, (LANES, gw), 0)
    ldec = lax.broadcasted_iota(jnp.int32, (LANES, gw), 1)
    dec_keep = jnp.logical_and((rdec & (ROWS_BF16 - 1)) == (ldec >> 6), rdec < 2 * ROWS_BF16)
    dec_keep_f = dec_keep.astype(F32)
    dec_sum = dec_keep.astype(BF16)
    dec_hi = jnp.logical_and(dec_keep, rdec < ROWS_BF16).astype(BF16)
    dec_lo = dec_sum - dec_hi
    masks = []
    for sgn in (1, -1):
        incl8 = (ri8 - pos8) * sgn >= 0
        strict = (ri - pos) * sgn > 0
        off_masks = []
        for lg in range(6):
            same_2m = (ri >> (lg + 1)) == (pos >> (lg + 1))
            diff_m = (ri >> lg) != (pos >> lg)
            off_masks.append(jnp.logical_and(strict, jnp.logical_and(same_2m, diff_m)))
        masks.append((incl8, strict, off_masks))

    head_bf = [m.astype(BF16) for m in head_of]

    def keep(x, hh):
        return x * head_bf[hh]

    def bdiag(x):
        return jnp.concatenate([keep(x, hh) for hh in range(SCAN_PACK)], axis=0)

    groups = range(D // gw)
    chains = [(q, g) for q in range(SCAN_NU) for g in groups]
    ar, rb, rbe, v4 = {}, {}, {}, {}
    for q in range(SCAN_NU):
        r_ref, v_ref, nkk_ref, lw_ref, kd_ref, bd_ref = ins[6 * q:6 * q + 6]
        bwd = q % 2
        lw = lw_ref[...]
        cum = (rc - cc) * (-1 if bwd else 1) >= 0
        lw_hi = lw.astype(BF16)
        lw_mid, lw_lo = _split(lw - lw_hi.astype(F32))
        cum_b = cum.astype(BF16)
        g_in = _dot(cum_b, lw_hi) + _dot(cum_b, lw_mid) + _dot(cum_b, lw_lo)
        gtot = g_in[0:1, :] if bwd else g_in[c - 1:c, :]
        e_inv = jnp.exp(-g_in)
        e_rem = jnp.exp(gtot - g_in)
        e_tot = jnp.exp(gtot)
        a_t = nkk_ref[...].astype(F32) * jnp.exp(g_in - lw)
        r_t = r_ref[...].astype(F32) * jnp.exp(g_in)
        kd, bd = kd_ref[...].astype(F32), bd_ref[...].astype(F32)
        b_t, k_t, b_e, k_e = bd * e_inv, kd * e_inv, bd * e_rem, kd * e_rem
        v = v_ref[...]
        for g in groups:
            sl = slice(g * gw, (g + 1) * gw)
            ar[q, g] = jnp.concatenate([a_t[:, sl], r_t[:, sl]], axis=0).astype(BF16)
            bt4, kt4 = b_t[:, sl].astype(BF16), k_t[:, sl].astype(BF16)
            rb[q, g] = jnp.concatenate([bdiag(bt4), bdiag(kt4)], axis=0)
            e_hi, e_lo = _split(e_tot[:, sl] * dec_keep_f)
            dec = e_hi * dec_hi + e_lo * dec_lo
            rbe[q, g] = jnp.concatenate([bdiag(b_e[:, sl].astype(BF16)), bdiag(k_e[:, sl].astype(BF16)), dec],
                                        axis=0)
            v4[q, g] = v[:, sl]
    h_old = {ch: s_ref[ch[0], ch[1]] for ch in chains}
    gm = {ch: _dot_nt(ar[ch], rb[ch]) for ch in chains}
    a_s = {ch: _dot(ar[ch], bdiag(h_old[ch].astype(BF16))) for ch in chains}
    bke_t = {ch: _dot_nt(eye, rbe[ch]) for ch in chains}
    lab = {ch: gm[ch][:c, :gw] for ch in chains}
    offs = lambda ch: masks[ch[0] % 2][2]
    tinv = {ch: eye_f + jnp.where(offs(ch)[0], lab[ch], 0.0) for ch in chains}
    for lg in range(1, 6):
        tb = {ch: tinv[ch].astype(BF16) for ch in chains}
        lt = {ch: _dot(jnp.where(offs(ch)[lg], lab[ch], 0.0).astype(BF16), bdiag(tb[ch])) for ch in chains}
        tinv = {ch: tinv[ch] + _dot(tb[ch], bdiag(lt[ch].astype(BF16))) for ch in chains}
    v_bd = {ch: bdiag(v4[ch]) for ch in chains}
    w1 = {ch: a_s[ch][:c] + _dot(jnp.where(masks[ch[0] % 2][1], gm[ch][:c, gw:], 0.0).astype(BF16), v_bd[ch])
          for ch in chains}
    u = {ch: _dot(tinv[ch].astype(BF16), bdiag(w1[ch].astype(BF16))) for ch in chains}
    lhs = {ch: jnp.concatenate([jnp.where(masks[ch[0] % 2][0], gm[ch][c:], 0.0), bke_t[ch][:, :2 * gw]],
                               axis=0).astype(BF16) for ch in chains}
    yu = {ch: _dot(lhs[ch], jnp.concatenate([bdiag(u[ch].astype(BF16)), v_bd[ch]], axis=0)) for ch in chains}
    e_col = {ch: _dot(bke_t[ch][:, 2 * gw:].astype(BF16), dec_sum) for ch in chains}
    for q in range(SCAN_NU):
        y_ref[q] = jnp.concatenate([a_s[q, g][c:] + yu[q, g][:c] for g in groups], axis=1)
    for ch in chains:
        s_ref[ch[0], ch[1]] = h_old[ch] * e_col[ch] + yu[ch][c:]


def _rw_scan_tables():
    rb = [[] for _ in range(SCAN_NU)]
    first = []
    for nb, L, blk0 in ((B_P, L_P, 0), (B_S, L_S, T_P // CHUNK)):
        nc = L // CHUNK
        for grp in range(nb // SCAN_NB):
            for j in range(nc):
                for q in range(SCAN_NU):
                    b = grp * SCAN_NB + q // 2
                    cn = j if q % 2 == 0 else nc - 1 - j
                    rb[q].append(blk0 + b * nc + cn)
                first.append(1 if j == 0 else 0)
    as_i32 = lambda a: jnp.asarray(np.asarray(a, np.int32))
    return [as_i32(a) for a in rb], as_i32(first)


def _rw_scan(r, v, nkk, lw, kd, bd, s0_all):
    rb, first = _rw_scan_tables()
    n_steps = first.shape[0]
    n_groups = s0_all.shape[0] // SCAN_NU
    steps_p = (B_P // SCAN_NB) * (L_P // CHUNK)
    in_specs = []
    args = []
    for q in range(SCAN_NU):
        tok = pl.BlockSpec((CHUNK, D), lambda s, *t, q=q: (t[q][s], 0))
        tok2 = pl.BlockSpec((None, CHUNK, D), lambda s, *t, q=q: (q % 2, t[q][s], 0))
        in_specs += [tok, tok, tok, tok2, tok2, tok2]
        args += [r, v, nkk, lw, kd, bd]
    grp = lambda s: jnp.where(s < steps_p, s // (L_P // CHUNK),
                              B_P // SCAN_NB + (s - steps_p) // (L_S // CHUNK))
    n_hg = RW_H // SCAN_PACK
    st = pl.BlockSpec((None, SCAN_NU, n_hg, RW_N, SCAN_GW), lambda s, *t: (grp(s), 0, 0, 0, 0))
    s0g = (s0_all.reshape(n_groups, SCAN_NU, n_hg, SCAN_PACK, RW_N, RW_N).transpose(0, 1, 2, 4, 3, 5)
           .reshape(n_groups, SCAN_NU, n_hg, RW_N, SCAN_GW))
    y, s_fin = pl.pallas_call(
        _rw_scan_kernel,
        grid_spec=pltpu.PrefetchScalarGridSpec(
            num_scalar_prefetch=SCAN_NU + 1,
            grid=(n_steps,),
            in_specs=in_specs + [st],
            out_specs=[pl.BlockSpec((None, SCAN_NU, CHUNK, D), lambda s, *t: (s, 0, 0, 0)), st],
        ),
        out_shape=[jax.ShapeDtypeStruct((n_steps, SCAN_NU, CHUNK, D), F32),
                   jax.ShapeDtypeStruct(s0g.shape, F32)],
        compiler_params=_cparams(("arbitrary",)),
        name="rwkv_scan",
    )(*rb, first, *args, s0g)
    s_fin = (s_fin.reshape(n_groups, SCAN_NU, n_hg, RW_N, SCAN_PACK, RW_N).transpose(0, 1, 2, 4, 3, 5)
             .reshape(s0_all.shape))
    return y, s_fin


def _scan_out_index(i, k, bwd):
    per_tile = TM // CHUNK
    nc_p, nc_s = L_P // CHUNK, L_S // CHUNK
    steps_p = (B_P // SCAN_NB) * nc_p
    cn_p = k
    b_s = (i - NT_P) // NT_S_SEQ
    cn_s = ((i - NT_P) % NT_S_SEQ) * per_tile + k
    step_p = (i // SCAN_NB) * nc_p + (nc_p - 1 - cn_p if bwd else cn_p)
    step_s = steps_p + (b_s // SCAN_NB) * nc_s + (nc_s - 1 - cn_s if bwd else cn_s)
    q_p = (i % SCAN_NB) * 2 + bwd
    q_s = (b_s % SCAN_NB) * 2 + bwd
    is_p = i < NT_P
    return jnp.where(is_p, step_p, step_s), jnp.where(is_p, q_p, q_s)


def _rw_out_kernel(*refs):
    per_tile = TM // CHUNK
    y_refs = refs[:2 * per_tile]
    bonus_ref, sz_ref, gg_ref, gb_ref, ones_ref, w_ref, x_ref, m_ref, lng_ref, lnb_ref, o_ref = refs[2 * per_tile:]
    ones_bd = ones_ref[...]
    y = jnp.concatenate([y_refs[k][...] + y_refs[per_tile + k][...] for k in range(per_tile)], axis=0)
    mean = _head_sum(y, ones_bd) * (1.0 / RW_N)
    yc = y - mean
    var = _head_sum(yc * yc, ones_bd) * (1.0 / RW_N)
    yn = yc * lax.rsqrt(var + RW_GN_EPS) * gg_ref[...] + gb_ref[...]
    g = (yn + bonus_ref[...]) * sz_ref[...]
    _out_ln_tail(g, w_ref, x_ref, m_ref, lng_ref, lnb_ref, o_ref)


def _rw_out_proj_ln(y, bonus, sz, gn_g, gn_b, ones_bd, w_out, x, mod, layer, ln_g, ln_b):
    row = pl.BlockSpec((TM, D), lambda i: (i, 0))
    vec = pl.BlockSpec((1, D), lambda i: (0, 0))
    mat = pl.BlockSpec((D, D), lambda i: (0, 0))
    y_specs = [pl.BlockSpec((None, None, CHUNK, D), lambda i, k=k, bwd=bwd: _scan_out_index(i, k, bwd) + (0, 0))
               for bwd in (0, 1) for k in range(TM // CHUNK)]
    return pl.pallas_call(
        _rw_out_kernel,
        grid=(NT,),
        in_specs=y_specs + [row, row, vec, vec,
                  pl.BlockSpec((LANES, LANES), lambda i: (0, 0)), mat, row,
                  _mod_spec(layer), vec, vec],
        out_specs=row,
        out_shape=jax.ShapeDtypeStruct((T, D), F32),
        compiler_params=_cparams(("arbitrary",)),
        name="rwkv_out_proj_ln",
    )(*[y] * len(y_specs), bonus, sz, gn_g.reshape(1, D), gn_b.reshape(1, D), ones_bd, w_out.astype(BF16), x, mod,
      ln_g.reshape(1, D), ln_b.reshape(1, D))


def _rwkv_layer(x, mod, layer, ln_g, ln_b, state, mu, w_in, w0, w1, w2, a0, a1, a2, k_k, k_a, r_k, gn_g, gn_b, w_out):
    ones_bd = _head_ones()
    r, v, sz, nkk, bonus, lw, kd, bd = _rw_in_proj(x, mod, layer, ones_bd, mu, w_in, w0, w1, w2, a0, a1, a2,
                                                  k_k, k_a, r_k)
    n_p = B_P * 2
    s0_all = jnp.concatenate([jnp.zeros((n_p, RW_H, RW_N, RW_N), F32),
                              state.astype(F32).reshape(B_S * 2, RW_H, RW_N, RW_N).swapaxes(-1, -2)], 0)
    y, h_fin = _rw_scan(r, v, nkk, lw, kd, bd, s0_all)
    x_new = _rw_out_proj_ln(y, bonus, sz, gn_g, gn_b, ones_bd, w_out, x, mod, layer, ln_g, ln_b)
    new_state = h_fin[:n_p].swapaxes(-1, -2).reshape(B_P, 1, 2, RW_H, RW_N, RW_N)
    return x_new, new_state


def kernel(x_prompt, x_sample, cache_mla_ckv, cache_mla_kpe, state_rwkv, c, c_ctx, mod_w, mod_b, ln_g, ln_b, hy_w_in, hy_conv_w, hy_conv_b, hy_ffn_w1, hy_ffn_b1, hy_ffn_w2, hy_ffn_b2, hy_ffn_w3, hy_ffn_b3, hy_freq, hy_decay, hy_skip, hy_w_out, mla_w_in, mla_q_norm, mla_kv_norm, mla_w_q_up, mla_w_kv_up, mla_w_out, rw_mu, rw_w_in, rw_w0, rw_w1, rw_w2, rw_a0, rw_a1, rw_a2, rw_k_k, rw_k_a, rw_r_k, rw_gn_g, rw_gn_b, rw_w_out):
    x = jnp.concatenate([x_prompt.reshape(T_P, D), x_sample.reshape(T_S, D)], 0)
    cond8 = jnp.concatenate([c_ctx[None, :], c, jnp.zeros((8 - 1 - B_S, D), F32)], 0)
    mod = _modulation_table(cond8, mod_w, mod_b).reshape(DEPTH * 8, 1, 3 * D)
    tabs_p = _dft_tables(L_P)
    tabs_s = _dft_tables(L_S)
    rope = _rope_tables()
    new_ckv = new_kpe = new_state = None
    for i in range(DEPTH):
        kind, j = i % 3, i // 3
        if kind == 0:
            x = _hyena_layer(x, mod, i, tabs_p, tabs_s, ln_g[i], ln_b[i], hy_w_in[j], hy_conv_w[j], hy_conv_b[j],
                             hy_ffn_w1[j], hy_ffn_b1[j], hy_ffn_w2[j], hy_ffn_b2[j], hy_ffn_w3[j], hy_ffn_b3[j],
                             hy_freq[j], hy_decay[j], hy_skip[j], hy_w_out[j])
        elif kind == 1:
            x, new_ckv, new_kpe = _mla_layer(x, mod, i, rope, ln_g[i], ln_b[i], cache_mla_ckv[:, j],
                                             cache_mla_kpe[:, j], mla_w_in[j], mla_q_norm[j], mla_kv_norm[j],
                                             mla_w_q_up[j], mla_w_kv_up[j], mla_w_out[j])
        else:
            x, new_state = _rwkv_layer(x, mod, i, ln_g[i], ln_b[i], state_rwkv[:, j], rw_mu[j], rw_w_in[j],
                                       rw_w0[j], rw_w1[j], rw_w2[j], rw_a0[j], rw_a1[j], rw_a2[j], rw_k_k[j],
                                       rw_k_a[j], rw_r_k[j], rw_gn_g[j], rw_gn_b[j], rw_w_out[j])
    return (x[:T_P].reshape(B_P, L_P, D), x[T_P:].reshape(B_S, L_S, D), new_ckv, new_kpe, new_state)
```

```python
import functools
import math

import numpy as np
import jax
import jax.numpy as jnp
from jax import lax
from jax.experimental import pallas as pl
from jax.experimental.pallas import tpu as pltpu

F32 = jnp.float32
BF16 = jnp.bfloat16
HIGHEST = lax.Precision.HIGHEST

D = 1024
B_P, L_P = 16, 256
B_S, L_S = 2, 2048
T_P = B_P * L_P
T_S = B_S * L_S
T = T_P + T_S
PAST = 512
DEPTH = 4
DEEPNORM_ALPHA = (2.0 * DEPTH) ** 0.25
LN_EPS = 1e-5
RMS_EPS = 1e-6
HY_BANDS = 16
HY_FFN = 64
MLA_HEADS = 16
MLA_Q_RANK = 256
MLA_KV_RANK = 128
MLA_NOPE = 64
MLA_ROPE = 32
MLA_V = 64
ROPE_BASE = 10000.0
GRID_W = 64
RW_N = 64
RW_H = D // RW_N
RW_LORA = 64
RW_GN_EPS = 64e-5

TM = 256
NT_P = T_P // TM
NT_S_SEQ = L_S // TM
NT = T // TM
HALO = 8
LANES = 128
ROWS_BF16 = 16
CHUNK = 64
DFT_TK = 512
VMEM_LIMIT = 52 * 1024 * 1024


def _cparams(sem):
    return pltpu.CompilerParams(dimension_semantics=sem, vmem_limit_bytes=VMEM_LIMIT)


def _group(i, tm=TM):
    return jnp.where(i < T_P // tm, 0, 1 + (i - T_P // tm) // (L_S // tm))


def _sigmoid(x):
    return 1.0 / (1.0 + jnp.exp(-x))


def _silu(x):
    return x * _sigmoid(x)


def _dot(a, b):
    return jnp.dot(a, b, preferred_element_type=F32)


def _dot_nt(a, b):
    return lax.dot_general(a, b, (((1,), (1,)), ((), ())), preferred_element_type=F32)


def _dot_hi(a, b):
    return jnp.dot(a, b, preferred_element_type=F32, precision=HIGHEST)


def _split(x):
    hi = x.astype(BF16)
    lo = (x - hi.astype(F32)).astype(BF16)
    return hi, lo


def _head_sum(x, ones_bd):
    hi, lo = _split(x)
    lanes = ones_bd.shape[0]
    parts = []
    for g in range(x.shape[1] // lanes):
        sl = slice(g * lanes, (g + 1) * lanes)
        parts.append(_dot(hi[:, sl], ones_bd) + _dot(lo[:, sl], ones_bd))
    return jnp.concatenate(parts, axis=1)


def _modulate(x, m):
    return x * (1.0 + m[:, D:2 * D]) + m[:, :D]


def _layer_norm_rows(y, g, b):
    mu = jnp.mean(y, axis=-1, keepdims=True)
    yc = y - mu
    var = jnp.mean(yc * yc, axis=-1, keepdims=True)
    return yc * lax.rsqrt(var + LN_EPS) * g + b


def _neighbour_rows(cur, prev_halo, next_halo, has_prev, has_next):
    rows = cur.shape[0]
    ridx = lax.broadcasted_iota(jnp.int32, cur.shape, 0)
    pr = jnp.where(has_prev, prev_halo[HALO - 1:HALO, :], 0.0)
    nx = jnp.where(has_next, next_halo[0:1, :], 0.0)
    prev = jnp.where(ridx == 0, pr, pltpu.roll(cur, 1, axis=0))
    nxt = jnp.where(ridx == rows - 1, nx, pltpu.roll(cur, rows - 1, axis=0))
    return prev, nxt


def _tile_has_neighbours(i):
    k = (i - NT_P) % NT_S_SEQ
    is_s = i >= NT_P
    return jnp.logical_and(is_s, k != 0), jnp.logical_and(is_s, k != NT_S_SEQ - 1)


def _halo_specs(width, col_of):
    r = TM // HALO
    prev = pl.BlockSpec((HALO, width), lambda i, *a: (jnp.maximum(i * r - 1, 0), col_of(i, *a)))
    nxt = pl.BlockSpec((HALO, width), lambda i, *a: (jnp.minimum((i + 1) * r, T // HALO - 1), col_of(i, *a)))
    return prev, nxt


def _mod_kernel(c_ref, w_ref, b_ref, o_ref):
    o_ref[...] = _dot_hi(_silu(c_ref[...]), w_ref[...]) + b_ref[...]


def _modulation_table(cond8, mod_w, mod_b):
    tn = 1024
    return pl.pallas_call(
        _mod_kernel,
        grid=(DEPTH, 3 * D // tn),
        in_specs=[pl.BlockSpec((8, D), lambda l, j: (0, 0)),
                  pl.BlockSpec((None, D, tn), lambda l, j: (l, 0, j)),
                  pl.BlockSpec((None, 1, tn), lambda l, j: (l, 0, j))],
        out_specs=pl.BlockSpec((None, 8, tn), lambda l, j: (l, 0, j)),
        out_shape=jax.ShapeDtypeStruct((DEPTH, 8, 3 * D), F32),
        compiler_params=_cparams(("arbitrary", "arbitrary")),
        name="modulation",
    )(cond8, mod_w, mod_b.reshape(DEPTH, 1, 3 * D))


def _mod_spec(layer, tm=TM):
    return pl.BlockSpec((None, 1, 3 * D), lambda i, *a: (layer * 8 + _group(i, tm), 0, 0))


def _out_ln_tail(g, w_ref, x_ref, m_ref, lng_ref, lnb_ref, o_ref):
    mix = _dot(g.astype(BF16), w_ref[...])
    gate = m_ref[...][:, 2 * D:]
    y = DEEPNORM_ALPHA * x_ref[...] + gate * mix
    o_ref[...] = _layer_norm_rows(y, lng_ref[...], lnb_ref[...])


TM_OUT = 512


def _out_ln2_kernel(gp_ref, gs_ref, w_ref, x_ref, m_ref, lng_ref, lnb_ref, o_ref):
    g = jnp.where(pl.program_id(0) < T_P // TM_OUT, gp_ref[...], gs_ref[...])
    _out_ln_tail(g, w_ref, x_ref, m_ref, lng_ref, lnb_ref, o_ref)


def _out_ln1_kernel(g_ref, w_ref, x_ref, m_ref, lng_ref, lnb_ref, o_ref):
    _out_ln_tail(g_ref[...], w_ref, x_ref, m_ref, lng_ref, lnb_ref, o_ref)


def _out_proj_ln1(g, w_out, x, mod, layer, ln_g, ln_b, tile0):
    vec = pl.BlockSpec((1, D), lambda i: (0, 0))
    return pl.pallas_call(
        _out_ln1_kernel,
        grid=(g.shape[0] // TM_OUT,),
        in_specs=[pl.BlockSpec((TM_OUT, D), lambda i: (i, 0)),
                  pl.BlockSpec((D, D), lambda i: (0, 0)),
                  pl.BlockSpec((TM_OUT, D), lambda i: (i + tile0, 0)),
                  pl.BlockSpec((None, 1, 3 * D), lambda i: (layer * 8 + _group(i + tile0, TM_OUT), 0, 0)),
                  vec, vec],
        out_specs=pl.BlockSpec((TM_OUT, D), lambda i: (i, 0)),
        out_shape=jax.ShapeDtypeStruct(g.shape, F32),
        compiler_params=_cparams(("arbitrary",)),
        name="out_proj_ln_part",
    )(g, w_out.astype(BF16), x, mod, ln_g.reshape(1, D), ln_b.reshape(1, D))


def _out_proj_ln2(g_p, g_s, w_out, x, mod, layer, ln_g, ln_b):
    nt_p = T_P // TM_OUT
    row = pl.BlockSpec((TM_OUT, D), lambda i: (i, 0))
    vec = pl.BlockSpec((1, D), lambda i: (0, 0))
    return pl.pallas_call(
        _out_ln2_kernel,
        grid=(T // TM_OUT,),
        in_specs=[pl.BlockSpec((TM_OUT, D), lambda i: (jnp.minimum(i, nt_p - 1), 0)),
                  pl.BlockSpec((TM_OUT, D), lambda i: (jnp.maximum(i - nt_p, 0), 0)),
                  pl.BlockSpec((D, D), lambda i: (0, 0)),
                  row, _mod_spec(layer, TM_OUT), vec, vec],
        out_specs=row,
        out_shape=jax.ShapeDtypeStruct((T, D), F32),
        compiler_params=_cparams(("arbitrary",)),
        name="out_proj_ln",
    )(g_p, g_s, w_out.astype(BF16), x, mod, ln_g.reshape(1, D), ln_b.reshape(1, D))


def _hy_in_conv_kernel(x_ref, xp_ref, xn_ref, m_ref, w_ref, cw_ref, cb_ref, vv_ref, gate_ref):
    has_prev, has_next = _tile_has_neighbours(pl.program_id(0))
    x_all = jnp.concatenate([xp_ref[...], x_ref[...], xn_ref[...]], axis=0)
    u = _dot(_modulate(x_all, m_ref[...]).astype(BF16), w_ref[...])
    rows = TM + 2 * HALO
    ridx = lax.broadcasted_iota(jnp.int32, (TM, 1), 0)
    no_prev = jnp.logical_and(ridx == 0, jnp.logical_not(has_prev))
    no_next = jnp.logical_and(ridx == TM - 1, jnp.logical_not(has_next))
    cw = cw_ref[...]
    cb = cb_ref[...]

    def conv(grp):
        sl = slice(grp * D, (grp + 1) * D)
        ug = u[:, sl]
        cur = ug[HALO:HALO + TM]
        prev = jnp.where(no_prev, 0.0, pltpu.roll(ug, 1, axis=0)[HALO:HALO + TM])
        nxt = jnp.where(no_next, 0.0, pltpu.roll(ug, rows - 1, axis=0)[HALO:HALO + TM])
        return prev * cw[0:1, sl] + cur * cw[1:2, sl] + nxt * cw[2:3, sl] + cb[:, sl]

    x0, x1, v = conv(0), conv(1), conv(2)
    vv_ref[...] = v * x1
    gate_ref[...] = x0 * _silu(u[HALO:HALO + TM, 3 * D:])


def _hy_in_conv(x, mod, layer, w_in, conv_w, conv_b):
    row = pl.BlockSpec((TM, D), lambda i: (i, 0))
    prev, nxt = _halo_specs(D, lambda i: 0)
    return pl.pallas_call(
        _hy_in_conv_kernel,
        grid=(NT,),
        in_specs=[row, prev, nxt, _mod_spec(layer), _resident(w_in.shape),
                  pl.BlockSpec((3, 3 * D), lambda i: (0, 0)), pl.BlockSpec((1, 3 * D), lambda i: (0, 0))],
        out_specs=[row, row],
        out_shape=[jax.ShapeDtypeStruct((T, D), F32)] * 2,
        compiler_params=_cparams(("arbitrary",)),
        name="hyena_in_proj_conv3",
    )(x, x, x, mod, w_in.astype(BF16), conv_w, conv_b.reshape(1, 3 * D))


def _hy_filter_kernel(t_ref, bands_ref, wt_ref, wc_ref, ws_ref, b1_ref, w2_ref, b2_ref, w3_ref, b3_ref,
                      f0_ref, f1_ref, dec_ref, hs_ref, hd_ref, nyq_ref, *, L, tr):
    i = pl.program_id(0)
    ridx = lax.broadcasted_iota(jnp.int32, (tr, 1), 0) + i * tr
    pos = ridx.astype(F32)
    t = t_ref[...]
    ang = ((2.0 * math.pi / L) * pos) * bands_ref[...]
    pre = t * wt_ref[...] + _dot_hi(jnp.cos(ang), wc_ref[...]) + _dot_hi(jnp.sin(ang), ws_ref[...])
    hdn = jnp.sin(f0_ref[...] * (pre + b1_ref[...]))
    hdn = jnp.sin(f1_ref[...] * (_dot_hi(hdn, w2_ref[...]) + b2_ref[...]))
    hf = _dot_hi(hdn, w3_ref[...]) + b3_ref[...]
    h = hf * jnp.exp(-t * jnp.abs(dec_ref[...]))
    h0 = h[:, :D]
    h1 = jnp.where(ridx == 0, 0.0, h[:, D:])
    hsum = h0 + h1
    hs_ref[...] = hsum
    hd_ref[...] = h1 - h0
    alt = jnp.where((ridx & 1) == 0, 1.0, -1.0)
    part =jnp.broadcast_to(jnp.sum(alt * hsum, axis=0, keepdims=True), (8, D))

    @pl.when(i == 0)
    def _():
        nyq_ref[...] = part

    @pl.when(i > 0)
    def _():
        nyq_ref[...] += part


def _hy_filter(L, w1, b1, w2, b2, w3, b3, freq, decay):
    tr = 256
    t = jnp.linspace(0.0, 1.0, L, dtype=F32).reshape(L, 1)
    bands = jnp.linspace(1e-4, HY_BANDS - 1, HY_BANDS, dtype=F32)
    bands = jnp.pad(bands, (0, 128 - HY_BANDS)).reshape(1, 128)
    wt = w1[0:1]
    wc = jnp.pad(w1[1:1 + HY_BANDS], ((0, 128 - HY_BANDS), (0, 0)))
    ws = jnp.pad(-w1[1 + HY_BANDS:], ((0, 128 - HY_BANDS), (0, 0)))
    full = lambda shape: pl.BlockSpec(shape, lambda i: (0, 0))
    rows = pl.BlockSpec((tr, D), lambda i: (i, 0))
    return pl.pallas_call(
        functools.partial(_hy_filter_kernel, L=L, tr=tr),
        grid=(L // tr,),
        in_specs=[pl.BlockSpec((tr, 1), lambda i: (i, 0)), full((1, 128)), full((1, HY_FFN)),
                  full((128, HY_FFN)), full((128, HY_FFN)), full((1, HY_FFN)),
                  full((HY_FFN, HY_FFN)), full((1, HY_FFN)), full((HY_FFN, 2 * D)), full((1, 2 * D)),
                  full((1, HY_FFN)), full((1, HY_FFN)), full((1, 2 * D))],
        out_specs=[rows, rows, pl.BlockSpec((8, D), lambda i: (0, 0))],
        out_shape=[jax.ShapeDtypeStruct((L, D), F32), jax.ShapeDtypeStruct((L, D), F32),
                   jax.ShapeDtypeStruct((8, D), F32)],
        compiler_params=_cparams(("arbitrary",)),
        name="hyena_filter",
    )(t, bands, wt, wc, ws, b1.reshape(1, -1), w2, b2.reshape(1, -1), w3, b3.reshape(1, -1),
      freq[0:1], freq[1:2], decay.reshape(1, 2 * D))


def _dft_tables(L):
    n = 2 * L
    w = LANES
    k = jnp.arange(L, dtype=jnp.int32)

    def cs(t):
        ang = ((k[:, None] * t[None, :]) % n).astype(F32) * (2.0 * math.pi / n)
        return jnp.cos(ang), jnp.sin(ang)

    tk = min(L, DFT_TK)
    nk = L // tk
    ca, sa = cs(jnp.arange(L // w, dtype=jnp.int32) * w)
    cb, sb = cs(jnp.arange(w, dtype=jnp.int32))
    alt = jnp.where(k % 2 == 0, 1.0, -1.0).astype(F32)

    def row_tiles(p, q, r, s, sgn):
        tab = p[:, :, None] * q[:, None, :] + sgn * r[:, :, None] * s[:, None, :]
        return tab.reshape(nk, tk, L)

    def col_tiles(p, q, r, s, sgn):
        tr = lambda a: a.reshape(nk, tk, -1).transpose(0, 2, 1)
        tab = tr(p)[:, :, None, :] * tr(q)[:, None, :, :] + sgn * tr(r)[:, :, None, :] * tr(s)[:, None, :, :]
        return tab.reshape(nk, L, tk)

    bins = k.reshape(nk, tk)
    c_rows = row_tiles(ca, cb, sa, sb, -1.0)
    c_cols = col_tiles(ca, cb, sa, sb, -1.0)
    s_rows = jnp.where(bins[:, :, None] == 0, alt[None, None, :], row_tiles(sa, cb, ca, sb, 1.0))
    s_cols = jnp.where(bins[:, None, :] == 0, alt[None, :, None], col_tiles(sa, cb, ca, sb, 1.0))
    return tuple(a.astype(BF16) for a in (c_rows, s_rows, c_cols, s_cols))


def _dft_fwd_kernel(c_ref, s_ref, *refs):
    x1_ref, x2_ref = refs[0], refs[-3]
    oc_ref, os_ref = refs[-2:]
    k = pl.program_id(2)
    oc_ref[...] = _dot(c_ref[k], x1_ref[...].astype(BF16))
    os_ref[...] = _dot(s_ref[k], x2_ref[...].astype(BF16))


def _resident(shape):
    return pl.BlockSpec(shape, lambda *_: (0,) * len(shape), pipeline_mode=pl.Buffered(1))


def _dft_fwd(tabs, xs, L, nb, row_blk0):
    c, s_rows = tabs[:2]
    nk, tk, _ = c.shape
    tn = 512
    a_spec = _resident(c.shape)
    x_spec = pl.BlockSpec((L, tn), lambda b, j, k: (row_blk0 + b, j))
    o_spec = pl.BlockSpec((tk, tn), lambda b, j, k: (b * nk + k, j))
    return pl.pallas_call(
        _dft_fwd_kernel,
        grid=(nb, D // tn, nk),
        in_specs=[a_spec] * 2 + [x_spec] * len(xs),
        out_specs=[o_spec, o_spec],
        out_shape=[jax.ShapeDtypeStruct((nb * L, D), F32)] * 2,
        compiler_params=_cparams(("arbitrary", "arbitrary", "arbitrary")),
        name="hyena_dft_fwd",
    )(c, s_rows, *xs)


def _spectrum_product(vc, vs, kre, kim, nyq, first_tile, L):
    bin0 = jnp.logical_and(lax.broadcasted_iota(jnp.int32, vc.shape, 0) == 0, first_tile)
    kim = jnp.where(bin0, nyq, kim)
    inv_n = 1.0 / (2 * L)
    yre = jnp.where(bin0, vc * kre * inv_n, (vc * kre + vs * kim) * (2.0 * inv_n))
    yim = jnp.where(bin0, vs * kim * inv_n, (vs * kre - vc * kim) * (2.0 * inv_n))
    return yre.astype(BF16), yim.astype(BF16)


def _dft_conv_kernel(c_ref, s_ref, st_ref, v_ref, kre_ref, kim_ref, nyq_ref, skip_ref, gate_ref, o_ref, *, L):
    v = v_ref[...]
    vb = v.astype(BF16)
    c = c_ref[0]
    yre, yim = _spectrum_product(_dot(c, vb), _dot(s_ref[0], vb), kre_ref[...], kim_ref[...], nyq_ref[0:1, :],
                                 True, L)
    y = _dot(c, yre) + _dot(st_ref[0], yim)
    o_ref[...] = (y + v * skip_ref[...]) * gate_ref[...]


def _dft_conv(tabs, kre, kim, nyq, vv, skip, gate, L, nb, row_blk0):
    c, s_rows, _, s_cols = tabs
    assert c.shape[0] == 1, "single frequency tile only"
    tn = 512
    seq = lambda b, j: (row_blk0 + b, j)
    spec = pl.BlockSpec((L, tn), lambda b, j: (0, j))
    return pl.pallas_call(
        functools.partial(_dft_conv_kernel, L=L),
        grid=(nb, D // tn),
        in_specs=[_resident(c.shape)] * 3 + [pl.BlockSpec((L, tn), seq), spec, spec,
                                             pl.BlockSpec((8, tn), lambda b, j: (0, j)),
                                             pl.BlockSpec((1, tn), lambda b, j: (0, j)), pl.BlockSpec((L, tn), seq)],
        out_specs=pl.BlockSpec((L, tn), lambda b, j: (b, j)),
        out_shape=jax.ShapeDtypeStruct((nb * L, D), F32),
        compiler_params=_cparams(("arbitrary", "arbitrary")),
        name="hyena_dft_conv_gate",
    )(c, s_rows, s_cols, vv, kre, kim, nyq, skip.reshape(1, D), gate)


def _dft_inv_kernel(c_ref, st_ref, vc_ref, vs_ref, kre_ref, kim_ref, nyq_ref,
                    vv_ref, skip_ref, gate_ref, o_ref, *, L):
    k = pl.program_id(2)
    nk = pl.num_programs(2)
    yre, yim = _spectrum_product(vc_ref[...], vs_ref[...], kre_ref[...], kim_ref[...], nyq_ref[0:1, :], k == 0, L)
    contrib = _dot(c_ref[k], yre) + _dot(st_ref[k], yim)

    @pl.when(k == 0)
    def _():
        o_ref[...] = contrib

    @pl.when(k > 0)
    def _():
        o_ref[...] += contrib

    @pl.when(k == nk - 1)
    def _():
        o_ref[...] = (o_ref[...] + vv_ref[...] * skip_ref[...]) * gate_ref[...]


def _dft_inv(tabs, vc, vs, kre, kim, nyq, vv, skip, gate, L, nb, row_blk0):
    c, s_cols = tabs[2:]
    nk, _, tk = c.shape
    tn = 512 if L <= 512 else 256
    a_spec = _resident(c.shape)
    v_spec = pl.BlockSpec((tk, tn), lambda b, j, k: (b * nk + k, j))
    k_spec = pl.BlockSpec((tk, tn), lambda b, j, k: (k, j))
    row_spec = pl.BlockSpec((L, tn), lambda b, j, k: (row_blk0 + b, j))
    return pl.pallas_call(
        functools.partial(_dft_inv_kernel, L=L),
        grid=(nb, D // tn, nk),
        in_specs=[a_spec] * 2 + [v_spec, v_spec, k_spec, k_spec,
                                 pl.BlockSpec((8, tn), lambda b, j, k: (0, j)),
                                 row_spec, pl.BlockSpec((1, tn), lambda b, j, k: (0, j)), row_spec],
        out_specs=pl.BlockSpec((L, tn), lambda b, j, k: (b, j)),
        out_shape=jax.ShapeDtypeStruct((nb * L, D), F32),
        compiler_params=_cparams(("arbitrary", "arbitrary", "arbitrary")),
        name="hyena_dft_inv_gate",
    )(c, s_cols, vc, vs, kre, kim, nyq, vv, skip.reshape(1, D), gate)


def _hyena_layer(x, mod, layer, tabs_p, tabs_s, ln_g, ln_b, w_in, conv_w, conv_b, w1, b1, w2, b2, w3, b3,
                 freq, decay, skip, w_out, split_out=False):
    vv, gate = _hy_in_conv(x, mod, layer, w_in, conv_w, conv_b)
    gs = []
    for L, nb, blk0, tabs in ((L_P, B_P, 0, tabs_p), (L_S, B_S, T_P // L_S, tabs_s)):
        hsum, hdiff, nyq = _hy_filter(L, w1, b1, w2, b2, w3, b3, freq, decay)
        kre, kim = _dft_fwd(tabs, (hsum, hdiff), L, 1, 0)
        if L <= DFT_TK:
            gs.append(_dft_conv(tabs, kre, kim, nyq, vv, skip, gate, L, nb, blk0))
        else:
            vc, vs = _dft_fwd(tabs, (vv,), L, nb, blk0)
            gs.append(_dft_inv(tabs, vc, vs, kre, kim, nyq, vv, skip, gate, L, nb, blk0))
    if split_out:
        return tuple(_out_proj_ln1(g, w_out, x, mod, layer, ln_g, ln_b, tile0)
                     for g, tile0 in ((gs[0], 0), (gs[1], T_P // TM_OUT)))
    return _out_proj_ln2(gs[0], gs[1], w_out, x, mod, layer, ln_g, ln_b)


def _rope_tables():
    rows = L_S // GRID_W
    half = MLA_ROPE // 2
    inv = ROPE_BASE ** (-jnp.arange(0, half, 2, dtype=F32) / half)
    r = jnp.repeat(jnp.arange(rows, dtype=F32), GRID_W)
    col = jnp.tile(jnp.arange(GRID_W, dtype=F32), rows)
    ar, ac = r[:, None] * inv, col[:, None] * inv
    ang = jnp.concatenate([ar, ar, ac, ac], -1)
    cos, sin = jnp.cos(ang), jnp.sin(ang)
    cos = jnp.concatenate([jnp.ones((TM, MLA_ROPE), F32), cos], 0)
    sin = jnp.concatenate([jnp.zeros((TM, MLA_ROPE), F32), sin], 0)
    return cos, sin, jnp.tile(cos, (1, MLA_HEADS)), jnp.tile(sin, (1, MLA_HEADS))


def _rope_rot_cols(w):
    idx = np.concatenate([np.arange(8, 16), np.arange(0, 8), np.arange(24, 32), np.arange(16, 24)])
    sign = np.concatenate([-np.ones(8), np.ones(8), -np.ones(8), np.ones(8)]).astype(np.float32)
    return w[..., idx] * sign


def _mla_in_kernel(x_ref, m_ref, wq_ref, wkv_ref, wkp_ref, wz_ref, qn_ref, kvn_ref, wqn_ref, wqp_ref, wqr_ref,
                   c32_ref, s32_ref, c512_ref, s512_ref,
                   qno_ref, qpe_ref, ckv_ref, kpe_ref, kpr_ref, sz_ref):
    h = _modulate(x_ref[...], m_ref[...]).astype(BF16)
    q_c = _dot(h, wq_ref[...])
    kv_c = _dot(h, wkv_ref[...])
    kp2 = _dot(h, wkp_ref[...])
    z = _dot(h, wz_ref[...])

    def rms(v, g):
        return v * lax.rsqrt(jnp.mean(v * v, axis=-1, keepdims=True) + RMS_EPS) * g

    qn = rms(q_c, qn_ref[...]).astype(BF16)
    scale = (MLA_NOPE + MLA_ROPE) ** -0.5
    qno_ref[...] = (_dot(qn, wqn_ref[...]) * scale).astype(BF16)
    q_pe = _dot(qn, wqp_ref[...]) * c512_ref[...] + _dot(qn, wqr_ref[...]) * s512_ref[...]
    qpe_ref[...] = (q_pe * scale).astype(BF16)
    ckv_ref[...] = rms(kv_c, kvn_ref[...])
    kpe = kp2[:, :MLA_ROPE]
    kpe_ref[...] = kpe
    kpr_ref[...] = kpe * c32_ref[...] + kp2[:, MLA_ROPE:] * s32_ref[...]
    sz_ref[...] = _silu(z)


def _mla_in_proj(x, mod, layer, w_in, q_norm, kv_norm, w_q_up, rope):
    c32, s32, c512, s512 = rope
    o1, o2, o3 = MLA_Q_RANK, MLA_Q_RANK + MLA_KV_RANK, MLA_Q_RANK + MLA_KV_RANK + MLA_ROPE
    wq, wkv, wkp, wz = w_in[:, :o1], w_in[:, o1:o2], w_in[:, o2:o3], w_in[:, o3:]
    wkp2 = jnp.concatenate([wkp, _rope_rot_cols(wkp)], -1)
    wqu = w_q_up.reshape(MLA_Q_RANK, MLA_HEADS, MLA_NOPE + MLA_ROPE)
    wqn = wqu[:, :, :MLA_NOPE].reshape(MLA_Q_RANK, MLA_HEADS * MLA_NOPE)
    wqp = wqu[:, :, MLA_NOPE:]
    wqr = _rope_rot_cols(wqp).reshape(MLA_Q_RANK, MLA_HEADS * MLA_ROPE)
    wqp = wqp.reshape(MLA_Q_RANK, MLA_HEADS * MLA_ROPE)
    full = lambda a: pl.BlockSpec(a.shape, lambda i: (0,) * a.ndim)
    rope_idx = lambda i: jnp.where(i < NT_P, 0, 1 + (i - NT_P) % NT_S_SEQ)
    rows = lambda n: pl.BlockSpec((TM, n), lambda i: (i, 0))
    tab = lambda n: pl.BlockSpec((TM, n), lambda i: (rope_idx(i), 0))
    weights = [wq.astype(BF16), wkv.astype(BF16), wkp2.astype(BF16), wz.astype(BF16),
               q_norm.reshape(1, -1), kv_norm.reshape(1, -1),
               wqn.astype(BF16), wqp.astype(BF16), wqr.astype(BF16)]
    npe = MLA_HEADS * MLA_ROPE
    return pl.pallas_call(
        _mla_in_kernel,
        grid=(NT,),
        in_specs=[rows(D), _mod_spec(layer)] + [full(a) for a in weights]
                 + [tab(MLA_ROPE), tab(MLA_ROPE), tab(npe), tab(npe)],
        out_specs=[rows(D), rows(npe), rows(MLA_KV_RANK), rows(MLA_ROPE), rows(MLA_ROPE), rows(D)],
        out_shape=[jax.ShapeDtypeStruct((T, D), BF16), jax.ShapeDtypeStruct((T, npe), BF16),
                   jax.ShapeDtypeStruct((T, MLA_KV_RANK), F32), jax.ShapeDtypeStruct((T, MLA_ROPE), F32),
                   jax.ShapeDtypeStruct((T, MLA_ROPE), F32), jax.ShapeDtypeStruct((T, D), F32)],
        compiler_params=_cparams(("arbitrary",)),
        name="mla_in_proj",
    )(x, mod, *weights, c32, s32, c512, s512)


def _mla_attn_kernel(*refs, n_cache, hg):
    if n_cache:
        (qn_ref, qp_ref, ckv_ref, kpr_ref, cckv_ref, ckpe_ref, wk_ref, wvt_ref, sz_ref,
         o_ref, kcat_s, vt_s) = refs
    else:
        qn_ref, qp_ref, ckv_ref, kpr_ref, wk_ref, wvt_ref, sz_ref, o_ref, kcat_s, vt_s = refs

    @pl.when(pl.program_id(2) == 0)
    def _():
        cc = ckv_ref[...].astype(BF16)
        kp = kpr_ref[...].astype(BF16)
        if n_cache:
            cc = jnp.concatenate([cckv_ref[...].astype(BF16), cc], axis=0)
            kp = jnp.concatenate([ckpe_ref[...].astype(BF16), kp], axis=0)
        kn = _dot(cc, wk_ref[...]).astype(BF16)
        for hh in range(hg):
            kcat_s[hh] = jnp.concatenate([kn[:, hh * MLA_NOPE:(hh + 1) * MLA_NOPE], kp], axis=1)
            vt_s[hh] = _dot_nt(wvt_ref[hh], cc).astype(BF16)

    qn_all, qp_all = qn_ref[...], qp_ref[...]

    def scores(hh):
        qcat = jnp.concatenate([qn_all[:, hh * MLA_NOPE:(hh + 1) * MLA_NOPE],
                                qp_all[:, hh * MLA_ROPE:(hh + 1) * MLA_ROPE]], axis=1)
        return _dot_nt(kcat_s[hh], qcat)

    outs = []
    ahead = 2
    pending = [scores(hh) for hh in range(min(ahead, hg))]
    for hh in range(hg):
        s = pending.pop(0)
        if hh + ahead < hg:
            pending.append(scores(hh + ahead))
        p = jnp.exp(s - jnp.max(s, axis=0, keepdims=True))
        l = jnp.sum(p, axis=0, keepdims=True)
        outs.append(_dot(vt_s[hh], p.astype(BF16)) / l)
    o_ref[...] = jnp.concatenate(outs, axis=0).T * sz_ref[...]


def _mla_attention(qno, qpe, ckv, kpr, sz, w_kv_up, cache_ckv, cache_kpe, *, nb, L, row_blk0, hg):
    wkv = w_kv_up.reshape(MLA_KV_RANK, MLA_HEADS, MLA_NOPE + MLA_V)
    wk = wkv[:, :, :MLA_NOPE].reshape(MLA_KV_RANK, D).astype(BF16)
    wvt = wkv[:, :, MLA_NOPE:].transpose(1, 2, 0).astype(BF16)
    n_cache = 0 if cache_ckv is None else cache_ckv.shape[1]
    lk = n_cache + L
    nq = L // TM
    wq = hg * MLA_NOPE
    wp = hg * MLA_ROPE
    qrow = lambda w: pl.BlockSpec((TM, w), lambda b, g, q: (row_blk0 + b * nq + q, g))
    seq = lambda w: pl.BlockSpec((L, w), lambda b, g, q: (row_blk0 * TM // L + b, 0))
    in_specs = [qrow(wq), qrow(wp), seq(MLA_KV_RANK), seq(MLA_ROPE)]
    args = [qno, qpe, ckv, kpr]
    if n_cache:
        in_specs += [pl.BlockSpec((None, n_cache, MLA_KV_RANK), lambda b, g, q: (b, 0, 0)),
                     pl.BlockSpec((None, n_cache, MLA_ROPE), lambda b, g, q: (b, 0, 0))]
        args += [cache_ckv, cache_kpe]
    in_specs += [pl.BlockSpec((MLA_KV_RANK, wq), lambda b, g, q: (0, g)),
                 pl.BlockSpec((hg, MLA_V, MLA_KV_RANK), lambda b, g, q: (g, 0, 0)), qrow(wq)]
    args += [wk, wvt, sz]
    return pl.pallas_call(
        functools.partial(_mla_attn_kernel, n_cache=n_cache, hg=hg),
        grid=(nb, MLA_HEADS // hg, nq),
        in_specs=in_specs,
        out_specs=pl.BlockSpec((TM, wq), lambda b, g, q: (b * nq + q, g)),
        out_shape=jax.ShapeDtypeStruct((nb * L, D), F32),
        scratch_shapes=[pltpu.VMEM((hg, lk, MLA_NOPE + MLA_ROPE), BF16), pltpu.VMEM((hg, MLA_V, lk), BF16)],
        compiler_params=_cparams(("arbitrary", "arbitrary", "arbitrary")),
        name="mla_attention",
    )(*args)


def _mla_layer(x, mod, layer, rope, ln_g, ln_b, cache_ckv, cache_kpe, w_in, q_norm, kv_norm, w_q_up, w_kv_up, w_out):
    qno, qpe, ckv, kpe, kpr, sz = _mla_in_proj(x, mod, layer, w_in, q_norm, kv_norm, w_q_up, rope)
    g_p = _mla_attention(qno, qpe, ckv, kpr, sz, w_kv_up, None, None, nb=B_P, L=L_P, row_blk0=0,
                         hg=MLA_HEADS)
    g_s = _mla_attention(qno, qpe, ckv, kpr, sz, w_kv_up, cache_ckv, cache_kpe, nb=B_S, L=L_S, row_blk0=NT_P,
                         hg=MLA_HEADS // 2)
    x_new = _out_proj_ln2(g_p, g_s, w_out, x, mod, layer, ln_g, ln_b)
    new_ckv = ckv[:T_P].reshape(B_P, 1, L_P, MLA_KV_RANK)
    new_kpe = kpe[:T_P].reshape(B_P, 1, L_P, MLA_ROPE)
    return x_new, new_ckv, new_kpe


def _rw_in_kernel(x_ref, xp_ref, xn_ref, m_ref, mu_ref, wr_ref, wk_ref, wv_ref, wg_ref, w1_ref, a1_ref,
                  w2_ref, a2_ref, w0_ref, a0_ref, kk_ref, ka_ref, rk_ref, ones_ref,
                  r_ref, v_ref, sz_ref, nkk_ref, bonus_ref, lw_ref, kd_ref, bd_ref):
    i = pl.program_id(0)
    has_prev, has_next = _tile_has_neighbours(i)
    m = m_ref[...]
    h = _modulate(x_ref[...], m)
    prev, nxt = _neighbour_rows(h, _modulate(xp_ref[...], m), _modulate(xn_ref[...], m), has_prev, has_next)
    d = 0.5 * (prev + nxt) - h
    mu = mu_ref[...]

    def mix(p):
        return (h + d * mu[p:p + 1]).astype(BF16)

    r = _dot(mix(0), wr_ref[...])
    tw = jnp.tanh(_dot(mix(1), w1_ref[...])).astype(BF16)
    k = _dot(mix(2), wk_ref[...])
    v = _dot(mix(3), wv_ref[...])
    ta = _dot(mix(4), a1_ref[...]).astype(BF16)
    z = _dot(mix(5), wg_ref[...])
    r_ref[...] = r.astype(r_ref.dtype)
    v_ref[...] = v.astype(v_ref.dtype)
    sz_ref[...] = _silu(z)
    ones_bd = ones_ref[...]
    kk = k * kk_ref[...]
    kk = kk * lax.rsqrt(_head_sum(kk * kk, ones_bd) + 1e-12)
    nkk_ref[...] = (-kk).astype(nkk_ref.dtype)
    coef = jnp.zeros_like(r)
    for n in range(2):
        wl = w0_ref[n:n + 1, :] + _dot(tw, w2_ref[n])
        lw_ref[n] = -math.exp(-0.5) * _sigmoid(wl)
        a = _sigmoid(a0_ref[n:n + 1, :] + _dot(ta, a2_ref[n]))
        kd = k * (1.0 + (a - 1.0) * ka_ref[...])
        kd_ref[n] = kd.astype(kd_ref.dtype)
        bd_ref[n] = (kk * a).astype(bd_ref.dtype)
        coef = coef + r * kd * rk_ref[...]
    bonus_ref[...] = _head_sum(coef, ones_bd) * v


def _pad_lora_up(w):
    z = jnp.zeros_like(w[0])
    return jnp.stack([jnp.concatenate([w[0], z], 0), jnp.concatenate([z, w[1]], 0)])


def _head_ones():
    h = np.arange(LANES) // RW_N
    return jnp.asarray(h[:, None] == h[None, :], dtype=BF16)


def _rw_in_proj(x, mod, layer, ones_bd, mu, w_in, w0, w1, w2, a0, a1, a2, k_k, k_a, r_k):
    mu8 = jnp.pad(mu, ((0, 2), (0, 0)))
    w1c = jnp.concatenate([w1[0], w1[1]], -1).astype(BF16)
    a1c = jnp.concatenate([a1[0], a1[1]], -1).astype(BF16)
    w2p = _pad_lora_up(w2).astype(BF16)
    a2p = _pad_lora_up(a2).astype(BF16)
    row = pl.BlockSpec((TM, D), lambda i: (i, 0))
    row2 = pl.BlockSpec((2, TM, D), lambda i: (0, i, 0))
    prev, nxt = _halo_specs(D, lambda i: 0)
    full = lambda a: pl.BlockSpec(a.shape, lambda i: (0,) * a.ndim)
    consts = [mu8, w_in[0].astype(BF16), w_in[1].astype(BF16), w_in[2].astype(BF16), w_in[3].astype(BF16),
              w1c, a1c, w2p, a2p, w0, a0, k_k.reshape(1, D), k_a.reshape(1, D), r_k.reshape(1, D), ones_bd]
    return pl.pallas_call(
        _rw_in_kernel,
        grid=(NT,),
        in_specs=[row, prev, nxt, _mod_spec(layer)] + [full(a) for a in consts],
        out_specs=[row] * 5 + [row2] * 3,
        out_shape=[jax.ShapeDtypeStruct((T, D), dt) for dt in (BF16, BF16, F32, BF16, F32)]
                  + [jax.ShapeDtypeStruct((2, T, D), dt) for dt in (F32, BF16, BF16)],
        compiler_params=_cparams(("arbitrary",)),
        name="rwkv_in_proj",
    )(x, x, x, mod, *consts)


SCAN_NB = 2
SCAN_NU = 2 * SCAN_NB


SCAN_PACK = 4
SCAN_GW = SCAN_PACK * RW_N


def _rw_scan_kernel(*refs):
    rb_refs = refs[:SCAN_NU]
    first_ref = refs[SCAN_NU]
    ins = refs[SCAN_NU + 1:SCAN_NU + 1 + 6 * SCAN_NU]
    s0_ref, y_ref, s_ref = refs[SCAN_NU + 1 + 6 * SCAN_NU:]
    del rb_refs
    step = pl.program_id(0)
    c, gw = CHUNK, SCAN_GW

    @pl.when(first_ref[step] == 1)
    def _():
        s_ref[...] = s0_ref[...]

    rc = lax.broadcasted_iota(jnp.int32, (c, c), 0)
    cc = lax.broadcasted_iota(jnp.int32, (c, c), 1)
    ri = lax.broadcasted_iota(jnp.int32, (c, gw), 0)
    lane = lax.broadcasted_iota(jnp.int32, (c, gw), 1)
    pos = lane & (RW_N - 1)
    ri8 = lax.broadcasted_iota(jnp.int32, (c, 2 * gw), 0)
    pos8 = lax.broadcasted_iota(jnp.int32, (c, 2 * gw), 1) & (RW_N - 1)
    head_of = [(lane >> 6) == hh for hh in range(SCAN_PACK)]
    eye_f = (ri == pos).astype(F32)
    eye = eye_f.astype(BF16)
    rdec = lax.broadcasted_iota(jnp.int32, (LANES, gw), 0)
    ldec = lax.broadcasted_iota(jnp.int32, (LANES, gw), 1)
    dec_keep = jnp.logical_and((rdec & (ROWS_BF16 - 1)) == (ldec >> 6), rdec < 2 * ROWS_BF16)
    dec_keep_f = dec_keep.astype(F32)
    dec_sum = dec_keep.astype(BF16)
    dec_hi = jnp.logical_and(dec_keep, rdec < ROWS_BF16).astype(BF16)
    dec_lo = dec_sum - dec_hi
    masks = []
    for sgn in (1, -1):
        incl8 = (ri8 - pos8) * sgn >= 0
        strict = (ri - pos) * sgn > 0
        off_masks = []
        for lg in range(6):
            same_2m = (ri >> (lg + 1)) == (pos >> (lg + 1))
            diff_m = (ri >> lg) != (pos >> lg)
            off_masks.append(jnp.logical_and(strict, jnp.logical_and(same_2m, diff_m)))
        masks.append((incl8, strict, off_masks))

    head_bf = [m.astype(BF16) for m in head_of]

    def keep(x, hh):
        return x * head_bf[hh]

    def bdiag(x):
        return jnp.concatenate([keep(x, hh) for hh in range(SCAN_PACK)], axis=0)

    groups = range(D // gw)
    chains = [(q, g) for q in range(SCAN_NU) for g in groups]
    ar, rb, rbe, v4 = {}, {}, {}, {}
    for q in range(SCAN_NU):
        r_ref, v_ref, nkk_ref, lw_ref, kd_ref, bd_ref = ins[6 * q:6 * q + 6]
        bwd = q % 2
        lw = lw_ref[...]
        cum = (rc - cc) * (-1 if bwd else 1) >= 0
        lw_hi = lw.astype(BF16)
        lw_mid, lw_lo = _split(lw - lw_hi.astype(F32))
        cum_b = cum.astype(BF16)
        g_in = _dot(cum_b, lw_hi) + _dot(cum_b, lw_mid) + _dot(cum_b, lw_lo)
        gtot = g_in[0:1, :] if bwd else g_in[c - 1:c, :]
        e_inv = jnp.exp(-g_in)
        e_rem = jnp.exp(gtot - g_in)
        e_tot = jnp.exp(gtot)
        a_t = nkk_ref[...].astype(F32) * jnp.exp(g_in - lw)
        r_t = r_ref[...].astype(F32) * jnp.exp(g_in)
        kd, bd = kd_ref[...].astype(F32), bd_ref[...].astype(F32)
        b_t, k_t, b_e, k_e = bd * e_inv, kd * e_inv, bd * e_rem, kd * e_rem
        v = v_ref[...]
        for g in groups:
            sl = slice(g * gw, (g + 1) * gw)
            ar[q, g] = jnp.concatenate([a_t[:, sl], r_t[:, sl]], axis=0).astype(BF16)
            bt4, kt4 = b_t[:, sl].astype(BF16), k_t[:, sl].astype(BF16)
            rb[q, g] = jnp.concatenate([bdiag(bt4), bdiag(kt4)], axis=0)
            e_hi, e_lo = _split(e_tot[:, sl] * dec_keep_f)
            dec = e_hi * dec_hi + e_lo * dec_lo
            rbe[q, g] = jnp.concatenate([bdiag(b_e[:, sl].astype(BF16)), bdiag(k_e[:, sl].astype(BF16)), dec],
                                        axis=0)
            v4[q, g] = v[:, sl]
    h_old = {ch: s_ref[ch[0], ch[1]] for ch in chains}
    gm = {ch: _dot_nt(ar[ch], rb[ch]) for ch in chains}
    a_s = {ch: _dot(ar[ch], bdiag(h_old[ch].astype(BF16))) for ch in chains}
    bke_t = {ch: _dot_nt(eye, rbe[ch]) for ch in chains}
    lab = {ch: gm[ch][:c, :gw] for ch in chains}
    offs = lambda ch: masks[ch[0] % 2][2]
    tinv = {ch: eye_f + jnp.where(offs(ch)[0], lab[ch], 0.0) for ch in chains}
    for lg in range(1, 6):
        tb = {ch: tinv[ch].astype(BF16) for ch in chains}
        lt = {ch: _dot(jnp.where(offs(ch)[lg], lab[ch], 0.0).astype(BF16), bdiag(tb[ch])) for ch in chains}
        tinv = {ch: tinv[ch] + _dot(tb[ch], bdiag(lt[ch].astype(BF16))) for ch in chains}
    v_bd = {ch: bdiag(v4[ch]) for ch in chains}
    w1 = {ch: a_s[ch][:c] + _dot(jnp.where(masks[ch[0] % 2][1], gm[ch][:c, gw:], 0.0).astype(BF16), v_bd[ch])
          for ch in chains}
    u = {ch: _dot(tinv[ch].astype(BF16), bdiag(w1[ch].astype(BF16))) for ch in chains}
    lhs = {ch: jnp.concatenate([jnp.where(masks[ch[0] % 2][0], gm[ch][c:], 0.0), bke_t[ch][:, :2 * gw]],
                               axis=0).astype(BF16) for ch in chains}
    yu = {ch: _dot(lhs[ch], jnp.concatenate([bdiag(u[ch].astype(BF16)), v_bd[ch]], axis=0)) for ch in chains}
    e_col = {ch: _dot(bke_t[ch][:, 2 * gw:].astype(BF16), dec_sum) for ch in chains}
    for q in range(SCAN_NU):
        y_ref[q] = jnp.concatenate([a_s[q, g][c:] + yu[q, g][:c] for g in groups], axis=1)
    for ch in chains:
        s_ref[ch[0], ch[1]] = h_old[ch] * e_col[ch] + yu[ch][c:]


def _rw_scan_tables():
    rb = [[] for _ in range(SCAN_NU)]
    first = []
    for nb, L, blk0 in ((B_P, L_P, 0), (B_S, L_S, T_P // CHUNK)):
        nc = L // CHUNK
        for grp in range(nb // SCAN_NB):
            for j in range(nc):
                for q in range(SCAN_NU):
                    b = grp * SCAN_NB + q // 2
                    cn = j if q % 2 == 0 else nc - 1 - j
                    rb[q].append(blk0 + b * nc + cn)
                first.append(1 if j == 0 else 0)
    as_i32 = lambda a: jnp.asarray(np.asarray(a, np.int32))
    return [as_i32(a) for a in rb], as_i32(first)


def _rw_scan(r, v, nkk, lw, kd, bd, s0_all):
    rb, first = _rw_scan_tables()
    n_steps = first.shape[0]
    n_groups = s0_all.shape[0] // SCAN_NU
    steps_p = (B_P // SCAN_NB) * (L_P // CHUNK)
    in_specs = []
    args = []
    for q in range(SCAN_NU):
        tok = pl.BlockSpec((CHUNK, D), lambda s, *t, q=q: (t[q][s], 0))
        tok2 = pl.BlockSpec((None, CHUNK, D), lambda s, *t, q=q: (q % 2, t[q][s], 0))
        in_specs += [tok, tok, tok, tok2, tok2, tok2]
        args += [r, v, nkk, lw, kd, bd]
    grp = lambda s: jnp.where(s < steps_p, s // (L_P // CHUNK),
                              B_P // SCAN_NB + (s - steps_p) // (L_S // CHUNK))
    n_hg = RW_H // SCAN_PACK
    st = pl.BlockSpec((None, SCAN_NU, n_hg, RW_N, SCAN_GW), lambda s, *t: (grp(s), 0, 0, 0, 0))
    s0g = (s0_all.reshape(n_groups, SCAN_NU, n_hg, SCAN_PACK, RW_N, RW_N).transpose(0, 1, 2, 4, 3, 5)
           .reshape(n_groups, SCAN_NU, n_hg, RW_N, SCAN_GW))
    y, s_fin = pl.pallas_call(
        _rw_scan_kernel,
        grid_spec=pltpu.PrefetchScalarGridSpec(
            num_scalar_prefetch=SCAN_NU + 1,
            grid=(n_steps,),
            in_specs=in_specs + [st],
            out_specs=[pl.BlockSpec((None, SCAN_NU, CHUNK, D), lambda s, *t: (s, 0, 0, 0)), st],
        ),
        out_shape=[jax.ShapeDtypeStruct((n_steps, SCAN_NU, CHUNK, D), F32),
                   jax.ShapeDtypeStruct(s0g.shape, F32)],
        compiler_params=_cparams(("arbitrary",)),
        name="rwkv_scan",
    )(*rb, first, *args, s0g)
    s_fin = (s_fin.reshape(n_groups, SCAN_NU, n_hg, RW_N, SCAN_PACK, RW_N).transpose(0, 1, 2, 4, 3, 5)
             .reshape(s0_all.shape))
    return y, s_fin


def _scan_out_index(i, k, bwd):
    per_tile = TM // CHUNK
    nc_p, nc_s = L_P // CHUNK, L_S // CHUNK
    steps_p = (B_P // SCAN_NB) * nc_p
    cn_p = k
    b_s = (i - NT_P) // NT_S_SEQ
    cn_s = ((i - NT_P) % NT_S_SEQ) * per_tile + k
    step_p = (i // SCAN_NB) * nc_p + (nc_p - 1 - cn_p if bwd else cn_p)
    step_s = steps_p + (b_s // SCAN_NB) * nc_s + (nc_s - 1 - cn_s if bwd else cn_s)
    q_p = (i % SCAN_NB) * 2 + bwd
    q_s = (b_s % SCAN_NB) * 2 + bwd
    is_p = i < NT_P
    return jnp.where(is_p, step_p, step_s), jnp.where(is_p, q_p, q_s)


def _rw_out_kernel(*refs):
    per_tile = TM // CHUNK
    y_refs = refs[:2 * per_tile]
    bonus_ref, sz_ref, gg_ref, gb_ref, ones_ref, w_ref, x_ref, m_ref, lng_ref, lnb_ref, o_ref = refs[2 * per_tile:]
    ones_bd = ones_ref[...]
    y = jnp.concatenate([y_refs[k][...] + y_refs[per_tile + k][...] for k in range(per_tile)], axis=0)
    mean = _head_sum(y, ones_bd) * (1.0 / RW_N)
    yc = y - mean
    var = _head_sum(yc * yc, ones_bd) * (1.0 / RW_N)
    yn = yc * lax.rsqrt(var + RW_GN_EPS) * gg_ref[...] + gb_ref[...]
    g = (yn + bonus_ref[...]) * sz_ref[...]
    _out_ln_tail(g, w_ref, x_ref, m_ref, lng_ref, lnb_ref, o_ref)


def _rw_out_proj_ln(y, bonus, sz, gn_g, gn_b, ones_bd, w_out, x, mod, layer, ln_g, ln_b):
    row = pl.BlockSpec((TM, D), lambda i: (i, 0))
    vec = pl.BlockSpec((1, D), lambda i: (0, 0))
    mat = pl.BlockSpec((D, D), lambda i: (0, 0))
    y_specs = [pl.BlockSpec((None, None, CHUNK, D), lambda i, k=k, bwd=bwd: _scan_out_index(i, k, bwd) + (0, 0))
               for bwd in (0, 1) for k in range(TM // CHUNK)]
    return pl.pallas_call(
        _rw_out_kernel,
        grid=(NT,),
        in_specs=y_specs + [row, row, vec, vec,
                  pl.BlockSpec((LANES, LANES), lambda i: (0, 0)), mat, row,
                  _mod_spec(layer), vec, vec],
        out_specs=row,
        out_shape=jax.ShapeDtypeStruct((T, D), F32),
        compiler_params=_cparams(("arbitrary",)),
        name="rwkv_out_proj_ln",
    )(*[y] * len(y_specs), bonus, sz, gn_g.reshape(1, D), gn_b.reshape(1, D), ones_bd, w_out.astype(BF16), x, mod,
      ln_g.reshape(1, D), ln_b.reshape(1, D))


def _rwkv_layer(x, mod, layer, ln_g, ln_b, state, mu, w_in, w0, w1, w2, a0, a1, a2, k_k, k_a, r_k, gn_g, gn_b, w_out):
    ones_bd = _head_ones()
    r, v, sz, nkk, bonus, lw, kd, bd = _rw_in_proj(x, mod, layer, ones_bd, mu, w_in, w0, w1, w2, a0, a1, a2,
                                                  k_k, k_a, r_k)
    n_p = B_P * 2
    s0_all = jnp.concatenate([jnp.zeros((n_p, RW_H, RW_N, RW_N), F32),
                              state.astype(F32).reshape(B_S * 2, RW_H, RW_N, RW_N).swapaxes(-1, -2)], 0)
    y, h_fin = _rw_scan(r, v, nkk, lw, kd, bd, s0_all)
    x_new = _rw_out_proj_ln(y, bonus, sz, gn_g, gn_b, ones_bd, w_out, x, mod, layer, ln_g, ln_b)
    new_state = h_fin[:n_p].swapaxes(-1, -2).reshape(B_P, 1, 2, RW_H, RW_N, RW_N)
    return x_new, new_state


def kernel(x_prompt, x_sample, cache_mla_ckv, cache_mla_kpe, state_rwkv, c, c_ctx, mod_w, mod_b, ln_g, ln_b, hy_w_in, hy_conv_w, hy_conv_b, hy_ffn_w1, hy_ffn_b1, hy_ffn_w2, hy_ffn_b2, hy_ffn_w3, hy_ffn_b3, hy_freq, hy_decay, hy_skip, hy_w_out, mla_w_in, mla_q_norm, mla_kv_norm, mla_w_q_up, mla_w_kv_up, mla_w_out, rw_mu, rw_w_in, rw_w0, rw_w1, rw_w2, rw_a0, rw_a1, rw_a2, rw_k_k, rw_k_a, rw_r_k, rw_gn_g, rw_gn_b, rw_w_out):
    x = jnp.concatenate([x_prompt.reshape(T_P, D), x_sample.reshape(T_S, D)], 0)
    cond8 = jnp.concatenate([c_ctx[None, :], c, jnp.zeros((8 - 1 - B_S, D), F32)], 0)
    mod = _modulation_table(cond8, mod_w, mod_b).reshape(DEPTH * 8, 1, 3 * D)
    tabs_p = _dft_tables(L_P)
    tabs_s = _dft_tables(L_S)
    rope = _rope_tables()
    new_ckv = new_kpe = new_state = None
    for i in range(DEPTH):
        kind, j = i % 3, i // 3
        if kind == 0:
            x = _hyena_layer(x, mod, i, tabs_p, tabs_s, ln_g[i], ln_b[i], hy_w_in[j], hy_conv_w[j], hy_conv_b[j],
                             hy_ffn_w1[j], hy_ffn_b1[j], hy_ffn_w2[j], hy_ffn_b2[j], hy_ffn_w3[j], hy_ffn_b3[j],
                             hy_freq[j], hy_decay[j], hy_skip[j], hy_w_out[j], split_out=(i == DEPTH - 1))
        elif kind == 1:
            x, new_ckv, new_kpe = _mla_layer(x, mod, i, rope, ln_g[i], ln_b[i], cache_mla_ckv[:, j],
                                             cache_mla_kpe[:, j], mla_w_in[j], mla_q_norm[j], mla_kv_norm[j],
                                             mla_w_q_up[j], mla_w_kv_up[j], mla_w_out[j])
        else:
            x, new_state = _rwkv_layer(x, mod, i, ln_g[i], ln_b[i], state_rwkv[:, j], rw_mu[j], rw_w_in[j],
                                       rw_w0[j], rw_w1[j], rw_w2[j], rw_a0[j], rw_a1[j], rw_a2[j], rw_k_k[j],
                                       rw_k_a[j], rw_r_k[j], rw_gn_g[j], rw_gn_b[j], rw_w_out[j])
    x_p, x_s = x if isinstance(x, tuple) else (x[:T_P], x[T_P:])
    return (x_p.reshape(B_P, L_P, D), x_s.reshape(B_S, L_S, D), new_ckv, new_kpe, new_state)
```

```python
import functools
import math

import numpy as np
import jax
import jax.numpy as jnp
from jax import lax
from jax.experimental import pallas as pl
from jax.experimental.pallas import tpu as pltpu

F32 = jnp.float32
BF16 = jnp.bfloat16
HIGHEST = lax.Precision.HIGHEST

D = 1024
B_P, L_P = 16, 256
B_S, L_S = 2, 2048
T_P = B_P * L_P
T_S = B_S * L_S
T = T_P + T_S
PAST = 512
DEPTH = 4
DEEPNORM_ALPHA = (2.0 * DEPTH) ** 0.25
LN_EPS = 1e-5
RMS_EPS = 1e-6
HY_BANDS = 16
HY_FFN = 64
MLA_HEADS = 16
MLA_Q_RANK = 256
MLA_KV_RANK = 128
MLA_NOPE = 64
MLA_ROPE = 32
MLA_V = 64
ROPE_BASE = 10000.0
GRID_W = 64
RW_N = 64
RW_H = D // RW_N
RW_LORA = 64
RW_GN_EPS = 64e-5

TM = 256
NT_P = T_P // TM
NT_S_SEQ = L_S // TM
NT = T // TM
HALO = 8
LANES = 128
ROWS_BF16 = 16
CHUNK = 64
DFT_TK = 512
VMEM_LIMIT = 52 * 1024 * 1024


def _cparams(sem):
    return pltpu.CompilerParams(dimension_semantics=sem, vmem_limit_bytes=VMEM_LIMIT)


def _group(i, tm=TM):
    return jnp.where(i < T_P // tm, 0, 1 + (i - T_P // tm) // (L_S // tm))


def _sigmoid(x):
    return 1.0 / (1.0 + jnp.exp(-x))


def _silu(x):
    return x * _sigmoid(x)


def _dot(a, b):
    return jnp.dot(a, b, preferred_element_type=F32)


def _dot_nt(a, b):
    return lax.dot_general(a, b, (((1,), (1,)), ((), ())), preferred_element_type=F32)


def _dot_hi(a, b):
    return jnp.dot(a, b, preferred_element_type=F32, precision=HIGHEST)


def _split(x):
    hi = x.astype(BF16)
    lo = (x - hi.astype(F32)).astype(BF16)
    return hi, lo


def _head_sum(x, ones_bd):
    hi, lo = _split(x)
    lanes = ones_bd.shape[0]
    parts = []
    for g in range(x.shape[1] // lanes):
        sl = slice(g * lanes, (g + 1) * lanes)
        parts.append(_dot(hi[:, sl], ones_bd) + _dot(lo[:, sl], ones_bd))
    return jnp.concatenate(parts, axis=1)


def _modulate(x, m):
    return x * (1.0 + m[:, D:2 * D]) + m[:, :D]


def _layer_norm_rows(y, g, b):
    mu = jnp.mean(y, axis=-1, keepdims=True)
    yc = y - mu
    var = jnp.mean(yc * yc, axis=-1, keepdims=True)
    return yc * lax.rsqrt(var + LN_EPS) * g + b


def _neighbour_rows(cur, prev_halo, next_halo, has_prev, has_next):
    rows = cur.shape[0]
    ridx = lax.broadcasted_iota(jnp.int32, cur.shape, 0)
    pr = jnp.where(has_prev, prev_halo[HALO - 1:HALO, :], 0.0)
    nx = jnp.where(has_next, next_halo[0:1, :], 0.0)
    prev = jnp.where(ridx == 0, pr, pltpu.roll(cur, 1, axis=0))
    nxt = jnp.where(ridx == rows - 1, nx, pltpu.roll(cur, rows - 1, axis=0))
    return prev, nxt


def _tile_has_neighbours(i):
    k = (i - NT_P) % NT_S_SEQ
    is_s = i >= NT_P
    return jnp.logical_and(is_s, k != 0), jnp.logical_and(is_s, k != NT_S_SEQ - 1)


def _halo_specs(width, col_of):
    r = TM // HALO
    prev = pl.BlockSpec((HALO, width), lambda i, *a: (jnp.maximum(i * r - 1, 0), col_of(i, *a)))
    nxt = pl.BlockSpec((HALO, width), lambda i, *a: (jnp.minimum((i + 1) * r, T // HALO - 1), col_of(i, *a)))
    return prev, nxt


def _mod_kernel(c_ref, w_ref, b_ref, o_ref):
    o_ref[...] = _dot_hi(_silu(c_ref[...]), w_ref[...]) + b_ref[...]


def _modulation_table(cond8, mod_w, mod_b):
    tn = 1024
    return pl.pallas_call(
        _mod_kernel,
        grid=(DEPTH, 3 * D // tn),
        in_specs=[pl.BlockSpec((8, D), lambda l, j: (0, 0)),
                  pl.BlockSpec((None, D, tn), lambda l, j: (l, 0, j)),
                  pl.BlockSpec((None, 1, tn), lambda l, j: (l, 0, j))],
        out_specs=pl.BlockSpec((None, 8, tn), lambda l, j: (l, 0, j)),
        out_shape=jax.ShapeDtypeStruct((DEPTH, 8, 3 * D), F32),
        compiler_params=_cparams(("arbitrary", "arbitrary")),
        name="modulation",
    )(cond8, mod_w, mod_b.reshape(DEPTH, 1, 3 * D))


def _mod_spec(layer, tm=TM):
    return pl.BlockSpec((None, 1, 3 * D), lambda i, *a: (layer * 8 + _group(i, tm), 0, 0))


def _out_ln_tail(g, w_ref, x_ref, m_ref, lng_ref, lnb_ref, o_ref):
    mix = _dot(g.astype(BF16), w_ref[...])
    gate = m_ref[...][:, 2 * D:]
    y = DEEPNORM_ALPHA * x_ref[...] + gate * mix
    o_ref[...] = _layer_norm_rows(y, lng_ref[...], lnb_ref[...])


TM_OUT = 512


def _out_ln2_kernel(gp_ref, gs_ref, w_ref, x_ref, m_ref, lng_ref, lnb_ref, o_ref):
    g = jnp.where(pl.program_id(0) < T_P // TM_OUT, gp_ref[...], gs_ref[...])
    _out_ln_tail(g, w_ref, x_ref, m_ref, lng_ref, lnb_ref, o_ref)


def _out_ln1_kernel(g_ref, w_ref, x_ref, m_ref, lng_ref, lnb_ref, o_ref):
    _out_ln_tail(g_ref[...], w_ref, x_ref, m_ref, lng_ref, lnb_ref, o_ref)


def _out_proj_ln1(g, w_out, x, mod, layer, ln_g, ln_b, tile0):
    vec = pl.BlockSpec((1, D), lambda i: (0, 0))
    return pl.pallas_call(
        _out_ln1_kernel,
        grid=(g.shape[0] // TM_OUT,),
        in_specs=[pl.BlockSpec((TM_OUT, D), lambda i: (i, 0)),
                  pl.BlockSpec((D, D), lambda i: (0, 0)),
                  pl.BlockSpec((TM_OUT, D), lambda i: (i + tile0, 0)),
                  pl.BlockSpec((None, 1, 3 * D), lambda i: (layer * 8 + _group(i + tile0, TM_OUT), 0, 0)),
                  vec, vec],
        out_specs=pl.BlockSpec((TM_OUT, D), lambda i: (i, 0)),
        out_shape=jax.ShapeDtypeStruct(g.shape, F32),
        compiler_params=_cparams(("arbitrary",)),
        name="out_proj_ln_part",
    )(g, w_out.astype(BF16), x, mod, ln_g.reshape(1, D), ln_b.reshape(1, D))


def _out_proj_ln2(g_p, g_s, w_out, x, mod, layer, ln_g, ln_b):
    nt_p = T_P // TM_OUT
    row = pl.BlockSpec((TM_OUT, D), lambda i: (i, 0))
    vec = pl.BlockSpec((1, D), lambda i: (0, 0))
    return pl.pallas_call(
        _out_ln2_kernel,
        grid=(T // TM_OUT,),
        in_specs=[pl.BlockSpec((TM_OUT, D), lambda i: (jnp.minimum(i, nt_p - 1), 0)),
                  pl.BlockSpec((TM_OUT, D), lambda i: (jnp.maximum(i - nt_p, 0), 0)),
                  pl.BlockSpec((D, D), lambda i: (0, 0)),
                  row, _mod_spec(layer, TM_OUT), vec, vec],
        out_specs=row,
        out_shape=jax.ShapeDtypeStruct((T, D), F32),
        compiler_params=_cparams(("arbitrary",)),
        name="out_proj_ln",
    )(g_p, g_s, w_out.astype(BF16), x, mod, ln_g.reshape(1, D), ln_b.reshape(1, D))


def _hy_in_conv_kernel(x_ref, xp_ref, xn_ref, m_ref, w_ref, cw_ref, cb_ref, vv_ref, gate_ref):
    has_prev, has_next = _tile_has_neighbours(pl.program_id(0))
    x_all = jnp.concatenate([xp_ref[...], x_ref[...], xn_ref[...]], axis=0)
    u = _dot(_modulate(x_all, m_ref[...]).astype(BF16), w_ref[...])
    rows = TM + 2 * HALO
    ridx = lax.broadcasted_iota(jnp.int32, (TM, 1), 0)
    no_prev = jnp.logical_and(ridx == 0, jnp.logical_not(has_prev))
    no_next = jnp.logical_and(ridx == TM - 1, jnp.logical_not(has_next))
    cw = cw_ref[...]
    cb = cb_ref[...]

    def conv(grp):
        sl = slice(grp * D, (grp + 1) * D)
        ug = u[:, sl]
        cur = ug[HALO:HALO + TM]
        prev = jnp.where(no_prev, 0.0, pltpu.roll(ug, 1, axis=0)[HALO:HALO + TM])
        nxt = jnp.where(no_next, 0.0, pltpu.roll(ug, rows - 1, axis=0)[HALO:HALO + TM])
        return prev * cw[0:1, sl] + cur * cw[1:2, sl] + nxt * cw[2:3, sl] + cb[:, sl]

    x0, x1, v = conv(0), conv(1), conv(2)
    vv_ref[...] = v * x1
    gate_ref[...] = x0 * _silu(u[HALO:HALO + TM, 3 * D:])


def _hy_in_conv(x, mod, layer, w_in, conv_w, conv_b):
    row = pl.BlockSpec((TM, D), lambda i: (i, 0))
    prev, nxt = _halo_specs(D, lambda i: 0)
    return pl.pallas_call(
        _hy_in_conv_kernel,
        grid=(NT,),
        in_specs=[row, prev, nxt, _mod_spec(layer), _resident(w_in.shape),
                  pl.BlockSpec((3, 3 * D), lambda i: (0, 0)), pl.BlockSpec((1, 3 * D), lambda i: (0, 0))],
        out_specs=[row, row],
        out_shape=[jax.ShapeDtypeStruct((T, D), F32)] * 2,
        compiler_params=_cparams(("arbitrary",)),
        name="hyena_in_proj_conv3",
    )(x, x, x, mod, w_in.astype(BF16), conv_w, conv_b.reshape(1, 3 * D))


def _hy_filter_kernel(t_ref, bands_ref, wt_ref, wc_ref, ws_ref, b1_ref, w2_ref, b2_ref, w3_ref, b3_ref,
                      f0_ref, f1_ref, dec_ref, hs_ref, hd_ref, nyq_ref, *, L, tr):
    i = pl.program_id(0)
    ridx = lax.broadcasted_iota(jnp.int32, (tr, 1), 0) + i * tr
    pos = ridx.astype(F32)
    t = t_ref[...]
    ang = ((2.0 * math.pi / L) * pos) * bands_ref[...]
    pre = t * wt_ref[...] + _dot_hi(jnp.cos(ang), wc_ref[...]) + _dot_hi(jnp.sin(ang), ws_ref[...])
    hdn = jnp.sin(f0_ref[...] * (pre + b1_ref[...]))
    hdn = jnp.sin(f1_ref[...] * (_dot_hi(hdn, w2_ref[...]) + b2_ref[...]))
    hf = _dot_hi(hdn, w3_ref[...]) + b3_ref[...]
    h = hf * jnp.exp(-t * jnp.abs(dec_ref[...]))
    h0 = h[:, :D]
    h1 = jnp.where(ridx == 0, 0.0, h[:, D:])
    hsum = h0 + h1
    hs_ref[...] = hsum
    hd_ref[...] = h1 - h0
    alt = jnp.where((ridx & 1) == 0, 1.0, -1.0)
    part =jnp.broadcast_to(jnp.sum(alt * hsum, axis=0, keepdims=True), (8, D))

    @pl.when(i == 0)
    def _():
        nyq_ref[...] = part

    @pl.when(i > 0)
    def _():
        nyq_ref[...] += part


def _hy_filter(L, w1, b1, w2, b2, w3, b3, freq, decay):
    tr = 256
    t = jnp.linspace(0.0, 1.0, L, dtype=F32).reshape(L, 1)
    bands = jnp.linspace(1e-4, HY_BANDS - 1, HY_BANDS, dtype=F32)
    bands = jnp.pad(bands, (0, 128 - HY_BANDS)).reshape(1, 128)
    wt = w1[0:1]
    wc = jnp.pad(w1[1:1 + HY_BANDS], ((0, 128 - HY_BANDS), (0, 0)))
    ws = jnp.pad(-w1[1 + HY_BANDS:], ((0, 128 - HY_BANDS), (0, 0)))
    full = lambda shape: pl.BlockSpec(shape, lambda i: (0, 0))
    rows = pl.BlockSpec((tr, D), lambda i: (i, 0))
    return pl.pallas_call(
        functools.partial(_hy_filter_kernel, L=L, tr=tr),
        grid=(L // tr,),
        in_specs=[pl.BlockSpec((tr, 1), lambda i: (i, 0)), full((1, 128)), full((1, HY_FFN)),
                  full((128, HY_FFN)), full((128, HY_FFN)), full((1, HY_FFN)),
                  full((HY_FFN, HY_FFN)), full((1, HY_FFN)), full((HY_FFN, 2 * D)), full((1, 2 * D)),
                  full((1, HY_FFN)), full((1, HY_FFN)), full((1, 2 * D))],
        out_specs=[rows, rows, pl.BlockSpec((8, D), lambda i: (0, 0))],
        out_shape=[jax.ShapeDtypeStruct((L, D), F32), jax.ShapeDtypeStruct((L, D), F32),
                   jax.ShapeDtypeStruct((8, D), F32)],
        compiler_params=_cparams(("arbitrary",)),
        name="hyena_filter",
    )(t, bands, wt, wc, ws, b1.reshape(1, -1), w2, b2.reshape(1, -1), w3, b3.reshape(1, -1),
      freq[0:1], freq[1:2], decay.reshape(1, 2 * D))


def _dft_table_kernel(ca_ref, sa_ref, cb_ref, sb_ref, cat_ref, sat_ref, cbt_ref, sbt_ref,
                      cr_ref, sr_ref, cc_ref, sc_ref, *, n_t1):
    first = pl.program_id(0) == 0
    ca, sa, cb, sb = ca_ref[...], sa_ref[...], cb_ref[...], sb_ref[...]
    cat, sat, cbt, sbt = cat_ref[...], sat_ref[...], cbt_ref[...], sbt_ref[...]
    tk = cb.shape[0]
    row0 = jnp.logical_and(lax.broadcasted_iota(jnp.int32, (tk, LANES), 0) == 0, first)
    alt_r = jnp.where((lax.broadcasted_iota(jnp.int32, (tk, LANES), 1) & 1) == 0, 1.0, -1.0)
    col0 = jnp.logical_and(lax.broadcasted_iota(jnp.int32, (LANES, tk), 1) == 0, first)
    alt_c = jnp.where((lax.broadcasted_iota(jnp.int32, (LANES, tk), 0) & 1) == 0, 1.0, -1.0)
    for t1 in range(n_t1):
        sl = slice(t1 * LANES, (t1 + 1) * LANES)
        a_c, a_s = ca[:, t1:t1 + 1], sa[:, t1:t1 + 1]
        cr_ref[:, sl] = (a_c * cb - a_s * sb).astype(BF16)
        sr_ref[:, sl] = jnp.where(row0, alt_r, a_s * cb + a_c * sb).astype(BF16)
        b_c, b_s = cat[t1:t1 + 1, :], sat[t1:t1 + 1, :]
        cc_ref[sl, :] = (b_c * cbt - b_s * sbt).astype(BF16)
        sc_ref[sl, :] = jnp.where(col0, alt_c, b_s * cbt + b_c * sbt).astype(BF16)


def _dft_tables(L):
    n = 2 * L
    k = jnp.arange(L, dtype=jnp.int32)

    def cs(t):
        ang = ((k[:, None] * t[None, :]) % n).astype(F32) * (2.0 * math.pi / n)
        return jnp.cos(ang), jnp.sin(ang)

    tk = min(L, DFT_TK)
    nk = L // tk
    n_t1 = L // LANES
    ca, sa = cs(jnp.arange(n_t1, dtype=jnp.int32) * LANES)
    cb, sb = cs(jnp.arange(LANES, dtype=jnp.int32))
    by_bin = lambda w: pl.BlockSpec((tk, w), lambda i: (i, 0))
    by_bin_t = lambda w: pl.BlockSpec((w, tk), lambda i: (0, i))
    return tuple(pl.pallas_call(
        functools.partial(_dft_table_kernel, n_t1=n_t1),
        grid=(nk,),
        in_specs=[by_bin(n_t1), by_bin(n_t1), by_bin(LANES), by_bin(LANES),
                  by_bin_t(n_t1), by_bin_t(n_t1), by_bin_t(LANES), by_bin_t(LANES)],
        out_specs=[pl.BlockSpec((None, tk, L), lambda i: (i, 0, 0))] * 2
                  + [pl.BlockSpec((None, L, tk), lambda i: (i, 0, 0))] * 2,
        out_shape=[jax.ShapeDtypeStruct((nk, tk, L), BF16)] * 2 + [jax.ShapeDtypeStruct((nk, L, tk), BF16)] * 2,
        compiler_params=_cparams(("arbitrary",)),
        name="dft_tables",
    )(ca, sa, cb, sb, ca.T, sa.T, cb.T, sb.T))


def _dft_fwd_kernel(c_ref, s_ref, *refs):
    x1_ref, x2_ref = refs[0], refs[-3]
    oc_ref, os_ref = refs[-2:]
    k = pl.program_id(2)
    oc_ref[...] = _dot(c_ref[k], x1_ref[...].astype(BF16))
    os_ref[...] = _dot(s_ref[k], x2_ref[...].astype(BF16))


def _resident(shape):
    return pl.BlockSpec(shape, lambda *_: (0,) * len(shape), pipeline_mode=pl.Buffered(1))


def _dft_fwd(tabs, xs, L, nb, row_blk0):
    c, s_rows = tabs[:2]
    nk, tk, _ = c.shape
    tn = 512
    a_spec = _resident(c.shape)
    x_spec = pl.BlockSpec((L, tn), lambda b, j, k: (row_blk0 + b, j))
    o_spec = pl.BlockSpec((tk, tn), lambda b, j, k: (b * nk + k, j))
    return pl.pallas_call(
        _dft_fwd_kernel,
        grid=(nb, D // tn, nk),
        in_specs=[a_spec] * 2 + [x_spec] * len(xs),
        out_specs=[o_spec, o_spec],
        out_shape=[jax.ShapeDtypeStruct((nb * L, D), F32)] * 2,
        compiler_params=_cparams(("arbitrary", "arbitrary", "arbitrary")),
        name="hyena_dft_fwd",
    )(c, s_rows, *xs)


def _spectrum_product(vc, vs, kre, kim, nyq, first_tile, L):
    bin0 = jnp.logical_and(lax.broadcasted_iota(jnp.int32, vc.shape, 0) == 0, first_tile)
    kim = jnp.where(bin0, nyq, kim)
    inv_n = 1.0 / (2 * L)
    yre = jnp.where(bin0, vc * kre * inv_n, (vc * kre + vs * kim) * (2.0 * inv_n))
    yim = jnp.where(bin0, vs * kim * inv_n, (vs * kre - vc * kim) * (2.0 * inv_n))
    return yre.astype(BF16), yim.astype(BF16)


def _dft_conv_kernel(c_ref, s_ref, st_ref, v_ref, kre_ref, kim_ref, nyq_ref, skip_ref, gate_ref, o_ref, *, L):
    v = v_ref[...]
    vb = v.astype(BF16)
    c = c_ref[0]
    yre, yim = _spectrum_product(_dot(c, vb), _dot(s_ref[0], vb), kre_ref[...], kim_ref[...], nyq_ref[0:1, :],
                                 True, L)
    y = _dot(c, yre) + _dot(st_ref[0], yim)
    o_ref[...] = (y + v * skip_ref[...]) * gate_ref[...]


def _dft_conv(tabs, kre, kim, nyq, vv, skip, gate, L, nb, row_blk0):
    c, s_rows, _, s_cols = tabs
    assert c.shape[0] == 1, "single frequency tile only"
    tn = 512
    seq = lambda b, j: (row_blk0 + b, j)
    spec = pl.BlockSpec((L, tn), lambda b, j: (0, j))
    return pl.pallas_call(
        functools.partial(_dft_conv_kernel, L=L),
        grid=(nb, D // tn),
        in_specs=[_resident(c.shape)] * 3 + [pl.BlockSpec((L, tn), seq), spec, spec,
                                             pl.BlockSpec((8, tn), lambda b, j: (0, j)),
                                             pl.BlockSpec((1, tn), lambda b, j: (0, j)), pl.BlockSpec((L, tn), seq)],
        out_specs=pl.BlockSpec((L, tn), lambda b, j: (b, j)),
        out_shape=jax.ShapeDtypeStruct((nb * L, D), F32),
        compiler_params=_cparams(("arbitrary", "arbitrary")),
        name="hyena_dft_conv_gate",
    )(c, s_rows, s_cols, vv, kre, kim, nyq, skip.reshape(1, D), gate)


def _dft_inv_kernel(c_ref, st_ref, vc_ref, vs_ref, kre_ref, kim_ref, nyq_ref,
                    vv_ref, skip_ref, gate_ref, o_ref, *, L):
    k = pl.program_id(2)
    nk = pl.num_programs(2)
    yre, yim = _spectrum_product(vc_ref[...], vs_ref[...], kre_ref[...], kim_ref[...], nyq_ref[0:1, :], k == 0, L)
    contrib = _dot(c_ref[k], yre) + _dot(st_ref[k], yim)

    @pl.when(k == 0)
    def _():
        o_ref[...] = contrib

    @pl.when(k > 0)
    def _():
        o_ref[...] += contrib

    @pl.when(k == nk - 1)
    def _():
        o_ref[...] = (o_ref[...] + vv_ref[...] * skip_ref[...]) * gate_ref[...]


def _dft_inv(tabs, vc, vs, kre, kim, nyq, vv, skip, gate, L, nb, row_blk0):
    c, s_cols = tabs[2:]
    nk, _, tk = c.shape
    tn = 512 if L <= 512 else 256
    a_spec = _resident(c.shape)
    v_spec = pl.BlockSpec((tk, tn), lambda b, j, k: (b * nk + k, j))
    k_spec = pl.BlockSpec((tk, tn), lambda b, j, k: (k, j))
    row_spec = pl.BlockSpec((L, tn), lambda b, j, k: (row_blk0 + b, j))
    return pl.pallas_call(
        functools.partial(_dft_inv_kernel, L=L),
        grid=(nb, D // tn, nk),
        in_specs=[a_spec] * 2 + [v_spec, v_spec, k_spec, k_spec,
                                 pl.BlockSpec((8, tn), lambda b, j, k: (0, j)),
                                 row_spec, pl.BlockSpec((1, tn), lambda b, j, k: (0, j)), row_spec],
        out_specs=pl.BlockSpec((L, tn), lambda b, j, k: (b, j)),
        out_shape=jax.ShapeDtypeStruct((nb * L, D), F32),
        compiler_params=_cparams(("arbitrary", "arbitrary", "arbitrary")),
        name="hyena_dft_inv_gate",
    )(c, s_cols, vc, vs, kre, kim, nyq, vv, skip.reshape(1, D), gate)


def _hyena_layer(x, mod, layer, tabs_p, tabs_s, ln_g, ln_b, w_in, conv_w, conv_b, w1, b1, w2, b2, w3, b3,
                 freq, decay, skip, w_out, split_out=False):
    vv, gate = _hy_in_conv(x, mod, layer, w_in, conv_w, conv_b)
    gs = []
    for L, nb, blk0, tabs in ((L_P, B_P, 0, tabs_p), (L_S, B_S, T_P // L_S, tabs_s)):
        hsum, hdiff, nyq = _hy_filter(L, w1, b1, w2, b2, w3, b3, freq, decay)
        kre, kim = _dft_fwd(tabs, (hsum, hdiff), L, 1, 0)
        if L <= DFT_TK:
            gs.append(_dft_conv(tabs, kre, kim, nyq, vv, skip, gate, L, nb, blk0))
        else:
            vc, vs = _dft_fwd(tabs, (vv,), L, nb, blk0)
            gs.append(_dft_inv(tabs, vc, vs, kre, kim, nyq, vv, skip, gate, L, nb, blk0))
    if split_out:
        return tuple(_out_proj_ln1(g, w_out, x, mod, layer, ln_g, ln_b, tile0)
                     for g, tile0 in ((gs[0], 0), (gs[1], T_P // TM_OUT)))
    return _out_proj_ln2(gs[0], gs[1], w_out, x, mod, layer, ln_g, ln_b)


def _rope_tables():
    rows = L_S // GRID_W
    half = MLA_ROPE // 2
    inv = ROPE_BASE ** (-jnp.arange(0, half, 2, dtype=F32) / half)
    r = jnp.repeat(jnp.arange(rows, dtype=F32), GRID_W)
    col = jnp.tile(jnp.arange(GRID_W, dtype=F32), rows)
    ar, ac = r[:, None] * inv, col[:, None] * inv
    ang = jnp.concatenate([ar, ar, ac, ac], -1)
    cos, sin = jnp.cos(ang), jnp.sin(ang)
    cos = jnp.concatenate([jnp.ones((TM, MLA_ROPE), F32), cos], 0)
    sin = jnp.concatenate([jnp.zeros((TM, MLA_ROPE), F32), sin], 0)
    return cos, sin, jnp.tile(cos, (1, MLA_HEADS)), jnp.tile(sin, (1, MLA_HEADS))


def _rope_rot_cols(w):
    idx = np.concatenate([np.arange(8, 16), np.arange(0, 8), np.arange(24, 32), np.arange(16, 24)])
    sign = np.concatenate([-np.ones(8), np.ones(8), -np.ones(8), np.ones(8)]).astype(np.float32)
    return w[..., idx] * sign


def _mla_in_kernel(x_ref, m_ref, wq_ref, wkv_ref, wkp_ref, wz_ref, qn_ref, kvn_ref, wqn_ref, wqp_ref, wqr_ref,
                   c32_ref, s32_ref, c512_ref, s512_ref,
                   qno_ref, qpe_ref, ckv_ref, kpe_ref, kpr_ref, sz_ref):
    h = _modulate(x_ref[...], m_ref[...]).astype(BF16)
    q_c = _dot(h, wq_ref[...])
    kv_c = _dot(h, wkv_ref[...])
    kp2 = _dot(h, wkp_ref[...])
    z = _dot(h, wz_ref[...])

    def rms(v, g):
        return v * lax.rsqrt(jnp.mean(v * v, axis=-1, keepdims=True) + RMS_EPS) * g

    qn = rms(q_c, qn_ref[...]).astype(BF16)
    scale = (MLA_NOPE + MLA_ROPE) ** -0.5
    qno_ref[...] = (_dot(qn, wqn_ref[...]) * scale).astype(BF16)
    q_pe = _dot(qn, wqp_ref[...]) * c512_ref[...] + _dot(qn, wqr_ref[...]) * s512_ref[...]
    qpe_ref[...] = (q_pe * scale).astype(BF16)
    ckv_ref[...] = rms(kv_c, kvn_ref[...])
    kpe = kp2[:, :MLA_ROPE]
    kpe_ref[...] = kpe
    kpr_ref[...] = kpe * c32_ref[...] + kp2[:, MLA_ROPE:] * s32_ref[...]
    sz_ref[...] = _silu(z)


def _mla_in_proj(x, mod, layer, w_in, q_norm, kv_norm, w_q_up, rope):
    c32, s32, c512, s512 = rope
    o1, o2, o3 = MLA_Q_RANK, MLA_Q_RANK + MLA_KV_RANK, MLA_Q_RANK + MLA_KV_RANK + MLA_ROPE
    wq, wkv, wkp, wz = w_in[:, :o1], w_in[:, o1:o2], w_in[:, o2:o3], w_in[:, o3:]
    wkp2 = jnp.concatenate([wkp, _rope_rot_cols(wkp)], -1)
    wqu = w_q_up.reshape(MLA_Q_RANK, MLA_HEADS, MLA_NOPE + MLA_ROPE)
    wqn = wqu[:, :, :MLA_NOPE].reshape(MLA_Q_RANK, MLA_HEADS * MLA_NOPE)
    wqp = wqu[:, :, MLA_NOPE:]
    wqr = _rope_rot_cols(wqp).reshape(MLA_Q_RANK, MLA_HEADS * MLA_ROPE)
    wqp = wqp.reshape(MLA_Q_RANK, MLA_HEADS * MLA_ROPE)
    full = lambda a: pl.BlockSpec(a.shape, lambda i: (0,) * a.ndim)
    rope_idx = lambda i: jnp.where(i < NT_P, 0, 1 + (i - NT_P) % NT_S_SEQ)
    rows = lambda n: pl.BlockSpec((TM, n), lambda i: (i, 0))
    tab = lambda n: pl.BlockSpec((TM, n), lambda i: (rope_idx(i), 0))
    weights = [wq.astype(BF16), wkv.astype(BF16), wkp2.astype(BF16), wz.astype(BF16),
               q_norm.reshape(1, -1), kv_norm.reshape(1, -1),
               wqn.astype(BF16), wqp.astype(BF16), wqr.astype(BF16)]
    npe = MLA_HEADS * MLA_ROPE
    return pl.pallas_call(
        _mla_in_kernel,
        grid=(NT,),
        in_specs=[rows(D), _mod_spec(layer)] + [full(a) for a in weights]
                 + [tab(MLA_ROPE), tab(MLA_ROPE), tab(npe), tab(npe)],
        out_specs=[rows(D), rows(npe), rows(MLA_KV_RANK), rows(MLA_ROPE), rows(MLA_ROPE), rows(D)],
        out_shape=[jax.ShapeDtypeStruct((T, D), BF16), jax.ShapeDtypeStruct((T, npe), BF16),
                   jax.ShapeDtypeStruct((T, MLA_KV_RANK), F32), jax.ShapeDtypeStruct((T, MLA_ROPE), F32),
                   jax.ShapeDtypeStruct((T, MLA_ROPE), F32), jax.ShapeDtypeStruct((T, D), F32)],
        compiler_params=_cparams(("arbitrary",)),
        name="mla_in_proj",
    )(x, mod, *weights, c32, s32, c512, s512)


def _mla_attn_kernel(*refs, n_cache, hg):
    if n_cache:
        (qn_ref, qp_ref, ckv_ref, kpr_ref, cckv_ref, ckpe_ref, wk_ref, wvt_ref, sz_ref,
         o_ref, kcat_s, vt_s) = refs
    else:
        qn_ref, qp_ref, ckv_ref, kpr_ref, wk_ref, wvt_ref, sz_ref, o_ref, kcat_s, vt_s = refs

    @pl.when(pl.program_id(2) == 0)
    def _():
        cc = ckv_ref[...].astype(BF16)
        kp = kpr_ref[...].astype(BF16)
        if n_cache:
            cc = jnp.concatenate([cckv_ref[...].astype(BF16), cc], axis=0)
            kp = jnp.concatenate([ckpe_ref[...].astype(BF16), kp], axis=0)
        kn = _dot(cc, wk_ref[...]).astype(BF16)
        for hh in range(hg):
            kcat_s[hh] = jnp.concatenate([kn[:, hh * MLA_NOPE:(hh + 1) * MLA_NOPE], kp], axis=1)
            vt_s[hh] = _dot_nt(wvt_ref[hh], cc).astype(BF16)

    qn_all, qp_all = qn_ref[...], qp_ref[...]

    def scores(hh):
        qcat = jnp.concatenate([qn_all[:, hh * MLA_NOPE:(hh + 1) * MLA_NOPE],
                                qp_all[:, hh * MLA_ROPE:(hh + 1) * MLA_ROPE]], axis=1)
        return _dot_nt(kcat_s[hh], qcat)

    outs = []
    ahead = 2
    pending = [scores(hh) for hh in range(min(ahead, hg))]
    for hh in range(hg):
        s = pending.pop(0)
        if hh + ahead < hg:
            pending.append(scores(hh + ahead))
        p = jnp.exp(s - jnp.max(s, axis=0, keepdims=True))
        l = jnp.sum(p, axis=0, keepdims=True)
        outs.append(_dot(vt_s[hh], p.astype(BF16)) / l)
    o_ref[...] = jnp.concatenate(outs, axis=0).T * sz_ref[...]


def _mla_attention(qno, qpe, ckv, kpr, sz, w_kv_up, cache_ckv, cache_kpe, *, nb, L, row_blk0, hg):
    wkv = w_kv_up.reshape(MLA_KV_RANK, MLA_HEADS, MLA_NOPE + MLA_V)
    wk = wkv[:, :, :MLA_NOPE].reshape(MLA_KV_RANK, D).astype(BF16)
    wvt = wkv[:, :, MLA_NOPE:].transpose(1, 2, 0).astype(BF16)
    n_cache = 0 if cache_ckv is None else cache_ckv.shape[1]
    lk = n_cache + L
    nq = L // TM
    wq = hg * MLA_NOPE
    wp = hg * MLA_ROPE
    qrow = lambda w: pl.BlockSpec((TM, w), lambda b, g, q: (row_blk0 + b * nq + q, g))
    seq = lambda w: pl.BlockSpec((L, w), lambda b, g, q: (row_blk0 * TM // L + b, 0))
    in_specs = [qrow(wq), qrow(wp), seq(MLA_KV_RANK), seq(MLA_ROPE)]
    args = [qno, qpe, ckv, kpr]
    if n_cache:
        in_specs += [pl.BlockSpec((None, n_cache, MLA_KV_RANK), lambda b, g, q: (b, 0, 0)),
                     pl.BlockSpec((None, n_cache, MLA_ROPE), lambda b, g, q: (b, 0, 0))]
        args += [cache_ckv, cache_kpe]
    in_specs += [pl.BlockSpec((MLA_KV_RANK, wq), lambda b, g, q: (0, g)),
                 pl.BlockSpec((hg, MLA_V, MLA_KV_RANK), lambda b, g, q: (g, 0, 0)), qrow(wq)]
    args += [wk, wvt, sz]
    return pl.pallas_call(
        functools.partial(_mla_attn_kernel, n_cache=n_cache, hg=hg),
        grid=(nb, MLA_HEADS // hg, nq),
        in_specs=in_specs,
        out_specs=pl.BlockSpec((TM, wq), lambda b, g, q: (b * nq + q, g)),
        out_shape=jax.ShapeDtypeStruct((nb * L, D), F32),
        scratch_shapes=[pltpu.VMEM((hg, lk, MLA_NOPE + MLA_ROPE), BF16), pltpu.VMEM((hg, MLA_V, lk), BF16)],
        compiler_params=_cparams(("arbitrary", "arbitrary", "arbitrary")),
        name="mla_attention",
    )(*args)


def _mla_layer(x, mod, layer, rope, ln_g, ln_b, cache_ckv, cache_kpe, w_in, q_norm, kv_norm, w_q_up, w_kv_up, w_out):
    qno, qpe, ckv, kpe, kpr, sz = _mla_in_proj(x, mod, layer, w_in, q_norm, kv_norm, w_q_up, rope)
    g_p = _mla_attention(qno, qpe, ckv, kpr, sz, w_kv_up, None, None, nb=B_P, L=L_P, row_blk0=0,
                         hg=MLA_HEADS)
    g_s = _mla_attention(qno, qpe, ckv, kpr, sz, w_kv_up, cache_ckv, cache_kpe, nb=B_S, L=L_S, row_blk0=NT_P,
                         hg=MLA_HEADS // 2)
    x_new = _out_proj_ln2(g_p, g_s, w_out, x, mod, layer, ln_g, ln_b)
    new_ckv = ckv[:T_P].reshape(B_P, 1, L_P, MLA_KV_RANK)
    new_kpe = kpe[:T_P].reshape(B_P, 1, L_P, MLA_ROPE)
    return x_new, new_ckv, new_kpe


def _rw_in_kernel(x_ref, xp_ref, xn_ref, m_ref, mu_ref, wr_ref, wk_ref, wv_ref, wg_ref, w1_ref, a1_ref,
                  w2_ref, a2_ref, w0_ref, a0_ref, kk_ref, ka_ref, rk_ref, ones_ref,
                  r_ref, v_ref, sz_ref, nkk_ref, bonus_ref, lw_ref, kd_ref, bd_ref):
    i = pl.program_id(0)
    has_prev, has_next = _tile_has_neighbours(i)
    m = m_ref[...]
    h = _modulate(x_ref[...], m)
    prev, nxt = _neighbour_rows(h, _modulate(xp_ref[...], m), _modulate(xn_ref[...], m), has_prev, has_next)
    d = 0.5 * (prev + nxt) - h
    mu = mu_ref[...]

    def mix(p):
        return (h + d * mu[p:p + 1]).astype(BF16)

    r = _dot(mix(0), wr_ref[...])
    tw = jnp.tanh(_dot(mix(1), w1_ref[...])).astype(BF16)
    k = _dot(mix(2), wk_ref[...])
    v = _dot(mix(3), wv_ref[...])
    ta = _dot(mix(4), a1_ref[...]).astype(BF16)
    z = _dot(mix(5), wg_ref[...])
    r_ref[...] = r.astype(r_ref.dtype)
    v_ref[...] = v.astype(v_ref.dtype)
    sz_ref[...] = _silu(z)
    ones_bd = ones_ref[...]
    kk = k * kk_ref[...]
    kk = kk * lax.rsqrt(_head_sum(kk * kk, ones_bd) + 1e-12)
    nkk_ref[...] = (-kk).astype(nkk_ref.dtype)
    coef = jnp.zeros_like(r)
    for n in range(2):
        wl = w0_ref[n:n + 1, :] + _dot(tw, w2_ref[n])
        lw_ref[n] = -math.exp(-0.5) * _sigmoid(wl)
        a = _sigmoid(a0_ref[n:n + 1, :] + _dot(ta, a2_ref[n]))
        kd = k * (1.0 + (a - 1.0) * ka_ref[...])
        kd_ref[n] = kd.astype(kd_ref.dtype)
        bd_ref[n] = (kk * a).astype(bd_ref.dtype)
        coef = coef + r * kd * rk_ref[...]
    bonus_ref[...] = _head_sum(coef, ones_bd) * v


def _pad_lora_up(w):
    z = jnp.zeros_like(w[0])
    return jnp.stack([jnp.concatenate([w[0], z], 0), jnp.concatenate([z, w[1]], 0)])


def _head_ones():
    h = np.arange(LANES) // RW_N
    return jnp.asarray(h[:, None] == h[None, :], dtype=BF16)


def _rw_in_proj(x, mod, layer, ones_bd, mu, w_in, w0, w1, w2, a0, a1, a2, k_k, k_a, r_k):
    mu8 = jnp.pad(mu, ((0, 2), (0, 0)))
    w1c = jnp.concatenate([w1[0], w1[1]], -1).astype(BF16)
    a1c = jnp.concatenate([a1[0], a1[1]], -1).astype(BF16)
    w2p = _pad_lora_up(w2).astype(BF16)
    a2p = _pad_lora_up(a2).astype(BF16)
    row = pl.BlockSpec((TM, D), lambda i: (i, 0))
    row2 = pl.BlockSpec((2, TM, D), lambda i: (0, i, 0))
    prev, nxt = _halo_specs(D, lambda i: 0)
    full = lambda a: pl.BlockSpec(a.shape, lambda i: (0,) * a.ndim)
    consts = [mu8, w_in[0].astype(BF16), w_in[1].astype(BF16), w_in[2].astype(BF16), w_in[3].astype(BF16),
              w1c, a1c, w2p, a2p, w0, a0, k_k.reshape(1, D), k_a.reshape(1, D), r_k.reshape(1, D), ones_bd]
    return pl.pallas_call(
        _rw_in_kernel,
        grid=(NT,),
        in_specs=[row, prev, nxt, _mod_spec(layer)] + [full(a) for a in consts],
        out_specs=[row] * 5 + [row2] * 3,
        out_shape=[jax.ShapeDtypeStruct((T, D), dt) for dt in (BF16, BF16, F32, BF16, F32)]
                  + [jax.ShapeDtypeStruct((2, T, D), dt) for dt in (F32, BF16, BF16)],
        compiler_params=_cparams(("arbitrary",)),
        name="rwkv_in_proj",
    )(x, x, x, mod, *consts)


SCAN_NB = 2
SCAN_NU = 2 * SCAN_NB


SCAN_PACK = 4
SCAN_GW = SCAN_PACK * RW_N


def _rw_scan_kernel(*refs):
    rb_refs = refs[:SCAN_NU]
    first_ref = refs[SCAN_NU]
    ins = refs[SCAN_NU + 1:SCAN_NU + 1 + 6 * SCAN_NU]
    s0_ref, y_ref, s_ref = refs[SCAN_NU + 1 + 6 * SCAN_NU:]
    del rb_refs
    step = pl.program_id(0)
    c, gw = CHUNK, SCAN_GW

    @pl.when(first_ref[step] == 1)
    def _():
        s_ref[...] = s0_ref[...]

    rc = lax.broadcasted_iota(jnp.int32, (c, c), 0)
    cc = lax.broadcasted_iota(jnp.int32, (c, c), 1)
    ri = lax.broadcasted_iota(jnp.int32, (c, gw), 0)
    lane = lax.broadcasted_iota(jnp.int32, (c, gw), 1)
    pos = lane & (RW_N - 1)
    ri8 = lax.broadcasted_iota(jnp.int32, (c, 2 * gw), 0)
    pos8 = lax.broadcasted_iota(jnp.int32, (c, 2 * gw), 1) & (RW_N - 1)
    head_of = [(lane >> 6) == hh for hh in range(SCAN_PACK)]
    eye_f = (ri == pos).astype(F32)
    eye = eye_f.astype(BF16)
    rdec = lax.broadcasted_iota(jnp.int32, (LANES, gw), 0)
    ldec = lax.broadcasted_iota(jnp.int32, (LANES, gw), 1)
    dec_keep = jnp.logical_and((rdec & (ROWS_BF16 - 1)) == (ldec >> 6), rdec < 2 * ROWS_BF16)
    dec_keep_f = dec_keep.astype(F32)
    dec_sum = dec_keep.astype(BF16)
    dec_hi = jnp.logical_and(dec_keep, rdec < ROWS_BF16).astype(BF16)
    dec_lo = dec_sum - dec_hi
    masks = []
    for sgn in (1, -1):
        incl8 = (ri8 - pos8) * sgn >= 0
        strict = (ri - pos) * sgn > 0
        off_masks = []
        for lg in range(6):
            same_2m = (ri >> (lg + 1)) == (pos >> (lg + 1))
            diff_m = (ri >> lg) != (pos >> lg)
            off_masks.append(jnp.logical_and(strict, jnp.logical_and(same_2m, diff_m)))
        masks.append((incl8, strict, off_masks))

    head_bf = [m.astype(BF16) for m in head_of]

    def keep(x, hh):
        return x * head_bf[hh]

    def bdiag(x):
        return jnp.concatenate([keep(x, hh) for hh in range(SCAN_PACK)], axis=0)

    groups = range(D // gw)
    chains = [(q, g) for q in range(SCAN_NU) for g in groups]
    ar, rb, rbe, v4 = {}, {}, {}, {}
    for q in range(SCAN_NU):
        r_ref, v_ref, nkk_ref, lw_ref, kd_ref, bd_ref = ins[6 * q:6 * q + 6]
        bwd = q % 2
        lw = lw_ref[...]
        cum = (rc - cc) * (-1 if bwd else 1) >= 0
        lw_hi = lw.astype(BF16)
        lw_mid, lw_lo = _split(lw - lw_hi.astype(F32))
        cum_b = cum.astype(BF16)
        g_in = _dot(cum_b, lw_hi) + _dot(cum_b, lw_mid) + _dot(cum_b, lw_lo)
        gtot = g_in[0:1, :] if bwd else g_in[c - 1:c, :]
        e_inv = jnp.exp(-g_in)
        e_rem = jnp.exp(gtot - g_in)
        e_tot = jnp.exp(gtot)
        a_t = nkk_ref[...].astype(F32) * jnp.exp(g_in - lw)
        r_t = r_ref[...].astype(F32) * jnp.exp(g_in)
        kd, bd = kd_ref[...].astype(F32), bd_ref[...].astype(F32)
        b_t, k_t, b_e, k_e = bd * e_inv, kd * e_inv, bd * e_rem, kd * e_rem
        v = v_ref[...]
        for g in groups:
            sl = slice(g * gw, (g + 1) * gw)
            ar[q, g] = jnp.concatenate([a_t[:, sl], r_t[:, sl]], axis=0).astype(BF16)
            bt4, kt4 = b_t[:, sl].astype(BF16), k_t[:, sl].astype(BF16)
            rb[q, g] = jnp.concatenate([bdiag(bt4), bdiag(kt4)], axis=0)
            e_hi, e_lo = _split(e_tot[:, sl] * dec_keep_f)
            dec = e_hi * dec_hi + e_lo * dec_lo
            rbe[q, g] = jnp.concatenate([bdiag(b_e[:, sl].astype(BF16)), bdiag(k_e[:, sl].astype(BF16)), dec],
                                        axis=0)
            v4[q, g] = v[:, sl]
    h_old = {ch: s_ref[ch[0], ch[1]] for ch in chains}
    gm = {ch: _dot_nt(ar[ch], rb[ch]) for ch in chains}
    a_s = {ch: _dot(ar[ch], bdiag(h_old[ch].astype(BF16))) for ch in chains}
    bke_t = {ch: _dot_nt(eye, rbe[ch]) for ch in chains}
    lab = {ch: gm[ch][:c, :gw] for ch in chains}
    offs = lambda ch: masks[ch[0] % 2][2]
    tinv = {ch: eye_f + jnp.where(offs(ch)[0], lab[ch], 0.0) for ch in chains}
    for lg in range(1, 6):
        tb = {ch: tinv[ch].astype(BF16) for ch in chains}
        lt = {ch: _dot(jnp.where(offs(ch)[lg], lab[ch], 0.0).astype(BF16), bdiag(tb[ch])) for ch in chains}
        tinv = {ch: tinv[ch] + _dot(tb[ch], bdiag(lt[ch].astype(BF16))) for ch in chains}
    v_bd = {ch: bdiag(v4[ch]) for ch in chains}
    w1 = {ch: a_s[ch][:c] + _dot(jnp.where(masks[ch[0] % 2][1], gm[ch][:c, gw:], 0.0).astype(BF16), v_bd[ch])
          for ch in chains}
    u = {ch: _dot(tinv[ch].astype(BF16), bdiag(w1[ch].astype(BF16))) for ch in chains}
    lhs = {ch: jnp.concatenate([jnp.where(masks[ch[0] % 2][0], gm[ch][c:], 0.0), bke_t[ch][:, :2 * gw]],
                               axis=0).astype(BF16) for ch in chains}
    yu = {ch: _dot(lhs[ch], jnp.concatenate([bdiag(u[ch].astype(BF16)), v_bd[ch]], axis=0)) for ch in chains}
    e_col = {ch: _dot(bke_t[ch][:, 2 * gw:].astype(BF16), dec_sum) for ch in chains}
    for q in range(SCAN_NU):
        y_ref[q] = jnp.concatenate([a_s[q, g][c:] + yu[q, g][:c] for g in groups], axis=1)
    for ch in chains:
        s_ref[ch[0], ch[1]] = h_old[ch] * e_col[ch] + yu[ch][c:]


def _rw_scan_tables():
    rb = [[] for _ in range(SCAN_NU)]
    first = []
    for nb, L, blk0 in ((B_P, L_P, 0), (B_S, L_S, T_P // CHUNK)):
        nc = L // CHUNK
        for grp in range(nb // SCAN_NB):
            for j in range(nc):
                for q in range(SCAN_NU):
                    b = grp * SCAN_NB + q // 2
                    cn = j if q % 2 == 0 else nc - 1 - j
                    rb[q].append(blk0 + b * nc + cn)
                first.append(1 if j == 0 else 0)
    as_i32 = lambda a: jnp.asarray(np.asarray(a, np.int32))
    return [as_i32(a) for a in rb], as_i32(first)


def _rw_scan(r, v, nkk, lw, kd, bd, s0_all):
    rb, first = _rw_scan_tables()
    n_steps = first.shape[0]
    n_groups = s0_all.shape[0] // SCAN_NU
    steps_p = (B_P // SCAN_NB) * (L_P // CHUNK)
    in_specs = []
    args = []
    for q in range(SCAN_NU):
        tok = pl.BlockSpec((CHUNK, D), lambda s, *t, q=q: (t[q][s], 0))
        tok2 = pl.BlockSpec((None, CHUNK, D), lambda s, *t, q=q: (q % 2, t[q][s], 0))
        in_specs += [tok, tok, tok, tok2, tok2, tok2]
        args += [r, v, nkk, lw, kd, bd]
    grp = lambda s: jnp.where(s < steps_p, s // (L_P // CHUNK),
                              B_P // SCAN_NB + (s - steps_p) // (L_S // CHUNK))
    n_hg = RW_H // SCAN_PACK
    st = pl.BlockSpec((None, SCAN_NU, n_hg, RW_N, SCAN_GW), lambda s, *t: (grp(s), 0, 0, 0, 0))
    s0g = (s0_all.reshape(n_groups, SCAN_NU, n_hg, SCAN_PACK, RW_N, RW_N).transpose(0, 1, 2, 4, 3, 5)
           .reshape(n_groups, SCAN_NU, n_hg, RW_N, SCAN_GW))
    y, s_fin = pl.pallas_call(
        _rw_scan_kernel,
        grid_spec=pltpu.PrefetchScalarGridSpec(
            num_scalar_prefetch=SCAN_NU + 1,
            grid=(n_steps,),
            in_specs=in_specs + [st],
            out_specs=[pl.BlockSpec((None, SCAN_NU, CHUNK, D), lambda s, *t: (s, 0, 0, 0)), st],
        ),
        out_shape=[jax.ShapeDtypeStruct((n_steps, SCAN_NU, CHUNK, D), F32),
                   jax.ShapeDtypeStruct(s0g.shape, F32)],
        compiler_params=_cparams(("arbitrary",)),
        name="rwkv_scan",
    )(*rb, first, *args, s0g)
    s_fin = (s_fin.reshape(n_groups, SCAN_NU, n_hg, RW_N, SCAN_PACK, RW_N).transpose(0, 1, 2, 4, 3, 5)
             .reshape(s0_all.shape))
    return y, s_fin


def _scan_out_index(i, k, bwd):
    per_tile = TM // CHUNK
    nc_p, nc_s = L_P // CHUNK, L_S // CHUNK
    steps_p = (B_P // SCAN_NB) * nc_p
    cn_p = k
    b_s = (i - NT_P) // NT_S_SEQ
    cn_s = ((i - NT_P) % NT_S_SEQ) * per_tile + k
    step_p = (i // SCAN_NB) * nc_p + (nc_p - 1 - cn_p if bwd else cn_p)
    step_s = steps_p + (b_s // SCAN_NB) * nc_s + (nc_s - 1 - cn_s if bwd else cn_s)
    q_p = (i % SCAN_NB) * 2 + bwd
    q_s = (b_s % SCAN_NB) * 2 + bwd
    is_p = i < NT_P
    return jnp.where(is_p, step_p, step_s), jnp.where(is_p, q_p, q_s)


def _rw_out_kernel(*refs):
    per_tile = TM // CHUNK
    y_refs = refs[:2 * per_tile]
    bonus_ref, sz_ref, gg_ref, gb_ref, ones_ref, w_ref, x_ref, m_ref, lng_ref, lnb_ref, o_ref = refs[2 * per_tile:]
    ones_bd = ones_ref[...]
    y = jnp.concatenate([y_refs[k][...] + y_refs[per_tile + k][...] for k in range(per_tile)], axis=0)
    mean = _head_sum(y, ones_bd) * (1.0 / RW_N)
    yc = y - mean
    var = _head_sum(yc * yc, ones_bd) * (1.0 / RW_N)
    yn = yc * lax.rsqrt(var + RW_GN_EPS) * gg_ref[...] + gb_ref[...]
    g = (yn + bonus_ref[...]) * sz_ref[...]
    _out_ln_tail(g, w_ref, x_ref, m_ref, lng_ref, lnb_ref, o_ref)


def _rw_out_proj_ln(y, bonus, sz, gn_g, gn_b, ones_bd, w_out, x, mod, layer, ln_g, ln_b):
    row = pl.BlockSpec((TM, D), lambda i: (i, 0))
    vec = pl.BlockSpec((1, D), lambda i: (0, 0))
    mat = pl.BlockSpec((D, D), lambda i: (0, 0))
    y_specs = [pl.BlockSpec((None, None, CHUNK, D), lambda i, k=k, bwd=bwd: _scan_out_index(i, k, bwd) + (0, 0))
               for bwd in (0, 1) for k in range(TM // CHUNK)]
    return pl.pallas_call(
        _rw_out_kernel,
        grid=(NT,),
        in_specs=y_specs + [row, row, vec, vec,
                  pl.BlockSpec((LANES, LANES), lambda i: (0, 0)), mat, row,
                  _mod_spec(layer), vec, vec],
        out_specs=row,
        out_shape=jax.ShapeDtypeStruct((T, D), F32),
        compiler_params=_cparams(("arbitrary",)),
        name="rwkv_out_proj_ln",
    )(*[y] * len(y_specs), bonus, sz, gn_g.reshape(1, D), gn_b.reshape(1, D), ones_bd, w_out.astype(BF16), x, mod,
      ln_g.reshape(1, D), ln_b.reshape(1, D))


def _rwkv_layer(x, mod, layer, ln_g, ln_b, state, mu, w_in, w0, w1, w2, a0, a1, a2, k_k, k_a, r_k, gn_g, gn_b, w_out):
    ones_bd = _head_ones()
    r, v, sz, nkk, bonus, lw, kd, bd = _rw_in_proj(x, mod, layer, ones_bd, mu, w_in, w0, w1, w2, a0, a1, a2,
                                                  k_k, k_a, r_k)
    n_p = B_P * 2
    s0_all = jnp.concatenate([jnp.zeros((n_p, RW_H, RW_N, RW_N), F32),
                              state.astype(F32).reshape(B_S * 2, RW_H, RW_N, RW_N).swapaxes(-1, -2)], 0)
    y, h_fin = _rw_scan(r, v, nkk, lw, kd, bd, s0_all)
    x_new = _rw_out_proj_ln(y, bonus, sz, gn_g, gn_b, ones_bd, w_out, x, mod, layer, ln_g, ln_b)
    new_state = h_fin[:n_p].swapaxes(-1, -2).reshape(B_P, 1, 2, RW_H, RW_N, RW_N)
    return x_new, new_state


def kernel(x_prompt, x_sample, cache_mla_ckv, cache_mla_kpe, state_rwkv, c, c_ctx, mod_w, mod_b, ln_g, ln_b, hy_w_in, hy_conv_w, hy_conv_b, hy_ffn_w1, hy_ffn_b1, hy_ffn_w2, hy_ffn_b2, hy_ffn_w3, hy_ffn_b3, hy_freq, hy_decay, hy_skip, hy_w_out, mla_w_in, mla_q_norm, mla_kv_norm, mla_w_q_up, mla_w_kv_up, mla_w_out, rw_mu, rw_w_in, rw_w0, rw_w1, rw_w2, rw_a0, rw_a1, rw_a2, rw_k_k, rw_k_a, rw_r_k, rw_gn_g, rw_gn_b, rw_w_out):
    x = jnp.concatenate([x_prompt.reshape(T_P, D), x_sample.reshape(T_S, D)], 0)
    cond8 = jnp.concatenate([c_ctx[None, :], c, jnp.zeros((8 - 1 - B_S, D), F32)], 0)
    mod = _modulation_table(cond8, mod_w, mod_b).reshape(DEPTH * 8, 1, 3 * D)
    tabs_p = _dft_tables(L_P)
    tabs_s = _dft_tables(L_S)
    rope = _rope_tables()
    new_ckv = new_kpe = new_state = None
    for i in range(DEPTH):
        kind, j = i % 3, i // 3
        if kind == 0:
            x = _hyena_layer(x, mod, i, tabs_p, tabs_s, ln_g[i], ln_b[i], hy_w_in[j], hy_conv_w[j], hy_conv_b[j],
                             hy_ffn_w1[j], hy_ffn_b1[j], hy_ffn_w2[j], hy_ffn_b2[j], hy_ffn_w3[j], hy_ffn_b3[j],
                             hy_freq[j], hy_decay[j], hy_skip[j], hy_w_out[j], split_out=(i == DEPTH - 1))
        elif kind == 1:
            x, new_ckv, new_kpe = _mla_layer(x, mod, i, rope, ln_g[i], ln_b[i], cache_mla_ckv[:, j],
                                             cache_mla_kpe[:, j], mla_w_in[j], mla_q_norm[j], mla_kv_norm[j],
                                             mla_w_q_up[j], mla_w_kv_up[j], mla_w_out[j])
        else:
            x, new_state = _rwkv_layer(x, mod, i, ln_g[i], ln_b[i], state_rwkv[:, j], rw_mu[j], rw_w_in[j],
                                       rw_w0[j], rw_w1[j], rw_w2[j], rw_a0[j], rw_a1[j], rw_a2[j], rw_k_k[j],
                                       rw_k_a[j], rw_r_k[j], rw_gn_g[j], rw_gn_b[j], rw_w_out[j])
    x_p, x_s = x if isinstance(x, tuple) else (x[:T_P], x[T_P:])
    return (x_p.reshape(B_P, L_P, D), x_s.reshape(B_S, L_S, D), new_ckv, new_kpe, new_state)
```

```python
import functools
import math

import numpy as np
import jax
import jax.numpy as jnp
from jax import lax
from jax.experimental import pallas as pl
from jax.experimental.pallas import tpu as pltpu

F32 = jnp.float32
BF16 = jnp.bfloat16
HIGHEST = lax.Precision.HIGHEST

D = 1024
B_P, L_P = 16, 256
B_S, L_S = 2, 2048
T_P = B_P * L_P
T_S = B_S * L_S
T = T_P + T_S
PAST = 512
DEPTH = 4
DEEPNORM_ALPHA = (2.0 * DEPTH) ** 0.25
LN_EPS = 1e-5
RMS_EPS = 1e-6
HY_BANDS = 16
HY_FFN = 64
MLA_HEADS = 16
MLA_Q_RANK = 256
MLA_KV_RANK = 128
MLA_NOPE = 64
MLA_ROPE = 32
MLA_V = 64
ROPE_BASE = 10000.0
GRID_W = 64
RW_N = 64
RW_H = D // RW_N
RW_LORA = 64
RW_GN_EPS = 64e-5

TM = 256
NT_P = T_P // TM
NT_S_SEQ = L_S // TM
NT = T // TM
HALO = 8
LANES = 128
ROWS_BF16 = 16
CHUNK = 64
DFT_TK = 512
VMEM_LIMIT = 52 * 1024 * 1024


def _cparams(sem):
    return pltpu.CompilerParams(dimension_semantics=sem, vmem_limit_bytes=VMEM_LIMIT)


def _group(i, tm=TM):
    return jnp.where(i < T_P // tm, 0, 1 + (i - T_P // tm) // (L_S // tm))


def _sigmoid(x):
    return 1.0 / (1.0 + jnp.exp(-x))


def _silu(x):
    return x * _sigmoid(x)


def _dot(a, b):
    return jnp.dot(a, b, preferred_element_type=F32)


def _dot_nt(a, b):
    return lax.dot_general(a, b, (((1,), (1,)), ((), ())), preferred_element_type=F32)


def _dot_hi(a, b):
    return jnp.dot(a, b, preferred_element_type=F32, precision=HIGHEST)


def _split(x):
    hi = x.astype(BF16)
    lo = (x - hi.astype(F32)).astype(BF16)
    return hi, lo


def _head_sum(x, ones_bd):
    hi, lo = _split(x)
    lanes = ones_bd.shape[0]
    parts = []
    for g in range(x.shape[1] // lanes):
        sl = slice(g * lanes, (g + 1) * lanes)
        parts.append(_dot(hi[:, sl], ones_bd) + _dot(lo[:, sl], ones_bd))
    return jnp.concatenate(parts, axis=1)


def _modulate(x, m):
    return x * (1.0 + m[:, D:2 * D]) + m[:, :D]


def _layer_norm_rows(y, g, b):
    mu = jnp.mean(y, axis=-1, keepdims=True)
    yc = y - mu
    var = jnp.mean(yc * yc, axis=-1, keepdims=True)
    return yc * lax.rsqrt(var + LN_EPS) * g + b


def _neighbour_rows(cur, prev_halo, next_halo, has_prev, has_next):
    rows = cur.shape[0]
    ridx = lax.broadcasted_iota(jnp.int32, cur.shape, 0)
    pr = jnp.where(has_prev, prev_halo[HALO - 1:HALO, :], 0.0)
    nx = jnp.where(has_next, next_halo[0:1, :], 0.0)
    prev = jnp.where(ridx == 0, pr, pltpu.roll(cur, 1, axis=0))
    nxt = jnp.where(ridx == rows - 1, nx, pltpu.roll(cur, rows - 1, axis=0))
    return prev, nxt


def _tile_has_neighbours(i):
    k = (i - NT_P) % NT_S_SEQ
    is_s = i >= NT_P
    return jnp.logical_and(is_s, k != 0), jnp.logical_and(is_s, k != NT_S_SEQ - 1)


def _halo_specs(width, col_of):
    r = TM // HALO
    prev = pl.BlockSpec((HALO, width), lambda i, *a: (jnp.maximum(i * r - 1, 0), col_of(i, *a)))
    nxt = pl.BlockSpec((HALO, width), lambda i, *a: (jnp.minimum((i + 1) * r, T // HALO - 1), col_of(i, *a)))
    return prev, nxt


def _mod_kernel(c_ref, w_ref, b_ref, o_ref):
    o_ref[...] = _dot_hi(_silu(c_ref[...]), w_ref[...]) + b_ref[...]


def _modulation_table(cond8, mod_w, mod_b):
    tn = 1024
    return pl.pallas_call(
        _mod_kernel,
        grid=(DEPTH, 3 * D // tn),
        in_specs=[pl.BlockSpec((8, D), lambda l, j: (0, 0)),
                  pl.BlockSpec((None, D, tn), lambda l, j: (l, 0, j)),
                  pl.BlockSpec((None, 1, tn), lambda l, j: (l, 0, j))],
        out_specs=pl.BlockSpec((None, 8, tn), lambda l, j: (l, 0, j)),
        out_shape=jax.ShapeDtypeStruct((DEPTH, 8, 3 * D), F32),
        compiler_params=_cparams(("arbitrary", "arbitrary")),
        name="modulation",
    )(cond8, mod_w, mod_b.reshape(DEPTH, 1, 3 * D))


def _mod_spec(layer, tm=TM):
    return pl.BlockSpec((None, 1, 3 * D), lambda i, *a: (layer * 8 + _group(i, tm), 0, 0))


def _out_ln_tail(g, w_ref, x_ref, m_ref, lng_ref, lnb_ref, o_ref):
    mix = _dot(g.astype(BF16), w_ref[...])
    gate = m_ref[...][:, 2 * D:]
    y = DEEPNORM_ALPHA * x_ref[...] + gate * mix
    o_ref[...] = _layer_norm_rows(y, lng_ref[...], lnb_ref[...])


TM_OUT = 512


def _out_ln2_kernel(gp_ref, gs_ref, w_ref, x_ref, m_ref, lng_ref, lnb_ref, o_ref):
    g = jnp.where(pl.program_id(0) < T_P // TM_OUT, gp_ref[...], gs_ref[...])
    _out_ln_tail(g, w_ref, x_ref, m_ref, lng_ref, lnb_ref, o_ref)


def _out_ln1_kernel(g_ref, w_ref, x_ref, m_ref, lng_ref, lnb_ref, o_ref):
    _out_ln_tail(g_ref[...], w_ref, x_ref, m_ref, lng_ref, lnb_ref, o_ref)


def _out_proj_ln1(g, w_out, x, mod, layer, ln_g, ln_b, tile0):
    vec = pl.BlockSpec((1, D), lambda i: (0, 0))
    return pl.pallas_call(
        _out_ln1_kernel,
        grid=(g.shape[0] // TM_OUT,),
        in_specs=[pl.BlockSpec((TM_OUT, D), lambda i: (i, 0)),
                  pl.BlockSpec((D, D), lambda i: (0, 0)),
                  pl.BlockSpec((TM_OUT, D), lambda i: (i + tile0, 0)),
                  pl.BlockSpec((None, 1, 3 * D), lambda i: (layer * 8 + _group(i + tile0, TM_OUT), 0, 0)),
                  vec, vec],
        out_specs=pl.BlockSpec((TM_OUT, D), lambda i: (i, 0)),
        out_shape=jax.ShapeDtypeStruct(g.shape, F32),
        compiler_params=_cparams(("arbitrary",)),
        name="out_proj_ln_part",
    )(g, w_out.astype(BF16), x, mod, ln_g.reshape(1, D), ln_b.reshape(1, D))


def _out_proj_ln2(g_p, g_s, w_out, x, mod, layer, ln_g, ln_b):
    nt_p = T_P // TM_OUT
    row = pl.BlockSpec((TM_OUT, D), lambda i: (i, 0))
    vec = pl.BlockSpec((1, D), lambda i: (0, 0))
    return pl.pallas_call(
        _out_ln2_kernel,
        grid=(T // TM_OUT,),
        in_specs=[pl.BlockSpec((TM_OUT, D), lambda i: (jnp.minimum(i, nt_p - 1), 0)),
                  pl.BlockSpec((TM_OUT, D), lambda i: (jnp.maximum(i - nt_p, 0), 0)),
                  pl.BlockSpec((D, D), lambda i: (0, 0)),
                  row, _mod_spec(layer, TM_OUT), vec, vec],
        out_specs=row,
        out_shape=jax.ShapeDtypeStruct((T, D), F32),
        compiler_params=_cparams(("arbitrary",)),
        name="out_proj_ln",
    )(g_p, g_s, w_out.astype(BF16), x, mod, ln_g.reshape(1, D), ln_b.reshape(1, D))


def _hy_in_conv_kernel(x_ref, xp_ref, xn_ref, m_ref, w_ref, cw_ref, cb_ref, vv_ref, gate_ref, wb_s):
    @pl.when(pl.program_id(0) == 0)
    def _():
        wb_s[...] = w_ref[...].astype(BF16)

    has_prev, has_next = _tile_has_neighbours(pl.program_id(0))
    x_all = jnp.concatenate([xp_ref[...], x_ref[...], xn_ref[...]], axis=0)
    u = _dot(_modulate(x_all, m_ref[...]).astype(BF16), wb_s[...])
    rows = TM + 2 * HALO
    ridx = lax.broadcasted_iota(jnp.int32, (TM, 1), 0)
    no_prev = jnp.logical_and(ridx == 0, jnp.logical_not(has_prev))
    no_next = jnp.logical_and(ridx == TM - 1, jnp.logical_not(has_next))
    cw = cw_ref[...]
    cb = cb_ref[...]

    def conv(grp):
        sl = slice(grp * D, (grp + 1) * D)
        ug = u[:, sl]
        cur = ug[HALO:HALO + TM]
        prev = jnp.where(no_prev, 0.0, pltpu.roll(ug, 1, axis=0)[HALO:HALO + TM])
        nxt = jnp.where(no_next, 0.0, pltpu.roll(ug, rows - 1, axis=0)[HALO:HALO + TM])
        return prev * cw[0:1, sl] + cur * cw[1:2, sl] + nxt * cw[2:3, sl] + cb[:, sl]

    x0, x1, v = conv(0), conv(1), conv(2)
    vv_ref[...] = v * x1
    gate_ref[...] = x0 * _silu(u[HALO:HALO + TM, 3 * D:])


def _hy_in_conv(x, mod, layer, w_in_all, j, conv_w, conv_b):
    row = pl.BlockSpec((TM, D), lambda i: (i, 0))
    prev, nxt = _halo_specs(D, lambda i: 0)
    return pl.pallas_call(
        _hy_in_conv_kernel,
        grid=(NT,),
        in_specs=[row, prev, nxt, _mod_spec(layer),
                  pl.BlockSpec((None,) + w_in_all.shape[1:], lambda i: (j, 0, 0), pipeline_mode=pl.Buffered(1)),
                  pl.BlockSpec((3, 3 * D), lambda i: (0, 0)), pl.BlockSpec((1, 3 * D), lambda i: (0, 0))],
        out_specs=[row, row],
        out_shape=[jax.ShapeDtypeStruct((T, D), F32)] * 2,
        compiler_params=_cparams(("arbitrary",)),
        scratch_shapes=[pltpu.VMEM(w_in_all.shape[1:], BF16)],
        name="hyena_in_proj_conv3",
    )(x, x, x, mod, w_in_all, conv_w, conv_b.reshape(1, 3 * D))


def _hy_filter_kernel(t_ref, bands_ref, wt_ref, wc_ref, ws_ref, b1_ref, w2_ref, b2_ref, w3_ref, b3_ref,
                      f0_ref, f1_ref, dec_ref, hs_ref, hd_ref, nyq_ref, *, L, tr):
    i = pl.program_id(0)
    ridx = lax.broadcasted_iota(jnp.int32, (tr, 1), 0) + i * tr
    pos = ridx.astype(F32)
    t = t_ref[...]
    ang = ((2.0 * math.pi / L) * pos) * bands_ref[...]
    pre = t * wt_ref[...] + _dot_hi(jnp.cos(ang), wc_ref[...]) + _dot_hi(jnp.sin(ang), ws_ref[...])
    hdn = jnp.sin(f0_ref[...] * (pre + b1_ref[...]))
    hdn = jnp.sin(f1_ref[...] * (_dot_hi(hdn, w2_ref[...]) + b2_ref[...]))
    hf = _dot_hi(hdn, w3_ref[...]) + b3_ref[...]
    h = hf * jnp.exp(-t * jnp.abs(dec_ref[...]))
    h0 = h[:, :D]
    h1 = jnp.where(ridx == 0, 0.0, h[:, D:])
    hsum = h0 + h1
    hs_ref[...] = hsum
    hd_ref[...] = h1 - h0
    alt = jnp.where((ridx & 1) == 0, 1.0, -1.0)
    part =jnp.broadcast_to(jnp.sum(alt * hsum, axis=0, keepdims=True), (8, D))

    @pl.when(i == 0)
    def _():
        nyq_ref[...] = part

    @pl.when(i > 0)
    def _():
        nyq_ref[...] += part


def _hy_filter(L, w1, b1, w2, b2, w3, b3, freq, decay):
    tr = 256
    t = jnp.linspace(0.0, 1.0, L, dtype=F32).reshape(L, 1)
    bands = jnp.linspace(1e-4, HY_BANDS - 1, HY_BANDS, dtype=F32)
    bands = jnp.pad(bands, (0, 128 - HY_BANDS)).reshape(1, 128)
    wt = w1[0:1]
    wc = jnp.pad(w1[1:1 + HY_BANDS], ((0, 128 - HY_BANDS), (0, 0)))
    ws = jnp.pad(-w1[1 + HY_BANDS:], ((0, 128 - HY_BANDS), (0, 0)))
    full = lambda shape: pl.BlockSpec(shape, lambda i: (0, 0))
    rows = pl.BlockSpec((tr, D), lambda i: (i, 0))
    return pl.pallas_call(
        functools.partial(_hy_filter_kernel, L=L, tr=tr),
        grid=(L // tr,),
        in_specs=[pl.BlockSpec((tr, 1), lambda i: (i, 0)), full((1, 128)), full((1, HY_FFN)),
                  full((128, HY_FFN)), full((128, HY_FFN)), full((1, HY_FFN)),
                  full((HY_FFN, HY_FFN)), full((1, HY_FFN)), full((HY_FFN, 2 * D)), full((1, 2 * D)),
                  full((1, HY_FFN)), full((1, HY_FFN)), full((1, 2 * D))],
        out_specs=[rows, rows, pl.BlockSpec((8, D), lambda i: (0, 0))],
        out_shape=[jax.ShapeDtypeStruct((L, D), F32), jax.ShapeDtypeStruct((L, D), F32),
                   jax.ShapeDtypeStruct((8, D), F32)],
        compiler_params=_cparams(("arbitrary",)),
        name="hyena_filter",
    )(t, bands, wt, wc, ws, b1.reshape(1, -1), w2, b2.reshape(1, -1), w3, b3.reshape(1, -1),
      freq[0:1], freq[1:2], decay.reshape(1, 2 * D))


def _dft_table_kernel(ca_ref, sa_ref, cb_ref, sb_ref, cat_ref, sat_ref, cbt_ref, sbt_ref,
                      cr_ref, sr_ref, cc_ref, sc_ref, *, n_t1):
    first = pl.program_id(0) == 0
    ca, sa, cb, sb = ca_ref[...], sa_ref[...], cb_ref[...], sb_ref[...]
    cat, sat, cbt, sbt = cat_ref[...], sat_ref[...], cbt_ref[...], sbt_ref[...]
    tk = cb.shape[0]
    row0 = jnp.logical_and(lax.broadcasted_iota(jnp.int32, (tk, LANES), 0) == 0, first)
    alt_r = jnp.where((lax.broadcasted_iota(jnp.int32, (tk, LANES), 1) & 1) == 0, 1.0, -1.0)
    col0 = jnp.logical_and(lax.broadcasted_iota(jnp.int32, (LANES, tk), 1) == 0, first)
    alt_c = jnp.where((lax.broadcasted_iota(jnp.int32, (LANES, tk), 0) & 1) == 0, 1.0, -1.0)
    for t1 in range(n_t1):
        sl = slice(t1 * LANES, (t1 + 1) * LANES)
        a_c, a_s = ca[:, t1:t1 + 1], sa[:, t1:t1 + 1]
        cr_ref[:, sl] = (a_c * cb - a_s * sb).astype(BF16)
        sr_ref[:, sl] = jnp.where(row0, alt_r, a_s * cb + a_c * sb).astype(BF16)
        b_c, b_s = cat[t1:t1 + 1, :], sat[t1:t1 + 1, :]
        cc_ref[sl, :] = (b_c * cbt - b_s * sbt).astype(BF16)
        sc_ref[sl, :] = jnp.where(col0, alt_c, b_s * cbt + b_c * sbt).astype(BF16)


def _dft_tables(L):
    n = 2 * L
    k = jnp.arange(L, dtype=jnp.int32)

    def cs(t):
        ang = ((k[:, None] * t[None, :]) % n).astype(F32) * (2.0 * math.pi / n)
        return jnp.cos(ang), jnp.sin(ang)

    tk = min(L, DFT_TK)
    nk = L // tk
    n_t1 = L // LANES
    ca, sa = cs(jnp.arange(n_t1, dtype=jnp.int32) * LANES)
    cb, sb = cs(jnp.arange(LANES, dtype=jnp.int32))
    by_bin = lambda w: pl.BlockSpec((tk, w), lambda i: (i, 0))
    by_bin_t = lambda w: pl.BlockSpec((w, tk), lambda i: (0, i))
    return tuple(pl.pallas_call(
        functools.partial(_dft_table_kernel, n_t1=n_t1),
        grid=(nk,),
        in_specs=[by_bin(n_t1), by_bin(n_t1), by_bin(LANES), by_bin(LANES),
                  by_bin_t(n_t1), by_bin_t(n_t1), by_bin_t(LANES), by_bin_t(LANES)],
        out_specs=[pl.BlockSpec((None, tk, L), lambda i: (i, 0, 0))] * 2
                  + [pl.BlockSpec((None, L, tk), lambda i: (i, 0, 0))] * 2,
        out_shape=[jax.ShapeDtypeStruct((nk, tk, L), BF16)] * 2 + [jax.ShapeDtypeStruct((nk, L, tk), BF16)] * 2,
        compiler_params=_cparams(("arbitrary",)),
        name="dft_tables",
    )(ca, sa, cb, sb, ca.T, sa.T, cb.T, sb.T))


def _dft_fwd_kernel(c_ref, s_ref, *refs):
    x1_ref, x2_ref = refs[0], refs[-3]
    oc_ref, os_ref = refs[-2:]
    k = pl.program_id(2)
    oc_ref[...] = _dot(c_ref[k], x1_ref[...].astype(BF16))
    os_ref[...] = _dot(s_ref[k], x2_ref[...].astype(BF16))


def _resident(shape):
    return pl.BlockSpec(shape, lambda *_: (0,) * len(shape), pipeline_mode=pl.Buffered(1))


def _dft_fwd(tabs, xs, L, nb, row_blk0):
    c, s_rows = tabs[:2]
    nk, tk, _ = c.shape
    tn = 512
    a_spec = _resident(c.shape)
    x_spec = pl.BlockSpec((L, tn), lambda b, j, k: (row_blk0 + b, j))
    o_spec = pl.BlockSpec((tk, tn), lambda b, j, k: (b * nk + k, j))
    return pl.pallas_call(
        _dft_fwd_kernel,
        grid=(nb, D // tn, nk),
        in_specs=[a_spec] * 2 + [x_spec] * len(xs),
        out_specs=[o_spec, o_spec],
        out_shape=[jax.ShapeDtypeStruct((nb * L, D), F32)] * 2,
        compiler_params=_cparams(("arbitrary", "arbitrary", "arbitrary")),
        name="hyena_dft_fwd",
    )(c, s_rows, *xs)


def _spectrum_product(vc, vs, kre, kim, nyq, first_tile, L):
    bin0 = jnp.logical_and(lax.broadcasted_iota(jnp.int32, vc.shape, 0) == 0, first_tile)
    kim = jnp.where(bin0, nyq, kim)
    inv_n = 1.0 / (2 * L)
    yre = jnp.where(bin0, vc * kre * inv_n, (vc * kre + vs * kim) * (2.0 * inv_n))
    yim = jnp.where(bin0, vs * kim * inv_n, (vs * kre - vc * kim) * (2.0 * inv_n))
    return yre.astype(BF16), yim.astype(BF16)


def _dft_conv_kernel(c_ref, s_ref, st_ref, v_ref, kre_ref, kim_ref, nyq_ref, skip_ref, gate_ref, o_ref, *, L):
    v = v_ref[...]
    vb = v.astype(BF16)
    c = c_ref[0]
    yre, yim = _spectrum_product(_dot(c, vb), _dot(s_ref[0], vb), kre_ref[...], kim_ref[...], nyq_ref[0:1, :],
                                 True, L)
    y = _dot(c, yre) + _dot(st_ref[0], yim)
    o_ref[...] = (y + v * skip_ref[...]) * gate_ref[...]


def _dft_conv(tabs, kre, kim, nyq, vv, skip, gate, L, nb, row_blk0):
    c, s_rows, _, s_cols = tabs
    assert c.shape[0] == 1, "single frequency tile only"
    tn = 512
    seq = lambda b, j: (row_blk0 + b, j)
    spec = pl.BlockSpec((L, tn), lambda b, j: (0, j))
    return pl.pallas_call(
        functools.partial(_dft_conv_kernel, L=L),
        grid=(nb, D // tn),
        in_specs=[_resident(c.shape)] * 3 + [pl.BlockSpec((L, tn), seq), spec, spec,
                                             pl.BlockSpec((8, tn), lambda b, j: (0, j)),
                                             pl.BlockSpec((1, tn), lambda b, j: (0, j)), pl.BlockSpec((L, tn), seq)],
        out_specs=pl.BlockSpec((L, tn), lambda b, j: (b, j)),
        out_shape=jax.ShapeDtypeStruct((nb * L, D), F32),
        compiler_params=_cparams(("arbitrary", "arbitrary")),
        name="hyena_dft_conv_gate",
    )(c, s_rows, s_cols, vv, kre, kim, nyq, skip.reshape(1, D), gate)


def _dft_inv_kernel(c_ref, st_ref, vc_ref, vs_ref, kre_ref, kim_ref, nyq_ref,
                    vv_ref, skip_ref, gate_ref, o_ref, *, L):
    k = pl.program_id(2)
    nk = pl.num_programs(2)
    yre, yim = _spectrum_product(vc_ref[...], vs_ref[...], kre_ref[...], kim_ref[...], nyq_ref[0:1, :], k == 0, L)
    contrib = _dot(c_ref[k], yre) + _dot(st_ref[k], yim)

    @pl.when(k == 0)
    def _():
        o_ref[...] = contrib

    @pl.when(k > 0)
    def _():
        o_ref[...] += contrib

    @pl.when(k == nk - 1)
    def _():
        o_ref[...] = (o_ref[...] + vv_ref[...] * skip_ref[...]) * gate_ref[...]


def _dft_inv(tabs, vc, vs, kre, kim, nyq, vv, skip, gate, L, nb, row_blk0):
    c, s_cols = tabs[2:]
    nk, _, tk = c.shape
    tn = 512 if L <= 512 else 256
    a_spec = _resident(c.shape)
    v_spec = pl.BlockSpec((tk, tn), lambda b, j, k: (b * nk + k, j))
    k_spec = pl.BlockSpec((tk, tn), lambda b, j, k: (k, j))
    row_spec = pl.BlockSpec((L, tn), lambda b, j, k: (row_blk0 + b, j))
    return pl.pallas_call(
        functools.partial(_dft_inv_kernel, L=L),
        grid=(nb, D // tn, nk),
        in_specs=[a_spec] * 2 + [v_spec, v_spec, k_spec, k_spec,
                                 pl.BlockSpec((8, tn), lambda b, j, k: (0, j)),
                                 row_spec, pl.BlockSpec((1, tn), lambda b, j, k: (0, j)), row_spec],
        out_specs=pl.BlockSpec((L, tn), lambda b, j, k: (b, j)),
        out_shape=jax.ShapeDtypeStruct((nb * L, D), F32),
        compiler_params=_cparams(("arbitrary", "arbitrary", "arbitrary")),
        name="hyena_dft_inv_gate",
    )(c, s_cols, vc, vs, kre, kim, nyq, vv, skip.reshape(1, D), gate)


def _hyena_layer(x, mod, layer, tabs_p, tabs_s, ln_g, ln_b, w_in, conv_w, conv_b, w1, b1, w2, b2, w3, b3,
                 freq, decay, skip, w_out, split_out=False):
    vv, gate = _hy_in_conv(x, mod, layer, *w_in, conv_w, conv_b)
    gs = []
    for L, nb, blk0, tabs in ((L_P, B_P, 0, tabs_p), (L_S, B_S, T_P // L_S, tabs_s)):
        hsum, hdiff, nyq = _hy_filter(L, w1, b1, w2, b2, w3, b3, freq, decay)
        kre, kim = _dft_fwd(tabs, (hsum, hdiff), L, 1, 0)
        if L <= DFT_TK:
            gs.append(_dft_conv(tabs, kre, kim, nyq, vv, skip, gate, L, nb, blk0))
        else:
            vc, vs = _dft_fwd(tabs, (vv,), L, nb, blk0)
            gs.append(_dft_inv(tabs, vc, vs, kre, kim, nyq, vv, skip, gate, L, nb, blk0))
    if split_out:
        return tuple(_out_proj_ln1(g, w_out, x, mod, layer, ln_g, ln_b, tile0)
                     for g, tile0 in ((gs[0], 0), (gs[1], T_P // TM_OUT)))
    return _out_proj_ln2(gs[0], gs[1], w_out, x, mod, layer, ln_g, ln_b)


def _rope_tables():
    rows = L_S // GRID_W
    half = MLA_ROPE // 2
    inv = ROPE_BASE ** (-jnp.arange(0, half, 2, dtype=F32) / half)
    r = jnp.repeat(jnp.arange(rows, dtype=F32), GRID_W)
    col = jnp.tile(jnp.arange(GRID_W, dtype=F32), rows)
    ar, ac = r[:, None] * inv, col[:, None] * inv
    ang = jnp.concatenate([ar, ar, ac, ac], -1)
    cos, sin = jnp.cos(ang), jnp.sin(ang)
    cos = jnp.concatenate([jnp.ones((TM, MLA_ROPE), F32), cos], 0)
    sin = jnp.concatenate([jnp.zeros((TM, MLA_ROPE), F32), sin], 0)
    return cos, sin, jnp.tile(cos, (1, MLA_HEADS)), jnp.tile(sin, (1, MLA_HEADS))


def _rope_rot_cols(w):
    idx = np.concatenate([np.arange(8, 16), np.arange(0, 8), np.arange(24, 32), np.arange(16, 24)])
    sign = np.concatenate([-np.ones(8), np.ones(8), -np.ones(8), np.ones(8)]).astype(np.float32)
    return w[..., idx] * sign


def _mla_in_kernel(x_ref, m_ref, wq_ref, wkv_ref, wkp_ref, wz_ref, qn_ref, kvn_ref, wqn_ref, wqp_ref, wqr_ref,
                   c32_ref, s32_ref, c512_ref, s512_ref,
                   qno_ref, qpe_ref, ckv_ref, kpe_ref, kpr_ref, sz_ref):
    h = _modulate(x_ref[...], m_ref[...]).astype(BF16)
    q_c = _dot(h, wq_ref[...])
    kv_c = _dot(h, wkv_ref[...])
    kp2 = _dot(h, wkp_ref[...])
    z = _dot(h, wz_ref[...])

    def rms(v, g):
        return v * lax.rsqrt(jnp.mean(v * v, axis=-1, keepdims=True) + RMS_EPS) * g

    qn = rms(q_c, qn_ref[...]).astype(BF16)
    scale = (MLA_NOPE + MLA_ROPE) ** -0.5
    qno_ref[...] = (_dot(qn, wqn_ref[...]) * scale).astype(BF16)
    q_pe = _dot(qn, wqp_ref[...]) * c512_ref[...] + _dot(qn, wqr_ref[...]) * s512_ref[...]
    qpe_ref[...] = (q_pe * scale).astype(BF16)
    ckv_ref[...] = rms(kv_c, kvn_ref[...])
    kpe = kp2[:, :MLA_ROPE]
    kpe_ref[...] = kpe
    kpr_ref[...] = kpe * c32_ref[...] + kp2[:, MLA_ROPE:] * s32_ref[...]
    sz_ref[...] = _silu(z)


def _mla_in_proj(x, mod, layer, w_in, q_norm, kv_norm, w_q_up, rope):
    c32, s32, c512, s512 = rope
    o1, o2, o3 = MLA_Q_RANK, MLA_Q_RANK + MLA_KV_RANK, MLA_Q_RANK + MLA_KV_RANK + MLA_ROPE
    wq, wkv, wkp, wz = w_in[:, :o1], w_in[:, o1:o2], w_in[:, o2:o3], w_in[:, o3:]
    wkp2 = jnp.concatenate([wkp, _rope_rot_cols(wkp)], -1)
    wqu = w_q_up.reshape(MLA_Q_RANK, MLA_HEADS, MLA_NOPE + MLA_ROPE)
    wqn = wqu[:, :, :MLA_NOPE].reshape(MLA_Q_RANK, MLA_HEADS * MLA_NOPE)
    wqp = wqu[:, :, MLA_NOPE:]
    wqr = _rope_rot_cols(wqp).reshape(MLA_Q_RANK, MLA_HEADS * MLA_ROPE)
    wqp = wqp.reshape(MLA_Q_RANK, MLA_HEADS * MLA_ROPE)
    full = lambda a: pl.BlockSpec(a.shape, lambda i: (0,) * a.ndim)
    rope_idx = lambda i: jnp.where(i < NT_P, 0, 1 + (i - NT_P) % NT_S_SEQ)
    rows = lambda n: pl.BlockSpec((TM, n), lambda i: (i, 0))
    tab = lambda n: pl.BlockSpec((TM, n), lambda i: (rope_idx(i), 0))
    weights = [wq.astype(BF16), wkv.astype(BF16), wkp2.astype(BF16), wz.astype(BF16),
               q_norm.reshape(1, -1), kv_norm.reshape(1, -1),
               wqn.astype(BF16), wqp.astype(BF16), wqr.astype(BF16)]
    npe = MLA_HEADS * MLA_ROPE
    return pl.pallas_call(
        _mla_in_kernel,
        grid=(NT,),
        in_specs=[rows(D), _mod_spec(layer)] + [full(a) for a in weights]
                 + [tab(MLA_ROPE), tab(MLA_ROPE), tab(npe), tab(npe)],
        out_specs=[rows(D), rows(npe), rows(MLA_KV_RANK), rows(MLA_ROPE), rows(MLA_ROPE), rows(D)],
        out_shape=[jax.ShapeDtypeStruct((T, D), BF16), jax.ShapeDtypeStruct((T, npe), BF16),
                   jax.ShapeDtypeStruct((T, MLA_KV_RANK), F32), jax.ShapeDtypeStruct((T, MLA_ROPE), F32),
                   jax.ShapeDtypeStruct((T, MLA_ROPE), F32), jax.ShapeDtypeStruct((T, D), F32)],
        compiler_params=_cparams(("arbitrary",)),
        name="mla_in_proj",
    )(x, mod, *weights, c32, s32, c512, s512)


def _mla_attn_kernel(*refs, n_cache, hg):
    if n_cache:
        (qn_ref, qp_ref, ckv_ref, kpr_ref, cckv_ref, ckpe_ref, wk_ref, wvt_ref, sz_ref,
         o_ref, kcat_s, vt_s) = refs
    else:
        qn_ref, qp_ref, ckv_ref, kpr_ref, wk_ref, wvt_ref, sz_ref, o_ref, kcat_s, vt_s = refs

    @pl.when(pl.program_id(2) == 0)
    def _():
        cc = ckv_ref[...].astype(BF16)
        kp = kpr_ref[...].astype(BF16)
        if n_cache:
            cc = jnp.concatenate([cckv_ref[...].astype(BF16), cc], axis=0)
            kp = jnp.concatenate([ckpe_ref[...].astype(BF16), kp], axis=0)
        kn = _dot(cc, wk_ref[...]).astype(BF16)
        for hh in range(hg):
            kcat_s[hh] = jnp.concatenate([kn[:, hh * MLA_NOPE:(hh + 1) * MLA_NOPE], kp], axis=1)
            vt_s[hh] = _dot_nt(wvt_ref[hh], cc).astype(BF16)

    qn_all, qp_all = qn_ref[...], qp_ref[...]

    def scores(hh):
        qcat = jnp.concatenate([qn_all[:, hh * MLA_NOPE:(hh + 1) * MLA_NOPE],
                                qp_all[:, hh * MLA_ROPE:(hh + 1) * MLA_ROPE]], axis=1)
        return _dot_nt(kcat_s[hh], qcat)

    outs = []
    ahead = 2
    pending = [scores(hh) for hh in range(min(ahead, hg))]
    for hh in range(hg):
        s = pending.pop(0)
        if hh + ahead < hg:
            pending.append(scores(hh + ahead))
        p = jnp.exp(s - jnp.max(s, axis=0, keepdims=True))
        l = jnp.sum(p, axis=0, keepdims=True)
        outs.append(_dot(vt_s[hh], p.astype(BF16)) / l)
    o_ref[...] = jnp.concatenate(outs, axis=0).T * sz_ref[...]


def _mla_attention(qno, qpe, ckv, kpr, sz, w_kv_up, cache_ckv, cache_kpe, *, nb, L, row_blk0, hg):
    wkv = w_kv_up.reshape(MLA_KV_RANK, MLA_HEADS, MLA_NOPE + MLA_V)
    wk = wkv[:, :, :MLA_NOPE].reshape(MLA_KV_RANK, D).astype(BF16)
    wvt = wkv[:, :, MLA_NOPE:].transpose(1, 2, 0).astype(BF16)
    n_cache = 0 if cache_ckv is None else cache_ckv.shape[1]
    lk = n_cache + L
    nq = L // TM
    wq = hg * MLA_NOPE
    wp = hg * MLA_ROPE
    qrow = lambda w: pl.BlockSpec((TM, w), lambda b, g, q: (row_blk0 + b * nq + q, g))
    seq = lambda w: pl.BlockSpec((L, w), lambda b, g, q: (row_blk0 * TM // L + b, 0))
    in_specs = [qrow(wq), qrow(wp), seq(MLA_KV_RANK), seq(MLA_ROPE)]
    args = [qno, qpe, ckv, kpr]
    if n_cache:
        in_specs += [pl.BlockSpec((None, n_cache, MLA_KV_RANK), lambda b, g, q: (b, 0, 0)),
                     pl.BlockSpec((None, n_cache, MLA_ROPE), lambda b, g, q: (b, 0, 0))]
        args += [cache_ckv, cache_kpe]
    in_specs += [pl.BlockSpec((MLA_KV_RANK, wq), lambda b, g, q: (0, g)),
                 pl.BlockSpec((hg, MLA_V, MLA_KV_RANK), lambda b, g, q: (g, 0, 0)), qrow(wq)]
    args += [wk, wvt, sz]
    return pl.pallas_call(
        functools.partial(_mla_attn_kernel, n_cache=n_cache, hg=hg),
        grid=(nb, MLA_HEADS // hg, nq),
        in_specs=in_specs,
        out_specs=pl.BlockSpec((TM, wq), lambda b, g, q: (b * nq + q, g)),
        out_shape=jax.ShapeDtypeStruct((nb * L, D), F32),
        scratch_shapes=[pltpu.VMEM((hg, lk, MLA_NOPE + MLA_ROPE), BF16), pltpu.VMEM((hg, MLA_V, lk), BF16)],
        compiler_params=_cparams(("arbitrary", "arbitrary", "arbitrary")),
        name="mla_attention",
    )(*args)


def _mla_layer(x, mod, layer, rope, ln_g, ln_b, cache_ckv, cache_kpe, w_in, q_norm, kv_norm, w_q_up, w_kv_up, w_out):
    qno, qpe, ckv, kpe, kpr, sz = _mla_in_proj(x, mod, layer, w_in, q_norm, kv_norm, w_q_up, rope)
    g_p = _mla_attention(qno, qpe, ckv, kpr, sz, w_kv_up, None, None, nb=B_P, L=L_P, row_blk0=0,
                         hg=MLA_HEADS)
    g_s = _mla_attention(qno, qpe, ckv, kpr, sz, w_kv_up, cache_ckv, cache_kpe, nb=B_S, L=L_S, row_blk0=NT_P,
                         hg=MLA_HEADS // 2)
    x_new = _out_proj_ln2(g_p, g_s, w_out, x, mod, layer, ln_g, ln_b)
    new_ckv = ckv[:T_P].reshape(B_P, 1, L_P, MLA_KV_RANK)
    new_kpe = kpe[:T_P].reshape(B_P, 1, L_P, MLA_ROPE)
    return x_new, new_ckv, new_kpe


def _rw_in_kernel(x_ref, xp_ref, xn_ref, m_ref, mu_ref, wr_ref, wk_ref, wv_ref, wg_ref, w1_ref, a1_ref,
                  w2_ref, a2_ref, w0_ref, a0_ref, kk_ref, ka_ref, rk_ref, ones_ref,
                  r_ref, v_ref, sz_ref, nkk_ref, bonus_ref, lw_ref, kd_ref, bd_ref):
    i = pl.program_id(0)
    has_prev, has_next = _tile_has_neighbours(i)
    m = m_ref[...]
    h = _modulate(x_ref[...], m)
    prev, nxt = _neighbour_rows(h, _modulate(xp_ref[...], m), _modulate(xn_ref[...], m), has_prev, has_next)
    d = 0.5 * (prev + nxt) - h
    mu = mu_ref[...]

    def mix(p):
        return (h + d * mu[p:p + 1]).astype(BF16)

    r = _dot(mix(0), wr_ref[...])
    tw = jnp.tanh(_dot(mix(1), w1_ref[...])).astype(BF16)
    k = _dot(mix(2), wk_ref[...])
    v = _dot(mix(3), wv_ref[...])
    ta = _dot(mix(4), a1_ref[...]).astype(BF16)
    z = _dot(mix(5), wg_ref[...])
    r_ref[...] = r.astype(r_ref.dtype)
    v_ref[...] = v.astype(v_ref.dtype)
    sz_ref[...] = _silu(z)
    ones_bd = ones_ref[...]
    kk = k * kk_ref[...]
    kk = kk * lax.rsqrt(_head_sum(kk * kk, ones_bd) + 1e-12)
    nkk_ref[...] = (-kk).astype(nkk_ref.dtype)
    coef = jnp.zeros_like(r)
    for n in range(2):
        wl = w0_ref[n:n + 1, :] + _dot(tw, w2_ref[n])
        lw_ref[n] = -math.exp(-0.5) * _sigmoid(wl)
        a = _sigmoid(a0_ref[n:n + 1, :] + _dot(ta, a2_ref[n]))
        kd = k * (1.0 + (a - 1.0) * ka_ref[...])
        kd_ref[n] = kd.astype(kd_ref.dtype)
        bd_ref[n] = (kk * a).astype(bd_ref.dtype)
        coef = coef + r * kd * rk_ref[...]
    bonus_ref[...] = _head_sum(coef, ones_bd) * v


def _pad_lora_up(w):
    z = jnp.zeros_like(w[0])
    return jnp.stack([jnp.concatenate([w[0], z], 0), jnp.concatenate([z, w[1]], 0)])


def _head_ones():
    h = np.arange(LANES) // RW_N
    return jnp.asarray(h[:, None] == h[None, :], dtype=BF16)


def _rw_in_proj(x, mod, layer, ones_bd, mu, w_in, w0, w1, w2, a0, a1, a2, k_k, k_a, r_k):
    mu8 = jnp.pad(mu, ((0, 2), (0, 0)))
    w1c = jnp.concatenate([w1[0], w1[1]], -1).astype(BF16)
    a1c = jnp.concatenate([a1[0], a1[1]], -1).astype(BF16)
    w2p = _pad_lora_up(w2).astype(BF16)
    a2p = _pad_lora_up(a2).astype(BF16)
    row = pl.BlockSpec((TM, D), lambda i: (i, 0))
    row2 = pl.BlockSpec((2, TM, D), lambda i: (0, i, 0))
    prev, nxt = _halo_specs(D, lambda i: 0)
    full = lambda a: pl.BlockSpec(a.shape, lambda i: (0,) * a.ndim)
    consts = [mu8, w_in[0].astype(BF16), w_in[1].astype(BF16), w_in[2].astype(BF16), w_in[3].astype(BF16),
              w1c, a1c, w2p, a2p, w0, a0, k_k.reshape(1, D), k_a.reshape(1, D), r_k.reshape(1, D), ones_bd]
    return pl.pallas_call(
        _rw_in_kernel,
        grid=(NT,),
        in_specs=[row, prev, nxt, _mod_spec(layer)] + [full(a) for a in consts],
        out_specs=[row] * 5 + [row2] * 3,
        out_shape=[jax.ShapeDtypeStruct((T, D), dt) for dt in (BF16, BF16, F32, BF16, F32)]
                  + [jax.ShapeDtypeStruct((2, T, D), dt) for dt in (F32, BF16, BF16)],
        compiler_params=_cparams(("arbitrary",)),
        name="rwkv_in_proj",
    )(x, x, x, mod, *consts)


SCAN_NB = 2
SCAN_NU = 2 * SCAN_NB


SCAN_PACK = 4
SCAN_GW = SCAN_PACK * RW_N


def _rw_scan_kernel(*refs):
    rb_refs = refs[:SCAN_NU]
    first_ref = refs[SCAN_NU]
    ins = refs[SCAN_NU + 1:SCAN_NU + 1 + 6 * SCAN_NU]
    s0_ref, y_ref, s_ref = refs[SCAN_NU + 1 + 6 * SCAN_NU:]
    del rb_refs
    step = pl.program_id(0)
    c, gw = CHUNK, SCAN_GW

    @pl.when(first_ref[step] == 1)
    def _():
        s_ref[...] = s0_ref[...]

    rc = lax.broadcasted_iota(jnp.int32, (c, c), 0)
    cc = lax.broadcasted_iota(jnp.int32, (c, c), 1)
    ri = lax.broadcasted_iota(jnp.int32, (c, gw), 0)
    lane = lax.broadcasted_iota(jnp.int32, (c, gw), 1)
    pos = lane & (RW_N - 1)
    ri8 = lax.broadcasted_iota(jnp.int32, (c, 2 * gw), 0)
    pos8 = lax.broadcasted_iota(jnp.int32, (c, 2 * gw), 1) & (RW_N - 1)
    head_of = [(lane >> 6) == hh for hh in range(SCAN_PACK)]
    eye_f = (ri == pos).astype(F32)
    eye = eye_f.astype(BF16)
    rdec = lax.broadcasted_iota(jnp.int32, (LANES, gw), 0)
    ldec = lax.broadcasted_iota(jnp.int32, (LANES, gw), 1)
    dec_keep = jnp.logical_and((rdec & (ROWS_BF16 - 1)) == (ldec >> 6), rdec < 2 * ROWS_BF16)
    dec_keep_f = dec_keep.astype(F32)
    dec_sum = dec_keep.astype(BF16)
    dec_hi = jnp.logical_and(dec_keep, rdec < ROWS_BF16).astype(BF16)
    dec_lo = dec_sum - dec_hi
    masks = []
    for sgn in (1, -1):
        incl8 = (ri8 - pos8) * sgn >= 0
        strict = (ri - pos) * sgn > 0
        off_masks = []
        for lg in range(6):
            same_2m = (ri >> (lg + 1)) == (pos >> (lg + 1))
            diff_m = (ri >> lg) != (pos >> lg)
            off_masks.append(jnp.logical_and(strict, jnp.logical_and(same_2m, diff_m)))
        masks.append((incl8, strict, off_masks))

    head_bf = [m.astype(BF16) for m in head_of]

    def keep(x, hh):
        return x * head_bf[hh]

    def bdiag(x):
        return jnp.concatenate([keep(x, hh) for hh in range(SCAN_PACK)], axis=0)

    groups = range(D // gw)
    chains = [(q, g) for q in range(SCAN_NU) for g in groups]
    ar, rb, rbe, v4 = {}, {}, {}, {}
    for q in range(SCAN_NU):
        r_ref, v_ref, nkk_ref, lw_ref, kd_ref, bd_ref = ins[6 * q:6 * q + 6]
        bwd = q % 2
        lw = lw_ref[...]
        cum = (rc - cc) * (-1 if bwd else 1) >= 0
        lw_hi = lw.astype(BF16)
        lw_mid, lw_lo = _split(lw - lw_hi.astype(F32))
        cum_b = cum.astype(BF16)
        g_in = _dot(cum_b, lw_hi) + _dot(cum_b, lw_mid) + _dot(cum_b, lw_lo)
        gtot = g_in[0:1, :] if bwd else g_in[c - 1:c, :]
        e_inv = jnp.exp(-g_in)
        e_rem = jnp.exp(gtot - g_in)
        e_tot = jnp.exp(gtot)
        a_t = nkk_ref[...].astype(F32) * jnp.exp(g_in - lw)
        r_t = r_ref[...].astype(F32) * jnp.exp(g_in)
        kd, bd = kd_ref[...].astype(F32), bd_ref[...].astype(F32)
        b_t, k_t, b_e, k_e = bd * e_inv, kd * e_inv, bd * e_rem, kd * e_rem
        v = v_ref[...]
        for g in groups:
            sl = slice(g * gw, (g + 1) * gw)
            ar[q, g] = jnp.concatenate([a_t[:, sl], r_t[:, sl]], axis=0).astype(BF16)
            bt4, kt4 = b_t[:, sl].astype(BF16), k_t[:, sl].astype(BF16)
            rb[q, g] = jnp.concatenate([bdiag(bt4), bdiag(kt4)], axis=0)
            e_hi, e_lo = _split(e_tot[:, sl] * dec_keep_f)
            dec = e_hi * dec_hi + e_lo * dec_lo
            rbe[q, g] = jnp.concatenate([bdiag(b_e[:, sl].astype(BF16)), bdiag(k_e[:, sl].astype(BF16)), dec],
                                        axis=0)
            v4[q, g] = v[:, sl]
    h_old = {ch: s_ref[ch[0], ch[1]] for ch in chains}
    gm = {ch: _dot_nt(ar[ch], rb[ch]) for ch in chains}
    a_s = {ch: _dot(ar[ch], bdiag(h_old[ch].astype(BF16))) for ch in chains}
    bke_t = {ch: _dot_nt(eye, rbe[ch]) for ch in chains}
    lab = {ch: gm[ch][:c, :gw] for ch in chains}
    offs = lambda ch: masks[ch[0] % 2][2]
    tinv = {ch: eye_f + jnp.where(offs(ch)[0], lab[ch], 0.0) for ch in chains}
    for lg in range(1, 6):
        tb = {ch: tinv[ch].astype(BF16) for ch in chains}
        lt = {ch: _dot(jnp.where(offs(ch)[lg], lab[ch], 0.0).astype(BF16), bdiag(tb[ch])) for ch in chains}
        tinv = {ch: tinv[ch] + _dot(tb[ch], bdiag(lt[ch].astype(BF16))) for ch in chains}
    v_bd = {ch: bdiag(v4[ch]) for ch in chains}
    w1 = {ch: a_s[ch][:c] + _dot(jnp.where(masks[ch[0] % 2][1], gm[ch][:c, gw:], 0.0).astype(BF16), v_bd[ch])
          for ch in chains}
    u = {ch: _dot(tinv[ch].astype(BF16), bdiag(w1[ch].astype(BF16))) for ch in chains}
    lhs = {ch: jnp.concatenate([jnp.where(masks[ch[0] % 2][0], gm[ch][c:], 0.0), bke_t[ch][:, :2 * gw]],
                               axis=0).astype(BF16) for ch in chains}
    yu = {ch: _dot(lhs[ch], jnp.concatenate([bdiag(u[ch].astype(BF16)), v_bd[ch]], axis=0)) for ch in chains}
    e_col = {ch: _dot(bke_t[ch][:, 2 * gw:].astype(BF16), dec_sum) for ch in chains}
    for q in range(SCAN_NU):
        y_ref[q] = jnp.concatenate([a_s[q, g][c:] + yu[q, g][:c] for g in groups], axis=1)
    for ch in chains:
        s_ref[ch[0], ch[1]] = h_old[ch] * e_col[ch] + yu[ch][c:]


def _rw_scan_tables():
    rb = [[] for _ in range(SCAN_NU)]
    first = []
    for nb, L, blk0 in ((B_P, L_P, 0), (B_S, L_S, T_P // CHUNK)):
        nc = L // CHUNK
        for grp in range(nb // SCAN_NB):
            for j in range(nc):
                for q in range(SCAN_NU):
                    b = grp * SCAN_NB + q // 2
                    cn = j if q % 2 == 0 else nc - 1 - j
                    rb[q].append(blk0 + b * nc + cn)
                first.append(1 if j == 0 else 0)
    as_i32 = lambda a: jnp.asarray(np.asarray(a, np.int32))
    return [as_i32(a) for a in rb], as_i32(first)


def _pack_states(s):
    n_hg = RW_H // SCAN_PACK
    s = s.reshape(-1, SCAN_NU, n_hg, SCAN_PACK, RW_N, RW_N).transpose(0, 1, 2, 5, 3, 4)
    return s.reshape(-1, SCAN_NU, n_hg, RW_N, SCAN_GW)


def _unpack_states(p):
    n_hg = RW_H // SCAN_PACK
    p = p.reshape(-1, SCAN_NU, n_hg, RW_N, SCAN_PACK, RW_N).transpose(0, 1, 2, 4, 5, 3)
    return p.reshape(-1, RW_H, RW_N, RW_N)


def _rw_scan(r, v, nkk, lw, kd, bd, s0g):
    rb, first = _rw_scan_tables()
    n_steps = first.shape[0]
    steps_p = (B_P // SCAN_NB) * (L_P // CHUNK)
    in_specs = []
    args = []
    for q in range(SCAN_NU):
        tok = pl.BlockSpec((CHUNK, D), lambda s, *t, q=q: (t[q][s], 0))
        tok2 = pl.BlockSpec((None, CHUNK, D), lambda s, *t, q=q: (q % 2, t[q][s], 0))
        in_specs += [tok, tok, tok, tok2, tok2, tok2]
        args += [r, v, nkk, lw, kd, bd]
    grp = lambda s: jnp.where(s < steps_p, s // (L_P // CHUNK),
                              B_P // SCAN_NB + (s - steps_p) // (L_S // CHUNK))
    st = pl.BlockSpec((None,) + s0g.shape[1:], lambda s, *t: (grp(s), 0, 0, 0, 0))
    return pl.pallas_call(
        _rw_scan_kernel,
        grid_spec=pltpu.PrefetchScalarGridSpec(
            num_scalar_prefetch=SCAN_NU + 1,
            grid=(n_steps,),
            in_specs=in_specs + [st],
            out_specs=[pl.BlockSpec((None, SCAN_NU, CHUNK, D), lambda s, *t: (s, 0, 0, 0)), st],
        ),
        out_shape=[jax.ShapeDtypeStruct((n_steps, SCAN_NU, CHUNK, D), F32),
                   jax.ShapeDtypeStruct(s0g.shape, F32)],
        compiler_params=_cparams(("arbitrary",)),
        name="rwkv_scan",
    )(*rb, first, *args, s0g)


def _scan_out_index(i, k, bwd):
    per_tile = TM // CHUNK
    nc_p, nc_s = L_P // CHUNK, L_S // CHUNK
    steps_p = (B_P // SCAN_NB) * nc_p
    cn_p = k
    b_s = (i - NT_P) // NT_S_SEQ
    cn_s = ((i - NT_P) % NT_S_SEQ) * per_tile + k
    step_p = (i // SCAN_NB) * nc_p + (nc_p - 1 - cn_p if bwd else cn_p)
    step_s = steps_p + (b_s // SCAN_NB) * nc_s + (nc_s - 1 - cn_s if bwd else cn_s)
    q_p = (i % SCAN_NB) * 2 + bwd
    q_s = (b_s % SCAN_NB) * 2 + bwd
    is_p = i < NT_P
    return jnp.where(is_p, step_p, step_s), jnp.where(is_p, q_p, q_s)


def _rw_out_kernel(*refs):
    per_tile = TM // CHUNK
    y_refs = refs[:2 * per_tile]
    bonus_ref, sz_ref, gg_ref, gb_ref, ones_ref, w_ref, x_ref, m_ref, lng_ref, lnb_ref, o_ref = refs[2 * per_tile:]
    ones_bd = ones_ref[...]
    y = jnp.concatenate([y_refs[k][...] + y_refs[per_tile + k][...] for k in range(per_tile)], axis=0)
    mean = _head_sum(y, ones_bd) * (1.0 / RW_N)
    yc = y - mean
    var = _head_sum(yc * yc, ones_bd) * (1.0 / RW_N)
    yn = yc * lax.rsqrt(var + RW_GN_EPS) * gg_ref[...] + gb_ref[...]
    g = (yn + bonus_ref[...]) * sz_ref[...]
    _out_ln_tail(g, w_ref, x_ref, m_ref, lng_ref, lnb_ref, o_ref)


def _rw_out_proj_ln(y, bonus, sz, gn_g, gn_b, ones_bd, w_out, x, mod, layer, ln_g, ln_b):
    row = pl.BlockSpec((TM, D), lambda i: (i, 0))
    vec = pl.BlockSpec((1, D), lambda i: (0, 0))
    mat = pl.BlockSpec((D, D), lambda i: (0, 0))
    y_specs = [pl.BlockSpec((None, None, CHUNK, D), lambda i, k=k, bwd=bwd: _scan_out_index(i, k, bwd) + (0, 0))
               for bwd in (0, 1) for k in range(TM // CHUNK)]
    return pl.pallas_call(
        _rw_out_kernel,
        grid=(NT,),
        in_specs=y_specs + [row, row, vec, vec,
                  pl.BlockSpec((LANES, LANES), lambda i: (0, 0)), mat, row,
                  _mod_spec(layer), vec, vec],
        out_specs=row,
        out_shape=jax.ShapeDtypeStruct((T, D), F32),
        compiler_params=_cparams(("arbitrary",)),
        name="rwkv_out_proj_ln",
    )(*[y] * len(y_specs), bonus, sz, gn_g.reshape(1, D), gn_b.reshape(1, D), ones_bd, w_out.astype(BF16), x, mod,
      ln_g.reshape(1, D), ln_b.reshape(1, D))


def _rwkv_layer(x, mod, layer, ln_g, ln_b, state, mu, w_in, w0, w1, w2, a0, a1, a2, k_k, k_a, r_k, gn_g, gn_b, w_out):
    ones_bd = _head_ones()
    r, v, sz, nkk, bonus, lw, kd, bd = _rw_in_proj(x, mod, layer, ones_bd, mu, w_in, w0, w1, w2, a0, a1, a2,
                                                  k_k, k_a, r_k)
    s_given = _pack_states(state.astype(F32).reshape(B_S * 2, RW_H, RW_N, RW_N))
    gp = B_P // SCAN_NB
    s0g = jnp.concatenate([jnp.zeros((gp,) + s_given.shape[1:], F32), s_given], 0)
    y, s_fin = _rw_scan(r, v, nkk, lw, kd, bd, s0g)
    x_new = _rw_out_proj_ln(y, bonus, sz, gn_g, gn_b, ones_bd, w_out, x, mod, layer, ln_g, ln_b)
    new_state = _unpack_states(s_fin[:gp]).reshape(B_P, 1, 2, RW_H, RW_N, RW_N)
    return x_new, new_state


def kernel(x_prompt, x_sample, cache_mla_ckv, cache_mla_kpe, state_rwkv, c, c_ctx, mod_w, mod_b, ln_g, ln_b, hy_w_in, hy_conv_w, hy_conv_b, hy_ffn_w1, hy_ffn_b1, hy_ffn_w2, hy_ffn_b2, hy_ffn_w3, hy_ffn_b3, hy_freq, hy_decay, hy_skip, hy_w_out, mla_w_in, mla_q_norm, mla_kv_norm, mla_w_q_up, mla_w_kv_up, mla_w_out, rw_mu, rw_w_in, rw_w0, rw_w1, rw_w2, rw_a0, rw_a1, rw_a2, rw_k_k, rw_k_a, rw_r_k, rw_gn_g, rw_gn_b, rw_w_out):
    x = jnp.concatenate([x_prompt.reshape(T_P, D), x_sample.reshape(T_S, D)], 0)
    cond8 = jnp.concatenate([c_ctx[None, :], c, jnp.zeros((8 - 1 - B_S, D), F32)], 0)
    mod = _modulation_table(cond8, mod_w, mod_b).reshape(DEPTH * 8, 1, 3 * D)
    tabs_p = _dft_tables(L_P)
    tabs_s = _dft_tables(L_S)
    rope = _rope_tables()
    new_ckv = new_kpe = new_state = None
    for i in range(DEPTH):
        kind, j = i % 3, i // 3
        if kind == 0:
            x = _hyena_layer(x, mod, i, tabs_p, tabs_s, ln_g[i], ln_b[i], (hy_w_in, j), hy_conv_w[j], hy_conv_b[j],
                             hy_ffn_w1[j], hy_ffn_b1[j], hy_ffn_w2[j], hy_ffn_b2[j], hy_ffn_w3[j], hy_ffn_b3[j],
                             hy_freq[j], hy_decay[j], hy_skip[j], hy_w_out[j], split_out=(i == DEPTH - 1))
        elif kind == 1:
            x, new_ckv, new_kpe = _mla_layer(x, mod, i, rope, ln_g[i], ln_b[i], cache_mla_ckv[:, j],
                                             cache_mla_kpe[:, j], mla_w_in[j], mla_q_norm[j], mla_kv_norm[j],
                                             mla_w_q_up[j], mla_w_kv_up[j], mla_w_out[j])
        else:
            x, new_state = _rwkv_layer(x, mod, i, ln_g[i], ln_b[i], state_rwkv[:, j], rw_mu[j], rw_w_in[j],
                                       rw_w0[j], rw_w1[j], rw_w2[j], rw_a0[j], rw_a1[j], rw_a2[j], rw_k_k[j],
                                       rw_k_a[j], rw_r_k[j], rw_gn_g[j], rw_gn_b[j], rw_w_out[j])
    x_p, x_s = x if isinstance(x, tuple) else (x[:T_P], x[T_P:])
    return (x_p.reshape(B_P, L_P, D), x_s.reshape(B_S, L_S, D), new_ckv, new_kpe, new_state)
```

```python
import functools
import math

import numpy as np
import jax
import jax.numpy as jnp
from jax import lax
from jax.experimental import pallas as pl
from jax.experimental.pallas import tpu as pltpu

F32 = jnp.float32
BF16 = jnp.bfloat16
HIGHEST = lax.Precision.HIGHEST

D = 1024
B_P, L_P = 16, 256
B_S, L_S = 2, 2048
T_P = B_P * L_P
T_S = B_S * L_S
T = T_P + T_S
PAST = 512
DEPTH = 4
DEEPNORM_ALPHA = (2.0 * DEPTH) ** 0.25
LN_EPS = 1e-5
RMS_EPS = 1e-6
HY_BANDS = 16
HY_FFN = 64
MLA_HEADS = 16
MLA_Q_RANK = 256
MLA_KV_RANK = 128
MLA_NOPE = 64
MLA_ROPE = 32
MLA_V = 64
ROPE_BASE = 10000.0
GRID_W = 64
RW_N = 64
RW_H = D // RW_N
RW_LORA = 64
RW_GN_EPS = 64e-5

TM = 256
NT_P = T_P // TM
NT_S_SEQ = L_S // TM
NT = T // TM
HALO = 8
LANES = 128
ROWS_BF16 = 16
CHUNK = 64
DFT_TK = 512
VMEM_LIMIT = 52 * 1024 * 1024


def _cparams(sem):
    return pltpu.CompilerParams(dimension_semantics=sem, vmem_limit_bytes=VMEM_LIMIT)


def _group(i, tm=TM):
    return jnp.where(i < T_P // tm, 0, 1 + (i - T_P // tm) // (L_S // tm))


def _sigmoid(x):
    return 1.0 / (1.0 + jnp.exp(-x))


def _silu(x):
    return x * _sigmoid(x)


def _dot(a, b):
    return jnp.dot(a, b, preferred_element_type=F32)


def _dot_nt(a, b):
    return lax.dot_general(a, b, (((1,), (1,)), ((), ())), preferred_element_type=F32)


def _dot_hi(a, b):
    return jnp.dot(a, b, preferred_element_type=F32, precision=HIGHEST)


def _split(x):
    hi = x.astype(BF16)
    lo = (x - hi.astype(F32)).astype(BF16)
    return hi, lo


def _head_sum(x, ones_bd):
    hi, lo = _split(x)
    lanes = ones_bd.shape[0]
    parts = []
    for g in range(x.shape[1] // lanes):
        sl = slice(g * lanes, (g + 1) * lanes)
        parts.append(_dot(hi[:, sl], ones_bd) + _dot(lo[:, sl], ones_bd))
    return jnp.concatenate(parts, axis=1)


def _modulate(x, m):
    return x * (1.0 + m[:, D:2 * D]) + m[:, :D]


def _layer_norm_rows(y, g, b):
    mu = jnp.mean(y, axis=-1, keepdims=True)
    yc = y - mu
    var = jnp.mean(yc * yc, axis=-1, keepdims=True)
    return yc * lax.rsqrt(var + LN_EPS) * g + b


def _neighbour_rows(cur, prev_halo, next_halo, has_prev, has_next):
    rows = cur.shape[0]
    ridx = lax.broadcasted_iota(jnp.int32, cur.shape, 0)
    pr = jnp.where(has_prev, prev_halo[HALO - 1:HALO, :], 0.0)
    nx = jnp.where(has_next, next_halo[0:1, :], 0.0)
    prev = jnp.where(ridx == 0, pr, pltpu.roll(cur, 1, axis=0))
    nxt = jnp.where(ridx == rows - 1, nx, pltpu.roll(cur, rows - 1, axis=0))
    return prev, nxt


def _tile_has_neighbours(i):
    k = (i - NT_P) % NT_S_SEQ
    is_s = i >= NT_P
    return jnp.logical_and(is_s, k != 0), jnp.logical_and(is_s, k != NT_S_SEQ - 1)


def _halo_specs(width, col_of):
    r = TM // HALO
    prev = pl.BlockSpec((HALO, width), lambda i, *a: (jnp.maximum(i * r - 1, 0), col_of(i, *a)))
    nxt = pl.BlockSpec((HALO, width), lambda i, *a: (jnp.minimum((i + 1) * r, T // HALO - 1), col_of(i, *a)))
    return prev, nxt


def _mod_kernel(c_ref, w_ref, b_ref, o_ref):
    o_ref[...] = _dot_hi(_silu(c_ref[...]), w_ref[...]) + b_ref[...]


def _modulation_table(cond8, mod_w, mod_b):
    tn = 1024
    return pl.pallas_call(
        _mod_kernel,
        grid=(DEPTH, 3 * D // tn),
        in_specs=[pl.BlockSpec((8, D), lambda l, j: (0, 0)),
                  pl.BlockSpec((None, D, tn), lambda l, j: (l, 0, j)),
                  pl.BlockSpec((None, 1, tn), lambda l, j: (l, 0, j))],
        out_specs=pl.BlockSpec((None, 8, tn), lambda l, j: (l, 0, j)),
        out_shape=jax.ShapeDtypeStruct((DEPTH, 8, 3 * D), F32),
        compiler_params=_cparams(("arbitrary", "arbitrary")),
        name="modulation",
    )(cond8, mod_w, mod_b.reshape(DEPTH, 1, 3 * D))


def _mod_spec(layer, tm=TM):
    return pl.BlockSpec((None, 1, 3 * D), lambda i, *a: (layer * 8 + _group(i, tm), 0, 0))


def _out_ln_tail(g, w_ref, x_ref, m_ref, lng_ref, lnb_ref, o_ref):
    mix = _dot(g.astype(BF16), w_ref[...])
    gate = m_ref[...][:, 2 * D:]
    y = DEEPNORM_ALPHA * x_ref[...] + gate * mix
    o_ref[...] = _layer_norm_rows(y, lng_ref[...], lnb_ref[...])


TM_OUT = 512


def _out_ln2_kernel(gp_ref, gs_ref, w_ref, x_ref, m_ref, lng_ref, lnb_ref, o_ref):
    g = jnp.where(pl.program_id(0) < T_P // TM_OUT, gp_ref[...], gs_ref[...])
    _out_ln_tail(g, w_ref, x_ref, m_ref, lng_ref, lnb_ref, o_ref)


def _out_ln1_kernel(g_ref, w_ref, x_ref, m_ref, lng_ref, lnb_ref, o_ref):
    _out_ln_tail(g_ref[...], w_ref, x_ref, m_ref, lng_ref, lnb_ref, o_ref)


def _out_proj_ln1(g, w_out, x, mod, layer, ln_g, ln_b, tile0):
    vec = pl.BlockSpec((1, D), lambda i: (0, 0))
    return pl.pallas_call(
        _out_ln1_kernel,
        grid=(g.shape[0] // TM_OUT,),
        in_specs=[pl.BlockSpec((TM_OUT, D), lambda i: (i, 0)),
                  pl.BlockSpec((D, D), lambda i: (0, 0)),
                  pl.BlockSpec((TM_OUT, D), lambda i: (i + tile0, 0)),
                  pl.BlockSpec((None, 1, 3 * D), lambda i: (layer * 8 + _group(i + tile0, TM_OUT), 0, 0)),
                  vec, vec],
        out_specs=pl.BlockSpec((TM_OUT, D), lambda i: (i, 0)),
        out_shape=jax.ShapeDtypeStruct(g.shape, F32),
        compiler_params=_cparams(("arbitrary",)),
        name="out_proj_ln_part",
    )(g, w_out.astype(BF16), x, mod, ln_g.reshape(1, D), ln_b.reshape(1, D))


def _out_proj_ln2(g_p, g_s, w_out, x, mod, layer, ln_g, ln_b):
    nt_p = T_P // TM_OUT
    row = pl.BlockSpec((TM_OUT, D), lambda i: (i, 0))
    vec = pl.BlockSpec((1, D), lambda i: (0, 0))
    return pl.pallas_call(
        _out_ln2_kernel,
        grid=(T // TM_OUT,),
        in_specs=[pl.BlockSpec((TM_OUT, D), lambda i: (jnp.minimum(i, nt_p - 1), 0)),
                  pl.BlockSpec((TM_OUT, D), lambda i: (jnp.maximum(i - nt_p, 0), 0)),
                  pl.BlockSpec((D, D), lambda i: (0, 0)),
                  row, _mod_spec(layer, TM_OUT), vec, vec],
        out_specs=row,
        out_shape=jax.ShapeDtypeStruct((T, D), F32),
        compiler_params=_cparams(("arbitrary",)),
        name="out_proj_ln",
    )(g_p, g_s, w_out.astype(BF16), x, mod, ln_g.reshape(1, D), ln_b.reshape(1, D))


def _hy_in_conv_kernel(x_ref, xp_ref, xn_ref, m_ref, w_ref, cw_ref, cb_ref, vv_ref, gate_ref, wb_s):
    @pl.when(pl.program_id(0) == 0)
    def _():
        wb_s[...] = w_ref[...].astype(BF16)

    has_prev, has_next = _tile_has_neighbours(pl.program_id(0))
    x_all = jnp.concatenate([xp_ref[...], x_ref[...], xn_ref[...]], axis=0)
    u = _dot(_modulate(x_all, m_ref[...]).astype(BF16), wb_s[...])
    rows = TM + 2 * HALO
    ridx = lax.broadcasted_iota(jnp.int32, (TM, 1), 0)
    no_prev = jnp.logical_and(ridx == 0, jnp.logical_not(has_prev))
    no_next = jnp.logical_and(ridx == TM - 1, jnp.logical_not(has_next))
    cw = cw_ref[...]
    cb = cb_ref[...]

    def conv(grp):
        sl = slice(grp * D, (grp + 1) * D)
        ug = u[:, sl]
        cur = ug[HALO:HALO + TM]
        prev = jnp.where(no_prev, 0.0, pltpu.roll(ug, 1, axis=0)[HALO:HALO + TM])
        nxt = jnp.where(no_next, 0.0, pltpu.roll(ug, rows - 1, axis=0)[HALO:HALO + TM])
        return prev * cw[0:1, sl] + cur * cw[1:2, sl] + nxt * cw[2:3, sl] + cb[:, sl]

    x0, x1, v = conv(0), conv(1), conv(2)
    vv_ref[...] = v * x1
    gate_ref[...] = x0 * _silu(u[HALO:HALO + TM, 3 * D:])


def _hy_in_conv(x, mod, layer, w_in_all, j, conv_w, conv_b):
    row = pl.BlockSpec((TM, D), lambda i: (i, 0))
    prev, nxt = _halo_specs(D, lambda i: 0)
    return pl.pallas_call(
        _hy_in_conv_kernel,
        grid=(NT,),
        in_specs=[row, prev, nxt, _mod_spec(layer),
                  pl.BlockSpec((None,) + w_in_all.shape[1:], lambda i: (j, 0, 0), pipeline_mode=pl.Buffered(1)),
                  pl.BlockSpec((3, 3 * D), lambda i: (0, 0)), pl.BlockSpec((1, 3 * D), lambda i: (0, 0))],
        out_specs=[row, row],
        out_shape=[jax.ShapeDtypeStruct((T, D), F32)] * 2,
        compiler_params=_cparams(("arbitrary",)),
        scratch_shapes=[pltpu.VMEM(w_in_all.shape[1:], BF16)],
        name="hyena_in_proj_conv3",
    )(x, x, x, mod, w_in_all, conv_w, conv_b.reshape(1, 3 * D))


def _hy_filter_kernel(t_ref, bands_ref, wt_ref, wc_ref, ws_ref, b1_ref, w2_ref, b2_ref, w3_ref, b3_ref,
                      f0_ref, f1_ref, dec_ref, hs_ref, hd_ref, nyq_ref, *, L, tr):
    i = pl.program_id(0)
    ridx = lax.broadcasted_iota(jnp.int32, (tr, 1), 0) + i * tr
    pos = ridx.astype(F32)
    t = t_ref[...]
    ang = ((2.0 * math.pi / L) * pos) * bands_ref[...]
    pre = t * wt_ref[...] + _dot_hi(jnp.cos(ang), wc_ref[...]) + _dot_hi(jnp.sin(ang), ws_ref[...])
    hdn = jnp.sin(f0_ref[...] * (pre + b1_ref[...]))
    hdn = jnp.sin(f1_ref[...] * (_dot_hi(hdn, w2_ref[...]) + b2_ref[...]))
    hf = _dot_hi(hdn, w3_ref[...]) + b3_ref[...]
    h = hf * jnp.exp(-t * jnp.abs(dec_ref[...]))
    h0 = h[:, :D]
    h1 = jnp.where(ridx == 0, 0.0, h[:, D:])
    hsum = h0 + h1
    hs_ref[...] = hsum
    hd_ref[...] = h1 - h0
    alt = jnp.where((ridx & 1) == 0, 1.0, -1.0)
    part =jnp.broadcast_to(jnp.sum(alt * hsum, axis=0, keepdims=True), (8, D))

    @pl.when(i == 0)
    def _():
        nyq_ref[...] = part

    @pl.when(i > 0)
    def _():
        nyq_ref[...] += part


def _hy_filter(L, w1, b1, w2, b2, w3, b3, freq, decay):
    tr = 256
    t = jnp.linspace(0.0, 1.0, L, dtype=F32).reshape(L, 1)
    bands = jnp.linspace(1e-4, HY_BANDS - 1, HY_BANDS, dtype=F32)
    bands = jnp.pad(bands, (0, 128 - HY_BANDS)).reshape(1, 128)
    wt = w1[0:1]
    wc = jnp.pad(w1[1:1 + HY_BANDS], ((0, 128 - HY_BANDS), (0, 0)))
    ws = jnp.pad(-w1[1 + HY_BANDS:], ((0, 128 - HY_BANDS), (0, 0)))
    full = lambda shape: pl.BlockSpec(shape, lambda i: (0, 0))
    rows = pl.BlockSpec((tr, D), lambda i: (i, 0))
    return pl.pallas_call(
        functools.partial(_hy_filter_kernel, L=L, tr=tr),
        grid=(L // tr,),
        in_specs=[pl.BlockSpec((tr, 1), lambda i: (i, 0)), full((1, 128)), full((1, HY_FFN)),
                  full((128, HY_FFN)), full((128, HY_FFN)), full((1, HY_FFN)),
                  full((HY_FFN, HY_FFN)), full((1, HY_FFN)), full((HY_FFN, 2 * D)), full((1, 2 * D)),
                  full((1, HY_FFN)), full((1, HY_FFN)), full((1, 2 * D))],
        out_specs=[rows, rows, pl.BlockSpec((8, D), lambda i: (0, 0))],
        out_shape=[jax.ShapeDtypeStruct((L, D), F32), jax.ShapeDtypeStruct((L, D), F32),
                   jax.ShapeDtypeStruct((8, D), F32)],
        compiler_params=_cparams(("arbitrary",)),
        name="hyena_filter",
    )(t, bands, wt, wc, ws, b1.reshape(1, -1), w2, b2.reshape(1, -1), w3, b3.reshape(1, -1),
      freq[0:1], freq[1:2], decay.reshape(1, 2 * D))


def _dft_table_kernel(ca_ref, sa_ref, cb_ref, sb_ref, cat_ref, sat_ref, cbt_ref, sbt_ref,
                      cr_ref, sr_ref, cc_ref, sc_ref, *, n_t1):
    first = pl.program_id(0) == 0
    ca, sa, cb, sb = ca_ref[...], sa_ref[...], cb_ref[...], sb_ref[...]
    cat, sat, cbt, sbt = cat_ref[...], sat_ref[...], cbt_ref[...], sbt_ref[...]
    tk = cb.shape[0]
    row0 = jnp.logical_and(lax.broadcasted_iota(jnp.int32, (tk, LANES), 0) == 0, first)
    alt_r = jnp.where((lax.broadcasted_iota(jnp.int32, (tk, LANES), 1) & 1) == 0, 1.0, -1.0)
    col0 = jnp.logical_and(lax.broadcasted_iota(jnp.int32, (LANES, tk), 1) == 0, first)
    alt_c = jnp.where((lax.broadcasted_iota(jnp.int32, (LANES, tk), 0) & 1) == 0, 1.0, -1.0)
    for t1 in range(n_t1):
        sl = slice(t1 * LANES, (t1 + 1) * LANES)
        a_c, a_s = ca[:, t1:t1 + 1], sa[:, t1:t1 + 1]
        cr_ref[:, sl] = (a_c * cb - a_s * sb).astype(BF16)
        sr_ref[:, sl] = jnp.where(row0, alt_r, a_s * cb + a_c * sb).astype(BF16)
        b_c, b_s = cat[t1:t1 + 1, :], sat[t1:t1 + 1, :]
        cc_ref[sl, :] = (b_c * cbt - b_s * sbt).astype(BF16)
        sc_ref[sl, :] = jnp.where(col0, alt_c, b_s * cbt + b_c * sbt).astype(BF16)


def _dft_tables(L):
    n = 2 * L
    k = jnp.arange(L, dtype=jnp.int32)

    def cs(t):
        ang = ((k[:, None] * t[None, :]) % n).astype(F32) * (2.0 * math.pi / n)
        return jnp.cos(ang), jnp.sin(ang)

    tk = min(L, DFT_TK)
    nk = L // tk
    n_t1 = L // LANES
    ca, sa = cs(jnp.arange(n_t1, dtype=jnp.int32) * LANES)
    cb, sb = cs(jnp.arange(LANES, dtype=jnp.int32))
    by_bin = lambda w: pl.BlockSpec((tk, w), lambda i: (i, 0))
    by_bin_t = lambda w: pl.BlockSpec((w, tk), lambda i: (0, i))
    return tuple(pl.pallas_call(
        functools.partial(_dft_table_kernel, n_t1=n_t1),
        grid=(nk,),
        in_specs=[by_bin(n_t1), by_bin(n_t1), by_bin(LANES), by_bin(LANES),
                  by_bin_t(n_t1), by_bin_t(n_t1), by_bin_t(LANES), by_bin_t(LANES)],
        out_specs=[pl.BlockSpec((None, tk, L), lambda i: (i, 0, 0))] * 2
                  + [pl.BlockSpec((None, L, tk), lambda i: (i, 0, 0))] * 2,
        out_shape=[jax.ShapeDtypeStruct((nk, tk, L), BF16)] * 2 + [jax.ShapeDtypeStruct((nk, L, tk), BF16)] * 2,
        compiler_params=_cparams(("arbitrary",)),
        name="dft_tables",
    )(ca, sa, cb, sb, ca.T, sa.T, cb.T, sb.T))


def _dft_fwd_kernel(c_ref, s_ref, *refs):
    x1_ref, x2_ref = refs[0], refs[-3]
    oc_ref, os_ref = refs[-2:]
    k = pl.program_id(2)
    oc_ref[...] = _dot(c_ref[k], x1_ref[...].astype(BF16))
    os_ref[...] = _dot(s_ref[k], x2_ref[...].astype(BF16))


def _resident(shape):
    return pl.BlockSpec(shape, lambda *_: (0,) * len(shape), pipeline_mode=pl.Buffered(1))


def _dft_fwd(tabs, xs, L, nb, row_blk0):
    c, s_rows = tabs[:2]
    nk, tk, _ = c.shape
    tn = 512
    a_spec = _resident(c.shape)
    x_spec = pl.BlockSpec((L, tn), lambda b, j, k: (row_blk0 + b, j))
    o_spec = pl.BlockSpec((tk, tn), lambda b, j, k: (b * nk + k, j))
    return pl.pallas_call(
        _dft_fwd_kernel,
        grid=(nb, D // tn, nk),
        in_specs=[a_spec] * 2 + [x_spec] * len(xs),
        out_specs=[o_spec, o_spec],
        out_shape=[jax.ShapeDtypeStruct((nb * L, D), F32)] * 2,
        compiler_params=_cparams(("arbitrary", "arbitrary", "arbitrary")),
        name="hyena_dft_fwd",
    )(c, s_rows, *xs)


def _spectrum_product(vc, vs, kre, kim, nyq, first_tile, L):
    bin0 = jnp.logical_and(lax.broadcasted_iota(jnp.int32, vc.shape, 0) == 0, first_tile)
    kim = jnp.where(bin0, nyq, kim)
    inv_n = 1.0 / (2 * L)
    yre = jnp.where(bin0, vc * kre * inv_n, (vc * kre + vs * kim) * (2.0 * inv_n))
    yim = jnp.where(bin0, vs * kim * inv_n, (vs * kre - vc * kim) * (2.0 * inv_n))
    return yre.astype(BF16), yim.astype(BF16)


def _dft_conv_kernel(c_ref, s_ref, st_ref, v_ref, kre_ref, kim_ref, nyq_ref, skip_ref, gate_ref, o_ref, *, L):
    v = v_ref[...]
    vb = v.astype(BF16)
    c = c_ref[0]
    yre, yim = _spectrum_product(_dot(c, vb), _dot(s_ref[0], vb), kre_ref[...], kim_ref[...], nyq_ref[0:1, :],
                                 True, L)
    y = _dot(c, yre) + _dot(st_ref[0], yim)
    o_ref[...] = (y + v * skip_ref[...]) * gate_ref[...]


def _dft_conv(tabs, kre, kim, nyq, vv, skip, gate, L, nb, row_blk0):
    c, s_rows, _, s_cols = tabs
    assert c.shape[0] == 1, "single frequency tile only"
    tn = 512
    seq = lambda b, j: (row_blk0 + b, j)
    spec = pl.BlockSpec((L, tn), lambda b, j: (0, j))
    return pl.pallas_call(
        functools.partial(_dft_conv_kernel, L=L),
        grid=(nb, D // tn),
        in_specs=[_resident(c.shape)] * 3 + [pl.BlockSpec((L, tn), seq), spec, spec,
                                             pl.BlockSpec((8, tn), lambda b, j: (0, j)),
                                             pl.BlockSpec((1, tn), lambda b, j: (0, j)), pl.BlockSpec((L, tn), seq)],
        out_specs=pl.BlockSpec((L, tn), lambda b, j: (b, j)),
        out_shape=jax.ShapeDtypeStruct((nb * L, D), F32),
        compiler_params=_cparams(("arbitrary", "arbitrary")),
        name="hyena_dft_conv_gate",
    )(c, s_rows, s_cols, vv, kre, kim, nyq, skip.reshape(1, D), gate)


def _dft_inv_kernel(c_ref, st_ref, vc_ref, vs_ref, kre_ref, kim_ref, nyq_ref,
                    vv_ref, skip_ref, gate_ref, o_ref, *, L):
    k = pl.program_id(2)
    nk = pl.num_programs(2)
    yre, yim = _spectrum_product(vc_ref[...], vs_ref[...], kre_ref[...], kim_ref[...], nyq_ref[0:1, :], k == 0, L)
    contrib = _dot(c_ref[k], yre) + _dot(st_ref[k], yim)

    @pl.when(k == 0)
    def _():
        o_ref[...] = contrib

    @pl.when(k > 0)
    def _():
        o_ref[...] += contrib

    @pl.when(k == nk - 1)
    def _():
        o_ref[...] = (o_ref[...] + vv_ref[...] * skip_ref[...]) * gate_ref[...]


def _dft_inv(tabs, vc, vs, kre, kim, nyq, vv, skip, gate, L, nb, row_blk0):
    c, s_cols = tabs[2:]
    nk, _, tk = c.shape
    tn = 512 if L <= 512 else 256
    a_spec = _resident(c.shape)
    v_spec = pl.BlockSpec((tk, tn), lambda b, j, k: (b * nk + k, j))
    k_spec = pl.BlockSpec((tk, tn), lambda b, j, k: (k, j))
    row_spec = pl.BlockSpec((L, tn), lambda b, j, k: (row_blk0 + b, j))
    return pl.pallas_call(
        functools.partial(_dft_inv_kernel, L=L),
        grid=(nb, D // tn, nk),
        in_specs=[a_spec] * 2 + [v_spec, v_spec, k_spec, k_spec,
                                 pl.BlockSpec((8, tn), lambda b, j, k: (0, j)),
                                 row_spec, pl.BlockSpec((1, tn), lambda b, j, k: (0, j)), row_spec],
        out_specs=pl.BlockSpec((L, tn), lambda b, j, k: (b, j)),
        out_shape=jax.ShapeDtypeStruct((nb * L, D), F32),
        compiler_params=_cparams(("arbitrary", "arbitrary", "arbitrary")),
        name="hyena_dft_inv_gate",
    )(c, s_cols, vc, vs, kre, kim, nyq, vv, skip.reshape(1, D), gate)


def _hyena_layer(x, mod, layer, tabs_p, tabs_s, ln_g, ln_b, w_in, conv_w, conv_b, w1, b1, w2, b2, w3, b3,
                 freq, decay, skip, w_out, split_out=False):
    vv, gate = _hy_in_conv(x, mod, layer, *w_in, conv_w, conv_b)
    gs = []
    for L, nb, blk0, tabs in ((L_P, B_P, 0, tabs_p), (L_S, B_S, T_P // L_S, tabs_s)):
        hsum, hdiff, nyq = _hy_filter(L, w1, b1, w2, b2, w3, b3, freq, decay)
        kre, kim = _dft_fwd(tabs, (hsum, hdiff), L, 1, 0)
        if L <= DFT_TK:
            gs.append(_dft_conv(tabs, kre, kim, nyq, vv, skip, gate, L, nb, blk0))
        else:
            vc, vs = _dft_fwd(tabs, (vv,), L, nb, blk0)
            gs.append(_dft_inv(tabs, vc, vs, kre, kim, nyq, vv, skip, gate, L, nb, blk0))
    if split_out:
        return tuple(_out_proj_ln1(g, w_out, x, mod, layer, ln_g, ln_b, tile0)
                     for g, tile0 in ((gs[0], 0), (gs[1], T_P // TM_OUT)))
    return _out_proj_ln2(gs[0], gs[1], w_out, x, mod, layer, ln_g, ln_b)


def _rope_tables():
    rows = L_S // GRID_W
    half = MLA_ROPE // 2
    inv = ROPE_BASE ** (-jnp.arange(0, half, 2, dtype=F32) / half)
    r = jnp.repeat(jnp.arange(rows, dtype=F32), GRID_W)
    col = jnp.tile(jnp.arange(GRID_W, dtype=F32), rows)
    ar, ac = r[:, None] * inv, col[:, None] * inv
    ang = jnp.concatenate([ar, ar, ac, ac], -1)
    cos, sin = jnp.cos(ang), jnp.sin(ang)
    cos = jnp.concatenate([jnp.ones((TM, MLA_ROPE), F32), cos], 0)
    sin = jnp.concatenate([jnp.zeros((TM, MLA_ROPE), F32), sin], 0)
    return cos, sin, jnp.tile(cos, (1, MLA_HEADS)), jnp.tile(sin, (1, MLA_HEADS))


def _rope_rot_cols(w):
    idx = np.concatenate([np.arange(8, 16), np.arange(0, 8), np.arange(24, 32), np.arange(16, 24)])
    sign = np.concatenate([-np.ones(8), np.ones(8), -np.ones(8), np.ones(8)]).astype(np.float32)
    return w[..., idx] * sign


def _mla_in_kernel(x_ref, m_ref, wq_ref, wkv_ref, wkp_ref, wz_ref, qn_ref, kvn_ref, wqn_ref, wqp_ref, wqr_ref,
                   c32_ref, s32_ref, c512_ref, s512_ref,
                   qno_ref, qpe_ref, ckv_ref, kpe_ref, kpr_ref, sz_ref):
    h = _modulate(x_ref[...], m_ref[...]).astype(BF16)
    q_c = _dot(h, wq_ref[...])
    kv_c = _dot(h, wkv_ref[...])
    kp2 = _dot(h, wkp_ref[...])
    z = _dot(h, wz_ref[...])

    def rms(v, g):
        return v * lax.rsqrt(jnp.mean(v * v, axis=-1, keepdims=True) + RMS_EPS) * g

    qn = rms(q_c, qn_ref[...]).astype(BF16)
    scale = (MLA_NOPE + MLA_ROPE) ** -0.5 * math.log2(math.e)
    qno_ref[...] = (_dot(qn, wqn_ref[...]) * scale).astype(BF16)
    q_pe = _dot(qn, wqp_ref[...]) * c512_ref[...] + _dot(qn, wqr_ref[...]) * s512_ref[...]
    qpe_ref[...] = (q_pe * scale).astype(BF16)
    ckv_ref[...] = rms(kv_c, kvn_ref[...])
    kpe = kp2[:, :MLA_ROPE]
    kpe_ref[...] = kpe
    kpr_ref[...] = kpe * c32_ref[...] + kp2[:, MLA_ROPE:] * s32_ref[...]
    sz_ref[...] = _silu(z)


def _mla_in_proj(x, mod, layer, w_in, q_norm, kv_norm, w_q_up, rope):
    c32, s32, c512, s512 = rope
    o1, o2, o3 = MLA_Q_RANK, MLA_Q_RANK + MLA_KV_RANK, MLA_Q_RANK + MLA_KV_RANK + MLA_ROPE
    wq, wkv, wkp, wz = w_in[:, :o1], w_in[:, o1:o2], w_in[:, o2:o3], w_in[:, o3:]
    wkp2 = jnp.concatenate([wkp, _rope_rot_cols(wkp)], -1)
    wqu = w_q_up.reshape(MLA_Q_RANK, MLA_HEADS, MLA_NOPE + MLA_ROPE)
    wqn = wqu[:, :, :MLA_NOPE].reshape(MLA_Q_RANK, MLA_HEADS * MLA_NOPE)
    wqp = wqu[:, :, MLA_NOPE:]
    wqr = _rope_rot_cols(wqp).reshape(MLA_Q_RANK, MLA_HEADS * MLA_ROPE)
    wqp = wqp.reshape(MLA_Q_RANK, MLA_HEADS * MLA_ROPE)
    full = lambda a: pl.BlockSpec(a.shape, lambda i: (0,) * a.ndim)
    rope_idx = lambda i: jnp.where(i < NT_P, 0, 1 + (i - NT_P) % NT_S_SEQ)
    rows = lambda n: pl.BlockSpec((TM, n), lambda i: (i, 0))
    tab = lambda n: pl.BlockSpec((TM, n), lambda i: (rope_idx(i), 0))
    weights = [wq.astype(BF16), wkv.astype(BF16), wkp2.astype(BF16), wz.astype(BF16),
               q_norm.reshape(1, -1), kv_norm.reshape(1, -1),
               wqn.astype(BF16), wqp.astype(BF16), wqr.astype(BF16)]
    npe = MLA_HEADS * MLA_ROPE
    return pl.pallas_call(
        _mla_in_kernel,
        grid=(NT,),
        in_specs=[rows(D), _mod_spec(layer)] + [full(a) for a in weights]
                 + [tab(MLA_ROPE), tab(MLA_ROPE), tab(npe), tab(npe)],
        out_specs=[rows(D), rows(npe), rows(MLA_KV_RANK), rows(MLA_ROPE), rows(MLA_ROPE), rows(D)],
        out_shape=[jax.ShapeDtypeStruct((T, D), BF16), jax.ShapeDtypeStruct((T, npe), BF16),
                   jax.ShapeDtypeStruct((T, MLA_KV_RANK), F32), jax.ShapeDtypeStruct((T, MLA_ROPE), F32),
                   jax.ShapeDtypeStruct((T, MLA_ROPE), F32), jax.ShapeDtypeStruct((T, D), F32)],
        compiler_params=_cparams(("arbitrary",)),
        name="mla_in_proj",
    )(x, mod, *weights, c32, s32, c512, s512)


def _mla_attn_kernel(*refs, n_cache, hg):
    if n_cache:
        (qn_ref, qp_ref, ckv_ref, kpr_ref, cckv_ref, ckpe_ref, wk_ref, wvt_ref, sz_ref,
         o_ref, kcat_s, vt_s) = refs
    else:
        qn_ref, qp_ref, ckv_ref, kpr_ref, wk_ref, wvt_ref, sz_ref, o_ref, kcat_s, vt_s = refs

    @pl.when(pl.program_id(2) == 0)
    def _():
        cc = ckv_ref[...].astype(BF16)
        kp = kpr_ref[...].astype(BF16)
        if n_cache:
            cc = jnp.concatenate([cckv_ref[...].astype(BF16), cc], axis=0)
            kp = jnp.concatenate([ckpe_ref[...].astype(BF16), kp], axis=0)
        kn = _dot(cc, wk_ref[...]).astype(BF16)
        for hh in range(hg):
            kcat_s[hh] = jnp.concatenate([kn[:, hh * MLA_NOPE:(hh + 1) * MLA_NOPE], kp], axis=1)
            vt_s[hh] = _dot_nt(wvt_ref[hh], cc).astype(BF16)

    qn_all, qp_all = qn_ref[...], qp_ref[...]

    def scores(hh):
        qcat = jnp.concatenate([qn_all[:, hh * MLA_NOPE:(hh + 1) * MLA_NOPE],
                                qp_all[:, hh * MLA_ROPE:(hh + 1) * MLA_ROPE]], axis=1)
        return _dot_nt(kcat_s[hh], qcat)

    outs = []
    ahead = 2
    pending = [scores(hh) for hh in range(min(ahead, hg))]
    for hh in range(hg):
        s = pending.pop(0)
        if hh + ahead < hg:
            pending.append(scores(hh + ahead))
        p = jnp.exp2(s - jnp.max(s, axis=0, keepdims=True))
        l = jnp.sum(p, axis=0, keepdims=True)
        outs.append(_dot(vt_s[hh], p.astype(BF16)) / l)
    o_ref[...] = jnp.concatenate(outs, axis=0).T * sz_ref[...]


def _mla_attention(qno, qpe, ckv, kpr, sz, w_kv_up, cache_ckv, cache_kpe, *, nb, L, row_blk0, hg):
    wkv = w_kv_up.reshape(MLA_KV_RANK, MLA_HEADS, MLA_NOPE + MLA_V)
    wk = wkv[:, :, :MLA_NOPE].reshape(MLA_KV_RANK, D).astype(BF16)
    wvt = wkv[:, :, MLA_NOPE:].transpose(1, 2, 0).astype(BF16)
    n_cache = 0 if cache_ckv is None else cache_ckv.shape[1]
    lk = n_cache + L
    nq = L // TM
    wq = hg * MLA_NOPE
    wp = hg * MLA_ROPE
    qrow = lambda w: pl.BlockSpec((TM, w), lambda b, g, q: (row_blk0 + b * nq + q, g))
    seq = lambda w: pl.BlockSpec((L, w), lambda b, g, q: (row_blk0 * TM // L + b, 0))
    in_specs = [qrow(wq), qrow(wp), seq(MLA_KV_RANK), seq(MLA_ROPE)]
    args = [qno, qpe, ckv, kpr]
    if n_cache:
        in_specs += [pl.BlockSpec((None, n_cache, MLA_KV_RANK), lambda b, g, q: (b, 0, 0)),
                     pl.BlockSpec((None, n_cache, MLA_ROPE), lambda b, g, q: (b, 0, 0))]
        args += [cache_ckv, cache_kpe]
    in_specs += [pl.BlockSpec((MLA_KV_RANK, wq), lambda b, g, q: (0, g)),
                 pl.BlockSpec((hg, MLA_V, MLA_KV_RANK), lambda b, g, q: (g, 0, 0)), qrow(wq)]
    args += [wk, wvt, sz]
    return pl.pallas_call(
        functools.partial(_mla_attn_kernel, n_cache=n_cache, hg=hg),
        grid=(nb, MLA_HEADS // hg, nq),
        in_specs=in_specs,
        out_specs=pl.BlockSpec((TM, wq), lambda b, g, q: (b * nq + q, g)),
        out_shape=jax.ShapeDtypeStruct((nb * L, D), F32),
        scratch_shapes=[pltpu.VMEM((hg, lk, MLA_NOPE + MLA_ROPE), BF16), pltpu.VMEM((hg, MLA_V, lk), BF16)],
        compiler_params=_cparams(("arbitrary", "arbitrary", "arbitrary")),
        name="mla_attention",
    )(*args)


def _mla_layer(x, mod, layer, rope, ln_g, ln_b, cache_ckv, cache_kpe, w_in, q_norm, kv_norm, w_q_up, w_kv_up, w_out):
    qno, qpe, ckv, kpe, kpr, sz = _mla_in_proj(x, mod, layer, w_in, q_norm, kv_norm, w_q_up, rope)
    g_p = _mla_attention(qno, qpe, ckv, kpr, sz, w_kv_up, None, None, nb=B_P, L=L_P, row_blk0=0,
                         hg=MLA_HEADS)
    g_s = _mla_attention(qno, qpe, ckv, kpr, sz, w_kv_up, cache_ckv, cache_kpe, nb=B_S, L=L_S, row_blk0=NT_P,
                         hg=MLA_HEADS // 2)
    x_new = _out_proj_ln2(g_p, g_s, w_out, x, mod, layer, ln_g, ln_b)
    new_ckv = ckv[:T_P].reshape(B_P, 1, L_P, MLA_KV_RANK)
    new_kpe = kpe[:T_P].reshape(B_P, 1, L_P, MLA_ROPE)
    return x_new, new_ckv, new_kpe


def _rw_in_kernel(x_ref, xp_ref, xn_ref, m_ref, mu_ref, wr_ref, wk_ref, wv_ref, wg_ref, w1_ref, a1_ref,
                  w2_ref, a2_ref, w0_ref, a0_ref, kk_ref, ka_ref, rk_ref, ones_ref,
                  r_ref, v_ref, sz_ref, nkk_ref, bonus_ref, lw_ref, kd_ref, bd_ref):
    i = pl.program_id(0)
    has_prev, has_next = _tile_has_neighbours(i)
    m = m_ref[...]
    h = _modulate(x_ref[...], m)
    prev, nxt = _neighbour_rows(h, _modulate(xp_ref[...], m), _modulate(xn_ref[...], m), has_prev, has_next)
    d = 0.5 * (prev + nxt) - h
    mu = mu_ref[...]

    def mix(p):
        return (h + d * mu[p:p + 1]).astype(BF16)

    r = _dot(mix(0), wr_ref[...])
    tw = jnp.tanh(_dot(mix(1), w1_ref[...])).astype(BF16)
    k = _dot(mix(2), wk_ref[...])
    v = _dot(mix(3), wv_ref[...])
    ta = _dot(mix(4), a1_ref[...]).astype(BF16)
    z = _dot(mix(5), wg_ref[...])
    r_ref[...] = r.astype(r_ref.dtype)
    v_ref[...] = v.astype(v_ref.dtype)
    sz_ref[...] = _silu(z)
    ones_bd = ones_ref[...]
    kk = k * kk_ref[...]
    kk = kk * lax.rsqrt(_head_sum(kk * kk, ones_bd) + 1e-12)
    nkk_ref[...] = (-kk).astype(nkk_ref.dtype)
    coef = jnp.zeros_like(r)
    for n in range(2):
        wl = w0_ref[n:n + 1, :] + _dot(tw, w2_ref[n])
        lw_ref[n] = -math.exp(-0.5) * _sigmoid(wl)
        a = _sigmoid(a0_ref[n:n + 1, :] + _dot(ta, a2_ref[n]))
        kd = k * (1.0 + (a - 1.0) * ka_ref[...])
        kd_ref[n] = kd.astype(kd_ref.dtype)
        bd_ref[n] = (kk * a).astype(bd_ref.dtype)
        coef = coef + r * kd * rk_ref[...]
    bonus_ref[...] = _head_sum(coef, ones_bd) * v


def _pad_lora_up(w):
    z = jnp.zeros_like(w[0])
    return jnp.stack([jnp.concatenate([w[0], z], 0), jnp.concatenate([z, w[1]], 0)])


def _head_ones():
    h = np.arange(LANES) // RW_N
    return jnp.asarray(h[:, None] == h[None, :], dtype=BF16)


def _rw_in_proj(x, mod, layer, ones_bd, mu, w_in, w0, w1, w2, a0, a1, a2, k_k, k_a, r_k):
    mu8 = jnp.pad(mu, ((0, 2), (0, 0)))
    w1c = jnp.concatenate([w1[0], w1[1]], -1).astype(BF16)
    a1c = jnp.concatenate([a1[0], a1[1]], -1).astype(BF16)
    w2p = _pad_lora_up(w2).astype(BF16)
    a2p = _pad_lora_up(a2).astype(BF16)
    row = pl.BlockSpec((TM, D), lambda i: (i, 0))
    row2 = pl.BlockSpec((2, TM, D), lambda i: (0, i, 0))
    prev, nxt = _halo_specs(D, lambda i: 0)
    full = lambda a: pl.BlockSpec(a.shape, lambda i: (0,) * a.ndim)
    consts = [mu8, w_in[0].astype(BF16), w_in[1].astype(BF16), w_in[2].astype(BF16), w_in[3].astype(BF16),
              w1c, a1c, w2p, a2p, w0, a0, k_k.reshape(1, D), k_a.reshape(1, D), r_k.reshape(1, D), ones_bd]
    return pl.pallas_call(
        _rw_in_kernel,
        grid=(NT,),
        in_specs=[row, prev, nxt, _mod_spec(layer)] + [full(a) for a in consts],
        out_specs=[row] * 5 + [row2] * 3,
        out_shape=[jax.ShapeDtypeStruct((T, D), dt) for dt in (BF16, BF16, F32, BF16, F32)]
                  + [jax.ShapeDtypeStruct((2, T, D), dt) for dt in (F32, BF16, BF16)],
        compiler_params=_cparams(("arbitrary",)),
        name="rwkv_in_proj",
    )(x, x, x, mod, *consts)


SCAN_NB = 2
SCAN_NU = 2 * SCAN_NB


SCAN_PACK = 4
SCAN_GW = SCAN_PACK * RW_N


def _rw_scan_kernel(*refs):
    rb_refs = refs[:SCAN_NU]
    first_ref = refs[SCAN_NU]
    ins = refs[SCAN_NU + 1:SCAN_NU + 1 + 6 * SCAN_NU]
    s0_ref, y_ref, s_ref = refs[SCAN_NU + 1 + 6 * SCAN_NU:]
    del rb_refs
    step = pl.program_id(0)
    c, gw = CHUNK, SCAN_GW

    @pl.when(first_ref[step] == 1)
    def _():
        s_ref[...] = s0_ref[...]

    rc = lax.broadcasted_iota(jnp.int32, (c, c), 0)
    cc = lax.broadcasted_iota(jnp.int32, (c, c), 1)
    ri = lax.broadcasted_iota(jnp.int32, (c, gw), 0)
    lane = lax.broadcasted_iota(jnp.int32, (c, gw), 1)
    pos = lane & (RW_N - 1)
    ri8 = lax.broadcasted_iota(jnp.int32, (c, 2 * gw), 0)
    pos8 = lax.broadcasted_iota(jnp.int32, (c, 2 * gw), 1) & (RW_N - 1)
    head_of = [(lane >> 6) == hh for hh in range(SCAN_PACK)]
    eye_f = (ri == pos).astype(F32)
    eye = eye_f.astype(BF16)
    rdec = lax.broadcasted_iota(jnp.int32, (LANES, gw), 0)
    ldec = lax.broadcasted_iota(jnp.int32, (LANES, gw), 1)
    dec_keep = jnp.logical_and((rdec & (ROWS_BF16 - 1)) == (ldec >> 6), rdec < 2 * ROWS_BF16)
    dec_keep_f = dec_keep.astype(F32)
    dec_sum = dec_keep.astype(BF16)
    dec_hi = jnp.logical_and(dec_keep, rdec < ROWS_BF16).astype(BF16)
    dec_lo = dec_sum - dec_hi
    masks = []
    for sgn in (1, -1):
        incl8 = (ri8 - pos8) * sgn >= 0
        strict = (ri - pos) * sgn > 0
        off_masks = []
        for lg in range(6):
            same_2m = (ri >> (lg + 1)) == (pos >> (lg + 1))
            diff_m = (ri >> lg) != (pos >> lg)
            off_masks.append(jnp.logical_and(strict, jnp.logical_and(same_2m, diff_m)))
        masks.append((incl8, strict, off_masks))

    head_bf = [m.astype(BF16) for m in head_of]

    def keep(x, hh):
        return x * head_bf[hh]

    def bdiag(x):
        return jnp.concatenate([keep(x, hh) for hh in range(SCAN_PACK)], axis=0)

    groups = range(D // gw)
    chains = [(q, g) for q in range(SCAN_NU) for g in groups]
    ar, rb, rbe, v4 = {}, {}, {}, {}
    for q in range(SCAN_NU):
        r_ref, v_ref, nkk_ref, lw_ref, kd_ref, bd_ref = ins[6 * q:6 * q + 6]
        bwd = q % 2
        lw = lw_ref[...]
        cum = (rc - cc) * (-1 if bwd else 1) >= 0
        lw_hi = lw.astype(BF16)
        lw_mid, lw_lo = _split(lw - lw_hi.astype(F32))
        cum_b = cum.astype(BF16)
        g_in = _dot(cum_b, lw_hi) + _dot(cum_b, lw_mid) + _dot(cum_b, lw_lo)
        gtot = g_in[0:1, :] if bwd else g_in[c - 1:c, :]
        e_inv = jnp.exp(-g_in)
        e_rem = jnp.exp(gtot - g_in)
        e_tot = jnp.exp(gtot)
        a_t = nkk_ref[...].astype(F32) * jnp.exp(g_in - lw)
        r_t = r_ref[...].astype(F32) * jnp.exp(g_in)
        kd, bd = kd_ref[...].astype(F32), bd_ref[...].astype(F32)
        b_t, k_t, b_e, k_e = bd * e_inv, kd * e_inv, bd * e_rem, kd * e_rem
        v = v_ref[...]
        for g in groups:
            sl = slice(g * gw, (g + 1) * gw)
            ar[q, g] = jnp.concatenate([a_t[:, sl], r_t[:, sl]], axis=0).astype(BF16)
            bt4, kt4 = b_t[:, sl].astype(BF16), k_t[:, sl].astype(BF16)
            rb[q, g] = jnp.concatenate([bdiag(bt4), bdiag(kt4)], axis=0)
            e_hi, e_lo = _split(e_tot[:, sl] * dec_keep_f)
            dec = e_hi * dec_hi + e_lo * dec_lo
            rbe[q, g] = jnp.concatenate([bdiag(b_e[:, sl].astype(BF16)), bdiag(k_e[:, sl].astype(BF16)), dec],
                                        axis=0)
            v4[q, g] = v[:, sl]
    h_old = {ch: s_ref[ch[0], ch[1]] for ch in chains}
    gm = {ch: _dot_nt(ar[ch], rb[ch]) for ch in chains}
    a_s = {ch: _dot(ar[ch], bdiag(h_old[ch].astype(BF16))) for ch in chains}
    bke_t = {ch: _dot_nt(eye, rbe[ch]) for ch in chains}
    lab = {ch: gm[ch][:c, :gw] for ch in chains}
    offs = lambda ch: masks[ch[0] % 2][2]
    tinv = {ch: eye_f + jnp.where(offs(ch)[0], lab[ch], 0.0) for ch in chains}
    for lg in range(1, 6):
        tb = {ch: tinv[ch].astype(BF16) for ch in chains}
        lt = {ch: _dot(jnp.where(offs(ch)[lg], lab[ch], 0.0).astype(BF16), bdiag(tb[ch])) for ch in chains}
        tinv = {ch: tinv[ch] + _dot(tb[ch], bdiag(lt[ch].astype(BF16))) for ch in chains}
    v_bd = {ch: bdiag(v4[ch]) for ch in chains}
    w1 = {ch: a_s[ch][:c] + _dot(jnp.where(masks[ch[0] % 2][1], gm[ch][:c, gw:], 0.0).astype(BF16), v_bd[ch])
          for ch in chains}
    u = {ch: _dot(tinv[ch].astype(BF16), bdiag(w1[ch].astype(BF16))) for ch in chains}
    lhs = {ch: jnp.concatenate([jnp.where(masks[ch[0] % 2][0], gm[ch][c:], 0.0), bke_t[ch][:, :2 * gw]],
                               axis=0).astype(BF16) for ch in chains}
    yu = {ch: _dot(lhs[ch], jnp.concatenate([bdiag(u[ch].astype(BF16)), v_bd[ch]], axis=0)) for ch in chains}
    e_col = {ch: _dot(bke_t[ch][:, 2 * gw:].astype(BF16), dec_sum) for ch in chains}
    for q in range(SCAN_NU):
        y_ref[q] = jnp.concatenate([a_s[q, g][c:] + yu[q, g][:c] for g in groups], axis=1)
    for ch in chains:
        s_ref[ch[0], ch[1]] = h_old[ch] * e_col[ch] + yu[ch][c:]


def _rw_scan_tables():
    rb = [[] for _ in range(SCAN_NU)]
    first = []
    for nb, L, blk0 in ((B_P, L_P, 0), (B_S, L_S, T_P // CHUNK)):
        nc = L // CHUNK
        for grp in range(nb // SCAN_NB):
            for j in range(nc):
                for q in range(SCAN_NU):
                    b = grp * SCAN_NB + q // 2
                    cn = j if q % 2 == 0 else nc - 1 - j
                    rb[q].append(blk0 + b * nc + cn)
                first.append(1 if j == 0 else 0)
    as_i32 = lambda a: jnp.asarray(np.asarray(a, np.int32))
    return [as_i32(a) for a in rb], as_i32(first)


def _pack_states(s):
    n_hg = RW_H // SCAN_PACK
    s = s.reshape(-1, SCAN_NU, n_hg, SCAN_PACK, RW_N, RW_N).transpose(0, 1, 2, 5, 3, 4)
    return s.reshape(-1, SCAN_NU, n_hg, RW_N, SCAN_GW)


def _unpack_states(p):
    n_hg = RW_H // SCAN_PACK
    p = p.reshape(-1, SCAN_NU, n_hg, RW_N, SCAN_PACK, RW_N).transpose(0, 1, 2, 4, 5, 3)
    return p.reshape(-1, RW_H, RW_N, RW_N)


def _rw_scan(r, v, nkk, lw, kd, bd, s0g):
    rb, first = _rw_scan_tables()
    n_steps = first.shape[0]
    steps_p = (B_P // SCAN_NB) * (L_P // CHUNK)
    in_specs = []
    args = []
    for q in range(SCAN_NU):
        tok = pl.BlockSpec((CHUNK, D), lambda s, *t, q=q: (t[q][s], 0))
        tok2 = pl.BlockSpec((None, CHUNK, D), lambda s, *t, q=q: (q % 2, t[q][s], 0))
        in_specs += [tok, tok, tok, tok2, tok2, tok2]
        args += [r, v, nkk, lw, kd, bd]
    grp = lambda s: jnp.where(s < steps_p, s // (L_P // CHUNK),
                              B_P // SCAN_NB + (s - steps_p) // (L_S // CHUNK))
    st = pl.BlockSpec((None,) + s0g.shape[1:], lambda s, *t: (grp(s), 0, 0, 0, 0))
    return pl.pallas_call(
        _rw_scan_kernel,
        grid_spec=pltpu.PrefetchScalarGridSpec(
            num_scalar_prefetch=SCAN_NU + 1,
            grid=(n_steps,),
            in_specs=in_specs + [st],
            out_specs=[pl.BlockSpec((None, SCAN_NU, CHUNK, D), lambda s, *t: (s, 0, 0, 0)), st],
        ),
        out_shape=[jax.ShapeDtypeStruct((n_steps, SCAN_NU, CHUNK, D), F32),
                   jax.ShapeDtypeStruct(s0g.shape, F32)],
        compiler_params=_cparams(("arbitrary",)),
        name="rwkv_scan",
    )(*rb, first, *args, s0g)


def _scan_out_index(i, k, bwd):
    per_tile = TM // CHUNK
    nc_p, nc_s = L_P // CHUNK, L_S // CHUNK
    steps_p = (B_P // SCAN_NB) * nc_p
    cn_p = k
    b_s = (i - NT_P) // NT_S_SEQ
    cn_s = ((i - NT_P) % NT_S_SEQ) * per_tile + k
    step_p = (i // SCAN_NB) * nc_p + (nc_p - 1 - cn_p if bwd else cn_p)
    step_s = steps_p + (b_s // SCAN_NB) * nc_s + (nc_s - 1 - cn_s if bwd else cn_s)
    q_p = (i % SCAN_NB) * 2 + bwd
    q_s = (b_s % SCAN_NB) * 2 + bwd
    is_p = i < NT_P
    return jnp.where(is_p, step_p, step_s), jnp.where(is_p, q_p, q_s)


def _rw_out_kernel(*refs):
    per_tile = TM // CHUNK
    y_refs = refs[:2 * per_tile]
    bonus_ref, sz_ref, gg_ref, gb_ref, ones_ref, w_ref, x_ref, m_ref, lng_ref, lnb_ref, o_ref = refs[2 * per_tile:]
    ones_bd = ones_ref[...]
    y = jnp.concatenate([y_refs[k][...] + y_refs[per_tile + k][...] for k in range(per_tile)], axis=0)
    mean = _head_sum(y, ones_bd) * (1.0 / RW_N)
    yc = y - mean
    var = _head_sum(yc * yc, ones_bd) * (1.0 / RW_N)
    yn = yc * lax.rsqrt(var + RW_GN_EPS) * gg_ref[...] + gb_ref[...]
    g = (yn + bonus_ref[...]) * sz_ref[...]
    _out_ln_tail(g, w_ref, x_ref, m_ref, lng_ref, lnb_ref, o_ref)


def _rw_out_proj_ln(y, bonus, sz, gn_g, gn_b, ones_bd, w_out, x, mod, layer, ln_g, ln_b):
    row = pl.BlockSpec((TM, D), lambda i: (i, 0))
    vec = pl.BlockSpec((1, D), lambda i: (0, 0))
    mat = pl.BlockSpec((D, D), lambda i: (0, 0))
    y_specs = [pl.BlockSpec((None, None, CHUNK, D), lambda i, k=k, bwd=bwd: _scan_out_index(i, k, bwd) + (0, 0))
               for bwd in (0, 1) for k in range(TM // CHUNK)]
    return pl.pallas_call(
        _rw_out_kernel,
        grid=(NT,),
        in_specs=y_specs + [row, row, vec, vec,
                  pl.BlockSpec((LANES, LANES), lambda i: (0, 0)), mat, row,
                  _mod_spec(layer), vec, vec],
        out_specs=row,
        out_shape=jax.ShapeDtypeStruct((T, D), F32),
        compiler_params=_cparams(("arbitrary",)),
        name="rwkv_out_proj_ln",
    )(*[y] * len(y_specs), bonus, sz, gn_g.reshape(1, D), gn_b.reshape(1, D), ones_bd, w_out.astype(BF16), x, mod,
      ln_g.reshape(1, D), ln_b.reshape(1, D))


def _rwkv_layer(x, mod, layer, ln_g, ln_b, state, mu, w_in, w0, w1, w2, a0, a1, a2, k_k, k_a, r_k, gn_g, gn_b, w_out):
    ones_bd = _head_ones()
    r, v, sz, nkk, bonus, lw, kd, bd = _rw_in_proj(x, mod, layer, ones_bd, mu, w_in, w0, w1, w2, a0, a1, a2,
                                                  k_k, k_a, r_k)
    s_given = _pack_states(state.astype(F32).reshape(B_S * 2, RW_H, RW_N, RW_N))
    gp = B_P // SCAN_NB
    s0g = jnp.concatenate([jnp.zeros((gp,) + s_given.shape[1:], F32), s_given], 0)
    y, s_fin = _rw_scan(r, v, nkk, lw, kd, bd, s0g)
    x_new = _rw_out_proj_ln(y, bonus, sz, gn_g, gn_b, ones_bd, w_out, x, mod, layer, ln_g, ln_b)
    new_state = _unpack_states(s_fin[:gp]).reshape(B_P, 1, 2, RW_H, RW_N, RW_N)
    return x_new, new_state


def kernel(x_prompt, x_sample, cache_mla_ckv, cache_mla_kpe, state_rwkv, c, c_ctx, mod_w, mod_b, ln_g, ln_b, hy_w_in, hy_conv_w, hy_conv_b, hy_ffn_w1, hy_ffn_b1, hy_ffn_w2, hy_ffn_b2, hy_ffn_w3, hy_ffn_b3, hy_freq, hy_decay, hy_skip, hy_w_out, mla_w_in, mla_q_norm, mla_kv_norm, mla_w_q_up, mla_w_kv_up, mla_w_out, rw_mu, rw_w_in, rw_w0, rw_w1, rw_w2, rw_a0, rw_a1, rw_a2, rw_k_k, rw_k_a, rw_r_k, rw_gn_g, rw_gn_b, rw_w_out):
    x = jnp.concatenate([x_prompt.reshape(T_P, D), x_sample.reshape(T_S, D)], 0)
    cond8 = jnp.concatenate([c_ctx[None, :], c, jnp.zeros((8 - 1 - B_S, D), F32)], 0)
    mod = _modulation_table(cond8, mod_w, mod_b).reshape(DEPTH * 8, 1, 3 * D)
    tabs_p = _dft_tables(L_P)
    tabs_s = _dft_tables(L_S)
    rope = _rope_tables()
    new_ckv = new_kpe = new_state = None
    for i in range(DEPTH):
        kind, j = i % 3, i // 3
        if kind == 0:
            x = _hyena_layer(x, mod, i, tabs_p, tabs_s, ln_g[i], ln_b[i], (hy_w_in, j), hy_conv_w[j], hy_conv_b[j],
                             hy_ffn_w1[j], hy_ffn_b1[j], hy_ffn_w2[j], hy_ffn_b2[j], hy_ffn_w3[j], hy_ffn_b3[j],
                             hy_freq[j], hy_decay[j], hy_skip[j], hy_w_out[j], split_out=(i == DEPTH - 1))
        elif kind == 1:
            x, new_ckv, new_kpe = _mla_layer(x, mod, i, rope, ln_g[i], ln_b[i], cache_mla_ckv[:, j],
                                             cache_mla_kpe[:, j], mla_w_in[j], mla_q_norm[j], mla_kv_norm[j],
                                             mla_w_q_up[j], mla_w_kv_up[j], mla_w_out[j])
        else:
            x, new_state = _rwkv_layer(x, mod, i, ln_g[i], ln_b[i], state_rwkv[:, j], rw_mu[j], rw_w_in[j],
                                       rw_w0[j], rw_w1[j], rw_w2[j], rw_a0[j], rw_a1[j], rw_a2[j], rw_k_k[j],
                                       rw_k_a[j], rw_r_k[j], rw_gn_g[j], rw_gn_b[j], rw_w_out[j])
    x_p, x_s = x if isinstance(x, tuple) else (x[:T_P], x[T_P:])
    return (x_p.reshape(B_P, L_P, D), x_s.reshape(B_S, L_S, D), new_ckv, new_kpe, new_state)
```

```python
import functools
import math

import numpy as np
import jax
import jax.numpy as jnp
from jax import lax
from jax.experimental import pallas as pl
from jax.experimental.pallas import tpu as pltpu

F32 = jnp.float32
BF16 = jnp.bfloat16
HIGHEST = lax.Precision.HIGHEST

D = 1024
B_P, L_P = 16, 256
B_S, L_S = 2, 2048
T_P = B_P * L_P
T_S = B_S * L_S
T = T_P + T_S
PAST = 512
DEPTH = 4
DEEPNORM_ALPHA = (2.0 * DEPTH) ** 0.25
LN_EPS = 1e-5
RMS_EPS = 1e-6
HY_BANDS = 16
HY_FFN = 64
MLA_HEADS = 16
MLA_Q_RANK = 256
MLA_KV_RANK = 128
MLA_NOPE = 64
MLA_ROPE = 32
MLA_V = 64
ROPE_BASE = 10000.0
GRID_W = 64
RW_N = 64
RW_H = D // RW_N
RW_LORA = 64
RW_GN_EPS = 64e-5

TM = 256
NT_P = T_P // TM
NT_S_SEQ = L_S // TM
NT = T // TM
HALO = 8
LANES = 128
ROWS_BF16 = 16
CHUNK = 64
DFT_TK = 512
VMEM_LIMIT = 52 * 1024 * 1024


def _cparams(sem):
    return pltpu.CompilerParams(dimension_semantics=sem, vmem_limit_bytes=VMEM_LIMIT)


def _group(i, tm=TM):
    return jnp.where(i < T_P // tm, 0, 1 + (i - T_P // tm) // (L_S // tm))


def _sigmoid(x):
    return 1.0 / (1.0 + jnp.exp(-x))


def _silu(x):
    return x * _sigmoid(x)


def _dot(a, b):
    return jnp.dot(a, b, preferred_element_type=F32)


def _dot_nt(a, b):
    return lax.dot_general(a, b, (((1,), (1,)), ((), ())), preferred_element_type=F32)


def _dot_hi(a, b):
    return jnp.dot(a, b, preferred_element_type=F32, precision=HIGHEST)


def _split(x):
    hi = x.astype(BF16)
    lo = (x - hi.astype(F32)).astype(BF16)
    return hi, lo


def _head_sum(x, ones_bd):
    hi, lo = _split(x)
    lanes = ones_bd.shape[0]
    parts = []
    for g in range(x.shape[1] // lanes):
        sl = slice(g * lanes, (g + 1) * lanes)
        parts.append(_dot(hi[:, sl], ones_bd) + _dot(lo[:, sl], ones_bd))
    return jnp.concatenate(parts, axis=1)


def _modulate(x, m):
    return x * (1.0 + m[:, D:2 * D]) + m[:, :D]


def _layer_norm_rows(y, g, b):
    mu = jnp.mean(y, axis=-1, keepdims=True)
    yc = y - mu
    var = jnp.mean(yc * yc, axis=-1, keepdims=True)
    return yc * lax.rsqrt(var + LN_EPS) * g + b


def _neighbour_rows(cur, prev_halo, next_halo, has_prev, has_next):
    rows = cur.shape[0]
    ridx = lax.broadcasted_iota(jnp.int32, cur.shape, 0)
    pr = jnp.where(has_prev, prev_halo[HALO - 1:HALO, :], 0.0)
    nx = jnp.where(has_next, next_halo[0:1, :], 0.0)
    prev = jnp.where(ridx == 0, pr, pltpu.roll(cur, 1, axis=0))
    nxt = jnp.where(ridx == rows - 1, nx, pltpu.roll(cur, rows - 1, axis=0))
    return prev, nxt


def _tile_has_neighbours(i):
    k = (i - NT_P) % NT_S_SEQ
    is_s = i >= NT_P
    return jnp.logical_and(is_s, k != 0), jnp.logical_and(is_s, k != NT_S_SEQ - 1)


def _halo_specs(width, col_of):
    r = TM // HALO
    prev = pl.BlockSpec((HALO, width), lambda i, *a: (jnp.maximum(i * r - 1, 0), col_of(i, *a)))
    nxt = pl.BlockSpec((HALO, width), lambda i, *a: (jnp.minimum((i + 1) * r, T // HALO - 1), col_of(i, *a)))
    return prev, nxt


def _mod_kernel(c_ref, w_ref, b_ref, o_ref):
    o_ref[...] = _dot_hi(_silu(c_ref[...]), w_ref[...]) + b_ref[...]


def _modulation_table(cond8, mod_w, mod_b):
    tn = 1024
    return pl.pallas_call(
        _mod_kernel,
        grid=(DEPTH, 3 * D // tn),
        in_specs=[pl.BlockSpec((8, D), lambda l, j: (0, 0)),
                  pl.BlockSpec((None, D, tn), lambda l, j: (l, 0, j)),
                  pl.BlockSpec((None, 1, tn), lambda l, j: (l, 0, j))],
        out_specs=pl.BlockSpec((None, 8, tn), lambda l, j: (l, 0, j)),
        out_shape=jax.ShapeDtypeStruct((DEPTH, 8, 3 * D), F32),
        compiler_params=_cparams(("arbitrary", "arbitrary")),
        name="modulation",
    )(cond8, mod_w, mod_b.reshape(DEPTH, 1, 3 * D))


def _mod_spec(layer, tm=TM):
    return pl.BlockSpec((None, 1, 3 * D), lambda i, *a: (layer * 8 + _group(i, tm), 0, 0))


def _out_ln_tail(g, w_ref, x_ref, m_ref, lng_ref, lnb_ref, o_ref):
    mix = _dot(g.astype(BF16), w_ref[...])
    gate = m_ref[...][:, 2 * D:]
    y = DEEPNORM_ALPHA * x_ref[...] + gate * mix
    o_ref[...] = _layer_norm_rows(y, lng_ref[...], lnb_ref[...])


TM_OUT = 512


def _out_ln2_kernel(gp_ref, gs_ref, w_ref, x_ref, m_ref, lng_ref, lnb_ref, o_ref):
    g = jnp.where(pl.program_id(0) < T_P // TM_OUT, gp_ref[...], gs_ref[...])
    _out_ln_tail(g, w_ref, x_ref, m_ref, lng_ref, lnb_ref, o_ref)


def _out_ln1_kernel(g_ref, w_ref, x_ref, m_ref, lng_ref, lnb_ref, o_ref):
    _out_ln_tail(g_ref[...], w_ref, x_ref, m_ref, lng_ref, lnb_ref, o_ref)


def _out_proj_ln1(g, w_out, x, mod, layer, ln_g, ln_b, tile0):
    vec = pl.BlockSpec((1, D), lambda i: (0, 0))
    return pl.pallas_call(
        _out_ln1_kernel,
        grid=(g.shape[0] // TM_OUT,),
        in_specs=[pl.BlockSpec((TM_OUT, D), lambda i: (i, 0)),
                  pl.BlockSpec((D, D), lambda i: (0, 0)),
                  pl.BlockSpec((TM_OUT, D), lambda i: (i + tile0, 0)),
                  pl.BlockSpec((None, 1, 3 * D), lambda i: (layer * 8 + _group(i + tile0, TM_OUT), 0, 0)),
                  vec, vec],
        out_specs=pl.BlockSpec((TM_OUT, D), lambda i: (i, 0)),
        out_shape=jax.ShapeDtypeStruct(g.shape, F32),
        compiler_params=_cparams(("arbitrary",)),
        name="out_proj_ln_part",
    )(g, w_out.astype(BF16), x, mod, ln_g.reshape(1, D), ln_b.reshape(1, D))


def _out_proj_ln2(g_p, g_s, w_out, x, mod, layer, ln_g, ln_b):
    nt_p = T_P // TM_OUT
    row = pl.BlockSpec((TM_OUT, D), lambda i: (i, 0))
    vec = pl.BlockSpec((1, D), lambda i: (0, 0))
    return pl.pallas_call(
        _out_ln2_kernel,
        grid=(T // TM_OUT,),
        in_specs=[pl.BlockSpec((TM_OUT, D), lambda i: (jnp.minimum(i, nt_p - 1), 0)),
                  pl.BlockSpec((TM_OUT, D), lambda i: (jnp.maximum(i - nt_p, 0), 0)),
                  pl.BlockSpec((D, D), lambda i: (0, 0)),
                  row, _mod_spec(layer, TM_OUT), vec, vec],
        out_specs=row,
        out_shape=jax.ShapeDtypeStruct((T, D), F32),
        compiler_params=_cparams(("arbitrary",)),
        name="out_proj_ln",
    )(g_p, g_s, w_out.astype(BF16), x, mod, ln_g.reshape(1, D), ln_b.reshape(1, D))


def _hy_in_conv_kernel(x_ref, xp_ref, xn_ref, m_ref, w_ref, cw_ref, cb_ref, vv_ref, gate_ref, wb_s):
    @pl.when(pl.program_id(0) == 0)
    def _():
        wb_s[...] = w_ref[...].astype(BF16)

    has_prev, has_next = _tile_has_neighbours(pl.program_id(0))
    x_all = jnp.concatenate([xp_ref[...], x_ref[...], xn_ref[...]], axis=0)
    u = _dot(_modulate(x_all, m_ref[...]).astype(BF16), wb_s[...])
    rows = TM + 2 * HALO
    ridx = lax.broadcasted_iota(jnp.int32, (TM, 1), 0)
    no_prev = jnp.logical_and(ridx == 0, jnp.logical_not(has_prev))
    no_next = jnp.logical_and(ridx == TM - 1, jnp.logical_not(has_next))
    cw = cw_ref[...]
    cb = cb_ref[...]

    def conv(grp):
        sl = slice(grp * D, (grp + 1) * D)
        ug = u[:, sl]
        cur = ug[HALO:HALO + TM]
        prev = jnp.where(no_prev, 0.0, pltpu.roll(ug, 1, axis=0)[HALO:HALO + TM])
        nxt = jnp.where(no_next, 0.0, pltpu.roll(ug, rows - 1, axis=0)[HALO:HALO + TM])
        return prev * cw[0:1, sl] + cur * cw[1:2, sl] + nxt * cw[2:3, sl] + cb[:, sl]

    x0, x1, v = conv(0), conv(1), conv(2)
    vv_ref[...] = v * x1
    gate_ref[...] = x0 * _silu(u[HALO:HALO + TM, 3 * D:])


def _hy_in_conv(x, mod, layer, w_in_all, j, conv_w, conv_b):
    row = pl.BlockSpec((TM, D), lambda i: (i, 0))
    prev, nxt = _halo_specs(D, lambda i: 0)
    return pl.pallas_call(
        _hy_in_conv_kernel,
        grid=(NT,),
        in_specs=[row, prev, nxt, _mod_spec(layer),
                  pl.BlockSpec((None,) + w_in_all.shape[1:], lambda i: (j, 0, 0), pipeline_mode=pl.Buffered(1)),
                  pl.BlockSpec((3, 3 * D), lambda i: (0, 0)), pl.BlockSpec((1, 3 * D), lambda i: (0, 0))],
        out_specs=[row, row],
        out_shape=[jax.ShapeDtypeStruct((T, D), F32)] * 2,
        compiler_params=_cparams(("arbitrary",)),
        scratch_shapes=[pltpu.VMEM(w_in_all.shape[1:], BF16)],
        name="hyena_in_proj_conv3",
    )(x, x, x, mod, w_in_all, conv_w, conv_b.reshape(1, 3 * D))


def _hy_filter_kernel(t_ref, bands_ref, wt_ref, wc_ref, ws_ref, b1_ref, w2_ref, b2_ref, w3_ref, b3_ref,
                      f0_ref, f1_ref, dec_ref, hs_ref, hd_ref, nyq_ref, *, L, tr):
    i = pl.program_id(0)
    ridx = lax.broadcasted_iota(jnp.int32, (tr, 1), 0) + i * tr
    pos = ridx.astype(F32)
    t = t_ref[...]
    ang = ((2.0 * math.pi / L) * pos) * bands_ref[...]
    pre = t * wt_ref[...] + _dot_hi(jnp.cos(ang), wc_ref[...]) + _dot_hi(jnp.sin(ang), ws_ref[...])
    hdn = jnp.sin(f0_ref[...] * (pre + b1_ref[...]))
    hdn = jnp.sin(f1_ref[...] * (_dot_hi(hdn, w2_ref[...]) + b2_ref[...]))
    hf = _dot_hi(hdn, w3_ref[...]) + b3_ref[...]
    h = hf * jnp.exp(-t * jnp.abs(dec_ref[...]))
    h0 = h[:, :D]
    h1 = jnp.where(ridx == 0, 0.0, h[:, D:])
    hsum = h0 + h1
    hs_ref[...] = hsum
    hd_ref[...] = h1 - h0
    alt = jnp.where((ridx & 1) == 0, 1.0, -1.0)
    part =jnp.broadcast_to(jnp.sum(alt * hsum, axis=0, keepdims=True), (8, D))

    @pl.when(i == 0)
    def _():
        nyq_ref[...] = part

    @pl.when(i > 0)
    def _():
        nyq_ref[...] += part


def _hy_filter(L, w1, b1, w2, b2, w3, b3, freq, decay):
    tr = 256
    t = jnp.linspace(0.0, 1.0, L, dtype=F32).reshape(L, 1)
    bands = jnp.linspace(1e-4, HY_BANDS - 1, HY_BANDS, dtype=F32)
    bands = jnp.pad(bands, (0, 128 - HY_BANDS)).reshape(1, 128)
    wt = w1[0:1]
    wc = jnp.pad(w1[1:1 + HY_BANDS], ((0, 128 - HY_BANDS), (0, 0)))
    ws = jnp.pad(-w1[1 + HY_BANDS:], ((0, 128 - HY_BANDS), (0, 0)))
    full = lambda shape: pl.BlockSpec(shape, lambda i: (0, 0))
    rows = pl.BlockSpec((tr, D), lambda i: (i, 0))
    return pl.pallas_call(
        functools.partial(_hy_filter_kernel, L=L, tr=tr),
        grid=(L // tr,),
        in_specs=[pl.BlockSpec((tr, 1), lambda i: (i, 0)), full((1, 128)), full((1, HY_FFN)),
                  full((128, HY_FFN)), full((128, HY_FFN)), full((1, HY_FFN)),
                  full((HY_FFN, HY_FFN)), full((1, HY_FFN)), full((HY_FFN, 2 * D)), full((1, 2 * D)),
                  full((1, HY_FFN)), full((1, HY_FFN)), full((1, 2 * D))],
        out_specs=[rows, rows, pl.BlockSpec((8, D), lambda i: (0, 0))],
        out_shape=[jax.ShapeDtypeStruct((L, D), F32), jax.ShapeDtypeStruct((L, D), F32),
                   jax.ShapeDtypeStruct((8, D), F32)],
        compiler_params=_cparams(("arbitrary",)),
        name="hyena_filter",
    )(t, bands, wt, wc, ws, b1.reshape(1, -1), w2, b2.reshape(1, -1), w3, b3.reshape(1, -1),
      freq[0:1], freq[1:2], decay.reshape(1, 2 * D))


def _dft_table_kernel(ca_ref, sa_ref, cb_ref, sb_ref, cat_ref, sat_ref, cbt_ref, sbt_ref,
                      cr_ref, sr_ref, cc_ref, sc_ref, *, n_t1):
    first = pl.program_id(0) == 0
    ca, sa, cb, sb = ca_ref[...], sa_ref[...], cb_ref[...], sb_ref[...]
    cat, sat, cbt, sbt = cat_ref[...], sat_ref[...], cbt_ref[...], sbt_ref[...]
    tk = cb.shape[0]
    row0 = jnp.logical_and(lax.broadcasted_iota(jnp.int32, (tk, LANES), 0) == 0, first)
    alt_r = jnp.where((lax.broadcasted_iota(jnp.int32, (tk, LANES), 1) & 1) == 0, 1.0, -1.0)
    col0 = jnp.logical_and(lax.broadcasted_iota(jnp.int32, (LANES, tk), 1) == 0, first)
    alt_c = jnp.where((lax.broadcasted_iota(jnp.int32, (LANES, tk), 0) & 1) == 0, 1.0, -1.0)
    for t1 in range(n_t1):
        sl = slice(t1 * LANES, (t1 + 1) * LANES)
        a_c, a_s = ca[:, t1:t1 + 1], sa[:, t1:t1 + 1]
        cr_ref[:, sl] = (a_c * cb - a_s * sb).astype(BF16)
        sr_ref[:, sl] = jnp.where(row0, alt_r, a_s * cb + a_c * sb).astype(BF16)
        b_c, b_s = cat[t1:t1 + 1, :], sat[t1:t1 + 1, :]
        cc_ref[sl, :] = (b_c * cbt - b_s * sbt).astype(BF16)
        sc_ref[sl, :] = jnp.where(col0, alt_c, b_s * cbt + b_c * sbt).astype(BF16)


def _dft_tables(L):
    n = 2 * L
    k = jnp.arange(L, dtype=jnp.int32)

    def cs(t):
        ang = ((k[:, None] * t[None, :]) % n).astype(F32) * (2.0 * math.pi / n)
        return jnp.cos(ang), jnp.sin(ang)

    tk = min(L, DFT_TK)
    nk = L // tk
    n_t1 = L // LANES
    ca, sa = cs(jnp.arange(n_t1, dtype=jnp.int32) * LANES)
    cb, sb = cs(jnp.arange(LANES, dtype=jnp.int32))
    by_bin = lambda w: pl.BlockSpec((tk, w), lambda i: (i, 0))
    by_bin_t = lambda w: pl.BlockSpec((w, tk), lambda i: (0, i))
    return tuple(pl.pallas_call(
        functools.partial(_dft_table_kernel, n_t1=n_t1),
        grid=(nk,),
        in_specs=[by_bin(n_t1), by_bin(n_t1), by_bin(LANES), by_bin(LANES),
                  by_bin_t(n_t1), by_bin_t(n_t1), by_bin_t(LANES), by_bin_t(LANES)],
        out_specs=[pl.BlockSpec((None, tk, L), lambda i: (i, 0, 0))] * 2
                  + [pl.BlockSpec((None, L, tk), lambda i: (i, 0, 0))] * 2,
        out_shape=[jax.ShapeDtypeStruct((nk, tk, L), BF16)] * 2 + [jax.ShapeDtypeStruct((nk, L, tk), BF16)] * 2,
        compiler_params=_cparams(("arbitrary",)),
        name="dft_tables",
    )(ca, sa, cb, sb, ca.T, sa.T, cb.T, sb.T))


def _dft_fwd_kernel(c_ref, s_ref, *refs):
    x1_ref, x2_ref = refs[0], refs[-3]
    oc_ref, os_ref = refs[-2:]
    k = pl.program_id(2)
    oc_ref[...] = _dot(c_ref[k], x1_ref[...].astype(BF16))
    os_ref[...] = _dot(s_ref[k], x2_ref[...].astype(BF16))


def _resident(shape):
    return pl.BlockSpec(shape, lambda *_: (0,) * len(shape), pipeline_mode=pl.Buffered(1))


def _dft_fwd(tabs, xs, L, nb, row_blk0):
    c, s_rows = tabs[:2]
    nk, tk, _ = c.shape
    tn = 512
    a_spec = _resident(c.shape)
    x_spec = pl.BlockSpec((L, tn), lambda b, j, k: (row_blk0 + b, j))
    o_spec = pl.BlockSpec((tk, tn), lambda b, j, k: (b * nk + k, j))
    return pl.pallas_call(
        _dft_fwd_kernel,
        grid=(nb, D // tn, nk),
        in_specs=[a_spec] * 2 + [x_spec] * len(xs),
        out_specs=[o_spec, o_spec],
        out_shape=[jax.ShapeDtypeStruct((nb * L, D), F32)] * 2,
        compiler_params=_cparams(("arbitrary", "arbitrary", "arbitrary")),
        name="hyena_dft_fwd",
    )(c, s_rows, *xs)


def _spectrum_product(vc, vs, kre, kim, nyq, first_tile, L):
    bin0 = jnp.logical_and(lax.broadcasted_iota(jnp.int32, vc.shape, 0) == 0, first_tile)
    kim = jnp.where(bin0, nyq, kim)
    inv_n = 1.0 / (2 * L)
    yre = jnp.where(bin0, vc * kre * inv_n, (vc * kre + vs * kim) * (2.0 * inv_n))
    yim = jnp.where(bin0, vs * kim * inv_n, (vs * kre - vc * kim) * (2.0 * inv_n))
    return yre.astype(BF16), yim.astype(BF16)


def _dft_conv_kernel(c_ref, s_ref, st_ref, v_ref, kre_ref, kim_ref, nyq_ref, skip_ref, gate_ref, o_ref, *, L):
    v = v_ref[...]
    vb = v.astype(BF16)
    c = c_ref[0]
    yre, yim = _spectrum_product(_dot(c, vb), _dot(s_ref[0], vb), kre_ref[...], kim_ref[...], nyq_ref[0:1, :],
                                 True, L)
    y = _dot(c, yre) + _dot(st_ref[0], yim)
    o_ref[...] = (y + v * skip_ref[...]) * gate_ref[...]


def _dft_conv(tabs, kre, kim, nyq, vv, skip, gate, L, nb, row_blk0):
    c, s_rows, _, s_cols = tabs
    assert c.shape[0] == 1, "single frequency tile only"
    tn = 512
    seq = lambda b, j: (row_blk0 + b, j)
    spec = pl.BlockSpec((L, tn), lambda b, j: (0, j))
    return pl.pallas_call(
        functools.partial(_dft_conv_kernel, L=L),
        grid=(nb, D // tn),
        in_specs=[_resident(c.shape)] * 3 + [pl.BlockSpec((L, tn), seq), spec, spec,
                                             pl.BlockSpec((8, tn), lambda b, j: (0, j)),
                                             pl.BlockSpec((1, tn), lambda b, j: (0, j)), pl.BlockSpec((L, tn), seq)],
        out_specs=pl.BlockSpec((L, tn), lambda b, j: (b, j)),
        out_shape=jax.ShapeDtypeStruct((nb * L, D), F32),
        compiler_params=_cparams(("arbitrary", "arbitrary")),
        name="hyena_dft_conv_gate",
    )(c, s_rows, s_cols, vv, kre, kim, nyq, skip.reshape(1, D), gate)


def _dft_inv_kernel(c_ref, st_ref, vc_ref, vs_ref, kre_ref, kim_ref, nyq_ref,
                    vv_ref, skip_ref, gate_ref, o_ref, *, L):
    k = pl.program_id(2)
    nk = pl.num_programs(2)
    yre, yim = _spectrum_product(vc_ref[...], vs_ref[...], kre_ref[...], kim_ref[...], nyq_ref[0:1, :], k == 0, L)
    contrib = _dot(c_ref[k], yre) + _dot(st_ref[k], yim)

    @pl.when(k == 0)
    def _():
        o_ref[...] = contrib

    @pl.when(k > 0)
    def _():
        o_ref[...] += contrib

    @pl.when(k == nk - 1)
    def _():
        o_ref[...] = (o_ref[...] + vv_ref[...] * skip_ref[...]) * gate_ref[...]


def _dft_inv(tabs, vc, vs, kre, kim, nyq, vv, skip, gate, L, nb, row_blk0):
    c, s_cols = tabs[2:]
    nk, _, tk = c.shape
    tn = 512 if L <= 512 else 256
    a_spec = _resident(c.shape)
    v_spec = pl.BlockSpec((tk, tn), lambda b, j, k: (b * nk + k, j))
    k_spec = pl.BlockSpec((tk, tn), lambda b, j, k: (k, j))
    row_spec = pl.BlockSpec((L, tn), lambda b, j, k: (row_blk0 + b, j))
    return pl.pallas_call(
        functools.partial(_dft_inv_kernel, L=L),
        grid=(nb, D // tn, nk),
        in_specs=[a_spec] * 2 + [v_spec, v_spec, k_spec, k_spec,
                                 pl.BlockSpec((8, tn), lambda b, j, k: (0, j)),
                                 row_spec, pl.BlockSpec((1, tn), lambda b, j, k: (0, j)), row_spec],
        out_specs=pl.BlockSpec((L, tn), lambda b, j, k: (b, j)),
        out_shape=jax.ShapeDtypeStruct((nb * L, D), F32),
        compiler_params=_cparams(("arbitrary", "arbitrary", "arbitrary")),
        name="hyena_dft_inv_gate",
    )(c, s_cols, vc, vs, kre, kim, nyq, vv, skip.reshape(1, D), gate)


def _hyena_layer(x, mod, layer, tabs_p, tabs_s, ln_g, ln_b, w_in, conv_w, conv_b, w1, b1, w2, b2, w3, b3,
                 freq, decay, skip, w_out, split_out=False):
    vv, gate = _hy_in_conv(x, mod, layer, *w_in, conv_w, conv_b)
    gs = []
    for L, nb, blk0, tabs in ((L_P, B_P, 0, tabs_p), (L_S, B_S, T_P // L_S, tabs_s)):
        hsum, hdiff, nyq = _hy_filter(L, w1, b1, w2, b2, w3, b3, freq, decay)
        kre, kim = _dft_fwd(tabs, (hsum, hdiff), L, 1, 0)
        if L <= DFT_TK:
            gs.append(_dft_conv(tabs, kre, kim, nyq, vv, skip, gate, L, nb, blk0))
        else:
            vc, vs = _dft_fwd(tabs, (vv,), L, nb, blk0)
            gs.append(_dft_inv(tabs, vc, vs, kre, kim, nyq, vv, skip, gate, L, nb, blk0))
    if split_out:
        return tuple(_out_proj_ln1(g, w_out, x, mod, layer, ln_g, ln_b, tile0)
                     for g, tile0 in ((gs[0], 0), (gs[1], T_P // TM_OUT)))
    return _out_proj_ln2(gs[0], gs[1], w_out, x, mod, layer, ln_g, ln_b)


def _rope_tables():
    rows = L_S // GRID_W
    half = MLA_ROPE // 2
    inv = ROPE_BASE ** (-jnp.arange(0, half, 2, dtype=F32) / half)
    r = jnp.repeat(jnp.arange(rows, dtype=F32), GRID_W)
    col = jnp.tile(jnp.arange(GRID_W, dtype=F32), rows)
    ar, ac = r[:, None] * inv, col[:, None] * inv
    ang = jnp.concatenate([ar, ar, ac, ac], -1)
    cos, sin = jnp.cos(ang), jnp.sin(ang)
    cos = jnp.concatenate([jnp.ones((TM, MLA_ROPE), F32), cos], 0)
    sin = jnp.concatenate([jnp.zeros((TM, MLA_ROPE), F32), sin], 0)
    return cos, sin, jnp.tile(cos, (1, MLA_HEADS)), jnp.tile(sin, (1, MLA_HEADS))


def _rope_rot_cols(w):
    idx = np.concatenate([np.arange(8, 16), np.arange(0, 8), np.arange(24, 32), np.arange(16, 24)])
    sign = np.concatenate([-np.ones(8), np.ones(8), -np.ones(8), np.ones(8)]).astype(np.float32)
    return w[..., idx] * sign


def _mla_in_kernel(x_ref, m_ref, wq_ref, wkv_ref, wkp_ref, wz_ref, qn_ref, kvn_ref, wqn_ref, wqp_ref, wqr_ref,
                   c32_ref, s32_ref, c512_ref, s512_ref,
                   qno_ref, qpe_ref, ckv_ref, kpe_ref, kpr_ref, sz_ref):
    h = _modulate(x_ref[...], m_ref[...]).astype(BF16)
    q_c = _dot(h, wq_ref[...])
    kv_c = _dot(h, wkv_ref[...])
    kp2 = _dot(h, wkp_ref[...])
    z = _dot(h, wz_ref[...])

    def rms(v, g):
        return v * lax.rsqrt(jnp.mean(v * v, axis=-1, keepdims=True) + RMS_EPS) * g

    qn = rms(q_c, qn_ref[...]).astype(BF16)
    scale = (MLA_NOPE + MLA_ROPE) ** -0.5
    qno_ref[...] = (_dot(qn, wqn_ref[...]) * scale).astype(BF16)
    q_pe = _dot(qn, wqp_ref[...]) * c512_ref[...] + _dot(qn, wqr_ref[...]) * s512_ref[...]
    qpe_ref[...] = (q_pe * scale).astype(BF16)
    ckv_ref[...] = rms(kv_c, kvn_ref[...])
    kpe = kp2[:, :MLA_ROPE]
    kpe_ref[...] = kpe
    kpr_ref[...] = kpe * c32_ref[...] + kp2[:, MLA_ROPE:] * s32_ref[...]
    sz_ref[...] = _silu(z)


def _mla_in_proj(x, mod, layer, w_in, q_norm, kv_norm, w_q_up, rope):
    c32, s32, c512, s512 = rope
    o1, o2, o3 = MLA_Q_RANK, MLA_Q_RANK + MLA_KV_RANK, MLA_Q_RANK + MLA_KV_RANK + MLA_ROPE
    wq, wkv, wkp, wz = w_in[:, :o1], w_in[:, o1:o2], w_in[:, o2:o3], w_in[:, o3:]
    wkp2 = jnp.concatenate([wkp, _rope_rot_cols(wkp)], -1)
    wqu = w_q_up.reshape(MLA_Q_RANK, MLA_HEADS, MLA_NOPE + MLA_ROPE)
    wqn = wqu[:, :, :MLA_NOPE].reshape(MLA_Q_RANK, MLA_HEADS * MLA_NOPE)
    wqp = wqu[:, :, MLA_NOPE:]
    wqr = _rope_rot_cols(wqp).reshape(MLA_Q_RANK, MLA_HEADS * MLA_ROPE)
    wqp = wqp.reshape(MLA_Q_RANK, MLA_HEADS * MLA_ROPE)
    full = lambda a: pl.BlockSpec(a.shape, lambda i: (0,) * a.ndim)
    rope_idx = lambda i: jnp.where(i < NT_P, 0, 1 + (i - NT_P) % NT_S_SEQ)
    rows = lambda n: pl.BlockSpec((TM, n), lambda i: (i, 0))
    tab = lambda n: pl.BlockSpec((TM, n), lambda i: (rope_idx(i), 0))
    weights = [wq.astype(BF16), wkv.astype(BF16), wkp2.astype(BF16), wz.astype(BF16),
               q_norm.reshape(1, -1), kv_norm.reshape(1, -1),
               wqn.astype(BF16), wqp.astype(BF16), wqr.astype(BF16)]
    npe = MLA_HEADS * MLA_ROPE
    return pl.pallas_call(
        _mla_in_kernel,
        grid=(NT,),
        in_specs=[rows(D), _mod_spec(layer)] + [full(a) for a in weights]
                 + [tab(MLA_ROPE), tab(MLA_ROPE), tab(npe), tab(npe)],
        out_specs=[rows(D), rows(npe), rows(MLA_KV_RANK), rows(MLA_ROPE), rows(MLA_ROPE), rows(D)],
        out_shape=[jax.ShapeDtypeStruct((T, D), BF16), jax.ShapeDtypeStruct((T, npe), BF16),
                   jax.ShapeDtypeStruct((T, MLA_KV_RANK), F32), jax.ShapeDtypeStruct((T, MLA_ROPE), F32),
                   jax.ShapeDtypeStruct((T, MLA_ROPE), F32), jax.ShapeDtypeStruct((T, D), F32)],
        compiler_params=_cparams(("arbitrary",)),
        name="mla_in_proj",
    )(x, mod, *weights, c32, s32, c512, s512)


def _mla_attn_kernel(*refs, n_cache, hg):
    if n_cache:
        (qn_ref, qp_ref, ckv_ref, kpr_ref, cckv_ref, ckpe_ref, wk_ref, wvt_ref, sz_ref,
         o_ref, kcat_s, vt_s) = refs
    else:
        qn_ref, qp_ref, ckv_ref, kpr_ref, wk_ref, wvt_ref, sz_ref, o_ref, kcat_s, vt_s = refs

    @pl.when(pl.program_id(2) == 0)
    def _():
        cc = ckv_ref[...].astype(BF16)
        kp = kpr_ref[...].astype(BF16)
        if n_cache:
            cc = jnp.concatenate([cckv_ref[...].astype(BF16), cc], axis=0)
            kp = jnp.concatenate([ckpe_ref[...].astype(BF16), kp], axis=0)
        kn = _dot(cc, wk_ref[...]).astype(BF16)
        for hh in range(hg):
            kcat_s[hh] = jnp.concatenate([kn[:, hh * MLA_NOPE:(hh + 1) * MLA_NOPE], kp], axis=1)
            vt_s[hh] = _dot_nt(wvt_ref[hh], cc).astype(BF16)

    qn_all, qp_all = qn_ref[...], qp_ref[...]

    def scores(hh):
        qcat = jnp.concatenate([qn_all[:, hh * MLA_NOPE:(hh + 1) * MLA_NOPE],
                                qp_all[:, hh * MLA_ROPE:(hh + 1) * MLA_ROPE]], axis=1)
        return _dot_nt(kcat_s[hh], qcat)

    outs = []
    ahead = 2
    pending = [scores(hh) for hh in range(min(ahead, hg))]
    for hh in range(hg):
        s = pending.pop(0)
        if hh + ahead < hg:
            pending.append(scores(hh + ahead))
        p = jnp.exp(s - jnp.max(s, axis=0, keepdims=True))
        l = jnp.sum(p, axis=0, keepdims=True)
        outs.append(_dot(vt_s[hh], p.astype(BF16)) / l)
    o_ref[...] = jnp.concatenate(outs, axis=0).T * sz_ref[...]


def _mla_attention(qno, qpe, ckv, kpr, sz, w_kv_up, cache_ckv, cache_kpe, *, nb, L, row_blk0, hg):
    wkv = w_kv_up.reshape(MLA_KV_RANK, MLA_HEADS, MLA_NOPE + MLA_V)
    wk = wkv[:, :, :MLA_NOPE].reshape(MLA_KV_RANK, D).astype(BF16)
    wvt = wkv[:, :, MLA_NOPE:].transpose(1, 2, 0).astype(BF16)
    n_cache = 0 if cache_ckv is None else cache_ckv.shape[1]
    lk = n_cache + L
    nq = L // TM
    wq = hg * MLA_NOPE
    wp = hg * MLA_ROPE
    qrow = lambda w: pl.BlockSpec((TM, w), lambda b, g, q: (row_blk0 + b * nq + q, g))
    seq = lambda w: pl.BlockSpec((L, w), lambda b, g, q: (row_blk0 * TM // L + b, 0))
    in_specs = [qrow(wq), qrow(wp), seq(MLA_KV_RANK), seq(MLA_ROPE)]
    args = [qno, qpe, ckv, kpr]
    if n_cache:
        in_specs += [pl.BlockSpec((None, n_cache, MLA_KV_RANK), lambda b, g, q: (b, 0, 0)),
                     pl.BlockSpec((None, n_cache, MLA_ROPE), lambda b, g, q: (b, 0, 0))]
        args += [cache_ckv, cache_kpe]
    in_specs += [pl.BlockSpec((MLA_KV_RANK, wq), lambda b, g, q: (0, g)),
                 pl.BlockSpec((hg, MLA_V, MLA_KV_RANK), lambda b, g, q: (g, 0, 0)), qrow(wq)]
    args += [wk, wvt, sz]
    return pl.pallas_call(
        functools.partial(_mla_attn_kernel, n_cache=n_cache, hg=hg),
        grid=(nb, MLA_HEADS // hg, nq),
        in_specs=in_specs,
        out_specs=pl.BlockSpec((TM, wq), lambda b, g, q: (b * nq + q, g)),
        out_shape=jax.ShapeDtypeStruct((nb * L, D), F32),
        scratch_shapes=[pltpu.VMEM((hg, lk, MLA_NOPE + MLA_ROPE), BF16), pltpu.VMEM((hg, MLA_V, lk), BF16)],
        compiler_params=_cparams(("arbitrary", "arbitrary", "arbitrary")),
        name="mla_attention",
    )(*args)


def _mla_layer(x, mod, layer, rope, ln_g, ln_b, cache_ckv, cache_kpe, w_in, q_norm, kv_norm, w_q_up, w_kv_up, w_out):
    qno, qpe, ckv, kpe, kpr, sz = _mla_in_proj(x, mod, layer, w_in, q_norm, kv_norm, w_q_up, rope)
    g_p = _mla_attention(qno, qpe, ckv, kpr, sz, w_kv_up, None, None, nb=B_P, L=L_P, row_blk0=0,
                         hg=MLA_HEADS)
    g_s = _mla_attention(qno, qpe, ckv, kpr, sz, w_kv_up, cache_ckv, cache_kpe, nb=B_S, L=L_S, row_blk0=NT_P,
                         hg=MLA_HEADS // 2)
    x_new = _out_proj_ln2(g_p, g_s, w_out, x, mod, layer, ln_g, ln_b)
    new_ckv = ckv[:T_P].reshape(B_P, 1, L_P, MLA_KV_RANK)
    new_kpe = kpe[:T_P].reshape(B_P, 1, L_P, MLA_ROPE)
    return x_new, new_ckv, new_kpe


def _rw_in_kernel(x_ref, xp_ref, xn_ref, m_ref, mu_ref, wr_ref, wk_ref, wv_ref, wg_ref, w1_ref, a1_ref,
                  w2_ref, a2_ref, w0_ref, a0_ref, kk_ref, ka_ref, rk_ref, ones_ref,
                  r_ref, v_ref, sz_ref, nkk_ref, bonus_ref, lw_ref, kd_ref, bd_ref):
    i = pl.program_id(0)
    has_prev, has_next = _tile_has_neighbours(i)
    m = m_ref[...]
    h = _modulate(x_ref[...], m)
    prev, nxt = _neighbour_rows(h, _modulate(xp_ref[...], m), _modulate(xn_ref[...], m), has_prev, has_next)
    d = 0.5 * (prev + nxt) - h
    mu = mu_ref[...]

    def mix(p):
        return (h + d * mu[p:p + 1]).astype(BF16)

    r = _dot(mix(0), wr_ref[...])
    tw = jnp.tanh(_dot(mix(1), w1_ref[...])).astype(BF16)
    k = _dot(mix(2), wk_ref[...])
    v = _dot(mix(3), wv_ref[...])
    ta = _dot(mix(4), a1_ref[...]).astype(BF16)
    z = _dot(mix(5), wg_ref[...])
    r_ref[...] = r.astype(r_ref.dtype)
    v_ref[...] = v.astype(v_ref.dtype)
    sz_ref[...] = _silu(z).astype(sz_ref.dtype)
    ones_bd = ones_ref[...]
    kk = k * kk_ref[...]
    kk = kk * lax.rsqrt(_head_sum(kk * kk, ones_bd) + 1e-12)
    nkk_ref[...] = (-kk).astype(nkk_ref.dtype)
    coef = jnp.zeros_like(r)
    for n in range(2):
        wl = w0_ref[n:n + 1, :] + _dot(tw, w2_ref[n])
        lw_ref[n] = -math.exp(-0.5) * _sigmoid(wl)
        a = _sigmoid(a0_ref[n:n + 1, :] + _dot(ta, a2_ref[n]))
        kd = k * (1.0 + (a - 1.0) * ka_ref[...])
        kd_ref[n] = kd.astype(kd_ref.dtype)
        bd_ref[n] = (kk * a).astype(bd_ref.dtype)
        coef = coef + r * kd * rk_ref[...]
    bonus_ref[...] = (_head_sum(coef, ones_bd) * v).astype(bonus_ref.dtype)


def _pad_lora_up(w):
    z = jnp.zeros_like(w[0])
    return jnp.stack([jnp.concatenate([w[0], z], 0), jnp.concatenate([z, w[1]], 0)])


def _head_ones():
    h = np.arange(LANES) // RW_N
    return jnp.asarray(h[:, None] == h[None, :], dtype=BF16)


def _rw_in_proj(x, mod, layer, ones_bd, mu, w_in, w0, w1, w2, a0, a1, a2, k_k, k_a, r_k):
    mu8 = jnp.pad(mu, ((0, 2), (0, 0)))
    w1c = jnp.concatenate([w1[0], w1[1]], -1).astype(BF16)
    a1c = jnp.concatenate([a1[0], a1[1]], -1).astype(BF16)
    w2p = _pad_lora_up(w2).astype(BF16)
    a2p = _pad_lora_up(a2).astype(BF16)
    row = pl.BlockSpec((TM, D), lambda i: (i, 0))
    row2 = pl.BlockSpec((2, TM, D), lambda i: (0, i, 0))
    prev, nxt = _halo_specs(D, lambda i: 0)
    full = lambda a: pl.BlockSpec(a.shape, lambda i: (0,) * a.ndim)
    consts = [mu8, w_in[0].astype(BF16), w_in[1].astype(BF16), w_in[2].astype(BF16), w_in[3].astype(BF16),
              w1c, a1c, w2p, a2p, w0, a0, k_k.reshape(1, D), k_a.reshape(1, D), r_k.reshape(1, D), ones_bd]
    return pl.pallas_call(
        _rw_in_kernel,
        grid=(NT,),
        in_specs=[row, prev, nxt, _mod_spec(layer)] + [full(a) for a in consts],
        out_specs=[row] * 5 + [row2] * 3,
        out_shape=[jax.ShapeDtypeStruct((T, D), BF16)] * 5
                  + [jax.ShapeDtypeStruct((2, T, D), dt) for dt in (F32, BF16, BF16)],
        compiler_params=_cparams(("arbitrary",)),
        name="rwkv_in_proj",
    )(x, x, x, mod, *consts)


SCAN_NB = 2
SCAN_NU = 2 * SCAN_NB


SCAN_PACK = 4
SCAN_GW = SCAN_PACK * RW_N
RW_N_LOG2 = RW_N.bit_length() - 1
CHUNK_LOG2 = CHUNK.bit_length() - 1
assert 1 << RW_N_LOG2 == RW_N and 1 << CHUNK_LOG2 == CHUNK


def _rw_scan_kernel(*refs):
    rb_refs = refs[:SCAN_NU]
    first_ref = refs[SCAN_NU]
    ins = refs[SCAN_NU + 1:SCAN_NU + 1 + 6 * SCAN_NU]
    s0_ref, y_ref, s_ref = refs[SCAN_NU + 1 + 6 * SCAN_NU:]
    del rb_refs
    step = pl.program_id(0)
    c, gw = CHUNK, SCAN_GW

    @pl.when(first_ref[step] == 1)
    def _():
        s_ref[...] = s0_ref[...]

    rc = lax.broadcasted_iota(jnp.int32, (c, c), 0)
    cc = lax.broadcasted_iota(jnp.int32, (c, c), 1)
    ri = lax.broadcasted_iota(jnp.int32, (c, gw), 0)
    lane = lax.broadcasted_iota(jnp.int32, (c, gw), 1)
    pos = lane & (RW_N - 1)
    ri8 = lax.broadcasted_iota(jnp.int32, (c, 2 * gw), 0)
    pos8 = lax.broadcasted_iota(jnp.int32, (c, 2 * gw), 1) & (RW_N - 1)
    head_of = [(lane >> RW_N_LOG2) == hh for hh in range(SCAN_PACK)]
    eye_f = (ri == pos).astype(F32)
    eye = eye_f.astype(BF16)
    rdec = lax.broadcasted_iota(jnp.int32, (LANES, gw), 0)
    ldec = lax.broadcasted_iota(jnp.int32, (LANES, gw), 1)
    dec_keep = jnp.logical_and((rdec & (ROWS_BF16 - 1)) == (ldec >> RW_N_LOG2), rdec < 2 * ROWS_BF16)
    dec_keep_f = dec_keep.astype(F32)
    dec_sum = dec_keep.astype(BF16)
    dec_hi = jnp.logical_and(dec_keep, rdec < ROWS_BF16).astype(BF16)
    dec_lo = dec_sum - dec_hi
    masks = []
    for sgn in (1, -1):
        incl8 = (ri8 - pos8) * sgn >= 0
        strict = (ri - pos) * sgn > 0
        off_masks = []
        for lg in range(CHUNK_LOG2):
            same_2m = (ri >> (lg + 1)) == (pos >> (lg + 1))
            diff_m = (ri >> lg) != (pos >> lg)
            off_masks.append(jnp.logical_and(strict, jnp.logical_and(same_2m, diff_m)))
        masks.append((incl8, strict, off_masks))

    head_bf = [m.astype(BF16) for m in head_of]

    def keep(x, hh):
        return x * head_bf[hh]

    def bdiag(x):
        return jnp.concatenate([keep(x, hh) for hh in range(SCAN_PACK)], axis=0)

    groups = range(D // gw)
    chains = [(q, g) for q in range(SCAN_NU) for g in groups]
    ar, rb, rbe, v4 = {}, {}, {}, {}
    for q in range(SCAN_NU):
        r_ref, v_ref, nkk_ref, lw_ref, kd_ref, bd_ref = ins[6 * q:6 * q + 6]
        bwd = q % 2
        lw = lw_ref[...]
        cum = (rc - cc) * (-1 if bwd else 1) >= 0
        lw_hi = lw.astype(BF16)
        lw_mid, lw_lo = _split(lw - lw_hi.astype(F32))
        cum_b = cum.astype(BF16)
        g_in = _dot(cum_b, lw_hi) + _dot(cum_b, lw_mid) + _dot(cum_b, lw_lo)
        gtot = g_in[0:1, :] if bwd else g_in[c - 1:c, :]
        e_inv = jnp.exp(-g_in)
        e_rem = jnp.exp(gtot - g_in)
        e_tot = jnp.exp(gtot)
        a_t = nkk_ref[...].astype(F32) * jnp.exp(g_in - lw)
        r_t = r_ref[...].astype(F32) * jnp.exp(g_in)
        kd, bd = kd_ref[...].astype(F32), bd_ref[...].astype(F32)
        b_t, k_t, b_e, k_e = bd * e_inv, kd * e_inv, bd * e_rem, kd * e_rem
        v = v_ref[...]
        for g in groups:
            sl = slice(g * gw, (g + 1) * gw)
            ar[q, g] = jnp.concatenate([a_t[:, sl], r_t[:, sl]], axis=0).astype(BF16)
            bt4, kt4 = b_t[:, sl].astype(BF16), k_t[:, sl].astype(BF16)
            rb[q, g] = jnp.concatenate([bdiag(bt4), bdiag(kt4)], axis=0)
            e_hi, e_lo = _split(e_tot[:, sl] * dec_keep_f)
            dec = e_hi * dec_hi + e_lo * dec_lo
            rbe[q, g] = jnp.concatenate([bdiag(b_e[:, sl].astype(BF16)), bdiag(k_e[:, sl].astype(BF16)), dec],
                                        axis=0)
            v4[q, g] = v[:, sl]
    h_old = {ch: s_ref[ch[0], ch[1]] for ch in chains}
    gm = {ch: _dot_nt(ar[ch], rb[ch]) for ch in chains}
    a_s = {ch: _dot(ar[ch], bdiag(h_old[ch].astype(BF16))) for ch in chains}
    bke_t = {ch: _dot_nt(eye, rbe[ch]) for ch in chains}
    lab = {ch: gm[ch][:c, :gw] for ch in chains}
    offs = lambda ch: masks[ch[0] % 2][2]
    tinv = {ch: eye_f + jnp.where(offs(ch)[0], lab[ch], 0.0) for ch in chains}
    for lg in range(1, CHUNK_LOG2):
        tb = {ch: tinv[ch].astype(BF16) for ch in chains}
        lt = {ch: _dot(jnp.where(offs(ch)[lg], lab[ch], 0.0).astype(BF16), bdiag(tb[ch])) for ch in chains}
        tinv = {ch: tinv[ch] + _dot(tb[ch], bdiag(lt[ch].astype(BF16))) for ch in chains}
    v_bd = {ch: bdiag(v4[ch]) for ch in chains}
    w1 = {ch: a_s[ch][:c] + _dot(jnp.where(masks[ch[0] % 2][1], gm[ch][:c, gw:], 0.0).astype(BF16), v_bd[ch])
          for ch in chains}
    u = {ch: _dot(tinv[ch].astype(BF16), bdiag(w1[ch].astype(BF16))) for ch in chains}
    lhs = {ch: jnp.concatenate([jnp.where(masks[ch[0] % 2][0], gm[ch][c:], 0.0), bke_t[ch][:, :2 * gw]],
                               axis=0).astype(BF16) for ch in chains}
    yu = {ch: _dot(lhs[ch], jnp.concatenate([bdiag(u[ch].astype(BF16)), v_bd[ch]], axis=0)) for ch in chains}
    e_col = {ch: _dot(bke_t[ch][:, 2 * gw:].astype(BF16), dec_sum) for ch in chains}
    for q in range(SCAN_NU):
        y_ref[q] = jnp.concatenate([a_s[q, g][c:] + yu[q, g][:c] for g in groups], axis=1).astype(y_ref.dtype)
    for ch in chains:
        s_ref[ch[0], ch[1]] = h_old[ch] * e_col[ch] + yu[ch][c:]


def _rw_scan_tables():
    rb = [[] for _ in range(SCAN_NU)]
    first = []
    for nb, L, blk0 in ((B_P, L_P, 0), (B_S, L_S, T_P // CHUNK)):
        nc = L // CHUNK
        for grp in range(nb // SCAN_NB):
            for j in range(nc):
                for q in range(SCAN_NU):
                    b = grp * SCAN_NB + q // 2
                    cn = j if q % 2 == 0 else nc - 1 - j
                    rb[q].append(blk0 + b * nc + cn)
                first.append(1 if j == 0 else 0)
    as_i32 = lambda a: jnp.asarray(np.asarray(a, np.int32))
    return [as_i32(a) for a in rb], as_i32(first)


def _pack_states(s):
    n_hg = RW_H // SCAN_PACK
    s = s.reshape(-1, SCAN_NU, n_hg, SCAN_PACK, RW_N, RW_N).transpose(0, 1, 2, 5, 3, 4)
    return s.reshape(-1, SCAN_NU, n_hg, RW_N, SCAN_GW)


def _unpack_states(p):
    n_hg = RW_H // SCAN_PACK
    p = p.reshape(-1, SCAN_NU, n_hg, RW_N, SCAN_PACK, RW_N).transpose(0, 1, 2, 4, 5, 3)
    return p.reshape(-1, RW_H, RW_N, RW_N)


def _rw_scan(r, v, nkk, lw, kd, bd, s0g):
    rb, first = _rw_scan_tables()
    n_steps = first.shape[0]
    steps_p = (B_P // SCAN_NB) * (L_P // CHUNK)
    in_specs = []
    args = []
    for q in range(SCAN_NU):
        tok = pl.BlockSpec((CHUNK, D), lambda s, *t, q=q: (t[q][s], 0))
        tok2 = pl.BlockSpec((None, CHUNK, D), lambda s, *t, q=q: (q % 2, t[q][s], 0))
        in_specs += [tok, tok, tok, tok2, tok2, tok2]
        args += [r, v, nkk, lw, kd, bd]
    grp = lambda s: jnp.where(s < steps_p, s // (L_P // CHUNK),
                              B_P // SCAN_NB + (s - steps_p) // (L_S // CHUNK))
    st = pl.BlockSpec((None,) + s0g.shape[1:], lambda s, *t: (grp(s), 0, 0, 0, 0))
    return pl.pallas_call(
        _rw_scan_kernel,
        grid_spec=pltpu.PrefetchScalarGridSpec(
            num_scalar_prefetch=SCAN_NU + 1,
            grid=(n_steps,),
            in_specs=in_specs + [st],
            out_specs=[pl.BlockSpec((None, SCAN_NU, CHUNK, D), lambda s, *t: (s, 0, 0, 0)), st],
        ),
        out_shape=[jax.ShapeDtypeStruct((n_steps, SCAN_NU, CHUNK, D), BF16),
                   jax.ShapeDtypeStruct(s0g.shape, F32)],
        compiler_params=_cparams(("arbitrary",)),
        name="rwkv_scan",
    )(*rb, first, *args, s0g)


def _scan_out_index(i, k, bwd):
    per_tile = TM // CHUNK
    nc_p, nc_s = L_P // CHUNK, L_S // CHUNK
    steps_p = (B_P // SCAN_NB) * nc_p
    cn_p = k
    b_s = (i - NT_P) // NT_S_SEQ
    cn_s = ((i - NT_P) % NT_S_SEQ) * per_tile + k
    step_p = (i // SCAN_NB) * nc_p + (nc_p - 1 - cn_p if bwd else cn_p)
    step_s = steps_p + (b_s // SCAN_NB) * nc_s + (nc_s - 1 - cn_s if bwd else cn_s)
    q_p = (i % SCAN_NB) * 2 + bwd
    q_s = (b_s % SCAN_NB) * 2 + bwd
    is_p = i < NT_P
    return jnp.where(is_p, step_p, step_s), jnp.where(is_p, q_p, q_s)


def _rw_out_kernel(*refs):
    per_tile = TM // CHUNK
    y_refs = refs[:2 * per_tile]
    bonus_ref, sz_ref, gg_ref, gb_ref, ones_ref, w_ref, x_ref, m_ref, lng_ref, lnb_ref, o_ref = refs[2 * per_tile:]
    ones_bd = ones_ref[...]
    y = jnp.concatenate([y_refs[k][...].astype(F32) + y_refs[per_tile + k][...].astype(F32)
                         for k in range(per_tile)], axis=0)
    mean = _head_sum(y, ones_bd) * (1.0 / RW_N)
    yc = y - mean
    var = _head_sum(yc * yc, ones_bd) * (1.0 / RW_N)
    yn = yc * lax.rsqrt(var + RW_GN_EPS) * gg_ref[...] + gb_ref[...]
    g = (yn + bonus_ref[...]) * sz_ref[...]
    _out_ln_tail(g, w_ref, x_ref, m_ref, lng_ref, lnb_ref, o_ref)


def _rw_out_proj_ln(y, bonus, sz, gn_g, gn_b, ones_bd, w_out, x, mod, layer, ln_g, ln_b):
    row = pl.BlockSpec((TM, D), lambda i: (i, 0))
    vec = pl.BlockSpec((1, D), lambda i: (0, 0))
    mat = pl.BlockSpec((D, D), lambda i: (0, 0))
    y_specs = [pl.BlockSpec((None, None, CHUNK, D), lambda i, k=k, bwd=bwd: _scan_out_index(i, k, bwd) + (0, 0))
               for bwd in (0, 1) for k in range(TM // CHUNK)]
    return pl.pallas_call(
        _rw_out_kernel,
        grid=(NT,),
        in_specs=y_specs + [row, row, vec, vec,
                  pl.BlockSpec((LANES, LANES), lambda i: (0, 0)), mat, row,
                  _mod_spec(layer), vec, vec],
        out_specs=row,
        out_shape=jax.ShapeDtypeStruct((T, D), F32),
        compiler_params=_cparams(("arbitrary",)),
        name="rwkv_out_proj_ln",
    )(*[y] * len(y_specs), bonus, sz, gn_g.reshape(1, D), gn_b.reshape(1, D), ones_bd, w_out.astype(BF16), x, mod,
      ln_g.reshape(1, D), ln_b.reshape(1, D))


def _rwkv_layer(x, mod, layer, ln_g, ln_b, state, mu, w_in, w0, w1, w2, a0, a1, a2, k_k, k_a, r_k, gn_g, gn_b, w_out):
    ones_bd = _head_ones()
    r, v, sz, nkk, bonus, lw, kd, bd = _rw_in_proj(x, mod, layer, ones_bd, mu, w_in, w0, w1, w2, a0, a1, a2,
                                                  k_k, k_a, r_k)
    s_given = _pack_states(state.astype(F32).reshape(B_S * 2, RW_H, RW_N, RW_N))
    gp = B_P // SCAN_NB
    s0g = jnp.concatenate([jnp.zeros((gp,) + s_given.shape[1:], F32), s_given], 0)
    y, s_fin = _rw_scan(r, v, nkk, lw, kd, bd, s0g)
    x_new = _rw_out_proj_ln(y, bonus, sz, gn_g, gn_b, ones_bd, w_out, x, mod, layer, ln_g, ln_b)
    new_state = _unpack_states(s_fin[:gp]).reshape(B_P, 1, 2, RW_H, RW_N, RW_N)
    return x_new, new_state


def kernel(x_prompt, x_sample, cache_mla_ckv, cache_mla_kpe, state_rwkv, c, c_ctx, mod_w, mod_b, ln_g, ln_b, hy_w_in, hy_conv_w, hy_conv_b, hy_ffn_w1, hy_ffn_b1, hy_ffn_w2, hy_ffn_b2, hy_ffn_w3, hy_ffn_b3, hy_freq, hy_decay, hy_skip, hy_w_out, mla_w_in, mla_q_norm, mla_kv_norm, mla_w_q_up, mla_w_kv_up, mla_w_out, rw_mu, rw_w_in, rw_w0, rw_w1, rw_w2, rw_a0, rw_a1, rw_a2, rw_k_k, rw_k_a, rw_r_k, rw_gn_g, rw_gn_b, rw_w_out):
    x = jnp.concatenate([x_prompt.reshape(T_P, D), x_sample.reshape(T_S, D)], 0)
    cond8 = jnp.concatenate([c_ctx[None, :], c, jnp.zeros((8 - 1 - B_S, D), F32)], 0)
    mod = _modulation_table(cond8, mod_w, mod_b).reshape(DEPTH * 8, 1, 3 * D)
    tabs_p = _dft_tables(L_P)
    tabs_s = _dft_tables(L_S)
    rope = _rope_tables()
    new_ckv = new_kpe = new_state = None
    for i in range(DEPTH):
        kind, j = i % 3, i // 3
        if kind == 0:
            x = _hyena_layer(x, mod, i, tabs_p, tabs_s, ln_g[i], ln_b[i], (hy_w_in, j), hy_conv_w[j], hy_conv_b[j],
                             hy_ffn_w1[j], hy_ffn_b1[j], hy_ffn_w2[j], hy_ffn_b2[j], hy_ffn_w3[j], hy_ffn_b3[j],
                             hy_freq[j], hy_decay[j], hy_skip[j], hy_w_out[j], split_out=(i == DEPTH - 1))
        elif kind == 1:
            x, new_ckv, new_kpe = _mla_layer(x, mod, i, rope, ln_g[i], ln_b[i], cache_mla_ckv[:, j],
                                             cache_mla_kpe[:, j], mla_w_in[j], mla_q_norm[j], mla_kv_norm[j],
                                             mla_w_q_up[j], mla_w_kv_up[j], mla_w_out[j])
        else:
            x, new_state = _rwkv_layer(x, mod, i, ln_g[i], ln_b[i], state_rwkv[:, j], rw_mu[j], rw_w_in[j],
                                       rw_w0[j], rw_w1[j], rw_w2[j], rw_a0[j], rw_a1[j], rw_a2[j], rw_k_k[j],
                                       rw_k_a[j], rw_r_k[j], rw_gn_g[j], rw_gn_b[j], rw_w_out[j])
    x_p, x_s = x if isinstance(x, tuple) else (x[:T_P], x[T_P:])
    return (x_p.reshape(B_P, L_P, D), x_s.reshape(B_S, L_S, D), new_ckv, new_kpe, new_state)
```
